```python
import jax, jax.numpy as jnp
from jax import lax
import numpy as np

D_MODEL = 1024
BATCH = 8
SEQ = 2048
DEPTH = 1

CHUNK = 64
MIX_WIDTH = D_MODEL
HG_WIDTH = MIX_WIDTH // 2
HG_HEAD_DIM = 128
HG_HEADS = HG_WIDTH // HG_HEAD_DIM
GM_WIDTH = MIX_WIDTH - HG_WIDTH
GM_BLOCK = 128
GM_GROUPS = 4
GM_GROUP_DIM = GM_WIDTH // GM_GROUPS
IN_WIDTH = 4 * HG_WIDTH + 2 * GM_WIDTH
N_EXPERTS = 256
TOP_K = 8
N_EXPERT_GROUPS = 8
TOPK_GROUPS = 4
EXPERT_DIM = 128 * max(1, round(D_MODEL * 2048 / 7168 / 128))
SHARED_DIM = EXPERT_DIM
ROUTED_SCALE = 2.5
MOE_BLOCK = 128
ALPHA = (2.0 * DEPTH) ** 0.25
BETA = (8.0 * DEPTH) ** -0.25
LN_EPS = 1e-5
RMS_EPS = 1e-6

kernel_name = "hybrid_hgrn2_gmlp_moe_deepnorm"


def layer_norm(x, g, b):
    xf = x.astype(jnp.float32)
    mu = jnp.mean(xf, axis=-1, keepdims=True)
    var = jnp.mean(jnp.square(xf - mu), axis=-1, keepdims=True)
    return ((xf - mu) * lax.rsqrt(var + LN_EPS) * g + b).astype(x.dtype)


def rms_norm(x, g):
    xf = x.astype(jnp.float32)
    inv = lax.rsqrt(jnp.mean(jnp.square(xf), axis=-1, keepdims=True) + RMS_EPS)
    return (xf * inv * g).astype(x.dtype)


def swiglu(x, w_gu, w_down):
    gate, up = jnp.split(x @ w_gu, 2, axis=-1)
    return (jax.nn.silu(gate) * up) @ w_down


def hgrn2_chunkwise(q, log_f, k, v):
    B, S, H, DK = q.shape
    DV = v.shape[-1]
    nc = S // CHUNK

    def to_chunks(t):
        return t.reshape(B, nc, CHUNK, H, t.shape[-1]).transpose(1, 0, 3, 2, 4)

    causal = jnp.tril(jnp.ones((CHUNK, CHUNK), dtype=bool))[:, :, None]

    def chunk_step(state, inp):
        qc, lfc, kc, vc = inp
        b = jnp.cumsum(lfc, axis=2)
        o_inter = jnp.einsum("bhck,bhkv->bhcv", qc * jnp.exp(b), state)
        diff = b[:, :, :, None, :] - b[:, :, None, :, :]
        decay = jnp.exp(jnp.where(causal, diff, -jnp.inf))
        scores = jnp.einsum("bhik,bhjk,bhijk->bhij", qc, kc, decay)
        o_intra = jnp.einsum("bhij,bhjv->bhiv", scores, vc)
        b_last = b[:, :, -1:, :]
        k_dec = kc * jnp.exp(b_last - b)
        new_state = (jnp.exp(b_last[:, :, 0, :])[..., None] * state
                     + jnp.einsum("bhck,bhcv->bhkv", k_dec, vc))
        return new_state, o_inter + o_intra

    state0 = jnp.zeros((B, H, DK, DV), jnp.float32)
    _, o = lax.scan(chunk_step, state0,
                    (to_chunks(q), to_chunks(log_f), to_chunks(k), to_chunks(v)))
    return o.transpose(1, 0, 3, 2, 4).reshape(B, S, H, DV)


def token_mixer(x, w_in, lb, hg_norm_g, gm_v_norm_g, gm_v_norm_b, gm_w_s, gm_b_s,
                gm_out_norm_g, w_out):
    B, S, _ = x.shape
    proj = x @ w_in
    q, f_pre, i_in, g_pre, u, v = jnp.split(
        proj, [HG_WIDTH, 2 * HG_WIDTH, 3 * HG_WIDTH, 4 * HG_WIDTH, 4 * HG_WIDTH + GM_WIDTH], axis=-1)

    def heads(t):
        return t.reshape(B, S, HG_HEADS, HG_HEAD_DIM).astype(jnp.float32)
    lb_h = lb.reshape(HG_HEADS, HG_HEAD_DIM)
    f = lb_h + (1.0 - lb_h) * jax.nn.sigmoid(heads(f_pre))
    o = hgrn2_chunkwise(jax.nn.silu(heads(q)), jnp.log(f), 1.0 - f, heads(i_in))
    o = rms_norm(o, hg_norm_g.reshape(HG_HEADS, HG_HEAD_DIM)) * jax.nn.silu(heads(g_pre))
    y_hg = o.reshape(B, S, HG_WIDTH).astype(x.dtype)

    u = jax.nn.gelu(u, approximate=False)
    v = layer_norm(jax.nn.gelu(v, approximate=False), gm_v_norm_g, gm_v_norm_b)
    vb = v.reshape(B, S // GM_BLOCK, GM_BLOCK, GM_GROUPS, GM_GROUP_DIM)
    chunk_id = jnp.arange(GM_BLOCK) // CHUNK
    allowed = chunk_id[:, None] >= chunk_id[None, :]
    w_s = jnp.where(allowed[None], gm_w_s, 0.0)
    mixed = jnp.einsum("gij,bnjgc->bnigc", w_s, vb) + gm_b_s.T[None, None, :, :, None]
    y_gm = rms_norm(u * mixed.reshape(B, S, GM_WIDTH), gm_out_norm_g)

    return jnp.concatenate([y_hg, y_gm], axis=-1) @ w_out


def moe_ffn(h, w_router, router_bias, w_exp_gu, w_exp_down, w_shared_gu, w_shared_down):
    B, S, D = h.shape
    T = B * S
    A = T * TOP_K
    ht = h.reshape(T, D)
    scores = jax.nn.sigmoid((ht @ w_router).astype(jnp.float32))
    biased = scores + router_bias.astype(jnp.float32)
    per_group = N_EXPERTS // N_EXPERT_GROUPS
    group_score = lax.top_k(biased.reshape(T, N_EXPERT_GROUPS, per_group), 2)[0].sum(-1)
    _, group_idx = lax.top_k(group_score, TOPK_GROUPS)
    group_keep = jax.nn.one_hot(group_idx, N_EXPERT_GROUPS, dtype=jnp.float32).sum(1) > 0
    candidates = jnp.where(jnp.repeat(group_keep, per_group, axis=1), biased, -jnp.inf)
    _, expert_idx = lax.top_k(candidates, TOP_K)
    weights = jnp.take_along_axis(scores, expert_idx, axis=1)
    weights = weights / jnp.sum(weights, axis=-1, keepdims=True) * ROUTED_SCALE

    flat_e = expert_idx.reshape(A)
    order = jnp.argsort(flat_e)
    sorted_e = flat_e[order]
    counts = jnp.bincount(flat_e, length=N_EXPERTS)
    padded = (counts + MOE_BLOCK - 1) // MOE_BLOCK * MOE_BLOCK
    pad_end = jnp.cumsum(padded)
    pad_start = pad_end - padded
    start = jnp.cumsum(counts) - counts
    dest = pad_start[sorted_e] + jnp.arange(A, dtype=jnp.int32) - start[sorted_e]
    n_blocks = -(-(A + N_EXPERTS * (MOE_BLOCK - 1)) // MOE_BLOCK)
    P = n_blocks * MOE_BLOCK
    buf_tok = jnp.full((P,), T, jnp.int32).at[dest].set((order // TOP_K).astype(jnp.int32))
    buf_gate = jnp.zeros((P,), jnp.float32).at[dest].set(weights.reshape(A)[order])
    block_expert = jnp.minimum(
        jnp.searchsorted(pad_end, jnp.arange(n_blocks) * MOE_BLOCK, side="right"), N_EXPERTS - 1)
    x_pad = jnp.concatenate([ht, jnp.zeros((1, D), ht.dtype)], axis=0)

    def expert_block(acc, blk):
        tok, gate, e = blk
        y = swiglu(x_pad[tok], w_exp_gu[e], w_exp_down[e])
        return acc.at[tok].add(y * gate[:, None].astype(y.dtype)), None

    routed, _ = lax.scan(expert_block, jnp.zeros((T + 1, D), ht.dtype),
                         (buf_tok.reshape(n_blocks, MOE_BLOCK),
                          buf_gate.reshape(n_blocks, MOE_BLOCK), block_expert))
    out = routed[:T] + swiglu(ht, w_shared_gu, w_shared_down)
    return out.reshape(B, S, D)


def setup_inputs(seed: int = 0) -> dict:
    key = jax.random.key(seed)
    ks = jax.random.split(key, 24)
    f32 = jnp.float32

    def nrm(k, shape, scale):
        return jax.random.normal(k, shape, f32) * scale

    L = DEPTH
    return {
        "x": nrm(ks[0], (BATCH, SEQ, D_MODEL), 1.0),
        "ln_in_g": 1.0 + nrm(ks[1], (D_MODEL,), 0.05),
        "ln_in_b": nrm(ks[2], (D_MODEL,), 0.05),
        "w_in": nrm(ks[3], (L, D_MODEL, IN_WIDTH), D_MODEL ** -0.5),
        "hg_lb_logits": nrm(ks[4], (L + 1, HG_WIDTH), 0.1),
        "hg_norm_g": 1.0 + nrm(ks[5], (L, HG_WIDTH), 0.05),
        "gm_v_norm_g": 1.0 + nrm(ks[6], (L, GM_WIDTH), 0.05),
        "gm_v_norm_b": nrm(ks[7], (L, GM_WIDTH), 0.05),
        "gm_w_s": nrm(ks[8], (L, GM_GROUPS, GM_BLOCK, GM_BLOCK), GM_BLOCK ** -0.5),
        "gm_b_s": 1.0 + nrm(ks[9], (L, GM_GROUPS, GM_BLOCK), 0.05),
        "gm_out_norm_g": 1.0 + nrm(ks[10], (L, GM_WIDTH), 0.05),
        "w_out": nrm(ks[11], (L, MIX_WIDTH, D_MODEL), BETA * MIX_WIDTH ** -0.5),
        "ln1_g": 1.0 + nrm(ks[12], (L, D_MODEL), 0.05),
        "ln1_b": nrm(ks[13], (L, D_MODEL), 0.05),
        "w_router": nrm(ks[14], (L, D_MODEL, N_EXPERTS), D_MODEL ** -0.5),
        "router_bias": nrm(ks[15], (L, N_EXPERTS), 0.01),
        "w_exp_gu": nrm(ks[16], (L, N_EXPERTS, D_MODEL, 2 * EXPERT_DIM), D_MODEL ** -0.5),
        "w_exp_down": nrm(ks[17], (L, N_EXPERTS, EXPERT_DIM, D_MODEL), BETA * EXPERT_DIM ** -0.5),
        "w_shared_gu": nrm(ks[18], (L, D_MODEL, 2 * SHARED_DIM), D_MODEL ** -0.5),
        "w_shared_down": nrm(ks[19], (L, SHARED_DIM, D_MODEL), BETA * SHARED_DIM ** -0.5),
        "ln2_g": 1.0 + nrm(ks[20], (L, D_MODEL), 0.05),
        "ln2_b": nrm(ks[21], (L, D_MODEL), 0.05),
    }


def reference(x, ln_in_g, ln_in_b, w_in, hg_lb_logits, hg_norm_g, gm_v_norm_g, gm_v_norm_b,
              gm_w_s, gm_b_s, gm_out_norm_g, w_out, ln1_g, ln1_b, w_router, router_bias,
              w_exp_gu, w_exp_down, w_shared_gu, w_shared_down, ln2_g, ln2_b):
    lb_all = jnp.cumsum(jax.nn.softmax(hg_lb_logits.astype(jnp.float32), axis=0), axis=0)
    h = layer_norm(x, ln_in_g, ln_in_b)
    for l in range(DEPTH):
        mix = token_mixer(h, w_in[l], lb_all[l], hg_norm_g[l], gm_v_norm_g[l], gm_v_norm_b[l],
                          gm_w_s[l], gm_b_s[l], gm_out_norm_g[l], w_out[l])
        h = layer_norm(ALPHA * h + mix, ln1_g[l], ln1_b[l])
        ffn = moe_ffn(h, w_router[l], router_bias[l], w_exp_gu[l], w_exp_down[l],
                      w_shared_gu[l], w_shared_down[l])
        h = layer_norm(ALPHA * h + ffn, ln2_g[l], ln2_b[l])
    return h
```

```python
import functools

import jax
import jax.numpy as jnp
from jax import lax
from jax.experimental import pallas as pl
from jax.experimental.pallas import tpu as pltpu

F32 = jnp.float32
BF16 = jnp.bfloat16

LN_EPS = 1e-5
RMS_EPS = 1e-6
CHUNK = 64
SUB = 16
HEAD_DIM = 128
GM_BLOCK = 128
TOP_K = 8
N_GROUPS = 8
TOPK_GROUPS = 4
ROUTED_SCALE = 2.5
MOE_BLOCK = 128
VMEM_LIMIT = 48 * 1024 * 1024


def _cparams(sem):
    return pltpu.CompilerParams(dimension_semantics=sem, vmem_limit_bytes=VMEM_LIMIT)


def _layer_norm(x, g, b):
    mu = jnp.mean(x, axis=-1, keepdims=True)
    xc = x - mu
    var = jnp.mean(xc * xc, axis=-1, keepdims=True)
    return xc * lax.rsqrt(var + LN_EPS) * g + b


def _silu(x):
    return x * jax.nn.sigmoid(x)


def _gelu(x):
    return 0.5 * x * (1.0 + lax.erf(x * (2.0 ** -0.5)))


def _dot(a, b):
    return jnp.dot(a, b, preferred_element_type=F32)


def _dot_nt(a, b, precision=None):
    return lax.dot_general(a, b, (((1,), (1,)), ((), ())), preferred_element_type=F32,
                           precision=precision)


def _dot_tn(a, b):
    return lax.dot_general(a, b, (((0,), (0,)), ((), ())), preferred_element_type=F32)


def _in_proj_kernel(x_ref, g_ref, b_ref, w_ref, h_ref, p_ref):
    h = _layer_norm(x_ref[...], g_ref[...], b_ref[...])
    h_ref[...] = h
    p_ref[...] = _dot(h.astype(BF16), w_ref[...])


def _in_proj(x2, g, b, w_bf, tm):
    t, d = x2.shape
    n = w_bf.shape[1]
    return pl.pallas_call(
        _in_proj_kernel,
        grid=(t // tm,),
        in_specs=[pl.BlockSpec((tm, d), lambda i: (i, 0)),
                  pl.BlockSpec((1, d), lambda i: (0, 0)),
                  pl.BlockSpec((1, d), lambda i: (0, 0)),
                  pl.BlockSpec((d, n), lambda i: (0, 0))],
        out_specs=[pl.BlockSpec((tm, d), lambda i: (i, 0)),
                   pl.BlockSpec((tm, n), lambda i: (i, 0))],
        out_shape=[jax.ShapeDtypeStruct((t, d), F32), jax.ShapeDtypeStruct((t, n), F32)],
        compiler_params=_cparams(("parallel",)),
        name="in_proj",
    )(x2, g.reshape(1, d), b.reshape(1, d), w_bf)


def _hgrn_kernel(q_ref, f_ref, i_ref, g_ref, lbl_ref, gn_ref, o_ref, st_ref, *, n_chunks):
    @pl.when(pl.program_id(2) == 0)
    def _():
        st_ref[...] = jnp.zeros_like(st_ref)

    lg = lbl_ref[...]
    ex = jnp.exp(lg - jnp.max(lg, axis=0, keepdims=True))
    lb = ex[0:1, :] / jnp.sum(ex, axis=0, keepdims=True)
    gn = gn_ref[...]

    c = CHUNK
    row = lax.broadcasted_iota(jnp.int32, (c, c), 0)
    col = lax.broadcasted_iota(jnp.int32, (c, c), 1)
    tril = (row >= col).astype(BF16)
    sub_row = lax.broadcasted_iota(jnp.int32, (SUB, HEAD_DIM), 0)
    ones = jnp.ones((HEAD_DIM, HEAD_DIM), BF16)
    neg_inf = jnp.float32(-jnp.inf)

    def chunk_body(ci, carry):
        r0 = pl.multiple_of(ci * c, c)
        q = _silu(q_ref[pl.ds(r0, c), :])
        f = lb + (1.0 - lb) * jax.nn.sigmoid(f_ref[pl.ds(r0, c), :])
        v = i_ref[pl.ds(r0, c), :]
        lf = jnp.log(f)
        k = 1.0 - f
        lf_hi = lf.astype(BF16)
        lf_lo = (lf - lf_hi.astype(F32)).astype(BF16)
        b = _dot(tril, lf_hi) + _dot(tril, lf_lo)
        b_last = b[c - 1:c, :]
        st = st_ref[...]
        o_inter = _dot_nt((q * jnp.exp(b)).astype(BF16), st.astype(BF16))
        k_dec = (k * jnp.exp(b_last - b)).astype(BF16)
        st_ref[...] = st * jnp.exp(b_last) + _dot_tn(v.astype(BF16), k_dec)

        v_bf = v.astype(BF16)
        outs = []
        for blk in range(c // SUB):
            lo = blk * SUB
            b_i = b[lo:lo + SUB, :]
            q_i = q[lo:lo + SUB, :]
            k_i = k[lo:lo + SUB, :]
            v_i = v[lo:lo + SUB, :]
            prods = []
            for j in range(SUB):
                diff = jnp.where(sub_row >= j, b_i - b_i[j:j + 1, :], neg_inf)
                prods.append(q_i * k_i[j:j + 1, :] * jnp.exp(diff))
            p_all = jnp.concatenate(prods, axis=0).astype(BF16)
            r_all = _dot(p_all, ones)
            o_blk = o_inter[lo:lo + SUB, :]
            for j in range(SUB):
                o_blk = o_blk + r_all[j * SUB:(j + 1) * SUB, :] * v_i[j:j + 1, :]
            if blk > 0:
                b_ref = b[lo - 1:lo, :]
                q_hat = (q_i * jnp.exp(b_i - b_ref)).astype(BF16)
                k_hat = (k[0:lo, :] * jnp.exp(b_ref - b[0:lo, :])).astype(BF16)
                scores = _dot_nt(q_hat, k_hat)
                o_blk = o_blk + _dot(scores.astype(BF16), v_bf[0:lo, :])
            outs.append(o_blk)
        o = jnp.concatenate(outs, axis=0)
        inv = lax.rsqrt(jnp.mean(o * o, axis=-1, keepdims=True) + RMS_EPS)
        o_ref[pl.ds(r0, c), :] = o * inv * gn * _silu(g_ref[pl.ds(r0, c), :])
        return carry

    lax.fori_loop(0, n_chunks, chunk_body, 0)


def _hgrn(proj3, lb_logits, norm_g, sb):
    bsz, s, _ = proj3.shape
    hg_width = norm_g.shape[0]
    heads = hg_width // HEAD_DIM

    def sect(k):
        return pl.BlockSpec((None, sb, HEAD_DIM), lambda b, h, j, k=k: (b, j, k * heads + h))

    return pl.pallas_call(
        functools.partial(_hgrn_kernel, n_chunks=sb // CHUNK),
        grid=(bsz, heads, s // sb),
        in_specs=[sect(0), sect(1), sect(2), sect(3),
                  pl.BlockSpec((lb_logits.shape[0], HEAD_DIM), lambda b, h, j: (0, h)),
                  pl.BlockSpec((1, HEAD_DIM), lambda b, h, j: (0, h))],
        out_specs=pl.BlockSpec((None, sb, HEAD_DIM), lambda b, h, j: (b, j, h)),
        out_shape=jax.ShapeDtypeStruct((bsz, s, hg_width), F32),
        scratch_shapes=[pltpu.VMEM((HEAD_DIM, HEAD_DIM), F32)],
        compiler_params=_cparams(("parallel", "parallel", "arbitrary")),
        name="hgrn",
    )(proj3, proj3, proj3, proj3, lb_logits, norm_g.reshape(1, hg_width))


def _gmlp_kernel(u_ref, v_ref, vg_ref, vb_ref, ws_ref, bs_ref, og_ref, o_ref, *, n_blocks):
    groups = ws_ref.shape[0]
    gdim = u_ref.shape[-1] // groups
    row = lax.broadcasted_iota(jnp.int32, (GM_BLOCK, GM_BLOCK), 0) // CHUNK
    col = lax.broadcasted_iota(jnp.int32, (GM_BLOCK, GM_BLOCK), 1) // CHUNK
    allowed = row >= col
    ws = [jnp.where(allowed, ws_ref[g], 0.0).astype(BF16) for g in range(groups)]
    for n in range(n_blocks):
        rows = pl.ds(n * GM_BLOCK, GM_BLOCK)
        u = _gelu(u_ref[rows, :])
        v = _layer_norm(_gelu(v_ref[rows, :]), vg_ref[...], vb_ref[...])
        v_bf = v.astype(BF16)
        mixed = jnp.concatenate(
            [_dot(ws[g], v_bf[:, g * gdim:(g + 1) * gdim]) for g in range(groups)], axis=1)
        y = u * (mixed + bs_ref[...])
        inv = lax.rsqrt(jnp.mean(y * y, axis=-1, keepdims=True) + RMS_EPS)
        o_ref[rows, :] = y * inv * og_ref[...]


def _gmlp(proj3, v_norm_g, v_norm_b, w_s, b_s, out_norm_g, u_blk, n_blocks):
    bsz, s, _ = proj3.shape
    gw = v_norm_g.shape[0]
    groups = w_s.shape[0]
    rows = n_blocks * GM_BLOCK
    bias_full = jnp.repeat(b_s.T, gw // groups, axis=1)
    return pl.pallas_call(
        functools.partial(_gmlp_kernel, n_blocks=n_blocks),
        grid=(bsz, s // rows),
        in_specs=[pl.BlockSpec((None, rows, gw), lambda b, j: (b, j, u_blk)),
                  pl.BlockSpec((None, rows, gw), lambda b, j: (b, j, u_blk + 1)),
                  pl.BlockSpec((1, gw), lambda b, j: (0, 0)),
                  pl.BlockSpec((1, gw), lambda b, j: (0, 0)),
                  pl.BlockSpec((groups, GM_BLOCK, GM_BLOCK), lambda b, j: (0, 0, 0)),
                  pl.BlockSpec((GM_BLOCK, gw), lambda b, j: (0, 0)),
                  pl.BlockSpec((1, gw), lambda b, j: (0, 0))],
        out_specs=pl.BlockSpec((None, rows, gw), lambda b, j: (b, j, 0)),
        out_shape=jax.ShapeDtypeStruct((bsz, s, gw), F32),
        compiler_params=_cparams(("parallel", "parallel")),
        name="gmlp",
    )(proj3, proj3, v_norm_g.reshape(1, gw), v_norm_b.reshape(1, gw), w_s, bias_full,
      out_norm_g.reshape(1, gw))


def _out_proj_kernel(yh_ref, yg_ref, h_ref, wa_ref, wb_ref, g_ref, b_ref, wr_ref,
                     h1_ref, lg_ref, *, alpha):
    mix = _dot(yh_ref[...].astype(BF16), wa_ref[...]) + _dot(yg_ref[...].astype(BF16), wb_ref[...])
    h1 = _layer_norm(alpha * h_ref[...] + mix, g_ref[...], b_ref[...])
    h1_ref[...] = h1
    lg_ref[...] = _dot_nt(wr_ref[...], h1, precision=lax.Precision.HIGHEST)


def _out_proj(y_hg, y_gm, h0, w_out_bf, g, b, w_router_t, alpha, tm):
    t, d = h0.shape
    hw = y_hg.shape[1]
    gw = y_gm.shape[1]
    e = w_router_t.shape[0]
    return pl.pallas_call(
        functools.partial(_out_proj_kernel, alpha=alpha),
        grid=(t // tm,),
        in_specs=[pl.BlockSpec((tm, hw), lambda i: (i, 0)),
                  pl.BlockSpec((tm, gw), lambda i: (i, 0)),
                  pl.BlockSpec((tm, d), lambda i: (i, 0)),
                  pl.BlockSpec((hw, d), lambda i: (0, 0)),
                  pl.BlockSpec((gw, d), lambda i: (0, 0)),
                  pl.BlockSpec((1, d), lambda i: (0, 0)),
                  pl.BlockSpec((1, d), lambda i: (0, 0)),
                  pl.BlockSpec((e, d), lambda i: (0, 0))],
        out_specs=[pl.BlockSpec((tm, d), lambda i: (i, 0)),
                   pl.BlockSpec((e, tm), lambda i: (0, i))],
        out_shape=[jax.ShapeDtypeStruct((t, d), F32), jax.ShapeDtypeStruct((e, t), F32)],
        compiler_params=_cparams(("parallel",)),
        name="out_proj",
    )(y_hg, y_gm, h0, w_out_bf[:hw], w_out_bf[hw:], g.reshape(1, d), b.reshape(1, d), w_router_t)


def _route_kernel(lg_ref, bias_ref, eidx_ref, w_ref, rank_ref, cnt_ref, carry_ref):
    @pl.when(pl.program_id(0) == 0)
    def _():
        carry_ref[...] = jnp.zeros_like(carry_ref)

    n_exp, tt = lg_ref.shape
    per_group = n_exp // N_GROUPS
    neg_inf = jnp.float32(-jnp.inf)
    scores = jax.nn.sigmoid(lg_ref[...])
    biased = scores + bias_ref[...]

    gio = lax.broadcasted_iota(jnp.int32, (per_group, tt), 0)
    blocks, gs_rows = [], []
    for g in range(N_GROUPS):
        blk = biased[g * per_group:(g + 1) * per_group, :]
        m1 = jnp.max(blk, axis=0, keepdims=True)
        first = jnp.min(jnp.where(blk == m1, gio, per_group), axis=0, keepdims=True)
        m2 = jnp.max(jnp.where(gio == first, neg_inf, blk), axis=0, keepdims=True)
        blocks.append(blk)
        gs_rows.append(m1 + m2)
    gs = jnp.concatenate(gs_rows, axis=0)
    gidx = lax.broadcasted_iota(jnp.int32, (N_GROUPS, tt), 0)
    beaten = jnp.zeros((N_GROUPS, tt), jnp.int32)
    for g in range(N_GROUPS):
        r = gs_rows[g]
        beaten = beaten + jnp.where((r > gs) | ((r == gs) & (g < gidx)), 1, 0)
    keep = beaten < TOPK_GROUPS
    cand = jnp.concatenate(
        [jnp.where(keep[g:g + 1, :], blocks[g], neg_inf) for g in range(N_GROUPS)], axis=0)

    eio = lax.broadcasted_iota(jnp.int32, (n_exp, tt), 0)
    idx_rows, w_rows = [], []
    chosen = jnp.zeros((n_exp, tt), F32)
    for _ in range(TOP_K):
        m = jnp.max(cand, axis=0, keepdims=True)
        idx = jnp.min(jnp.where(cand == m, eio, n_exp), axis=0, keepdims=True)
        sel = eio == idx
        w_rows.append(jnp.sum(jnp.where(sel, scores, 0.0), axis=0, keepdims=True))
        idx_rows.append(idx)
        cand = jnp.where(sel, neg_inf, cand)
        chosen = jnp.where(sel, 1.0, chosen)
    eidx = jnp.concatenate(idx_rows, axis=0)
    w = jnp.concatenate(w_rows, axis=0)
    eidx_ref[...] = eidx
    w_ref[...] = w / jnp.sum(w, axis=0, keepdims=True) * ROUTED_SCALE

    tr = lax.broadcasted_iota(jnp.int32, (tt, tt), 0)
    tc = lax.broadcasted_iota(jnp.int32, (tt, tt), 1)
    before = (tr < tc).astype(BF16)
    cum = _dot(chosen.astype(BF16), before) + carry_ref[...]
    rank_ref[...] = jnp.concatenate(
        [jnp.sum(jnp.where(eio == idx_rows[k], cum, 0.0), axis=0, keepdims=True)
         for k in range(TOP_K)], axis=0).astype(jnp.int32)
    total = carry_ref[...] + jnp.sum(chosen, axis=1, keepdims=True)
    carry_ref[...] = total
    cnt_ref[...] = total.astype(jnp.int32)


def _route(logits_t, router_bias, tt):
    e, t = logits_t.shape
    return pl.pallas_call(
        _route_kernel,
        grid=(t // tt,),
        in_specs=[pl.BlockSpec((e, tt), lambda i: (0, i)),
                  pl.BlockSpec((e, 1), lambda i: (0, 0))],
        out_specs=[pl.BlockSpec((TOP_K, tt), lambda i: (0, i)),
                   pl.BlockSpec((TOP_K, tt), lambda i: (0, i)),
                   pl.BlockSpec((TOP_K, tt), lambda i: (0, i)),
                   pl.BlockSpec((e, 1), lambda i: (0, 0))],
        out_shape=[jax.ShapeDtypeStruct((TOP_K, t), jnp.int32),
                   jax.ShapeDtypeStruct((TOP_K, t), F32),
                   jax.ShapeDtypeStruct((TOP_K, t), jnp.int32),
                   jax.ShapeDtypeStruct((e, 1), jnp.int32)],
        scratch_shapes=[pltpu.VMEM((e, 1), F32)],
        compiler_params=_cparams(("arbitrary",)),
        name="route",
    )(logits_t, router_bias.astype(F32).reshape(e, 1))


def _dest_kernel(ps_ref, eidx_ref, rank_ref, dest_ref, *, n_exp):
    eidx = eidx_ref[...]

    def body(e, acc):
        return acc + jnp.where(eidx == e, ps_ref[e], 0)

    dest_ref[...] = lax.fori_loop(0, n_exp, body, rank_ref[...])


def _dest(pad_start, eidx, rank, tt):
    k, t = eidx.shape
    n_exp = pad_start.shape[0]
    return pl.pallas_call(
        functools.partial(_dest_kernel, n_exp=n_exp),
        grid_spec=pltpu.PrefetchScalarGridSpec(
            num_scalar_prefetch=1,
            grid=(t // tt,),
            in_specs=[pl.BlockSpec((k, tt), lambda i, ps: (0, i)),
                      pl.BlockSpec((k, tt), lambda i, ps: (0, i))],
            out_specs=pl.BlockSpec((k, tt), lambda i, ps: (0, i))),
        out_shape=jax.ShapeDtypeStruct((k, t), jnp.int32),
        compiler_params=_cparams(("parallel",)),
        name="dest",
    )(pad_start, eidx, rank)


def _dispatch_kernel(dest_ref, x_ref, xs_hbm, sem, *, td):
    def row_copy(t, k):
        return pltpu.make_async_copy(x_ref.at[pl.ds(t, 1)], xs_hbm.at[pl.ds(dest_ref[k, t], 1)], sem)

    def issue(t, c):
        for k in range(TOP_K):
            row_copy(t, k).start()
        return c

    lax.fori_loop(0, td, issue, 0)

    def drain(t, c):
        for k in range(TOP_K):
            row_copy(t, k).wait()
        return c

    lax.fori_loop(0, td, drain, 0)


def _dispatch(dest, h1, n_rows, td):
    t, d = h1.shape
    return pl.pallas_call(
        functools.partial(_dispatch_kernel, td=td),
        grid=(t // td,),
        in_specs=[pl.BlockSpec((TOP_K, td), lambda i: (0, i), memory_space=pltpu.SMEM),
                  pl.BlockSpec((td, d), lambda i: (i, 0))],
        out_specs=pl.BlockSpec(memory_space=pl.ANY),
        out_shape=jax.ShapeDtypeStruct((n_rows, d), F32),
        scratch_shapes=[pltpu.SemaphoreType.DMA],
        compiler_params=_cparams(("arbitrary",)),
        name="dispatch",
    )(dest, h1)


def _experts_kernel(be_ref, bv_ref, bi_ref, x_ref, wgu_ref, wd_ref, y_ref, gu_bf, d_bf):
    i = pl.program_id(0)
    valid = bv_ref[i]
    prev = be_ref[jnp.maximum(i - 1, 0)]

    @pl.when((i == 0) | (be_ref[i] != prev))
    def _():
        gu_bf[...] = wgu_ref[...].astype(BF16)
        d_bf[...] = wd_ref[...].astype(BF16)

    @pl.when(valid > 0)
    def _():
        rows = lax.broadcasted_iota(jnp.int32, x_ref.shape, 0)
        x = jnp.where(rows < valid, x_ref[...], 0.0).astype(BF16)
        gu = _dot(x, gu_bf[...])
        half = gu.shape[1] // 2
        hid = _silu(gu[:, :half]) * gu[:, half:]
        y_ref[...] = _dot(hid.astype(BF16), d_bf[...])


def _experts(block_expert, block_valid, block_index, xs, w_gu, w_down):
    n_rows, d = xs.shape
    n_blocks = n_rows // MOE_BLOCK
    _, _, gu_w = w_gu.shape
    ed = w_down.shape[1]
    return pl.pallas_call(
        _experts_kernel,
        grid_spec=pltpu.PrefetchScalarGridSpec(
            num_scalar_prefetch=3,
            grid=(n_blocks,),
            in_specs=[pl.BlockSpec((MOE_BLOCK, d), lambda i, be, bv, bi: (bi[i], 0)),
                      pl.BlockSpec((None, d, gu_w), lambda i, be, bv, bi: (be[i], 0, 0)),
                      pl.BlockSpec((None, ed, d), lambda i, be, bv, bi: (be[i], 0, 0))],
            out_specs=pl.BlockSpec((MOE_BLOCK, d), lambda i, be, bv, bi: (bi[i], 0)),
            scratch_shapes=[pltpu.VMEM((d, gu_w), BF16), pltpu.VMEM((ed, d), BF16)]),
        out_shape=jax.ShapeDtypeStruct((n_rows, d), F32),
        compiler_params=_cparams(("arbitrary",)),
        name="experts",
    )(block_expert, block_valid, block_index, xs, w_gu, w_down)


def _combine_kernel(dest_ref, h_ref, w_ref, wsg_ref, wsd_ref, g_ref, b_ref, ys_hbm, o_ref,
                    buf, sem, *, tc, alpha):
    def row_copy(t, k):
        return pltpu.make_async_copy(ys_hbm.at[pl.ds(dest_ref[k, t], 1)],
                                     buf.at[k, pl.ds(t, 1)], sem)

    def issue(t, c):
        for k in range(TOP_K):
            row_copy(t, k).start()
        return c

    lax.fori_loop(0, tc, issue, 0)

    h = h_ref[...]
    gu = _dot(h.astype(BF16), wsg_ref[...])
    half = gu.shape[1] // 2
    shared = _dot((_silu(gu[:, :half]) * gu[:, half:]).astype(BF16), wsd_ref[...])
    acc = alpha * h + shared

    def drain(t, c):
        for k in range(TOP_K):
            row_copy(t, k).wait()
        return c

    lax.fori_loop(0, tc, drain, 0)

    w = w_ref[...]
    for k in range(TOP_K):
        acc = acc + buf[k] * w[:, k:k + 1]
    o_ref[...] = _layer_norm(acc, g_ref[...], b_ref[...])


def _combine(dest, h1, w_tk, ws_gu_bf, ws_down_bf, g, b, ys, alpha, tc):
    t, d = h1.shape
    sg = ws_gu_bf.shape[1]
    sd = ws_down_bf.shape[0]
    return pl.pallas_call(
        functools.partial(_combine_kernel, tc=tc, alpha=alpha),
        grid=(t // tc,),
        in_specs=[pl.BlockSpec((TOP_K, tc), lambda i: (0, i), memory_space=pltpu.SMEM),
                  pl.BlockSpec((tc, d), lambda i: (i, 0)),
                  pl.BlockSpec((tc, TOP_K), lambda i: (i, 0)),
                  pl.BlockSpec((d, sg), lambda i: (0, 0)),
                  pl.BlockSpec((sd, d), lambda i: (0, 0)),
                  pl.BlockSpec((1, d), lambda i: (0, 0)),
                  pl.BlockSpec((1, d), lambda i: (0, 0)),
                  pl.BlockSpec(memory_space=pl.ANY)],
        out_specs=pl.BlockSpec((tc, d), lambda i: (i, 0)),
        out_shape=jax.ShapeDtypeStruct((t, d), F32),
        scratch_shapes=[pltpu.VMEM((TOP_K, tc, d), F32), pltpu.SemaphoreType.DMA],
        compiler_params=_cparams(("arbitrary",)),
        name="combine",
    )(dest, h1, w_tk, ws_gu_bf, ws_down_bf, g.reshape(1, d), b.reshape(1, d), ys)


def _moe(h1, logits_t, router_bias, w_exp_gu, w_exp_down, w_shared_gu, w_shared_down,
         ln_g, ln_b, alpha):
    t, d = h1.shape
    n_exp = logits_t.shape[0]
    eidx, w_kt, rank, counts = _route(logits_t, router_bias, tt=min(512, t))

    counts = counts.reshape(n_exp)
    padded = (counts + MOE_BLOCK - 1) // MOE_BLOCK * MOE_BLOCK
    pad_end = jnp.cumsum(padded)
    pad_start = (pad_end - padded).astype(jnp.int32)
    n_blocks = -(-(t * TOP_K + n_exp * (MOE_BLOCK - 1)) // MOE_BLOCK)
    n_used = pad_end[-1] // MOE_BLOCK
    blk = jnp.arange(n_blocks, dtype=jnp.int32)
    block_index = jnp.minimum(blk, n_used - 1).astype(jnp.int32)
    block_expert = jnp.minimum(
        jnp.searchsorted(pad_end, block_index * MOE_BLOCK, side="right"), n_exp - 1).astype(jnp.int32)
    block_valid = jnp.where(
        blk < n_used,
        jnp.clip(counts[block_expert] - (block_index * MOE_BLOCK - pad_start[block_expert]),
                 0, MOE_BLOCK), 0).astype(jnp.int32)

    dest = _dest(pad_start, eidx, rank, tt=min(2048, t))
    xs = _dispatch(dest, h1, n_blocks * MOE_BLOCK, td=min(256, t))
    ys = _experts(block_expert, block_valid, block_index, xs, w_exp_gu, w_exp_down)
    return _combine(dest, h1, w_kt.T, w_shared_gu.astype(BF16), w_shared_down.astype(BF16),
                    ln_g, ln_b, ys, alpha, tc=min(128, t))


def kernel(x, ln_in_g, ln_in_b, w_in, hg_lb_logits, hg_norm_g, gm_v_norm_g, gm_v_norm_b, gm_w_s, gm_b_s, gm_out_norm_g, w_out, ln1_g, ln1_b, w_router, router_bias, w_exp_gu, w_exp_down, w_shared_gu, w_shared_down, ln2_g, ln2_b):
    bsz, s, d = x.shape
    depth = w_in.shape[0]
    assert depth == 1, "the lower-bound table row used in the hgrn kernel assumes one layer"
    alpha = (2.0 * depth) ** 0.25
    t = bsz * s
    hg_width = hg_norm_g.shape[1]
    gm_width = gm_v_norm_g.shape[1]
    tm = min(512, t)

    h0, proj = _in_proj(x.reshape(t, d), ln_in_g, ln_in_b, w_in[0].astype(BF16), tm)
    proj3 = proj.reshape(bsz, s, proj.shape[1])
    y_hg = _hgrn(proj3, hg_lb_logits.astype(F32), hg_norm_g[0], sb=min(512, s))
    y_gm = _gmlp(proj3, gm_v_norm_g[0], gm_v_norm_b[0], gm_w_s[0], gm_b_s[0], gm_out_norm_g[0],
                 u_blk=4 * hg_width // gm_width, n_blocks=min(4, s // GM_BLOCK))
    h1, logits_t = _out_proj(y_hg.reshape(t, hg_width), y_gm.reshape(t, gm_width), h0,
                             w_out[0].astype(BF16), ln1_g[0], ln1_b[0], w_router[0].T, alpha, tm)
    out = _moe(h1, logits_t, router_bias[0], w_exp_gu[0], w_exp_down[0], w_shared_gu[0],
               w_shared_down[0], ln2_g[0], ln2_b[0], alpha)
    return out.reshape(bsz, s, d)
```

```python
import functools

import jax
import jax.numpy as jnp
from jax import lax
from jax.experimental import pallas as pl
from jax.experimental.pallas import tpu as pltpu

F32 = jnp.float32
BF16 = jnp.bfloat16

LN_EPS = 1e-5
RMS_EPS = 1e-6
CHUNK = 64
SUB = 16
HEAD_DIM = 128
GM_BLOCK = 128
TOP_K = 8
N_GROUPS = 8
TOPK_GROUPS = 4
ROUTED_SCALE = 2.5
MOE_BLOCK = 128
VMEM_LIMIT = 48 * 1024 * 1024


def _cparams(sem):
    return pltpu.CompilerParams(dimension_semantics=sem, vmem_limit_bytes=VMEM_LIMIT)


def _layer_norm(x, g, b):
    mu = jnp.mean(x, axis=-1, keepdims=True)
    xc = x - mu
    var = jnp.mean(xc * xc, axis=-1, keepdims=True)
    return xc * lax.rsqrt(var + LN_EPS) * g + b


def _silu(x):
    return x * jax.nn.sigmoid(x)


def _gelu(x):
    return 0.5 * x * (1.0 + lax.erf(x * (2.0 ** -0.5)))


def _dot(a, b):
    return jnp.dot(a, b, preferred_element_type=F32)


def _dot_nt(a, b, precision=None):
    return lax.dot_general(a, b, (((1,), (1,)), ((), ())), preferred_element_type=F32,
                           precision=precision)


def _dot_tn(a, b):
    return lax.dot_general(a, b, (((0,), (0,)), ((), ())), preferred_element_type=F32)


def _pack_halves(x):
    n = x.shape[1] // 2
    hi = lax.bitcast_convert_type(x[:, :n].astype(BF16).astype(F32), jnp.uint32)
    lo = lax.bitcast_convert_type(x[:, n:].astype(BF16).astype(F32), jnp.uint32)
    return hi | (lo >> 16)


def _unpack_halves(w):
    a = lax.bitcast_convert_type(w & jnp.uint32(0xFFFF0000), F32)
    b = lax.bitcast_convert_type(w << 16, F32)
    return a, b


def _in_proj_kernel(x_ref, g_ref, b_ref, w_ref, h_ref, p_ref):
    h = _layer_norm(x_ref[...], g_ref[...], b_ref[...])
    h_ref[...] = h
    p_ref[...] = _dot(h.astype(BF16), w_ref[...])


def _in_proj(x2, g, b, w_bf, tm):
    t, d = x2.shape
    n = w_bf.shape[1]
    return pl.pallas_call(
        _in_proj_kernel,
        grid=(t // tm,),
        in_specs=[pl.BlockSpec((tm, d), lambda i: (i, 0)),
                  pl.BlockSpec((1, d), lambda i: (0, 0)),
                  pl.BlockSpec((1, d), lambda i: (0, 0)),
                  pl.BlockSpec((d, n), lambda i: (0, 0))],
        out_specs=[pl.BlockSpec((tm, d), lambda i: (i, 0)),
                   pl.BlockSpec((tm, n), lambda i: (i, 0))],
        out_shape=[jax.ShapeDtypeStruct((t, d), F32), jax.ShapeDtypeStruct((t, n), F32)],
        compiler_params=_cparams(("parallel",)),
        name="in_proj",
    )(x2, g.reshape(1, d), b.reshape(1, d), w_bf)


def _hgrn_kernel(q_ref, f_ref, i_ref, g_ref, lbl_ref, gn_ref, o_ref, st_ref, *, n_chunks):
    @pl.when(pl.program_id(1) == 0)
    def _():
        st_ref[...] = jnp.zeros_like(st_ref)

    lg = lbl_ref[...]
    ex = jnp.exp(lg - jnp.max(lg, axis=0, keepdims=True))
    lb_all = ex[0:1, :] / jnp.sum(ex, axis=0, keepdims=True)
    gn_all = gn_ref[...]

    c = CHUNK
    heads = st_ref.shape[0]
    row = lax.broadcasted_iota(jnp.int32, (c, c), 0)
    col = lax.broadcasted_iota(jnp.int32, (c, c), 1)
    tril = (row >= col).astype(BF16)
    sub_row = lax.broadcasted_iota(jnp.int32, (SUB, HEAD_DIM), 0)
    ones = jnp.ones((HEAD_DIM, HEAD_DIM), BF16)
    neg_inf = jnp.float32(-jnp.inf)

    def head_chunk(r0, h):
        lanes = pl.ds(h * HEAD_DIM, HEAD_DIM)
        lb = lb_all[:, h * HEAD_DIM:(h + 1) * HEAD_DIM]
        gn = gn_all[:, h * HEAD_DIM:(h + 1) * HEAD_DIM]
        q = _silu(q_ref[pl.ds(r0, c), lanes])
        f = lb + (1.0 - lb) * jax.nn.sigmoid(f_ref[pl.ds(r0, c), lanes])
        v = i_ref[pl.ds(r0, c), lanes]
        lf = jnp.log(f)
        k = 1.0 - f
        lf_hi = lf.astype(BF16)
        lf_lo = (lf - lf_hi.astype(F32)).astype(BF16)
        b = _dot(tril, lf_hi) + _dot(tril, lf_lo)
        b_last = b[c - 1:c, :]
        st = st_ref[h]
        o_inter = _dot_nt((q * jnp.exp(b)).astype(BF16), st.astype(BF16))
        k_dec = (k * jnp.exp(b_last - b)).astype(BF16)
        st_ref[h] = st * jnp.exp(b_last) + _dot_tn(v.astype(BF16), k_dec)

        v_bf = v.astype(BF16)
        outs = []
        for blk in range(c // SUB):
            lo = blk * SUB
            b_i = b[lo:lo + SUB, :]
            q_i = q[lo:lo + SUB, :]
            k_i = k[lo:lo + SUB, :]
            v_i = v[lo:lo + SUB, :]
            prods = []
            for j in range(SUB):
                diff = jnp.where(sub_row >= j, b_i - b_i[j:j + 1, :], neg_inf)
                prods.append(q_i * k_i[j:j + 1, :] * jnp.exp(diff))
            p_all = jnp.concatenate(prods, axis=0).astype(BF16)
            r_all = _dot(p_all, ones)
            o_blk = o_inter[lo:lo + SUB, :]
            for j in range(SUB):
                o_blk = o_blk + r_all[j * SUB:(j + 1) * SUB, :] * v_i[j:j + 1, :]
            if blk > 0:
                b_ref = b[lo - 1:lo, :]
                q_hat = (q_i * jnp.exp(b_i - b_ref)).astype(BF16)
                k_hat = (k[0:lo, :] * jnp.exp(b_ref - b[0:lo, :])).astype(BF16)
                scores = _dot_nt(q_hat, k_hat)
                o_blk = o_blk + _dot(scores.astype(BF16), v_bf[0:lo, :])
            outs.append(o_blk)
        o = jnp.concatenate(outs, axis=0)
        inv = lax.rsqrt(jnp.mean(o * o, axis=-1, keepdims=True) + RMS_EPS)
        o_ref[pl.ds(r0, c), lanes] = o * inv * gn * _silu(g_ref[pl.ds(r0, c), lanes])

    def chunk_body(ci, carry):
        r0 = pl.multiple_of(ci * c, c)
        for h in range(heads):
            head_chunk(r0, h)
        return carry

    lax.fori_loop(0, n_chunks, chunk_body, 0)


def _hgrn(proj3, lb_logits, norm_g, sb):
    bsz, s, _ = proj3.shape
    hg_width = norm_g.shape[0]
    heads = hg_width // HEAD_DIM

    def sect(k):
        return pl.BlockSpec((None, sb, hg_width), lambda b, j, k=k: (b, j, k))

    return pl.pallas_call(
        functools.partial(_hgrn_kernel, n_chunks=sb // CHUNK),
        grid=(bsz, s // sb),
        in_specs=[sect(0), sect(1), sect(2), sect(3),
                  pl.BlockSpec((lb_logits.shape[0], hg_width), lambda b, j: (0, 0)),
                  pl.BlockSpec((1, hg_width), lambda b, j: (0, 0))],
        out_specs=pl.BlockSpec((None, sb, hg_width), lambda b, j: (b, j, 0)),
        out_shape=jax.ShapeDtypeStruct((bsz, s, hg_width), F32),
        scratch_shapes=[pltpu.VMEM((heads, HEAD_DIM, HEAD_DIM), F32)],
        compiler_params=_cparams(("parallel", "arbitrary")),
        name="hgrn",
    )(proj3, proj3, proj3, proj3, lb_logits, norm_g.reshape(1, hg_width))


def _gmlp_kernel(u_ref, v_ref, vg_ref, vb_ref, ws_ref, bs_ref, og_ref, o_ref, *, n_blocks):
    groups = ws_ref.shape[0]
    gdim = u_ref.shape[-1] // groups
    row = lax.broadcasted_iota(jnp.int32, (GM_BLOCK, GM_BLOCK), 0) // CHUNK
    col = lax.broadcasted_iota(jnp.int32, (GM_BLOCK, GM_BLOCK), 1) // CHUNK
    allowed = row >= col
    ws = [jnp.where(allowed, ws_ref[g], 0.0).astype(BF16) for g in range(groups)]
    for n in range(n_blocks):
        rows = pl.ds(n * GM_BLOCK, GM_BLOCK)
        u = _gelu(u_ref[rows, :])
        v = _layer_norm(_gelu(v_ref[rows, :]), vg_ref[...], vb_ref[...])
        v_bf = v.astype(BF16)
        mixed = jnp.concatenate(
            [_dot(ws[g], v_bf[:, g * gdim:(g + 1) * gdim]) for g in range(groups)], axis=1)
        y = u * (mixed + bs_ref[...])
        inv = lax.rsqrt(jnp.mean(y * y, axis=-1, keepdims=True) + RMS_EPS)
        o_ref[rows, :] = y * inv * og_ref[...]


def _gmlp(proj3, v_norm_g, v_norm_b, w_s, b_s, out_norm_g, u_blk, n_blocks):
    bsz, s, _ = proj3.shape
    gw = v_norm_g.shape[0]
    groups = w_s.shape[0]
    rows = n_blocks * GM_BLOCK
    bias_full = jnp.repeat(b_s.T, gw // groups, axis=1)
    return pl.pallas_call(
        functools.partial(_gmlp_kernel, n_blocks=n_blocks),
        grid=(bsz, s // rows),
        in_specs=[pl.BlockSpec((None, rows, gw), lambda b, j: (b, j, u_blk)),
                  pl.BlockSpec((None, rows, gw), lambda b, j: (b, j, u_blk + 1)),
                  pl.BlockSpec((1, gw), lambda b, j: (0, 0)),
                  pl.BlockSpec((1, gw), lambda b, j: (0, 0)),
                  pl.BlockSpec((groups, GM_BLOCK, GM_BLOCK), lambda b, j: (0, 0, 0)),
                  pl.BlockSpec((GM_BLOCK, gw), lambda b, j: (0, 0)),
                  pl.BlockSpec((1, gw), lambda b, j: (0, 0))],
        out_specs=pl.BlockSpec((None, rows, gw), lambda b, j: (b, j, 0)),
        out_shape=jax.ShapeDtypeStruct((bsz, s, gw), F32),
        compiler_params=_cparams(("parallel", "parallel")),
        name="gmlp",
    )(proj3, proj3, v_norm_g.reshape(1, gw), v_norm_b.reshape(1, gw), w_s, bias_full,
      out_norm_g.reshape(1, gw))


def _out_proj_kernel(yh_ref, yg_ref, h_ref, wa_ref, wb_ref, g_ref, b_ref, wr_ref,
                     h1_ref, hp_ref, lg_ref, *, alpha):
    mix = _dot(yh_ref[...].astype(BF16), wa_ref[...]) + _dot(yg_ref[...].astype(BF16), wb_ref[...])
    h1 = _layer_norm(alpha * h_ref[...] + mix, g_ref[...], b_ref[...])
    h1_ref[...] = h1
    hp_ref[...] = _pack_halves(h1)
    lg_ref[...] = _dot_nt(wr_ref[...], h1, precision=lax.Precision.HIGHEST)


def _out_proj(y_hg, y_gm, h0, w_out_bf, g, b, w_router_t, alpha, tm):
    t, d = h0.shape
    hw = y_hg.shape[1]
    gw = y_gm.shape[1]
    e = w_router_t.shape[0]
    return pl.pallas_call(
        functools.partial(_out_proj_kernel, alpha=alpha),
        grid=(t // tm,),
        in_specs=[pl.BlockSpec((tm, hw), lambda i: (i, 0)),
                  pl.BlockSpec((tm, gw), lambda i: (i, 0)),
                  pl.BlockSpec((tm, d), lambda i: (i, 0)),
                  pl.BlockSpec((hw, d), lambda i: (0, 0)),
                  pl.BlockSpec((gw, d), lambda i: (0, 0)),
                  pl.BlockSpec((1, d), lambda i: (0, 0)),
                  pl.BlockSpec((1, d), lambda i: (0, 0)),
                  pl.BlockSpec((e, d), lambda i: (0, 0))],
        out_specs=[pl.BlockSpec((tm, d), lambda i: (i, 0)),
                   pl.BlockSpec((tm, d // 2), lambda i: (i, 0)),
                   pl.BlockSpec((e, tm), lambda i: (0, i))],
        out_shape=[jax.ShapeDtypeStruct((t, d), F32), jax.ShapeDtypeStruct((t, d // 2), jnp.uint32),
                   jax.ShapeDtypeStruct((e, t), F32)],
        compiler_params=_cparams(("parallel",)),
        name="out_proj",
    )(y_hg, y_gm, h0, w_out_bf[:hw], w_out_bf[hw:], g.reshape(1, d), b.reshape(1, d), w_router_t)


def _route_kernel(lg_ref, bias_ref, eidx_ref, w_ref, rank_ref, cnt_ref, carry_ref):
    @pl.when(pl.program_id(0) == 0)
    def _():
        carry_ref[...] = jnp.zeros_like(carry_ref)

    n_exp, tt = lg_ref.shape
    per_group = n_exp // N_GROUPS
    neg_inf = jnp.float32(-jnp.inf)
    scores = jax.nn.sigmoid(lg_ref[...])
    biased = scores + bias_ref[...]

    gio = lax.broadcasted_iota(jnp.int32, (per_group, tt), 0)
    blocks, gs_rows = [], []
    for g in range(N_GROUPS):
        blk = biased[g * per_group:(g + 1) * per_group, :]
        m1 = jnp.max(blk, axis=0, keepdims=True)
        first = jnp.min(jnp.where(blk == m1, gio, per_group), axis=0, keepdims=True)
        m2 = jnp.max(jnp.where(gio == first, neg_inf, blk), axis=0, keepdims=True)
        blocks.append(blk)
        gs_rows.append(m1 + m2)
    gs = jnp.concatenate(gs_rows, axis=0)
    gidx = lax.broadcasted_iota(jnp.int32, (N_GROUPS, tt), 0)
    beaten = jnp.zeros((N_GROUPS, tt), jnp.int32)
    for g in range(N_GROUPS):
        r = gs_rows[g]
        beaten = beaten + jnp.where((r > gs) | ((r == gs) & (g < gidx)), 1, 0)
    keep = beaten < TOPK_GROUPS
    cand = jnp.concatenate(
        [jnp.where(keep[g:g + 1, :], blocks[g], neg_inf) for g in range(N_GROUPS)], axis=0)

    eio = lax.broadcasted_iota(jnp.int32, (n_exp, tt), 0)
    idx_rows, w_rows = [], []
    chosen = jnp.zeros((n_exp, tt), F32)
    for _ in range(TOP_K):
        m = jnp.max(cand, axis=0, keepdims=True)
        idx = jnp.min(jnp.where(cand == m, eio, n_exp), axis=0, keepdims=True)
        sel = eio == idx
        w_rows.append(jnp.sum(jnp.where(sel, scores, 0.0), axis=0, keepdims=True))
        idx_rows.append(idx)
        cand = jnp.where(sel, neg_inf, cand)
        chosen = jnp.where(sel, 1.0, chosen)
    eidx = jnp.concatenate(idx_rows, axis=0)
    w = jnp.concatenate(w_rows, axis=0)
    eidx_ref[...] = eidx
    w_ref[...] = w / jnp.sum(w, axis=0, keepdims=True) * ROUTED_SCALE

    tr = lax.broadcasted_iota(jnp.int32, (tt, tt), 0)
    tc = lax.broadcasted_iota(jnp.int32, (tt, tt), 1)
    before = (tr < tc).astype(BF16)
    cum = _dot(chosen.astype(BF16), before) + carry_ref[...]
    rank_ref[...] = jnp.concatenate(
        [jnp.sum(jnp.where(eio == idx_rows[k], cum, 0.0), axis=0, keepdims=True)
         for k in range(TOP_K)], axis=0).astype(jnp.int32)
    total = carry_ref[...] + jnp.sum(chosen, axis=1, keepdims=True)
    carry_ref[...] = total
    cnt_ref[...] = total.astype(jnp.int32)


def _route(logits_t, router_bias, tt):
    e, t = logits_t.shape
    return pl.pallas_call(
        _route_kernel,
        grid=(t // tt,),
        in_specs=[pl.BlockSpec((e, tt), lambda i: (0, i)),
                  pl.BlockSpec((e, 1), lambda i: (0, 0))],
        out_specs=[pl.BlockSpec((TOP_K, tt), lambda i: (0, i)),
                   pl.BlockSpec((TOP_K, tt), lambda i: (0, i)),
                   pl.BlockSpec((TOP_K, tt), lambda i: (0, i)),
                   pl.BlockSpec((e, 1), lambda i: (0, 0))],
        out_shape=[jax.ShapeDtypeStruct((TOP_K, t), jnp.int32),
                   jax.ShapeDtypeStruct((TOP_K, t), F32),
                   jax.ShapeDtypeStruct((TOP_K, t), jnp.int32),
                   jax.ShapeDtypeStruct((e, 1), jnp.int32)],
        scratch_shapes=[pltpu.VMEM((e, 1), F32)],
        compiler_params=_cparams(("arbitrary",)),
        name="route",
    )(logits_t, router_bias.astype(F32).reshape(e, 1))


def _dest_kernel(ps_ref, eidx_ref, rank_ref, dest_ref, *, n_exp):
    eidx = eidx_ref[...]

    def body(e, acc):
        return acc + jnp.where(eidx == e, ps_ref[e], 0)

    dest_ref[...] = lax.fori_loop(0, n_exp, body, rank_ref[...])


def _dest(pad_start, eidx, rank, tt):
    k, t = eidx.shape
    n_exp = pad_start.shape[0]
    return pl.pallas_call(
        functools.partial(_dest_kernel, n_exp=n_exp),
        grid_spec=pltpu.PrefetchScalarGridSpec(
            num_scalar_prefetch=1,
            grid=(t // tt,),
            in_specs=[pl.BlockSpec((k, tt), lambda i, ps: (0, i)),
                      pl.BlockSpec((k, tt), lambda i, ps: (0, i))],
            out_specs=pl.BlockSpec((k, tt), lambda i, ps: (0, i))),
        out_shape=jax.ShapeDtypeStruct((k, t), jnp.int32),
        compiler_params=_cparams(("parallel",)),
        name="dest",
    )(pad_start, eidx, rank)


def _dispatch_kernel(dest_ref, x_ref, xs_hbm, sem, *, td):
    def row_copy(t, k):
        return pltpu.make_async_copy(x_ref.at[pl.ds(t, 1)], xs_hbm.at[pl.ds(dest_ref[k, t], 1)], sem)

    def issue(t, c):
        for k in range(TOP_K):
            row_copy(t, k).start(priority=k % 2)
        return c

    lax.fori_loop(0, td, issue, 0)

    def drain(t, c):
        for k in range(TOP_K):
            row_copy(t, k).wait()
        return c

    lax.fori_loop(0, td, drain, 0)


def _dispatch(dest, h1, n_rows, td):
    t, d = h1.shape
    return pl.pallas_call(
        functools.partial(_dispatch_kernel, td=td),
        grid=(t // td,),
        in_specs=[pl.BlockSpec((TOP_K, td), lambda i: (0, i), memory_space=pltpu.SMEM),
                  pl.BlockSpec((td, d), lambda i: (i, 0))],
        out_specs=pl.BlockSpec(memory_space=pl.ANY),
        out_shape=jax.ShapeDtypeStruct((n_rows, d), h1.dtype),
        scratch_shapes=[pltpu.SemaphoreType.DMA],
        compiler_params=_cparams(("arbitrary",)),
        name="dispatch",
    )(dest, h1)


def _experts_kernel(bs_ref, cnt_ref, wgu_ref, wd_ref, xs_hbm, ys_hbm, gu_bf, d_bf, xbuf, ybuf,
                    xsem, ysem):
    e = pl.program_id(0)
    n_exp = pl.num_programs(0)
    mb = xbuf.shape[1]
    n_blocks = bs_ref[n_exp]
    b0 = bs_ref[e]
    b1 = bs_ref[e + 1]
    count = cnt_ref[e]

    def x_copy(b, slot):
        return pltpu.make_async_copy(xs_hbm.at[pl.ds(pl.multiple_of(b * mb, mb), mb)],
                                     xbuf.at[slot], xsem.at[slot])

    def y_copy(b, slot):
        return pltpu.make_async_copy(ybuf.at[slot],
                                     ys_hbm.at[pl.ds(pl.multiple_of(b * mb, mb), mb)], ysem.at[slot])

    @pl.when(e == 0)
    def _():
        x_copy(0, 0).start()

    @pl.when(b1 > b0)
    def _():
        gu_bf[...] = wgu_ref[...].astype(BF16)
        d_bf[...] = wd_ref[...].astype(BF16)

    kh = gu_bf.shape[0] // 2
    half = gu_bf.shape[1] // 2

    def block(b, carry):
        slot = lax.rem(b, 2)
        x_copy(b, slot).wait()

        @pl.when(b + 1 < n_blocks)
        def _():
            x_copy(b + 1, 1 - slot).start()

        @pl.when(b >= 2)
        def _():
            y_copy(b - 2, slot).wait()

        rows = lax.broadcasted_iota(jnp.int32, (mb, xbuf.shape[2]), 0)
        valid = count - (b - b0) * mb
        words = jnp.where(rows < valid, xbuf[slot], jnp.uint32(0))
        xa, xb = _unpack_halves(words)
        gu = _dot(xa.astype(BF16), gu_bf[0:kh, :]) + _dot(xb.astype(BF16), gu_bf[kh:, :])
        hid = _silu(gu[:, :half]) * gu[:, half:]
        ybuf[slot] = _pack_halves(_dot(hid.astype(BF16), d_bf[...]))
        y_copy(b, slot).start()
        return carry

    lax.fori_loop(b0, b1, block, 0)

    @pl.when(e == n_exp - 1)
    def _():
        @pl.when(n_blocks >= 2)
        def _():
            y_copy(n_blocks - 2, lax.rem(n_blocks, 2)).wait()

        y_copy(n_blocks - 1, lax.rem(n_blocks - 1, 2)).wait()


def _experts(block_start, counts, xs, w_gu, w_down):
    n_rows, dh = xs.shape
    n_exp, d, gu_w = w_gu.shape
    ed = w_down.shape[1]
    return pl.pallas_call(
        _experts_kernel,
        grid_spec=pltpu.PrefetchScalarGridSpec(
            num_scalar_prefetch=2,
            grid=(n_exp,),
            in_specs=[pl.BlockSpec((None, d, gu_w), lambda e, bs, cnt: (e, 0, 0)),
                      pl.BlockSpec((None, ed, d), lambda e, bs, cnt: (e, 0, 0)),
                      pl.BlockSpec(memory_space=pl.ANY)],
            out_specs=pl.BlockSpec(memory_space=pl.ANY),
            scratch_shapes=[pltpu.VMEM((d, gu_w), BF16), pltpu.VMEM((ed, d), BF16),
                            pltpu.VMEM((2, MOE_BLOCK, dh), jnp.uint32),
                            pltpu.VMEM((2, MOE_BLOCK, dh), jnp.uint32),
                            pltpu.SemaphoreType.DMA((2,)), pltpu.SemaphoreType.DMA((2,))]),
        out_shape=jax.ShapeDtypeStruct((n_rows, dh), jnp.uint32),
        compiler_params=_cparams(("arbitrary",)),
        name="experts",
    )(block_start, counts, w_gu, w_down, xs)


def _combine_kernel(dest_ref, h_ref, w_ref, wsg_ref, wsd_ref, g_ref, b_ref, ys_hbm, o_ref,
                    buf, sem, *, tc, alpha):
    def row_copy(t, k):
        return pltpu.make_async_copy(ys_hbm.at[pl.ds(dest_ref[k, t], 1)],
                                     buf.at[k, pl.ds(t, 1)], sem)

    def issue(t, c):
        for k in range(TOP_K):
            row_copy(t, k).start(priority=k % 2)
        return c

    lax.fori_loop(0, tc, issue, 0)

    h = h_ref[...]
    gu = _dot(h.astype(BF16), wsg_ref[...])
    half = gu.shape[1] // 2
    shared = _dot((_silu(gu[:, :half]) * gu[:, half:]).astype(BF16), wsd_ref[...])
    acc = alpha * h + shared
    dh = acc.shape[1] // 2
    acc_a = acc[:, :dh]
    acc_b = acc[:, dh:]

    def drain(t, c):
        for k in range(TOP_K):
            row_copy(t, k).wait()
        return c

    lax.fori_loop(0, tc, drain, 0)

    w = w_ref[...]
    for k in range(TOP_K):
        ya, yb = _unpack_halves(buf[k])
        acc_a = acc_a + ya * w[:, k:k + 1]
        acc_b = acc_b + yb * w[:, k:k + 1]
    o_ref[...] = _layer_norm(jnp.concatenate([acc_a, acc_b], axis=1), g_ref[...], b_ref[...])


def _combine(dest, h1, w_tk, ws_gu_bf, ws_down_bf, g, b, ys, alpha, tc):
    t, d = h1.shape
    sg = ws_gu_bf.shape[1]
    sd = ws_down_bf.shape[0]
    return pl.pallas_call(
        functools.partial(_combine_kernel, tc=tc, alpha=alpha),
        grid=(t // tc,),
        in_specs=[pl.BlockSpec((TOP_K, tc), lambda i: (0, i), memory_space=pltpu.SMEM),
                  pl.BlockSpec((tc, d), lambda i: (i, 0)),
                  pl.BlockSpec((tc, TOP_K), lambda i: (i, 0)),
                  pl.BlockSpec((d, sg), lambda i: (0, 0)),
                  pl.BlockSpec((sd, d), lambda i: (0, 0)),
                  pl.BlockSpec((1, d), lambda i: (0, 0)),
                  pl.BlockSpec((1, d), lambda i: (0, 0)),
                  pl.BlockSpec(memory_space=pl.ANY)],
        out_specs=pl.BlockSpec((tc, d), lambda i: (i, 0)),
        out_shape=jax.ShapeDtypeStruct((t, d), F32),
        scratch_shapes=[pltpu.VMEM((TOP_K, tc, ys.shape[1]), ys.dtype), pltpu.SemaphoreType.DMA],
        compiler_params=_cparams(("arbitrary",)),
        name="combine",
    )(dest, h1, w_tk, ws_gu_bf, ws_down_bf, g.reshape(1, d), b.reshape(1, d), ys)


def _moe(h1, h1_packed, logits_t, router_bias, w_exp_gu, w_exp_down, w_shared_gu, w_shared_down,
         ln_g, ln_b, alpha):
    t, d = h1.shape
    n_exp = logits_t.shape[0]
    eidx, w_kt, rank, counts = _route(logits_t, router_bias, tt=min(512, t))

    counts = counts.reshape(n_exp)
    blocks_per_expert = (counts + MOE_BLOCK - 1) // MOE_BLOCK
    block_start = jnp.concatenate(
        [jnp.zeros((1,), jnp.int32), jnp.cumsum(blocks_per_expert).astype(jnp.int32)])
    pad_start = block_start[:n_exp] * MOE_BLOCK
    n_blocks = -(-(t * TOP_K + n_exp * (MOE_BLOCK - 1)) // MOE_BLOCK)

    dest = _dest(pad_start, eidx, rank, tt=min(2048, t))
    xs = _dispatch(dest, h1_packed, n_blocks * MOE_BLOCK, td=min(256, t))
    ys = _experts(block_start, counts, xs, w_exp_gu, w_exp_down)
    return _combine(dest, h1, w_kt.T, w_shared_gu.astype(BF16), w_shared_down.astype(BF16),
                    ln_g, ln_b, ys, alpha, tc=min(128, t))


def kernel(x, ln_in_g, ln_in_b, w_in, hg_lb_logits, hg_norm_g, gm_v_norm_g, gm_v_norm_b, gm_w_s, gm_b_s, gm_out_norm_g, w_out, ln1_g, ln1_b, w_router, router_bias, w_exp_gu, w_exp_down, w_shared_gu, w_shared_down, ln2_g, ln2_b):
    bsz, s, d = x.shape
    depth = w_in.shape[0]
    assert depth == 1, "the lower-bound table row used in the hgrn kernel assumes one layer"
    alpha = (2.0 * depth) ** 0.25
    t = bsz * s
    hg_width = hg_norm_g.shape[1]
    gm_width = gm_v_norm_g.shape[1]
    tm = min(512, t)

    h0, proj = _in_proj(x.reshape(t, d), ln_in_g, ln_in_b, w_in[0].astype(BF16), tm)
    proj3 = proj.reshape(bsz, s, proj.shape[1])
    y_hg = _hgrn(proj3, hg_lb_logits.astype(F32), hg_norm_g[0], sb=min(512, s))
    y_gm = _gmlp(proj3, gm_v_norm_g[0], gm_v_norm_b[0], gm_w_s[0], gm_b_s[0], gm_out_norm_g[0],
                 u_blk=4 * hg_width // gm_width, n_blocks=min(4, s // GM_BLOCK))
    h1, h1_packed, logits_t = _out_proj(y_hg.reshape(t, hg_width), y_gm.reshape(t, gm_width), h0,
                                        w_out[0].astype(BF16), ln1_g[0], ln1_b[0], w_router[0].T,
                                        alpha, tm)
    out = _moe(h1, h1_packed, logits_t, router_bias[0], w_exp_gu[0], w_exp_down[0], w_shared_gu[0],
               w_shared_down[0], ln2_g[0], ln2_b[0], alpha)
    return out.reshape(bsz, s, d)
```

```python
import functools

import jax
import jax.numpy as jnp
from jax import lax
from jax.experimental import pallas as pl
from jax.experimental.pallas import tpu as pltpu
from jax.experimental.pallas import tpu_sc as plsc

F32 = jnp.float32
BF16 = jnp.bfloat16

LN_EPS = 1e-5
RMS_EPS = 1e-6
CHUNK = 64
SUB = 16
HEAD_DIM = 128
GM_BLOCK = 128
TOP_K = 8
N_GROUPS = 8
TOPK_GROUPS = 4
ROUTED_SCALE = 2.5
MOE_BLOCK = 128
RING_DMA_PRIORITY = 1
DISPATCH_CHUNK = 64
GATHER_CHUNK = 16
VMEM_LIMIT = 48 * 1024 * 1024


def _cparams(sem):
    return pltpu.CompilerParams(dimension_semantics=sem, vmem_limit_bytes=VMEM_LIMIT)


def _layer_norm(x, g, b):
    mu = jnp.mean(x, axis=-1, keepdims=True)
    xc = x - mu
    var = jnp.mean(xc * xc, axis=-1, keepdims=True)
    return xc * lax.rsqrt(var + LN_EPS) * g + b


def _silu(x):
    return x * jax.nn.sigmoid(x)


def _gelu(x):
    return 0.5 * x * (1.0 + lax.erf(x * (2.0 ** -0.5)))


def _dot(a, b):
    return jnp.dot(a, b, preferred_element_type=F32)


def _dot_nt(a, b, precision=None):
    return lax.dot_general(a, b, (((1,), (1,)), ((), ())), preferred_element_type=F32,
                           precision=precision)


def _dot_tn(a, b):
    return lax.dot_general(a, b, (((0,), (0,)), ((), ())), preferred_element_type=F32)


def _pack_halves(x):
    n = x.shape[1] // 2
    hi = lax.bitcast_convert_type(x[:, :n].astype(BF16).astype(F32), jnp.uint32)
    lo = lax.bitcast_convert_type(x[:, n:].astype(BF16).astype(F32), jnp.uint32)
    return hi | (lo >> 16)


def _unpack_halves(w):
    a = lax.bitcast_convert_type(w & jnp.uint32(0xFFFF0000), F32)
    b = lax.bitcast_convert_type(w << 16, F32)
    return a, b


def _in_proj_kernel(x_ref, g_ref, b_ref, w_ref, h_ref, p_ref):
    h = _layer_norm(x_ref[...], g_ref[...], b_ref[...])
    h_ref[...] = h
    p_ref[...] = _dot(h.astype(BF16), w_ref[...])


def _in_proj(x2, g, b, w_bf, tm):
    t, d = x2.shape
    n = w_bf.shape[1]
    return pl.pallas_call(
        _in_proj_kernel,
        grid=(t // tm,),
        in_specs=[pl.BlockSpec((tm, d), lambda i: (i, 0)),
                  pl.BlockSpec((1, d), lambda i: (0, 0)),
                  pl.BlockSpec((1, d), lambda i: (0, 0)),
                  pl.BlockSpec((d, n), lambda i: (0, 0))],
        out_specs=[pl.BlockSpec((tm, d), lambda i: (i, 0)),
                   pl.BlockSpec((tm, n), lambda i: (i, 0))],
        out_shape=[jax.ShapeDtypeStruct((t, d), F32), jax.ShapeDtypeStruct((t, n), F32)],
        compiler_params=_cparams(("parallel",)),
        name="in_proj",
    )(x2, g.reshape(1, d), b.reshape(1, d), w_bf)


def _hgrn_kernel(q_ref, f_ref, i_ref, g_ref, lbl_ref, gn_ref, o_ref, st_ref, *, n_chunks):
    @pl.when(pl.program_id(1) == 0)
    def _():
        st_ref[...] = jnp.zeros_like(st_ref)

    lg = lbl_ref[...]
    ex = jnp.exp(lg - jnp.max(lg, axis=0, keepdims=True))
    lb_all = ex[0:1, :] / jnp.sum(ex, axis=0, keepdims=True)
    gn_all = gn_ref[...]

    c = CHUNK
    heads = st_ref.shape[0]
    row = lax.broadcasted_iota(jnp.int32, (c, c), 0)
    col = lax.broadcasted_iota(jnp.int32, (c, c), 1)
    tril = (row >= col).astype(BF16)
    sub_row = lax.broadcasted_iota(jnp.int32, (SUB, HEAD_DIM), 0)
    ones = jnp.ones((HEAD_DIM, HEAD_DIM), BF16)
    neg_inf = jnp.float32(-jnp.inf)

    def head_chunk(r0, h):
        lanes = pl.ds(h * HEAD_DIM, HEAD_DIM)
        lb = lb_all[:, h * HEAD_DIM:(h + 1) * HEAD_DIM]
        gn = gn_all[:, h * HEAD_DIM:(h + 1) * HEAD_DIM]
        q = _silu(q_ref[pl.ds(r0, c), lanes])
        f = lb + (1.0 - lb) * jax.nn.sigmoid(f_ref[pl.ds(r0, c), lanes])
        v = i_ref[pl.ds(r0, c), lanes]
        lf = jnp.log(f)
        k = 1.0 - f
        lf_hi = lf.astype(BF16)
        lf_lo = (lf - lf_hi.astype(F32)).astype(BF16)
        b = _dot(tril, lf_hi) + _dot(tril, lf_lo)
        b_last = b[c - 1:c, :]
        st = st_ref[h]
        o_inter = _dot_nt((q * jnp.exp(b)).astype(BF16), st.astype(BF16))
        k_dec = (k * jnp.exp(b_last - b)).astype(BF16)
        st_ref[h] = st * jnp.exp(b_last) + _dot_tn(v.astype(BF16), k_dec)

        v_bf = v.astype(BF16)
        outs = []
        for blk in range(c // SUB):
            lo = blk * SUB
            b_i = b[lo:lo + SUB, :]
            q_i = q[lo:lo + SUB, :]
            k_i = k[lo:lo + SUB, :]
            v_i = v[lo:lo + SUB, :]
            prods = []
            for j in range(SUB):
                diff = jnp.where(sub_row >= j, b_i - b_i[j:j + 1, :], neg_inf)
                prods.append(q_i * k_i[j:j + 1, :] * jnp.exp(diff))
            p_all = jnp.concatenate(prods, axis=0).astype(BF16)
            r_all = _dot(p_all, ones)
            o_blk = o_inter[lo:lo + SUB, :]
            for j in range(SUB):
                o_blk = o_blk + r_all[j * SUB:(j + 1) * SUB, :] * v_i[j:j + 1, :]
            if blk > 0:
                b_ref = b[lo - 1:lo, :]
                q_hat = (q_i * jnp.exp(b_i - b_ref)).astype(BF16)
                k_hat = (k[0:lo, :] * jnp.exp(b_ref - b[0:lo, :])).astype(BF16)
                scores = _dot_nt(q_hat, k_hat)
                o_blk = o_blk + _dot(scores.astype(BF16), v_bf[0:lo, :])
            outs.append(o_blk)
        o = jnp.concatenate(outs, axis=0)
        inv = lax.rsqrt(jnp.mean(o * o, axis=-1, keepdims=True) + RMS_EPS)
        o_ref[pl.ds(r0, c), lanes] = o * inv * gn * _silu(g_ref[pl.ds(r0, c), lanes])

    def chunk_body(ci, carry):
        r0 = pl.multiple_of(ci * c, c)
        for h in range(heads):
            head_chunk(r0, h)
        return carry

    lax.fori_loop(0, n_chunks, chunk_body, 0)


def _hgrn(proj3, lb_logits, norm_g, sb):
    bsz, s, _ = proj3.shape
    hg_width = norm_g.shape[0]
    heads = hg_width // HEAD_DIM

    def sect(k):
        return pl.BlockSpec((None, sb, hg_width), lambda b, j, k=k: (b, j, k))

    return pl.pallas_call(
        functools.partial(_hgrn_kernel, n_chunks=sb // CHUNK),
        grid=(bsz, s // sb),
        in_specs=[sect(0), sect(1), sect(2), sect(3),
                  pl.BlockSpec((lb_logits.shape[0], hg_width), lambda b, j: (0, 0)),
                  pl.BlockSpec((1, hg_width), lambda b, j: (0, 0))],
        out_specs=pl.BlockSpec((None, sb, hg_width), lambda b, j: (b, j, 0)),
        out_shape=jax.ShapeDtypeStruct((bsz, s, hg_width), F32),
        scratch_shapes=[pltpu.VMEM((heads, HEAD_DIM, HEAD_DIM), F32)],
        compiler_params=_cparams(("parallel", "arbitrary")),
        name="hgrn",
    )(proj3, proj3, proj3, proj3, lb_logits, norm_g.reshape(1, hg_width))


def _gmlp_kernel(u_ref, v_ref, vg_ref, vb_ref, ws_ref, bs_ref, og_ref, o_ref, *, n_blocks):
    groups = ws_ref.shape[0]
    gdim = u_ref.shape[-1] // groups
    row = lax.broadcasted_iota(jnp.int32, (GM_BLOCK, GM_BLOCK), 0) // CHUNK
    col = lax.broadcasted_iota(jnp.int32, (GM_BLOCK, GM_BLOCK), 1) // CHUNK
    allowed = row >= col
    ws = [jnp.where(allowed, ws_ref[g], 0.0).astype(BF16) for g in range(groups)]
    for n in range(n_blocks):
        rows = pl.ds(n * GM_BLOCK, GM_BLOCK)
        u = _gelu(u_ref[rows, :])
        v = _layer_norm(_gelu(v_ref[rows, :]), vg_ref[...], vb_ref[...])
        v_bf = v.astype(BF16)
        mixed = jnp.concatenate(
            [_dot(ws[g], v_bf[:, g * gdim:(g + 1) * gdim]) for g in range(groups)], axis=1)
        y = u * (mixed + bs_ref[...])
        inv = lax.rsqrt(jnp.mean(y * y, axis=-1, keepdims=True) + RMS_EPS)
        o_ref[rows, :] = y * inv * og_ref[...]


def _gmlp(proj3, v_norm_g, v_norm_b, w_s, b_s, out_norm_g, u_blk, n_blocks):
    bsz, s, _ = proj3.shape
    gw = v_norm_g.shape[0]
    groups = w_s.shape[0]
    rows = n_blocks * GM_BLOCK
    bias_full = jnp.repeat(b_s.T, gw // groups, axis=1)
    return pl.pallas_call(
        functools.partial(_gmlp_kernel, n_blocks=n_blocks),
        grid=(bsz, s // rows),
        in_specs=[pl.BlockSpec((None, rows, gw), lambda b, j: (b, j, u_blk)),
                  pl.BlockSpec((None, rows, gw), lambda b, j: (b, j, u_blk + 1)),
                  pl.BlockSpec((1, gw), lambda b, j: (0, 0)),
                  pl.BlockSpec((1, gw), lambda b, j: (0, 0)),
                  pl.BlockSpec((groups, GM_BLOCK, GM_BLOCK), lambda b, j: (0, 0, 0)),
                  pl.BlockSpec((GM_BLOCK, gw), lambda b, j: (0, 0)),
                  pl.BlockSpec((1, gw), lambda b, j: (0, 0))],
        out_specs=pl.BlockSpec((None, rows, gw), lambda b, j: (b, j, 0)),
        out_shape=jax.ShapeDtypeStruct((bsz, s, gw), F32),
        compiler_params=_cparams(("parallel", "parallel")),
        name="gmlp",
    )(proj3, proj3, v_norm_g.reshape(1, gw), v_norm_b.reshape(1, gw), w_s, bias_full,
      out_norm_g.reshape(1, gw))


def _out_proj_kernel(yh_ref, yg_ref, h_ref, wa_ref, wb_ref, g_ref, b_ref, wr_ref,
                     h1_ref, hp_ref, lg_ref, *, alpha):
    mix = _dot(yh_ref[...].astype(BF16), wa_ref[...]) + _dot(yg_ref[...].astype(BF16), wb_ref[...])
    h1 = _layer_norm(alpha * h_ref[...] + mix, g_ref[...], b_ref[...])
    h1_ref[...] = h1
    hp_ref[...] = _pack_halves(h1)
    lg_ref[...] = _dot_nt(wr_ref[...], h1, precision=lax.Precision.HIGHEST)


def _out_proj(y_hg, y_gm, h0, w_out_bf, g, b, w_router_t, alpha, tm):
    t, d = h0.shape
    hw = y_hg.shape[1]
    gw = y_gm.shape[1]
    e = w_router_t.shape[0]
    return pl.pallas_call(
        functools.partial(_out_proj_kernel, alpha=alpha),
        grid=(t // tm,),
        in_specs=[pl.BlockSpec((tm, hw), lambda i: (i, 0)),
                  pl.BlockSpec((tm, gw), lambda i: (i, 0)),
                  pl.BlockSpec((tm, d), lambda i: (i, 0)),
                  pl.BlockSpec((hw, d), lambda i: (0, 0)),
                  pl.BlockSpec((gw, d), lambda i: (0, 0)),
                  pl.BlockSpec((1, d), lambda i: (0, 0)),
                  pl.BlockSpec((1, d), lambda i: (0, 0)),
                  pl.BlockSpec((e, d), lambda i: (0, 0))],
        out_specs=[pl.BlockSpec((tm, d), lambda i: (i, 0)),
                   pl.BlockSpec((tm, d // 2), lambda i: (i, 0)),
                   pl.BlockSpec((e, tm), lambda i: (0, i))],
        out_shape=[jax.ShapeDtypeStruct((t, d), F32), jax.ShapeDtypeStruct((t, d // 2), jnp.uint32),
                   jax.ShapeDtypeStruct((e, t), F32)],
        compiler_params=_cparams(("parallel",)),
        name="out_proj",
    )(y_hg, y_gm, h0, w_out_bf[:hw], w_out_bf[hw:], g.reshape(1, d), b.reshape(1, d), w_router_t)


def _route_kernel(lg_ref, bias_ref, eidx_ref, w_ref, rank_ref, cnt_ref, carry_ref):
    @pl.when(pl.program_id(0) == 0)
    def _():
        carry_ref[...] = jnp.zeros_like(carry_ref)

    n_exp, tt = lg_ref.shape
    per_group = n_exp // N_GROUPS
    neg_inf = jnp.float32(-jnp.inf)
    scores = jax.nn.sigmoid(lg_ref[...])
    biased = scores + bias_ref[...]

    gio = lax.broadcasted_iota(jnp.int32, (per_group, tt), 0)
    blocks, gs_rows = [], []
    for g in range(N_GROUPS):
        blk = biased[g * per_group:(g + 1) * per_group, :]
        m1 = jnp.max(blk, axis=0, keepdims=True)
        first = jnp.min(jnp.where(blk == m1, gio, per_group), axis=0, keepdims=True)
        m2 = jnp.max(jnp.where(gio == first, neg_inf, blk), axis=0, keepdims=True)
        blocks.append(blk)
        gs_rows.append(m1 + m2)
    gs = jnp.concatenate(gs_rows, axis=0)
    gidx = lax.broadcasted_iota(jnp.int32, (N_GROUPS, tt), 0)
    beaten = jnp.zeros((N_GROUPS, tt), jnp.int32)
    for g in range(N_GROUPS):
        r = gs_rows[g]
        beaten = beaten + jnp.where((r > gs) | ((r == gs) & (g < gidx)), 1, 0)
    keep = beaten < TOPK_GROUPS
    cand = jnp.concatenate(
        [jnp.where(keep[g:g + 1, :], blocks[g], neg_inf) for g in range(N_GROUPS)], axis=0)

    eio = lax.broadcasted_iota(jnp.int32, (n_exp, tt), 0)
    idx_rows, w_rows = [], []
    chosen = jnp.zeros((n_exp, tt), F32)
    for _ in range(TOP_K):
        m = jnp.max(cand, axis=0, keepdims=True)
        idx = jnp.min(jnp.where(cand == m, eio, n_exp), axis=0, keepdims=True)
        sel = eio == idx
        w_rows.append(jnp.sum(jnp.where(sel, scores, 0.0), axis=0, keepdims=True))
        idx_rows.append(idx)
        cand = jnp.where(sel, neg_inf, cand)
        chosen = jnp.where(sel, 1.0, chosen)
    eidx = jnp.concatenate(idx_rows, axis=0)
    w = jnp.concatenate(w_rows, axis=0)
    eidx_ref[...] = eidx
    w_ref[...] = w / jnp.sum(w, axis=0, keepdims=True) * ROUTED_SCALE

    tr = lax.broadcasted_iota(jnp.int32, (tt, tt), 0)
    tc = lax.broadcasted_iota(jnp.int32, (tt, tt), 1)
    before = (tr < tc).astype(BF16)
    cum = _dot(chosen.astype(BF16), before) + carry_ref[...]
    rank_ref[...] = jnp.concatenate(
        [jnp.sum(jnp.where(eio == idx_rows[k], cum, 0.0), axis=0, keepdims=True)
         for k in range(TOP_K)], axis=0).astype(jnp.int32)
    total = carry_ref[...] + jnp.sum(chosen, axis=1, keepdims=True)
    carry_ref[...] = total
    cnt_ref[...] = total.astype(jnp.int32)


def _route(logits_t, router_bias, tt):
    e, t = logits_t.shape
    return pl.pallas_call(
        _route_kernel,
        grid=(t // tt,),
        in_specs=[pl.BlockSpec((e, tt), lambda i: (0, i)),
                  pl.BlockSpec((e, 1), lambda i: (0, 0))],
        out_specs=[pl.BlockSpec((TOP_K, tt), lambda i: (0, i)),
                   pl.BlockSpec((TOP_K, tt), lambda i: (0, i)),
                   pl.BlockSpec((TOP_K, tt), lambda i: (0, i)),
                   pl.BlockSpec((e, 1), lambda i: (0, 0))],
        out_shape=[jax.ShapeDtypeStruct((TOP_K, t), jnp.int32),
                   jax.ShapeDtypeStruct((TOP_K, t), F32),
                   jax.ShapeDtypeStruct((TOP_K, t), jnp.int32),
                   jax.ShapeDtypeStruct((e, 1), jnp.int32)],
        scratch_shapes=[pltpu.VMEM((e, 1), F32)],
        compiler_params=_cparams(("arbitrary",)),
        name="route",
    )(logits_t, router_bias.astype(F32).reshape(e, 1))


def _dest_kernel(ps_ref, eidx_ref, rank_ref, dest_ref, *, n_exp):
    eidx = eidx_ref[...]

    def body(e, acc):
        return acc + jnp.where(eidx == e, ps_ref[e], 0)

    dest_ref[...] = lax.fori_loop(0, n_exp, body, rank_ref[...])


def _dest(pad_start, eidx, rank, tt):
    k, t = eidx.shape
    n_exp = pad_start.shape[0]
    return pl.pallas_call(
        functools.partial(_dest_kernel, n_exp=n_exp),
        grid_spec=pltpu.PrefetchScalarGridSpec(
            num_scalar_prefetch=1,
            grid=(t // tt,),
            in_specs=[pl.BlockSpec((k, tt), lambda i, ps: (0, i)),
                      pl.BlockSpec((k, tt), lambda i, ps: (0, i))],
            out_specs=pl.BlockSpec((k, tt), lambda i, ps: (0, i))),
        out_shape=jax.ShapeDtypeStruct((k, t), jnp.int32),
        compiler_params=_cparams(("parallel",)),
        name="dest",
    )(pad_start, eidx, rank)


def _chunked(dest, chunk):
    k, t = dest.shape
    return dest.reshape(k, t // chunk, chunk).transpose(1, 0, 2)


def _sc_workers():
    info = plsc.get_sparse_core_info()
    return info.num_cores, info.num_cores * info.num_subcores


def _dispatch(dest, h1, n_rows, chunk):
    t, dh = h1.shape
    n_cores, n_workers = _sc_workers()
    per_worker = t // chunk // n_workers
    mesh = plsc.VectorSubcoreMesh(core_axis_name="c", subcore_axis_name="s")

    @functools.partial(
        pl.kernel, mesh=mesh,
        out_type=jax.ShapeDtypeStruct((n_rows, dh), h1.dtype),
        scratch_types=[pltpu.VMEM((TOP_K, chunk), jnp.int32),
                       pltpu.VMEM((chunk, dh), h1.dtype),
                       pltpu.SemaphoreType.DMA],
        name="dispatch",
    )
    def scatter_rows(x_hbm, dest_hbm, o_hbm, idx_v, rows_v, sem):
        wid = lax.axis_index("s") * n_cores + lax.axis_index("c")

        @pl.loop(0, per_worker)
        def _(j):
            c = wid * per_worker + j
            pltpu.sync_copy(dest_hbm.at[c], idx_v)
            pltpu.sync_copy(x_hbm.at[pl.ds(c * chunk, chunk)], rows_v)
            copies = [pltpu.async_copy(rows_v, o_hbm.at[idx_v.at[k]], sem) for k in range(TOP_K)]
            for cp in copies:
                cp.wait()

    return scatter_rows(h1, _chunked(dest, chunk))


def _gather_back(dest, ys, chunk):
    k_top, t = dest.shape
    dh = ys.shape[1]
    n_cores, n_workers = _sc_workers()
    per_worker = t // chunk // n_workers
    mesh = plsc.VectorSubcoreMesh(core_axis_name="c", subcore_axis_name="s")

    @functools.partial(
        pl.kernel, mesh=mesh,
        out_type=jax.ShapeDtypeStruct((k_top, t, dh), ys.dtype),
        scratch_types=[pltpu.VMEM((k_top, chunk), jnp.int32),
                       pltpu.VMEM((k_top, chunk, dh), ys.dtype),
                       pltpu.SemaphoreType.DMA],
        name="gather_back",
    )
    def gather_rows(ys_hbm, dest_hbm, o_hbm, idx_v, rows_v, sem):
        wid = lax.axis_index("s") * n_cores + lax.axis_index("c")

        @pl.loop(0, per_worker)
        def _(j):
            c = wid * per_worker + j
            pltpu.sync_copy(dest_hbm.at[c], idx_v)
            copies = [pltpu.async_copy(ys_hbm.at[idx_v.at[k]], rows_v.at[k], sem)
                      for k in range(k_top)]
            for cp in copies:
                cp.wait()
            for k in range(k_top):
                pltpu.sync_copy(rows_v.at[k], o_hbm.at[k, pl.ds(c * chunk, chunk)])

    return gather_rows(ys, _chunked(dest, chunk))


def _experts_kernel(bs_ref, cnt_ref, wgu_ref, wd_ref, xs_hbm, ys_hbm, gu_bf, d_bf, xbuf, ybuf,
                    xsem, ysem):
    e = pl.program_id(0)
    n_exp = pl.num_programs(0)
    mb = xbuf.shape[1]
    n_blocks = bs_ref[n_exp]
    b0 = bs_ref[e]
    b1 = bs_ref[e + 1]
    count = cnt_ref[e]

    def x_copy(b, slot):
        return pltpu.make_async_copy(xs_hbm.at[pl.ds(pl.multiple_of(b * mb, mb), mb)],
                                     xbuf.at[slot], xsem.at[slot])

    def y_copy(b, slot):
        return pltpu.make_async_copy(ybuf.at[slot],
                                     ys_hbm.at[pl.ds(pl.multiple_of(b * mb, mb), mb)], ysem.at[slot])

    @pl.when(e == 0)
    def _():
        x_copy(0, 0).start(priority=RING_DMA_PRIORITY)

    @pl.when(b1 > b0)
    def _():
        gu_bf[...] = wgu_ref[...].astype(BF16)
        d_bf[...] = wd_ref[...].astype(BF16)

    kh = gu_bf.shape[0] // 2
    half = gu_bf.shape[1] // 2

    def block(b, carry):
        slot = lax.rem(b, 2)
        x_copy(b, slot).wait()

        @pl.when(b + 1 < n_blocks)
        def _():
            x_copy(b + 1, 1 - slot).start(priority=RING_DMA_PRIORITY)

        @pl.when(b >= 2)
        def _():
            y_copy(b - 2, slot).wait()

        rows = lax.broadcasted_iota(jnp.int32, (mb, xbuf.shape[2]), 0)
        valid = count - (b - b0) * mb
        words = jnp.where(rows < valid, xbuf[slot], jnp.uint32(0))
        xa, xb = _unpack_halves(words)
        gu = _dot(xa.astype(BF16), gu_bf[0:kh, :]) + _dot(xb.astype(BF16), gu_bf[kh:, :])
        hid = _silu(gu[:, :half]) * gu[:, half:]
        ybuf[slot] = _pack_halves(_dot(hid.astype(BF16), d_bf[...]))
        y_copy(b, slot).start(priority=RING_DMA_PRIORITY)
        return carry

    lax.fori_loop(b0, b1, block, 0)

    @pl.when(e == n_exp - 1)
    def _():
        @pl.when(n_blocks >= 2)
        def _():
            y_copy(n_blocks - 2, lax.rem(n_blocks, 2)).wait()

        y_copy(n_blocks - 1, lax.rem(n_blocks - 1, 2)).wait()


def _experts(block_start, counts, xs, w_gu, w_down):
    n_rows, dh = xs.shape
    n_exp, d, gu_w = w_gu.shape
    ed = w_down.shape[1]
    return pl.pallas_call(
        _experts_kernel,
        grid_spec=pltpu.PrefetchScalarGridSpec(
            num_scalar_prefetch=2,
            grid=(n_exp,),
            in_specs=[pl.BlockSpec((None, d, gu_w), lambda e, bs, cnt: (e, 0, 0)),
                      pl.BlockSpec((None, ed, d), lambda e, bs, cnt: (e, 0, 0)),
                      pl.BlockSpec(memory_space=pl.ANY)],
            out_specs=pl.BlockSpec(memory_space=pl.ANY),
            scratch_shapes=[pltpu.VMEM((d, gu_w), BF16), pltpu.VMEM((ed, d), BF16),
                            pltpu.VMEM((2, MOE_BLOCK, dh), jnp.uint32),
                            pltpu.VMEM((2, MOE_BLOCK, dh), jnp.uint32),
                            pltpu.SemaphoreType.DMA((2,)), pltpu.SemaphoreType.DMA((2,))]),
        out_shape=jax.ShapeDtypeStruct((n_rows, dh), jnp.uint32),
        compiler_params=_cparams(("arbitrary",)),
        name="experts",
    )(block_start, counts, w_gu, w_down, xs)


def _combine_kernel(h_ref, w_ref, wsg_ref, wsd_ref, g_ref, b_ref, yg_ref, o_ref, *, alpha):
    h = h_ref[...]
    gu = _dot(h.astype(BF16), wsg_ref[...])
    half = gu.shape[1] // 2
    shared = _dot((_silu(gu[:, :half]) * gu[:, half:]).astype(BF16), wsd_ref[...])
    acc = alpha * h + shared
    dh = acc.shape[1] // 2
    acc_a = acc[:, :dh]
    acc_b = acc[:, dh:]
    w = w_ref[...]
    for k in range(TOP_K):
        ya, yb = _unpack_halves(yg_ref[k])
        acc_a = acc_a + ya * w[:, k:k + 1]
        acc_b = acc_b + yb * w[:, k:k + 1]
    o_ref[...] = _layer_norm(jnp.concatenate([acc_a, acc_b], axis=1), g_ref[...], b_ref[...])


def _combine(h1, w_tk, ws_gu_bf, ws_down_bf, g, b, yg, alpha, tc):
    t, d = h1.shape
    sg = ws_gu_bf.shape[1]
    sd = ws_down_bf.shape[0]
    return pl.pallas_call(
        functools.partial(_combine_kernel, alpha=alpha),
        grid=(t // tc,),
        in_specs=[pl.BlockSpec((tc, d), lambda i: (i, 0)),
                  pl.BlockSpec((tc, TOP_K), lambda i: (i, 0)),
                  pl.BlockSpec((d, sg), lambda i: (0, 0)),
                  pl.BlockSpec((sd, d), lambda i: (0, 0)),
                  pl.BlockSpec((1, d), lambda i: (0, 0)),
                  pl.BlockSpec((1, d), lambda i: (0, 0)),
                  pl.BlockSpec((TOP_K, tc, yg.shape[2]), lambda i: (0, i, 0))],
        out_specs=pl.BlockSpec((tc, d), lambda i: (i, 0)),
        out_shape=jax.ShapeDtypeStruct((t, d), F32),
        compiler_params=_cparams(("parallel",)),
        name="combine",
    )(h1, w_tk, ws_gu_bf, ws_down_bf, g.reshape(1, d), b.reshape(1, d), yg)


def _moe(h1, h1_packed, logits_t, router_bias, w_exp_gu, w_exp_down, w_shared_gu, w_shared_down,
         ln_g, ln_b, alpha):
    t, d = h1.shape
    n_exp = logits_t.shape[0]
    eidx, w_kt, rank, counts = _route(logits_t, router_bias, tt=min(512, t))

    counts = counts.reshape(n_exp)
    blocks_per_expert = (counts + MOE_BLOCK - 1) // MOE_BLOCK
    block_start = jnp.concatenate(
        [jnp.zeros((1,), jnp.int32), jnp.cumsum(blocks_per_expert).astype(jnp.int32)])
    pad_start = block_start[:n_exp] * MOE_BLOCK
    n_blocks = -(-(t * TOP_K + n_exp * (MOE_BLOCK - 1)) // MOE_BLOCK)

    dest = _dest(pad_start, eidx, rank, tt=min(2048, t))
    xs = _dispatch(dest, h1_packed, n_blocks * MOE_BLOCK, chunk=DISPATCH_CHUNK)
    ys = _experts(block_start, counts, xs, w_exp_gu, w_exp_down)
    yg = _gather_back(dest, ys, chunk=GATHER_CHUNK)
    return _combine(h1, w_kt.T, w_shared_gu.astype(BF16), w_shared_down.astype(BF16),
                    ln_g, ln_b, yg, alpha, tc=min(256, t))


def kernel(x, ln_in_g, ln_in_b, w_in, hg_lb_logits, hg_norm_g, gm_v_norm_g, gm_v_norm_b, gm_w_s, gm_b_s, gm_out_norm_g, w_out, ln1_g, ln1_b, w_router, router_bias, w_exp_gu, w_exp_down, w_shared_gu, w_shared_down, ln2_g, ln2_b):
    bsz, s, d = x.shape
    depth = w_in.shape[0]
    assert depth == 1, "the lower-bound table row used in the hgrn kernel assumes one layer"
    alpha = (2.0 * depth) ** 0.25
    t = bsz * s
    hg_width = hg_norm_g.shape[1]
    gm_width = gm_v_norm_g.shape[1]
    tm = min(512, t)

    h0, proj = _in_proj(x.reshape(t, d), ln_in_g, ln_in_b, w_in[0].astype(BF16), tm)
    proj3 = proj.reshape(bsz, s, proj.shape[1])
    y_hg = _hgrn(proj3, hg_lb_logits.astype(F32), hg_norm_g[0], sb=min(512, s))
    y_gm = _gmlp(proj3, gm_v_norm_g[0], gm_v_norm_b[0], gm_w_s[0], gm_b_s[0], gm_out_norm_g[0],
                 u_blk=4 * hg_width // gm_width, n_blocks=min(4, s // GM_BLOCK))
    h1, h1_packed, logits_t = _out_proj(y_hg.reshape(t, hg_width), y_gm.reshape(t, gm_width), h0,
                                        w_out[0].astype(BF16), ln1_g[0], ln1_b[0], w_router[0].T,
                                        alpha, tm)
    out = _moe(h1, h1_packed, logits_t, router_bias[0], w_exp_gu[0], w_exp_down[0], w_shared_gu[0],
               w_shared_down[0], ln2_g[0], ln2_b[0], alpha)
    return out.reshape(bsz, s, d)
```

```python
import functools

import jax
import jax.numpy as jnp
from jax import lax
from jax.experimental import pallas as pl
from jax.experimental.pallas import tpu as pltpu
from jax.experimental.pallas import tpu_sc as plsc

F32 = jnp.float32
BF16 = jnp.bfloat16

LN_EPS = 1e-5
RMS_EPS = 1e-6
CHUNK = 64
SUB = 16
MAX_FACTORED_EXPONENT = 80.0
FACTORED_MIN_LB = float(2.718281828459045 ** (-MAX_FACTORED_EXPONENT / CHUNK))
HEAD_DIM = 128
GM_BLOCK = 128
TOP_K = 8
N_GROUPS = 8
TOPK_GROUPS = 4
ROUTED_SCALE = 2.5
MOE_BLOCK = 128
RING_SLOTS = 4
RING_DMA_PRIORITY = 1
DISPATCH_CHUNK = 64
GATHER_CHUNK = 16
VMEM_LIMIT = 48 * 1024 * 1024


def _cparams(sem):
    return pltpu.CompilerParams(dimension_semantics=sem, vmem_limit_bytes=VMEM_LIMIT)


def _layer_norm(x, g, b):
    mu = jnp.mean(x, axis=-1, keepdims=True)
    xc = x - mu
    var = jnp.mean(xc * xc, axis=-1, keepdims=True)
    return xc * lax.rsqrt(var + LN_EPS) * g + b


def _silu(x):
    return x * jax.nn.sigmoid(x)


def _gelu(x):
    return 0.5 * x * (1.0 + lax.erf(x * (2.0 ** -0.5)))


def _dot(a, b):
    return jnp.dot(a, b, preferred_element_type=F32)


def _dot_nt(a, b, precision=None):
    return lax.dot_general(a, b, (((1,), (1,)), ((), ())), preferred_element_type=F32,
                           precision=precision)


def _dot_tn(a, b):
    return lax.dot_general(a, b, (((0,), (0,)), ((), ())), preferred_element_type=F32)


def _pack_halves(x):
    n = x.shape[1] // 2
    hi = lax.bitcast_convert_type(x[:, :n].astype(BF16).astype(F32), jnp.uint32)
    lo = lax.bitcast_convert_type(x[:, n:].astype(BF16).astype(F32), jnp.uint32)
    return hi | (lo >> 16)


def _unpack_halves(w):
    a = lax.bitcast_convert_type(w & jnp.uint32(0xFFFF0000), F32)
    b = lax.bitcast_convert_type(w << 16, F32)
    return a, b


def _in_proj_kernel(x_ref, g_ref, b_ref, w_ref, h_ref, p_ref):
    h = _layer_norm(x_ref[...], g_ref[...], b_ref[...])
    h_ref[...] = h
    p_ref[...] = _dot(h.astype(BF16), w_ref[...])


def _in_proj(x2, g, b, w_bf, tm):
    t, d = x2.shape
    n = w_bf.shape[1]
    return pl.pallas_call(
        _in_proj_kernel,
        grid=(t // tm,),
        in_specs=[pl.BlockSpec((tm, d), lambda i: (i, 0)),
                  pl.BlockSpec((1, d), lambda i: (0, 0)),
                  pl.BlockSpec((1, d), lambda i: (0, 0)),
                  pl.BlockSpec((d, n), lambda i: (0, 0))],
        out_specs=[pl.BlockSpec((tm, d), lambda i: (i, 0)),
                   pl.BlockSpec((tm, n), lambda i: (i, 0))],
        out_shape=[jax.ShapeDtypeStruct((t, d), F32), jax.ShapeDtypeStruct((t, n), F32)],
        compiler_params=_cparams(("parallel",)),
        name="in_proj",
    )(x2, g.reshape(1, d), b.reshape(1, d), w_bf)


def _hgrn_kernel(q_ref, f_ref, i_ref, g_ref, lbl_ref, gn_ref, o_ref, st_ref, *, n_chunks):
    @pl.when(pl.program_id(1) == 0)
    def _():
        st_ref[...] = jnp.zeros_like(st_ref)

    lg = lbl_ref[...]
    ex = jnp.exp(lg - jnp.max(lg, axis=0, keepdims=True))
    lb_all = ex[0:1, :] / jnp.sum(ex, axis=0, keepdims=True)
    gn_all = gn_ref[...]

    c = CHUNK
    heads = st_ref.shape[0]
    row = lax.broadcasted_iota(jnp.int32, (c, c), 0)
    col = lax.broadcasted_iota(jnp.int32, (c, c), 1)
    tril = (row >= col).astype(BF16)
    sub_row = lax.broadcasted_iota(jnp.int32, (SUB, HEAD_DIM), 0)
    ones = jnp.ones((HEAD_DIM, HEAD_DIM), BF16)
    neg_inf = jnp.float32(-jnp.inf)

    def out_factored(q_dec, k, v_bf, b, st_bf):
        k_grow = (k * jnp.exp(-b)).astype(BF16)
        both = _dot_nt(q_dec, jnp.concatenate([st_bf, k_grow], axis=0))
        scores = jnp.where(row >= col, both[:, HEAD_DIM:], 0.0)
        return both[:, :HEAD_DIM] + _dot(scores.astype(BF16), v_bf)

    def intra_pairwise(q, k, v, v_bf, b, o_inter):
        outs = []
        for blk in range(c // SUB):
            lo = blk * SUB
            b_i = b[lo:lo + SUB, :]
            q_i = q[lo:lo + SUB, :]
            k_i = k[lo:lo + SUB, :]
            v_i = v[lo:lo + SUB, :]
            prods = []
            for j in range(SUB):
                diff = jnp.where(sub_row >= j, b_i - b_i[j:j + 1, :], neg_inf)
                prods.append(q_i * k_i[j:j + 1, :] * jnp.exp(diff))
            p_all = jnp.concatenate(prods, axis=0).astype(BF16)
            r_all = _dot(p_all, ones)
            o_blk = o_inter[lo:lo + SUB, :]
            for j in range(SUB):
                o_blk = o_blk + r_all[j * SUB:(j + 1) * SUB, :] * v_i[j:j + 1, :]
            if blk > 0:
                b_ref = b[lo - 1:lo, :]
                q_hat = (q_i * jnp.exp(b_i - b_ref)).astype(BF16)
                k_hat = (k[0:lo, :] * jnp.exp(b_ref - b[0:lo, :])).astype(BF16)
                scores = _dot_nt(q_hat, k_hat)
                o_blk = o_blk + _dot(scores.astype(BF16), v_bf[0:lo, :])
            outs.append(o_blk)
        return jnp.concatenate(outs, axis=0)

    def one_chunk(r0, factored):
        rows = pl.ds(r0, c)
        hs = [slice(h * HEAD_DIM, (h + 1) * HEAD_DIM) for h in range(heads)]
        q = _silu(q_ref[rows, :])
        f = lb_all + (1.0 - lb_all) * jax.nn.sigmoid(f_ref[rows, :])
        v = i_ref[rows, :]
        v_bf = v.astype(BF16)
        lf = jnp.log(f)
        k = 1.0 - f
        lf_hi = lf.astype(BF16)
        lf_lo = (lf - lf_hi.astype(F32)).astype(BF16)
        b2 = _dot(tril, jnp.concatenate([lf_hi, lf_lo], axis=1))
        width = lf.shape[1]
        b = b2[:, :width] + b2[:, width:]
        b_last = b[c - 1:c, :]
        q_dec = (q * jnp.exp(b)).astype(BF16)
        k_dec = (k * jnp.exp(b_last - b)).astype(BF16)
        decay = jnp.exp(b_last)
        st = [st_ref[h] for h in range(heads)]
        st_bf = [s.astype(BF16) for s in st]
        if factored:
            k_grow = (k * jnp.exp(-b)).astype(BF16)
            both = [_dot_nt(q_dec[:, hs[h]], jnp.concatenate([st_bf[h], k_grow[:, hs[h]]], axis=0))
                    for h in range(heads)]
            scores = [jnp.where(row >= col, both[h][:, HEAD_DIM:], 0.0).astype(BF16)
                      for h in range(heads)]
            outs = [both[h][:, :HEAD_DIM] + _dot(scores[h], v_bf[:, hs[h]]) for h in range(heads)]
        else:
            outs = [intra_pairwise(q[:, hs[h]], k[:, hs[h]], v[:, hs[h]], v_bf[:, hs[h]],
                                   b[:, hs[h]], _dot_nt(q_dec[:, hs[h]], st_bf[h]))
                    for h in range(heads)]
        for h in range(heads):
            st_ref[h] = st[h] * decay[:, hs[h]] + _dot_tn(v_bf[:, hs[h]], k_dec[:, hs[h]])
        inv = [lax.rsqrt(jnp.mean(o * o, axis=-1, keepdims=True) + RMS_EPS) for o in outs]
        o = jnp.concatenate([outs[h] * inv[h] for h in range(heads)], axis=1)
        o_ref[rows, :] = o * gn_all * _silu(g_ref[rows, :])

    def run_chunks(factored):
        per_trip = 2 if (factored and n_chunks % 2 == 0) else 1

        def chunk_body(ci, carry):
            for u in range(per_trip):
                one_chunk(pl.multiple_of((ci * per_trip + u) * c, c), factored)
            return carry

        lax.fori_loop(0, n_chunks // per_trip, chunk_body, 0)

    bounded = jnp.min(lb_all) >= FACTORED_MIN_LB
    pl.when(bounded)(lambda: run_chunks(True))
    pl.when(jnp.logical_not(bounded))(lambda: run_chunks(False))


def _hgrn(proj3, lb_logits, norm_g, sb):
    bsz, s, _ = proj3.shape
    hg_width = norm_g.shape[0]
    heads = hg_width // HEAD_DIM

    def sect(k):
        return pl.BlockSpec((None, sb, hg_width), lambda b, j, k=k: (b, j, k))

    return pl.pallas_call(
        functools.partial(_hgrn_kernel, n_chunks=sb // CHUNK),
        grid=(bsz, s // sb),
        in_specs=[sect(0), sect(1), sect(2), sect(3),
                  pl.BlockSpec((lb_logits.shape[0], hg_width), lambda b, j: (0, 0)),
                  pl.BlockSpec((1, hg_width), lambda b, j: (0, 0))],
        out_specs=pl.BlockSpec((None, sb, hg_width), lambda b, j: (b, j, 0)),
        out_shape=jax.ShapeDtypeStruct((bsz, s, hg_width), F32),
        scratch_shapes=[pltpu.VMEM((heads, HEAD_DIM, HEAD_DIM), F32)],
        compiler_params=_cparams(("parallel", "arbitrary")),
        name="hgrn",
    )(proj3, proj3, proj3, proj3, lb_logits, norm_g.reshape(1, hg_width))


def _gmlp_kernel(u_ref, v_ref, vg_ref, vb_ref, ws_ref, bs_ref, og_ref, o_ref, *, n_blocks):
    groups = ws_ref.shape[0]
    gdim = u_ref.shape[-1] // groups
    row = lax.broadcasted_iota(jnp.int32, (GM_BLOCK, GM_BLOCK), 0) // CHUNK
    col = lax.broadcasted_iota(jnp.int32, (GM_BLOCK, GM_BLOCK), 1) // CHUNK
    allowed = row >= col
    ws = [jnp.where(allowed, ws_ref[g], 0.0).astype(BF16) for g in range(groups)]
    for n in range(n_blocks):
        rows = pl.ds(n * GM_BLOCK, GM_BLOCK)
        u = _gelu(u_ref[rows, :])
        v = _layer_norm(_gelu(v_ref[rows, :]), vg_ref[...], vb_ref[...])
        v_bf = v.astype(BF16)
        mixed = jnp.concatenate(
            [_dot(ws[g], v_bf[:, g * gdim:(g + 1) * gdim]) for g in range(groups)], axis=1)
        y = u * (mixed + bs_ref[...])
        inv = lax.rsqrt(jnp.mean(y * y, axis=-1, keepdims=True) + RMS_EPS)
        o_ref[rows, :] = y * inv * og_ref[...]


def _gmlp(proj3, v_norm_g, v_norm_b, w_s, b_s, out_norm_g, u_blk, n_blocks):
    bsz, s, _ = proj3.shape
    gw = v_norm_g.shape[0]
    groups = w_s.shape[0]
    rows = n_blocks * GM_BLOCK
    bias_full = jnp.repeat(b_s.T, gw // groups, axis=1)
    return pl.pallas_call(
        functools.partial(_gmlp_kernel, n_blocks=n_blocks),
        grid=(bsz, s // rows),
        in_specs=[pl.BlockSpec((None, rows, gw), lambda b, j: (b, j, u_blk)),
                  pl.BlockSpec((None, rows, gw), lambda b, j: (b, j, u_blk + 1)),
                  pl.BlockSpec((1, gw), lambda b, j: (0, 0)),
                  pl.BlockSpec((1, gw), lambda b, j: (0, 0)),
                  pl.BlockSpec((groups, GM_BLOCK, GM_BLOCK), lambda b, j: (0, 0, 0)),
                  pl.BlockSpec((GM_BLOCK, gw), lambda b, j: (0, 0)),
                  pl.BlockSpec((1, gw), lambda b, j: (0, 0))],
        out_specs=pl.BlockSpec((None, rows, gw), lambda b, j: (b, j, 0)),
        out_shape=jax.ShapeDtypeStruct((bsz, s, gw), F32),
        compiler_params=_cparams(("parallel", "parallel")),
        name="gmlp",
    )(proj3, proj3, v_norm_g.reshape(1, gw), v_norm_b.reshape(1, gw), w_s, bias_full,
      out_norm_g.reshape(1, gw))


def _out_proj_kernel(yh_ref, yg_ref, h_ref, wa_ref, wb_ref, g_ref, b_ref, wr_ref,
                     h1_ref, hp_ref, lg_ref, *, alpha):
    mix = _dot(yh_ref[...].astype(BF16), wa_ref[...]) + _dot(yg_ref[...].astype(BF16), wb_ref[...])
    h1 = _layer_norm(alpha * h_ref[...] + mix, g_ref[...], b_ref[...])
    h1_ref[...] = h1
    hp_ref[...] = _pack_halves(h1)
    lg_ref[...] = _dot_nt(wr_ref[...], h1, precision=lax.Precision.HIGHEST)


def _out_proj(y_hg, y_gm, h0, w_out_bf, g, b, w_router_t, alpha, tm):
    t, d = h0.shape
    hw = y_hg.shape[1]
    gw = y_gm.shape[1]
    e = w_router_t.shape[0]
    return pl.pallas_call(
        functools.partial(_out_proj_kernel, alpha=alpha),
        grid=(t // tm,),
        in_specs=[pl.BlockSpec((tm, hw), lambda i: (i, 0)),
                  pl.BlockSpec((tm, gw), lambda i: (i, 0)),
                  pl.BlockSpec((tm, d), lambda i: (i, 0)),
                  pl.BlockSpec((hw, d), lambda i: (0, 0)),
                  pl.BlockSpec((gw, d), lambda i: (0, 0)),
                  pl.BlockSpec((1, d), lambda i: (0, 0)),
                  pl.BlockSpec((1, d), lambda i: (0, 0)),
                  pl.BlockSpec((e, d), lambda i: (0, 0))],
        out_specs=[pl.BlockSpec((tm, d), lambda i: (i, 0)),
                   pl.BlockSpec((tm, d // 2), lambda i: (i, 0)),
                   pl.BlockSpec((e, tm), lambda i: (0, i))],
        out_shape=[jax.ShapeDtypeStruct((t, d), F32), jax.ShapeDtypeStruct((t, d // 2), jnp.uint32),
                   jax.ShapeDtypeStruct((e, t), F32)],
        compiler_params=_cparams(("parallel",)),
        name="out_proj",
    )(y_hg, y_gm, h0, w_out_bf[:hw], w_out_bf[hw:], g.reshape(1, d), b.reshape(1, d), w_router_t)


def _route_kernel(lg_ref, bias_ref, eidx_ref, w_ref, rank_ref, cnt_ref, carry_ref):
    @pl.when(pl.program_id(0) == 0)
    def _():
        carry_ref[...] = jnp.zeros_like(carry_ref)

    n_exp, tt = lg_ref.shape
    per_group = n_exp // N_GROUPS
    neg_inf = jnp.float32(-jnp.inf)
    scores = jax.nn.sigmoid(lg_ref[...])
    biased = scores + bias_ref[...]

    gio = lax.broadcasted_iota(jnp.int32, (per_group, tt), 0)
    blocks, gs_rows = [], []
    for g in range(N_GROUPS):
        blk = biased[g * per_group:(g + 1) * per_group, :]
        m1 = jnp.max(blk, axis=0, keepdims=True)
        first = jnp.min(jnp.where(blk == m1, gio, per_group), axis=0, keepdims=True)
        m2 = jnp.max(jnp.where(gio == first, neg_inf, blk), axis=0, keepdims=True)
        blocks.append(blk)
        gs_rows.append(m1 + m2)
    gs = jnp.concatenate(gs_rows, axis=0)
    gidx = lax.broadcasted_iota(jnp.int32, (N_GROUPS, tt), 0)
    beaten = jnp.zeros((N_GROUPS, tt), jnp.int32)
    for g in range(N_GROUPS):
        r = gs_rows[g]
        beaten = beaten + jnp.where((r > gs) | ((r == gs) & (g < gidx)), 1, 0)
    keep = beaten < TOPK_GROUPS
    cand = jnp.concatenate(
        [jnp.where(keep[g:g + 1, :], blocks[g], neg_inf) for g in range(N_GROUPS)], axis=0)

    eio = lax.broadcasted_iota(jnp.int32, (n_exp, tt), 0)
    idx_rows, w_rows = [], []
    chosen = jnp.zeros((n_exp, tt), F32)
    for _ in range(TOP_K):
        m = jnp.max(cand, axis=0, keepdims=True)
        idx = jnp.min(jnp.where(cand == m, eio, n_exp), axis=0, keepdims=True)
        sel = eio == idx
        w_rows.append(jnp.sum(jnp.where(sel, scores, 0.0), axis=0, keepdims=True))
        idx_rows.append(idx)
        cand = jnp.where(sel, neg_inf, cand)
        chosen = jnp.where(sel, 1.0, chosen)
    eidx = jnp.concatenate(idx_rows, axis=0)
    w = jnp.concatenate(w_rows, axis=0)
    eidx_ref[...] = eidx
    w_ref[...] = w / jnp.sum(w, axis=0, keepdims=True) * ROUTED_SCALE

    tr = lax.broadcasted_iota(jnp.int32, (tt, tt), 0)
    tc = lax.broadcasted_iota(jnp.int32, (tt, tt), 1)
    before = (tr < tc).astype(BF16)
    cum = _dot(chosen.astype(BF16), before) + carry_ref[...]
    rank_ref[...] = jnp.concatenate(
        [jnp.sum(jnp.where(eio == idx_rows[k], cum, 0.0), axis=0, keepdims=True)
         for k in range(TOP_K)], axis=0).astype(jnp.int32)
    total = carry_ref[...] + jnp.sum(chosen, axis=1, keepdims=True)
    carry_ref[...] = total
    cnt_ref[...] = total.astype(jnp.int32)


def _route(logits_t, router_bias, tt):
    e, t = logits_t.shape
    return pl.pallas_call(
        _route_kernel,
        grid=(t // tt,),
        in_specs=[pl.BlockSpec((e, tt), lambda i: (0, i)),
                  pl.BlockSpec((e, 1), lambda i: (0, 0))],
        out_specs=[pl.BlockSpec((TOP_K, tt), lambda i: (0, i)),
                   pl.BlockSpec((TOP_K, tt), lambda i: (0, i)),
                   pl.BlockSpec((TOP_K, tt), lambda i: (0, i)),
                   pl.BlockSpec((e, 1), lambda i: (0, 0))],
        out_shape=[jax.ShapeDtypeStruct((TOP_K, t), jnp.int32),
                   jax.ShapeDtypeStruct((TOP_K, t), F32),
                   jax.ShapeDtypeStruct((TOP_K, t), jnp.int32),
                   jax.ShapeDtypeStruct((e, 1), jnp.int32)],
        scratch_shapes=[pltpu.VMEM((e, 1), F32)],
        compiler_params=_cparams(("arbitrary",)),
        name="route",
    )(logits_t, router_bias.astype(F32).reshape(e, 1))


def _dest_kernel(ps_ref, eidx_ref, rank_ref, dest_ref, *, n_exp):
    eidx = eidx_ref[...]

    def body(e, acc):
        return acc + jnp.where(eidx == e, ps_ref[e], 0)

    dest_ref[...] = lax.fori_loop(0, n_exp, body, rank_ref[...])


def _dest(pad_start, eidx, rank, tt):
    k, t = eidx.shape
    n_exp = pad_start.shape[0]
    return pl.pallas_call(
        functools.partial(_dest_kernel, n_exp=n_exp),
        grid_spec=pltpu.PrefetchScalarGridSpec(
            num_scalar_prefetch=1,
            grid=(t // tt,),
            in_specs=[pl.BlockSpec((k, tt), lambda i, ps: (0, i)),
                      pl.BlockSpec((k, tt), lambda i, ps: (0, i))],
            out_specs=pl.BlockSpec((k, tt), lambda i, ps: (0, i))),
        out_shape=jax.ShapeDtypeStruct((k, t), jnp.int32),
        compiler_params=_cparams(("parallel",)),
        name="dest",
    )(pad_start, eidx, rank)


def _chunked(dest, chunk):
    k, t = dest.shape
    return dest.reshape(k, t // chunk, chunk).transpose(1, 0, 2)


def _sc_workers():
    info = plsc.get_sparse_core_info()
    return info.num_cores, info.num_cores * info.num_subcores


def _dispatch(dest, h1, n_rows, chunk):
    t, dh = h1.shape
    n_cores, n_workers = _sc_workers()
    per_worker = t // chunk // n_workers
    mesh = plsc.VectorSubcoreMesh(core_axis_name="c", subcore_axis_name="s")

    @functools.partial(
        pl.kernel, mesh=mesh,
        out_type=jax.ShapeDtypeStruct((n_rows, dh), h1.dtype),
        scratch_types=[pltpu.VMEM((TOP_K, chunk), jnp.int32),
                       pltpu.VMEM((chunk, dh), h1.dtype),
                       pltpu.SemaphoreType.DMA],
        name="dispatch",
    )
    def scatter_rows(x_hbm, dest_hbm, o_hbm, idx_v, rows_v, sem):
        wid = lax.axis_index("s") * n_cores + lax.axis_index("c")

        @pl.loop(0, per_worker)
        def _(j):
            c = wid * per_worker + j
            pltpu.sync_copy(dest_hbm.at[c], idx_v)
            pltpu.sync_copy(x_hbm.at[pl.ds(c * chunk, chunk)], rows_v)
            copies = [pltpu.async_copy(rows_v, o_hbm.at[idx_v.at[k]], sem) for k in range(TOP_K)]
            for cp in copies:
                cp.wait()

    return scatter_rows(h1, _chunked(dest, chunk))


def _gather_back(dest, ys, chunk):
    k_top, t = dest.shape
    dh = ys.shape[1]
    n_cores, n_workers = _sc_workers()
    per_worker = t // chunk // n_workers
    mesh = plsc.VectorSubcoreMesh(core_axis_name="c", subcore_axis_name="s")

    @functools.partial(
        pl.kernel, mesh=mesh,
        out_type=jax.ShapeDtypeStruct((k_top, t, dh), ys.dtype),
        scratch_types=[pltpu.VMEM((k_top, chunk), jnp.int32),
                       pltpu.VMEM((k_top, chunk, dh), ys.dtype),
                       pltpu.SemaphoreType.DMA],
        name="gather_back",
    )
    def gather_rows(ys_hbm, dest_hbm, o_hbm, idx_v, rows_v, sem):
        wid = lax.axis_index("s") * n_cores + lax.axis_index("c")

        @pl.loop(0, per_worker)
        def _(j):
            c = wid * per_worker + j
            pltpu.sync_copy(dest_hbm.at[c], idx_v)
            copies = [pltpu.async_copy(ys_hbm.at[idx_v.at[k]], rows_v.at[k], sem)
                      for k in range(k_top)]
            for cp in copies:
                cp.wait()
            for k in range(k_top):
                pltpu.sync_copy(rows_v.at[k], o_hbm.at[k, pl.ds(c * chunk, chunk)])

    return gather_rows(ys, _chunked(dest, chunk))


def _experts_kernel(bs_ref, cnt_ref, wgu_ref, wd_ref, xs_hbm, ys_hbm, gu_bf, d_bf, xbuf, ybuf,
                    xsem, ysem):
    e = pl.program_id(0)
    n_exp = pl.num_programs(0)
    n_slots, mb, _ = xbuf.shape
    ahead = n_slots - 1
    n_blocks = bs_ref[n_exp]
    b0 = bs_ref[e]
    b1 = bs_ref[e + 1]
    count = cnt_ref[e]

    def x_copy(b):
        slot = lax.rem(b, n_slots)
        return pltpu.make_async_copy(xs_hbm.at[pl.ds(pl.multiple_of(b * mb, mb), mb)],
                                     xbuf.at[slot], xsem.at[slot])

    def y_copy(b):
        slot = lax.rem(b, n_slots)
        return pltpu.make_async_copy(ybuf.at[slot],
                                     ys_hbm.at[pl.ds(pl.multiple_of(b * mb, mb), mb)], ysem.at[slot])

    @pl.when(e == 0)
    def _():
        for i in range(ahead):
            @pl.when(i < n_blocks)
            def _():
                x_copy(i).start(priority=RING_DMA_PRIORITY)

    @pl.when(b1 > b0)
    def _():
        gu_bf[...] = wgu_ref[...].astype(BF16)
        d_bf[...] = wd_ref[...].astype(BF16)

    kh = gu_bf.shape[0] // 2
    half = gu_bf.shape[1] // 2

    def block(b, carry):
        slot = lax.rem(b, n_slots)
        x_copy(b).wait()

        @pl.when(b + ahead < n_blocks)
        def _():
            x_copy(b + ahead).start(priority=RING_DMA_PRIORITY)

        @pl.when(b >= n_slots)
        def _():
            y_copy(b - n_slots).wait()

        rows = lax.broadcasted_iota(jnp.int32, (mb, xbuf.shape[2]), 0)
        valid = count - (b - b0) * mb
        words = jnp.where(rows < valid, xbuf[slot], jnp.uint32(0))
        xa, xb = _unpack_halves(words)
        gu = _dot(xa.astype(BF16), gu_bf[0:kh, :]) + _dot(xb.astype(BF16), gu_bf[kh:, :])
        hid = _silu(gu[:, :half]) * gu[:, half:]
        ybuf[slot] = _pack_halves(_dot(hid.astype(BF16), d_bf[...]))
        y_copy(b).start(priority=RING_DMA_PRIORITY)
        return carry

    lax.fori_loop(b0, b1, block, 0)

    @pl.when(e == n_exp - 1)
    def _():
        for i in range(1, n_slots + 1):
            @pl.when(n_blocks >= i)
            def _():
                y_copy(n_blocks - i).wait()


def _experts(block_start, counts, xs, w_gu, w_down):
    n_rows, dh = xs.shape
    n_exp, d, gu_w = w_gu.shape
    ed = w_down.shape[1]
    return pl.pallas_call(
        _experts_kernel,
        grid_spec=pltpu.PrefetchScalarGridSpec(
            num_scalar_prefetch=2,
            grid=(n_exp,),
            in_specs=[pl.BlockSpec((None, d, gu_w), lambda e, bs, cnt: (e, 0, 0)),
                      pl.BlockSpec((None, ed, d), lambda e, bs, cnt: (e, 0, 0)),
                      pl.BlockSpec(memory_space=pl.ANY)],
            out_specs=pl.BlockSpec(memory_space=pl.ANY),
            scratch_shapes=[pltpu.VMEM((d, gu_w), BF16), pltpu.VMEM((ed, d), BF16),
                            pltpu.VMEM((RING_SLOTS, MOE_BLOCK, dh), jnp.uint32),
                            pltpu.VMEM((RING_SLOTS, MOE_BLOCK, dh), jnp.uint32),
                            pltpu.SemaphoreType.DMA((RING_SLOTS,)),
                            pltpu.SemaphoreType.DMA((RING_SLOTS,))]),
        out_shape=jax.ShapeDtypeStruct((n_rows, dh), jnp.uint32),
        compiler_params=_cparams(("arbitrary",)),
        name="experts",
    )(block_start, counts, w_gu, w_down, xs)


def _combine_kernel(h_ref, w_ref, wsg_ref, wsd_ref, g_ref, b_ref, yg_ref, o_ref, *, alpha):
    h = h_ref[...]
    gu = _dot(h.astype(BF16), wsg_ref[...])
    half = gu.shape[1] // 2
    shared = _dot((_silu(gu[:, :half]) * gu[:, half:]).astype(BF16), wsd_ref[...])
    acc = alpha * h + shared
    dh = acc.shape[1] // 2
    acc_a = acc[:, :dh]
    acc_b = acc[:, dh:]
    w = w_ref[...]
    for k in range(TOP_K):
        ya, yb = _unpack_halves(yg_ref[k])
        acc_a = acc_a + ya * w[:, k:k + 1]
        acc_b = acc_b + yb * w[:, k:k + 1]
    o_ref[...] = _layer_norm(jnp.concatenate([acc_a, acc_b], axis=1), g_ref[...], b_ref[...])


def _combine(h1, w_tk, ws_gu_bf, ws_down_bf, g, b, yg, alpha, tc):
    t, d = h1.shape
    sg = ws_gu_bf.shape[1]
    sd = ws_down_bf.shape[0]
    return pl.pallas_call(
        functools.partial(_combine_kernel, alpha=alpha),
        grid=(t // tc,),
        in_specs=[pl.BlockSpec((tc, d), lambda i: (i, 0)),
                  pl.BlockSpec((tc, TOP_K), lambda i: (i, 0)),
                  pl.BlockSpec((d, sg), lambda i: (0, 0)),
                  pl.BlockSpec((sd, d), lambda i: (0, 0)),
                  pl.BlockSpec((1, d), lambda i: (0, 0)),
                  pl.BlockSpec((1, d), lambda i: (0, 0)),
                  pl.BlockSpec((TOP_K, tc, yg.shape[2]), lambda i: (0, i, 0))],
        out_specs=pl.BlockSpec((tc, d), lambda i: (i, 0)),
        out_shape=jax.ShapeDtypeStruct((t, d), F32),
        compiler_params=_cparams(("parallel",)),
        name="combine",
    )(h1, w_tk, ws_gu_bf, ws_down_bf, g.reshape(1, d), b.reshape(1, d), yg)


def _moe(h1, h1_packed, logits_t, router_bias, w_exp_gu, w_exp_down, w_shared_gu, w_shared_down,
         ln_g, ln_b, alpha):
    t, d = h1.shape
    n_exp = logits_t.shape[0]
    eidx, w_kt, rank, counts = _route(logits_t, router_bias, tt=min(512, t))

    counts = counts.reshape(n_exp)
    blocks_per_expert = (counts + MOE_BLOCK - 1) // MOE_BLOCK
    block_start = jnp.concatenate(
        [jnp.zeros((1,), jnp.int32), jnp.cumsum(blocks_per_expert).astype(jnp.int32)])
    pad_start = block_start[:n_exp] * MOE_BLOCK
    n_blocks = -(-(t * TOP_K + n_exp * (MOE_BLOCK - 1)) // MOE_BLOCK)

    dest = _dest(pad_start, eidx, rank, tt=min(2048, t))
    xs = _dispatch(dest, h1_packed, n_blocks * MOE_BLOCK, chunk=DISPATCH_CHUNK)
    ys = _experts(block_start, counts, xs, w_exp_gu, w_exp_down)
    yg = _gather_back(dest, ys, chunk=GATHER_CHUNK)
    return _combine(h1, w_kt.T, w_shared_gu.astype(BF16), w_shared_down.astype(BF16),
                    ln_g, ln_b, yg, alpha, tc=min(256, t))


def kernel(x, ln_in_g, ln_in_b, w_in, hg_lb_logits, hg_norm_g, gm_v_norm_g, gm_v_norm_b, gm_w_s, gm_b_s, gm_out_norm_g, w_out, ln1_g, ln1_b, w_router, router_bias, w_exp_gu, w_exp_down, w_shared_gu, w_shared_down, ln2_g, ln2_b):
    bsz, s, d = x.shape
    depth = w_in.shape[0]
    assert depth == 1, "the lower-bound table row used in the hgrn kernel assumes one layer"
    alpha = (2.0 * depth) ** 0.25
    t = bsz * s
    hg_width = hg_norm_g.shape[1]
    gm_width = gm_v_norm_g.shape[1]
    tm = min(512, t)

    h0, proj = _in_proj(x.reshape(t, d), ln_in_g, ln_in_b, w_in[0].astype(BF16), tm)
    proj3 = proj.reshape(bsz, s, proj.shape[1])
    y_hg = _hgrn(proj3, hg_lb_logits.astype(F32), hg_norm_g[0], sb=min(512, s))
    y_gm = _gmlp(proj3, gm_v_norm_g[0], gm_v_norm_b[0], gm_w_s[0], gm_b_s[0], gm_out_norm_g[0],
                 u_blk=4 * hg_width // gm_width, n_blocks=min(4, s // GM_BLOCK))
    h1, h1_packed, logits_t = _out_proj(y_hg.reshape(t, hg_width), y_gm.reshape(t, gm_width), h0,
                                        w_out[0].astype(BF16), ln1_g[0], ln1_b[0], w_router[0].T,
                                        alpha, tm)
    out = _moe(h1, h1_packed, logits_t, router_bias[0], w_exp_gu[0], w_exp_down[0], w_shared_gu[0],
               w_shared_down[0], ln2_g[0], ln2_b[0], alpha)
    return out.reshape(bsz, s, d)
```

```python
import functools

import jax
import jax.numpy as jnp
from jax import lax
from jax.experimental import pallas as pl
from jax.experimental.pallas import tpu as pltpu
from jax.experimental.pallas import tpu_sc as plsc

F32 = jnp.float32
BF16 = jnp.bfloat16

LN_EPS = 1e-5
RMS_EPS = 1e-6
CHUNK = 64
SUB = 16
MAX_FACTORED_EXPONENT = 80.0
FACTORED_MIN_LB = float(2.718281828459045 ** (-MAX_FACTORED_EXPONENT / CHUNK))
HEAD_DIM = 128
GM_BLOCK = 128
TOP_K = 8
N_GROUPS = 8
TOPK_GROUPS = 4
ROUTED_SCALE = 2.5
MOE_BLOCK = 128
WEIGHT_SLOTS = 3
RING_SLOTS = 4
RING_DMA_PRIORITY = 1
DISPATCH_CHUNK = 64
GATHER_CHUNK = 16
VMEM_LIMIT = 48 * 1024 * 1024


def _cparams(sem):
    return pltpu.CompilerParams(dimension_semantics=sem, vmem_limit_bytes=VMEM_LIMIT)


def _layer_norm(x, g, b):
    mu = jnp.mean(x, axis=-1, keepdims=True)
    xc = x - mu
    var = jnp.mean(xc * xc, axis=-1, keepdims=True)
    return xc * lax.rsqrt(var + LN_EPS) * g + b


def _silu(x):
    return x * jax.nn.sigmoid(x)


def _gelu(x):
    return 0.5 * x * (1.0 + lax.erf(x * (2.0 ** -0.5)))


def _dot(a, b):
    return jnp.dot(a, b, preferred_element_type=F32)


def _dot_nt(a, b, precision=None):
    return lax.dot_general(a, b, (((1,), (1,)), ((), ())), preferred_element_type=F32,
                           precision=precision)


def _dot_tn(a, b):
    return lax.dot_general(a, b, (((0,), (0,)), ((), ())), preferred_element_type=F32)


def _pack_halves(x):
    n = x.shape[1] // 2
    hi = lax.bitcast_convert_type(x[:, :n].astype(BF16).astype(F32), jnp.uint32)
    lo = lax.bitcast_convert_type(x[:, n:].astype(BF16).astype(F32), jnp.uint32)
    return hi | (lo >> 16)


def _unpack_halves(w):
    a = lax.bitcast_convert_type(w & jnp.uint32(0xFFFF0000), F32)
    b = lax.bitcast_convert_type(w << 16, F32)
    return a, b


def _in_proj_kernel(x_ref, g_ref, b_ref, w_ref, p_ref, f_ref, *, f_lo):
    h = _layer_norm(x_ref[...], g_ref[...], b_ref[...])
    p = _dot(h.astype(BF16), w_ref[...])
    p_ref[...] = p.astype(BF16)
    f_ref[...] = p[:, f_lo:f_lo + f_ref.shape[1]]


def _in_proj(x2, g, b, w_bf, f_lo, f_width, tm):
    t, d = x2.shape
    n = w_bf.shape[1]
    return pl.pallas_call(
        functools.partial(_in_proj_kernel, f_lo=f_lo),
        grid=(t // tm,),
        in_specs=[pl.BlockSpec((tm, d), lambda i: (i, 0)),
                  pl.BlockSpec((1, d), lambda i: (0, 0)),
                  pl.BlockSpec((1, d), lambda i: (0, 0)),
                  pl.BlockSpec((d, n), lambda i: (0, 0))],
        out_specs=[pl.BlockSpec((tm, n), lambda i: (i, 0)),
                   pl.BlockSpec((tm, f_width), lambda i: (i, 0))],
        out_shape=[jax.ShapeDtypeStruct((t, n), BF16), jax.ShapeDtypeStruct((t, f_width), F32)],
        compiler_params=_cparams(("parallel",)),
        name="in_proj",
    )(x2, g.reshape(1, d), b.reshape(1, d), w_bf)


def _hgrn_kernel(q_ref, f_ref, i_ref, g_ref, lbl_ref, gn_ref, o_ref, st_ref, *, n_chunks):
    @pl.when(pl.program_id(1) == 0)
    def _():
        st_ref[...] = jnp.zeros_like(st_ref)

    lg = lbl_ref[...]
    ex = jnp.exp(lg - jnp.max(lg, axis=0, keepdims=True))
    lb_all = ex[0:1, :] / jnp.sum(ex, axis=0, keepdims=True)
    gn_all = gn_ref[...]

    c = CHUNK
    heads = st_ref.shape[0]
    row = lax.broadcasted_iota(jnp.int32, (c, c), 0)
    col = lax.broadcasted_iota(jnp.int32, (c, c), 1)
    tril = (row >= col).astype(BF16)
    sub_row = lax.broadcasted_iota(jnp.int32, (SUB, HEAD_DIM), 0)
    ones = jnp.ones((HEAD_DIM, HEAD_DIM), BF16)
    neg_inf = jnp.float32(-jnp.inf)

    def intra_pairwise(q, k, v, v_bf, b, o_inter):
        outs = []
        for blk in range(c // SUB):
            lo = blk * SUB
            b_i = b[lo:lo + SUB, :]
            q_i = q[lo:lo + SUB, :]
            k_i = k[lo:lo + SUB, :]
            v_i = v[lo:lo + SUB, :]
            prods = []
            for j in range(SUB):
                diff = jnp.where(sub_row >= j, b_i - b_i[j:j + 1, :], neg_inf)
                prods.append(q_i * k_i[j:j + 1, :] * jnp.exp(diff))
            p_all = jnp.concatenate(prods, axis=0).astype(BF16)
            r_all = _dot(p_all, ones)
            o_blk = o_inter[lo:lo + SUB, :]
            for j in range(SUB):
                o_blk = o_blk + r_all[j * SUB:(j + 1) * SUB, :] * v_i[j:j + 1, :]
            if blk > 0:
                b_ref = b[lo - 1:lo, :]
                q_hat = (q_i * jnp.exp(b_i - b_ref)).astype(BF16)
                k_hat = (k[0:lo, :] * jnp.exp(b_ref - b[0:lo, :])).astype(BF16)
                scores = _dot_nt(q_hat, k_hat)
                o_blk = o_blk + _dot(scores.astype(BF16), v_bf[0:lo, :])
            outs.append(o_blk)
        return jnp.concatenate(outs, axis=0)

    def one_chunk(r0, factored):
        rows = pl.ds(r0, c)
        hs = [slice(h * HEAD_DIM, (h + 1) * HEAD_DIM) for h in range(heads)]
        q = _silu(q_ref[rows, :].astype(F32))
        f = lb_all + (1.0 - lb_all) * jax.nn.sigmoid(f_ref[rows, :])
        v_bf = i_ref[rows, :]
        v = v_bf.astype(F32)
        lf = jnp.log(f)
        k = 1.0 - f
        lf_hi = lf.astype(BF16)
        lf_lo = (lf - lf_hi.astype(F32)).astype(BF16)
        b2 = _dot(tril, jnp.concatenate([lf_hi, lf_lo], axis=1))
        width = lf.shape[1]
        b = b2[:, :width] + b2[:, width:]
        b_last = b[c - 1:c, :]
        q_dec = (q * jnp.exp(b)).astype(BF16)
        k_dec = (k * jnp.exp(b_last - b)).astype(BF16)
        decay = jnp.exp(b_last)
        st = [st_ref[h] for h in range(heads)]
        st_bf = [s.astype(BF16) for s in st]
        if factored:
            k_grow = (k * jnp.exp(-b)).astype(BF16)
            both = [_dot_nt(q_dec[:, hs[h]], jnp.concatenate([st_bf[h], k_grow[:, hs[h]]], axis=0))
                    for h in range(heads)]
            scores = [jnp.where(row >= col, both[h][:, HEAD_DIM:], 0.0).astype(BF16)
                      for h in range(heads)]
            outs = [both[h][:, :HEAD_DIM] + _dot(scores[h], v_bf[:, hs[h]]) for h in range(heads)]
        else:
            outs = [intra_pairwise(q[:, hs[h]], k[:, hs[h]], v[:, hs[h]], v_bf[:, hs[h]],
                                   b[:, hs[h]], _dot_nt(q_dec[:, hs[h]], st_bf[h]))
                    for h in range(heads)]
        for h in range(heads):
            st_ref[h] = st[h] * decay[:, hs[h]] + _dot_tn(v_bf[:, hs[h]], k_dec[:, hs[h]])
        inv = [lax.rsqrt(jnp.mean(o * o, axis=-1, keepdims=True) + RMS_EPS) for o in outs]
        o = jnp.concatenate([outs[h] * inv[h] for h in range(heads)], axis=1)
        o_ref[rows, :] = (o * gn_all * _silu(g_ref[rows, :].astype(F32))).astype(o_ref.dtype)

    def run_chunks(factored):
        per_trip = 2 if (factored and n_chunks % 2 == 0) else 1

        def chunk_body(ci, carry):
            for u in range(per_trip):
                one_chunk(pl.multiple_of((ci * per_trip + u) * c, c), factored)
            return carry

        lax.fori_loop(0, n_chunks // per_trip, chunk_body, 0)

    bounded = jnp.min(lb_all) >= FACTORED_MIN_LB
    pl.when(bounded)(lambda: run_chunks(True))
    pl.when(jnp.logical_not(bounded))(lambda: run_chunks(False))


def _hgrn(proj3, f_pre3, lb_logits, norm_g, sb):
    bsz, s, _ = proj3.shape
    hg_width = norm_g.shape[0]
    heads = hg_width // HEAD_DIM

    def sect(k):
        return pl.BlockSpec((None, sb, hg_width), lambda b, j, k=k: (b, j, k))

    return pl.pallas_call(
        functools.partial(_hgrn_kernel, n_chunks=sb // CHUNK),
        grid=(bsz, s // sb),
        in_specs=[sect(0), sect(0), sect(2), sect(3),
                  pl.BlockSpec((lb_logits.shape[0], hg_width), lambda b, j: (0, 0)),
                  pl.BlockSpec((1, hg_width), lambda b, j: (0, 0))],
        out_specs=pl.BlockSpec((None, sb, hg_width), lambda b, j: (b, j, 0)),
        out_shape=jax.ShapeDtypeStruct((bsz, s, hg_width), BF16),
        scratch_shapes=[pltpu.VMEM((heads, HEAD_DIM, HEAD_DIM), F32)],
        compiler_params=_cparams(("parallel", "arbitrary")),
        name="hgrn",
    )(proj3, f_pre3, proj3, proj3, lb_logits, norm_g.reshape(1, hg_width))


def _gmlp_kernel(u_ref, v_ref, vg_ref, vb_ref, ws_ref, bs_ref, og_ref, o_ref, *, n_blocks):
    groups = ws_ref.shape[0]
    gdim = u_ref.shape[-1] // groups
    row = lax.broadcasted_iota(jnp.int32, (GM_BLOCK, GM_BLOCK), 0) // CHUNK
    col = lax.broadcasted_iota(jnp.int32, (GM_BLOCK, GM_BLOCK), 1) // CHUNK
    allowed = row >= col
    ws = [jnp.where(allowed, ws_ref[g], 0.0).astype(BF16) for g in range(groups)]
    for n in range(n_blocks):
        rows = pl.ds(n * GM_BLOCK, GM_BLOCK)
        u = _gelu(u_ref[rows, :].astype(F32))
        v = _layer_norm(_gelu(v_ref[rows, :].astype(F32)), vg_ref[...], vb_ref[...])
        v_bf = v.astype(BF16)
        mixed = jnp.concatenate(
            [_dot(ws[g], v_bf[:, g * gdim:(g + 1) * gdim]) for g in range(groups)], axis=1)
        y = u * (mixed + bs_ref[...])
        inv = lax.rsqrt(jnp.mean(y * y, axis=-1, keepdims=True) + RMS_EPS)
        o_ref[rows, :] = (y * inv * og_ref[...]).astype(o_ref.dtype)


def _gmlp(proj3, v_norm_g, v_norm_b, w_s, b_s, out_norm_g, u_blk, n_blocks):
    bsz, s, _ = proj3.shape
    gw = v_norm_g.shape[0]
    groups = w_s.shape[0]
    rows = n_blocks * GM_BLOCK
    bias_full = jnp.repeat(b_s.T, gw // groups, axis=1)
    return pl.pallas_call(
        functools.partial(_gmlp_kernel, n_blocks=n_blocks),
        grid=(bsz, s // rows),
        in_specs=[pl.BlockSpec((None, rows, gw), lambda b, j: (b, j, u_blk)),
                  pl.BlockSpec((None, rows, gw), lambda b, j: (b, j, u_blk + 1)),
                  pl.BlockSpec((1, gw), lambda b, j: (0, 0)),
                  pl.BlockSpec((1, gw), lambda b, j: (0, 0)),
                  pl.BlockSpec((groups, GM_BLOCK, GM_BLOCK), lambda b, j: (0, 0, 0)),
                  pl.BlockSpec((GM_BLOCK, gw), lambda b, j: (0, 0)),
                  pl.BlockSpec((1, gw), lambda b, j: (0, 0))],
        out_specs=pl.BlockSpec((None, rows, gw), lambda b, j: (b, j, 0)),
        out_shape=jax.ShapeDtypeStruct((bsz, s, gw), BF16),
        compiler_params=_cparams(("parallel", "parallel")),
        name="gmlp",
    )(proj3, proj3, v_norm_g.reshape(1, gw), v_norm_b.reshape(1, gw), w_s, bias_full,
      out_norm_g.reshape(1, gw))


def _out_proj_kernel(yh_ref, yg_ref, x_ref, g0_ref, b0_ref, wa_ref, wb_ref, g_ref, b_ref, wr_ref,
                     h1_ref, hp_ref, lg_ref, *, alpha):
    mix = _dot(yh_ref[...], wa_ref[...]) + _dot(yg_ref[...], wb_ref[...])
    h0 = _layer_norm(x_ref[...], g0_ref[...], b0_ref[...])
    h1 = _layer_norm(alpha * h0 + mix, g_ref[...], b_ref[...])
    h1_ref[...] = h1
    hp_ref[...] = _pack_halves(h1)
    lg_ref[...] = _dot_nt(wr_ref[...], h1, precision=lax.Precision.HIGHEST)


def _out_proj(y_hg, y_gm, x2, g0, b0, w_out_bf, g, b, w_router_t, alpha, tm):
    t, d = x2.shape
    hw = y_hg.shape[1]
    gw = y_gm.shape[1]
    e = w_router_t.shape[0]
    return pl.pallas_call(
        functools.partial(_out_proj_kernel, alpha=alpha),
        grid=(t // tm,),
        in_specs=[pl.BlockSpec((tm, hw), lambda i: (i, 0)),
                  pl.BlockSpec((tm, gw), lambda i: (i, 0)),
                  pl.BlockSpec((tm, d), lambda i: (i, 0)),
                  pl.BlockSpec((1, d), lambda i: (0, 0)),
                  pl.BlockSpec((1, d), lambda i: (0, 0)),
                  pl.BlockSpec((hw, d), lambda i: (0, 0)),
                  pl.BlockSpec((gw, d), lambda i: (0, 0)),
                  pl.BlockSpec((1, d), lambda i: (0, 0)),
                  pl.BlockSpec((1, d), lambda i: (0, 0)),
                  pl.BlockSpec((e, d), lambda i: (0, 0))],
        out_specs=[pl.BlockSpec((tm, d), lambda i: (i, 0)),
                   pl.BlockSpec((tm, d // 2), lambda i: (i, 0)),
                   pl.BlockSpec((e, tm), lambda i: (0, i))],
        out_shape=[jax.ShapeDtypeStruct((t, d), F32), jax.ShapeDtypeStruct((t, d // 2), jnp.uint32),
                   jax.ShapeDtypeStruct((e, t), F32)],
        compiler_params=_cparams(("parallel",)),
        name="out_proj",
    )(y_hg, y_gm, x2, g0.reshape(1, d), b0.reshape(1, d), w_out_bf[:hw], w_out_bf[hw:],
      g.reshape(1, d), b.reshape(1, d), w_router_t)


def _route_kernel(lg_ref, bias_ref, eidx_ref, w_ref, rank_ref, cnt_ref, carry_ref):
    @pl.when(pl.program_id(0) == 0)
    def _():
        carry_ref[...] = jnp.zeros_like(carry_ref)

    n_exp, tt = lg_ref.shape
    per_group = n_exp // N_GROUPS
    neg_inf = jnp.float32(-jnp.inf)
    scores = jax.nn.sigmoid(lg_ref[...])
    biased = scores + bias_ref[...]

    gio = lax.broadcasted_iota(jnp.int32, (per_group, tt), 0)
    blocks, gs_rows = [], []
    for g in range(N_GROUPS):
        blk = biased[g * per_group:(g + 1) * per_group, :]
        m1 = jnp.max(blk, axis=0, keepdims=True)
        first = jnp.min(jnp.where(blk == m1, gio, per_group), axis=0, keepdims=True)
        m2 = jnp.max(jnp.where(gio == first, neg_inf, blk), axis=0, keepdims=True)
        blocks.append(blk)
        gs_rows.append(m1 + m2)
    gs = jnp.concatenate(gs_rows, axis=0)
    gidx = lax.broadcasted_iota(jnp.int32, (N_GROUPS, tt), 0)
    beaten = jnp.zeros((N_GROUPS, tt), jnp.int32)
    for g in range(N_GROUPS):
        r = gs_rows[g]
        beaten = beaten + jnp.where((r > gs) | ((r == gs) & (g < gidx)), 1, 0)
    keep = beaten < TOPK_GROUPS
    cand = jnp.concatenate(
        [jnp.where(keep[g:g + 1, :], blocks[g], neg_inf) for g in range(N_GROUPS)], axis=0)

    eio = lax.broadcasted_iota(jnp.int32, (n_exp, tt), 0)
    idx_rows, w_rows = [], []
    chosen = jnp.zeros((n_exp, tt), F32)
    for _ in range(TOP_K):
        m = jnp.max(cand, axis=0, keepdims=True)
        idx = jnp.min(jnp.where(cand == m, eio, n_exp), axis=0, keepdims=True)
        sel = eio == idx
        w_rows.append(jnp.sum(jnp.where(sel, scores, 0.0), axis=0, keepdims=True))
        idx_rows.append(idx)
        cand = jnp.where(sel, neg_inf, cand)
        chosen = jnp.where(sel, 1.0, chosen)
    eidx = jnp.concatenate(idx_rows, axis=0)
    w = jnp.concatenate(w_rows, axis=0)
    eidx_ref[...] = eidx
    w_ref[...] = w / jnp.sum(w, axis=0, keepdims=True) * ROUTED_SCALE

    tr = lax.broadcasted_iota(jnp.int32, (tt, tt), 0)
    tc = lax.broadcasted_iota(jnp.int32, (tt, tt), 1)
    before = (tr < tc).astype(BF16)
    cum = _dot(chosen.astype(BF16), before) + carry_ref[...]
    rank_ref[...] = jnp.concatenate(
        [jnp.sum(jnp.where(eio == idx_rows[k], cum, 0.0), axis=0, keepdims=True)
         for k in range(TOP_K)], axis=0).astype(jnp.int32)
    total = carry_ref[...] + jnp.sum(chosen, axis=1, keepdims=True)
    carry_ref[...] = total
    cnt_ref[...] = total.astype(jnp.int32)


def _route(logits_t, router_bias, tt):
    e, t = logits_t.shape
    return pl.pallas_call(
        _route_kernel,
        grid=(t // tt,),
        in_specs=[pl.BlockSpec((e, tt), lambda i: (0, i)),
                  pl.BlockSpec((e, 1), lambda i: (0, 0))],
        out_specs=[pl.BlockSpec((TOP_K, tt), lambda i: (0, i)),
                   pl.BlockSpec((TOP_K, tt), lambda i: (0, i)),
                   pl.BlockSpec((TOP_K, tt), lambda i: (0, i)),
                   pl.BlockSpec((e, 1), lambda i: (0, 0))],
        out_shape=[jax.ShapeDtypeStruct((TOP_K, t), jnp.int32),
                   jax.ShapeDtypeStruct((TOP_K, t), F32),
                   jax.ShapeDtypeStruct((TOP_K, t), jnp.int32),
                   jax.ShapeDtypeStruct((e, 1), jnp.int32)],
        scratch_shapes=[pltpu.VMEM((e, 1), F32)],
        compiler_params=_cparams(("arbitrary",)),
        name="route",
    )(logits_t, router_bias.astype(F32).reshape(e, 1))


def _dest_kernel(ps_ref, eidx_ref, rank_ref, dest_ref, *, n_exp):
    eidx = eidx_ref[...]

    def body(e, acc):
        return acc + jnp.where(eidx == e, ps_ref[e], 0)

    dest_ref[...] = lax.fori_loop(0, n_exp, body, rank_ref[...])


def _dest(pad_start, eidx, rank, tt):
    k, t = eidx.shape
    n_exp = pad_start.shape[0]
    return pl.pallas_call(
        functools.partial(_dest_kernel, n_exp=n_exp),
        grid_spec=pltpu.PrefetchScalarGridSpec(
            num_scalar_prefetch=1,
            grid=(t // tt,),
            in_specs=[pl.BlockSpec((k, tt), lambda i, ps: (0, i)),
                      pl.BlockSpec((k, tt), lambda i, ps: (0, i))],
            out_specs=pl.BlockSpec((k, tt), lambda i, ps: (0, i))),
        out_shape=jax.ShapeDtypeStruct((k, t), jnp.int32),
        compiler_params=_cparams(("parallel",)),
        name="dest",
    )(pad_start, eidx, rank)


def _chunked(dest, chunk):
    k, t = dest.shape
    return dest.reshape(k, t // chunk, chunk).transpose(1, 0, 2)


def _sc_workers():
    info = plsc.get_sparse_core_info()
    return info.num_cores, info.num_cores * info.num_subcores


def _dispatch(dest, h1, n_rows, chunk):
    t, dh = h1.shape
    n_cores, n_workers = _sc_workers()
    per_worker = t // chunk // n_workers
    mesh = plsc.VectorSubcoreMesh(core_axis_name="c", subcore_axis_name="s")

    @functools.partial(
        pl.kernel, mesh=mesh,
        out_type=jax.ShapeDtypeStruct((n_rows, dh), h1.dtype),
        scratch_types=[pltpu.VMEM((TOP_K, chunk), jnp.int32),
                       pltpu.VMEM((chunk, dh), h1.dtype),
                       pltpu.SemaphoreType.DMA],
        name="dispatch",
    )
    def scatter_rows(x_hbm, dest_hbm, o_hbm, idx_v, rows_v, sem):
        wid = lax.axis_index("s") * n_cores + lax.axis_index("c")

        @pl.loop(0, per_worker)
        def _(j):
            c = wid * per_worker + j
            pltpu.sync_copy(dest_hbm.at[c], idx_v)
            pltpu.sync_copy(x_hbm.at[pl.ds(c * chunk, chunk)], rows_v)
            copies = [pltpu.async_copy(rows_v, o_hbm.at[idx_v.at[k]], sem) for k in range(TOP_K)]
            for cp in copies:
                cp.wait()

    return scatter_rows(h1, _chunked(dest, chunk))


def _gather_back(dest, ys, chunk):
    k_top, t = dest.shape
    dh = ys.shape[1]
    n_cores, n_workers = _sc_workers()
    per_worker = t // chunk // n_workers
    mesh = plsc.VectorSubcoreMesh(core_axis_name="c", subcore_axis_name="s")

    @functools.partial(
        pl.kernel, mesh=mesh,
        out_type=jax.ShapeDtypeStruct((k_top, t, dh), ys.dtype),
        scratch_types=[pltpu.VMEM((k_top, chunk), jnp.int32),
                       pltpu.VMEM((k_top, chunk, dh), ys.dtype),
                       pltpu.SemaphoreType.DMA],
        name="gather_back",
    )
    def gather_rows(ys_hbm, dest_hbm, o_hbm, idx_v, rows_v, sem):
        wid = lax.axis_index("s") * n_cores + lax.axis_index("c")

        @pl.loop(0, per_worker)
        def _(j):
            c = wid * per_worker + j
            pltpu.sync_copy(dest_hbm.at[c], idx_v)
            copies = [pltpu.async_copy(ys_hbm.at[idx_v.at[k]], rows_v.at[k], sem)
                      for k in range(k_top)]
            for cp in copies:
                cp.wait()
            for k in range(k_top):
                pltpu.sync_copy(rows_v.at[k], o_hbm.at[k, pl.ds(c * chunk, chunk)])

    return gather_rows(ys, _chunked(dest, chunk))


def _experts_kernel(bs_ref, cnt_ref, wgu_hbm, wd_hbm, xs_hbm, ys_hbm, wgu_buf, wd_buf, gu_bf, d_bf,
                    xbuf, ybuf, wsem, xsem, ysem):
    e = pl.program_id(0)
    n_exp = pl.num_programs(0)
    n_slots, mb, _ = xbuf.shape
    ahead = n_slots - 1
    w_slots = wgu_buf.shape[0]
    n_blocks = bs_ref[n_exp]
    b0 = bs_ref[e]
    b1 = bs_ref[e + 1]
    count = cnt_ref[e]

    def w_copies(ex):
        slot = lax.rem(ex, w_slots)
        return (pltpu.make_async_copy(wgu_hbm.at[ex], wgu_buf.at[slot], wsem.at[slot]),
                pltpu.make_async_copy(wd_hbm.at[ex], wd_buf.at[slot], wsem.at[slot]))

    @pl.when(e == 0)
    def _():
        for i in range(w_slots - 1):
            @pl.when(i < n_exp)
            def _():
                for cp in w_copies(i):
                    cp.start()

    @pl.when(e + w_slots - 1 < n_exp)
    def _():
        for cp in w_copies(e + w_slots - 1):
            cp.start()

    def x_copy(b):
        slot = lax.rem(b, n_slots)
        return pltpu.make_async_copy(xs_hbm.at[pl.ds(pl.multiple_of(b * mb, mb), mb)],
                                     xbuf.at[slot], xsem.at[slot])

    def y_copy(b):
        slot = lax.rem(b, n_slots)
        return pltpu.make_async_copy(ybuf.at[slot],
                                     ys_hbm.at[pl.ds(pl.multiple_of(b * mb, mb), mb)], ysem.at[slot])

    @pl.when(e == 0)
    def _():
        for i in range(ahead):
            @pl.when(i < n_blocks)
            def _():
                x_copy(i).start(priority=RING_DMA_PRIORITY)

    for cp in w_copies(e):
        cp.wait()
    w_slot = lax.rem(e, w_slots)

    @pl.when(b1 > b0)
    def _():
        gu_bf[...] = wgu_buf[w_slot].astype(BF16)
        d_bf[...] = wd_buf[w_slot].astype(BF16)

    kh = gu_bf.shape[0] // 2
    half = gu_bf.shape[1] // 2

    def block(b, carry):
        slot = lax.rem(b, n_slots)
        x_copy(b).wait()

        @pl.when(b + ahead < n_blocks)
        def _():
            x_copy(b + ahead).start(priority=RING_DMA_PRIORITY)

        @pl.when(b >= n_slots)
        def _():
            y_copy(b - n_slots).wait()

        rows = lax.broadcasted_iota(jnp.int32, (mb, xbuf.shape[2]), 0)
        valid = count - (b - b0) * mb
        words = jnp.where(rows < valid, xbuf[slot], jnp.uint32(0))
        xa, xb = _unpack_halves(words)
        gu = _dot(xa.astype(BF16), gu_bf[0:kh, :]) + _dot(xb.astype(BF16), gu_bf[kh:, :])
        hid = _silu(gu[:, :half]) * gu[:, half:]
        ybuf[slot] = _pack_halves(_dot(hid.astype(BF16), d_bf[...]))
        y_copy(b).start(priority=RING_DMA_PRIORITY)
        return carry

    lax.fori_loop(b0, b1, block, 0)

    @pl.when(e == n_exp - 1)
    def _():
        for i in range(1, n_slots + 1):
            @pl.when(n_blocks >= i)
            def _():
                y_copy(n_blocks - i).wait()


def _experts(block_start, counts, xs, w_gu, w_down):
    n_rows, dh = xs.shape
    n_exp, d, gu_w = w_gu.shape
    ed = w_down.shape[1]
    return pl.pallas_call(
        _experts_kernel,
        grid_spec=pltpu.PrefetchScalarGridSpec(
            num_scalar_prefetch=2,
            grid=(n_exp,),
            in_specs=[pl.BlockSpec(memory_space=pl.ANY),
                      pl.BlockSpec(memory_space=pl.ANY),
                      pl.BlockSpec(memory_space=pl.ANY)],
            out_specs=pl.BlockSpec(memory_space=pl.ANY),
            scratch_shapes=[pltpu.VMEM((WEIGHT_SLOTS, d, gu_w), w_gu.dtype),
                            pltpu.VMEM((WEIGHT_SLOTS, ed, d), w_down.dtype),
                            pltpu.VMEM((d, gu_w), BF16), pltpu.VMEM((ed, d), BF16),
                            pltpu.VMEM((RING_SLOTS, MOE_BLOCK, dh), jnp.uint32),
                            pltpu.VMEM((RING_SLOTS, MOE_BLOCK, dh), jnp.uint32),
                            pltpu.SemaphoreType.DMA((WEIGHT_SLOTS,)),
                            pltpu.SemaphoreType.DMA((RING_SLOTS,)),
                            pltpu.SemaphoreType.DMA((RING_SLOTS,))]),
        out_shape=jax.ShapeDtypeStruct((n_rows, dh), jnp.uint32),
        compiler_params=_cparams(("arbitrary",)),
        name="experts",
    )(block_start, counts, w_gu, w_down, xs)


def _combine_kernel(h_ref, w_ref, wsg_ref, wsd_ref, g_ref, b_ref, yg_ref, o_ref, *, alpha):
    h = h_ref[...]
    gu = _dot(h.astype(BF16), wsg_ref[...])
    half = gu.shape[1] // 2
    shared = _dot((_silu(gu[:, :half]) * gu[:, half:]).astype(BF16), wsd_ref[...])
    acc = alpha * h + shared
    dh = acc.shape[1] // 2
    acc_a = acc[:, :dh]
    acc_b = acc[:, dh:]
    w = w_ref[...]
    for k in range(TOP_K):
        ya, yb = _unpack_halves(yg_ref[k])
        acc_a = acc_a + ya * w[:, k:k + 1]
        acc_b = acc_b + yb * w[:, k:k + 1]
    o_ref[...] = _layer_norm(jnp.concatenate([acc_a, acc_b], axis=1), g_ref[...], b_ref[...])


def _combine(h1, w_tk, ws_gu_bf, ws_down_bf, g, b, yg, alpha, tc):
    t, d = h1.shape
    sg = ws_gu_bf.shape[1]
    sd = ws_down_bf.shape[0]
    return pl.pallas_call(
        functools.partial(_combine_kernel, alpha=alpha),
        grid=(t // tc,),
        in_specs=[pl.BlockSpec((tc, d), lambda i: (i, 0)),
                  pl.BlockSpec((tc, TOP_K), lambda i: (i, 0)),
                  pl.BlockSpec((d, sg), lambda i: (0, 0)),
                  pl.BlockSpec((sd, d), lambda i: (0, 0)),
                  pl.BlockSpec((1, d), lambda i: (0, 0)),
                  pl.BlockSpec((1, d), lambda i: (0, 0)),
                  pl.BlockSpec((TOP_K, tc, yg.shape[2]), lambda i: (0, i, 0))],
        out_specs=pl.BlockSpec((tc, d), lambda i: (i, 0)),
        out_shape=jax.ShapeDtypeStruct((t, d), F32),
        compiler_params=_cparams(("parallel",)),
        name="combine",
    )(h1, w_tk, ws_gu_bf, ws_down_bf, g.reshape(1, d), b.reshape(1, d), yg)


def _moe(h1, h1_packed, logits_t, router_bias, w_exp_gu, w_exp_down, w_shared_gu, w_shared_down,
         ln_g, ln_b, alpha):
    t, d = h1.shape
    n_exp = logits_t.shape[0]
    eidx, w_kt, rank, counts = _route(logits_t, router_bias, tt=min(512, t))

    counts = counts.reshape(n_exp)
    blocks_per_expert = (counts + MOE_BLOCK - 1) // MOE_BLOCK
    block_start = jnp.concatenate(
        [jnp.zeros((1,), jnp.int32), jnp.cumsum(blocks_per_expert).astype(jnp.int32)])
    pad_start = block_start[:n_exp] * MOE_BLOCK
    n_blocks = -(-(t * TOP_K + n_exp * (MOE_BLOCK - 1)) // MOE_BLOCK)

    dest = _dest(pad_start, eidx, rank, tt=min(2048, t))
    xs = _dispatch(dest, h1_packed, n_blocks * MOE_BLOCK, chunk=DISPATCH_CHUNK)
    ys = _experts(block_start, counts, xs, w_exp_gu, w_exp_down)
    yg = _gather_back(dest, ys, chunk=GATHER_CHUNK)
    return _combine(h1, w_kt.T, w_shared_gu.astype(BF16), w_shared_down.astype(BF16),
                    ln_g, ln_b, yg, alpha, tc=min(256, t))


def kernel(x, ln_in_g, ln_in_b, w_in, hg_lb_logits, hg_norm_g, gm_v_norm_g, gm_v_norm_b, gm_w_s, gm_b_s, gm_out_norm_g, w_out, ln1_g, ln1_b, w_router, router_bias, w_exp_gu, w_exp_down, w_shared_gu, w_shared_down, ln2_g, ln2_b):
    bsz, s, d = x.shape
    depth = w_in.shape[0]
    assert depth == 1, "the lower-bound table row used in the hgrn kernel assumes one layer"
    alpha = (2.0 * depth) ** 0.25
    t = bsz * s
    hg_width = hg_norm_g.shape[1]
    gm_width = gm_v_norm_g.shape[1]
    tm = min(512, t)

    x2 = x.reshape(t, d)
    proj, f_pre = _in_proj(x2, ln_in_g, ln_in_b, w_in[0].astype(BF16), hg_width, hg_width, tm)
    proj3 = proj.reshape(bsz, s, proj.shape[1])
    y_hg = _hgrn(proj3, f_pre.reshape(bsz, s, hg_width), hg_lb_logits.astype(F32), hg_norm_g[0],
                 sb=min(512, s))
    y_gm = _gmlp(proj3, gm_v_norm_g[0], gm_v_norm_b[0], gm_w_s[0], gm_b_s[0], gm_out_norm_g[0],
                 u_blk=4 * hg_width // gm_width, n_blocks=min(4, s // GM_BLOCK))
    h1, h1_packed, logits_t = _out_proj(y_hg.reshape(t, hg_width), y_gm.reshape(t, gm_width), x2,
                                        ln_in_g, ln_in_b, w_out[0].astype(BF16), ln1_g[0], ln1_b[0],
                                        w_router[0].T, alpha, tm)
    out = _moe(h1, h1_packed, logits_t, router_bias[0], w_exp_gu[0], w_exp_down[0], w_shared_gu[0],
               w_shared_down[0], ln2_g[0], ln2_b[0], alpha)
    return out.reshape(bsz, s, d)
```

```python
import functools

import jax
import jax.numpy as jnp
from jax import lax
from jax.experimental import pallas as pl
from jax.experimental.pallas import tpu as pltpu
from jax.experimental.pallas import tpu_sc as plsc

F32 = jnp.float32
BF16 = jnp.bfloat16

LN_EPS = 1e-5
RMS_EPS = 1e-6
CHUNK = 64
SUB = 16
MAX_FACTORED_EXPONENT = 80.0
FACTORED_MIN_LB = float(2.718281828459045 ** (-MAX_FACTORED_EXPONENT / CHUNK))
HEAD_DIM = 128
GM_BLOCK = 128
TOP_K = 8
N_GROUPS = 8
TOPK_GROUPS = 4
ROUTED_SCALE = 2.5
MOE_BLOCK = 128
WEIGHT_SLOTS = 3
RING_SLOTS = 4
RING_DMA_PRIORITY = 1
DISPATCH_CHUNK = 64
COMBINE_PARTS = 4
GATHER_CHUNK = 16
VMEM_LIMIT = 48 * 1024 * 1024


def _cparams(sem):
    return pltpu.CompilerParams(dimension_semantics=sem, vmem_limit_bytes=VMEM_LIMIT)


def _layer_norm(x, g, b):
    mu = jnp.mean(x, axis=-1, keepdims=True)
    xc = x - mu
    var = jnp.mean(xc * xc, axis=-1, keepdims=True)
    return xc * lax.rsqrt(var + LN_EPS) * g + b


def _silu(x):
    return x * jax.nn.sigmoid(x)


def _gelu(x):
    return 0.5 * x * (1.0 + lax.erf(x * (2.0 ** -0.5)))


def _dot(a, b):
    return jnp.dot(a, b, preferred_element_type=F32)


def _dot_nt(a, b, precision=None):
    return lax.dot_general(a, b, (((1,), (1,)), ((), ())), preferred_element_type=F32,
                           precision=precision)


def _dot_tn(a, b):
    return lax.dot_general(a, b, (((0,), (0,)), ((), ())), preferred_element_type=F32)


def _pack_halves(x):
    n = x.shape[1] // 2
    hi = lax.bitcast_convert_type(x[:, :n].astype(BF16).astype(F32), jnp.uint32)
    lo = lax.bitcast_convert_type(x[:, n:].astype(BF16).astype(F32), jnp.uint32)
    return hi | (lo >> 16)


def _unpack_halves(w):
    a = lax.bitcast_convert_type(w & jnp.uint32(0xFFFF0000), F32)
    b = lax.bitcast_convert_type(w << 16, F32)
    return a, b


def _in_proj_kernel(x_ref, g_ref, b_ref, w_ref, p_ref, f_ref, *, f_lo):
    h = _layer_norm(x_ref[...], g_ref[...], b_ref[...])
    p = _dot(h.astype(BF16), w_ref[...])
    p_ref[...] = p.astype(BF16)
    f_ref[...] = p[:, f_lo:f_lo + f_ref.shape[1]]


def _in_proj(x2, g, b, w_bf, f_lo, f_width, tm):
    t, d = x2.shape
    n = w_bf.shape[1]
    return pl.pallas_call(
        functools.partial(_in_proj_kernel, f_lo=f_lo),
        grid=(t // tm,),
        in_specs=[pl.BlockSpec((tm, d), lambda i: (i, 0)),
                  pl.BlockSpec((1, d), lambda i: (0, 0)),
                  pl.BlockSpec((1, d), lambda i: (0, 0)),
                  pl.BlockSpec((d, n), lambda i: (0, 0))],
        out_specs=[pl.BlockSpec((tm, n), lambda i: (i, 0)),
                   pl.BlockSpec((tm, f_width), lambda i: (i, 0))],
        out_shape=[jax.ShapeDtypeStruct((t, n), BF16), jax.ShapeDtypeStruct((t, f_width), F32)],
        compiler_params=_cparams(("parallel",)),
        name="in_proj",
    )(x2, g.reshape(1, d), b.reshape(1, d), w_bf)


def _hgrn_kernel(q_ref, f_ref, i_ref, g_ref, lbl_ref, gn_ref, o_ref, st_ref, *, n_chunks):
    @pl.when(pl.program_id(1) == 0)
    def _():
        st_ref[...] = jnp.zeros_like(st_ref)

    lg = lbl_ref[...]
    ex = jnp.exp(lg - jnp.max(lg, axis=0, keepdims=True))
    lb_all = ex[0:1, :] / jnp.sum(ex, axis=0, keepdims=True)
    gn_all = gn_ref[...]

    c = CHUNK
    heads = st_ref.shape[0]
    row = lax.broadcasted_iota(jnp.int32, (c, c), 0)
    col = lax.broadcasted_iota(jnp.int32, (c, c), 1)
    tril = (row >= col).astype(BF16)
    sub_row = lax.broadcasted_iota(jnp.int32, (SUB, HEAD_DIM), 0)
    ones = jnp.ones((HEAD_DIM, HEAD_DIM), BF16)
    neg_inf = jnp.float32(-jnp.inf)

    def intra_pairwise(q, k, v, v_bf, b, o_inter):
        outs = []
        for blk in range(c // SUB):
            lo = blk * SUB
            b_i = b[lo:lo + SUB, :]
            q_i = q[lo:lo + SUB, :]
            k_i = k[lo:lo + SUB, :]
            v_i = v[lo:lo + SUB, :]
            prods = []
            for j in range(SUB):
                diff = jnp.where(sub_row >= j, b_i - b_i[j:j + 1, :], neg_inf)
                prods.append(q_i * k_i[j:j + 1, :] * jnp.exp(diff))
            p_all = jnp.concatenate(prods, axis=0).astype(BF16)
            r_all = _dot(p_all, ones)
            o_blk = o_inter[lo:lo + SUB, :]
            for j in range(SUB):
                o_blk = o_blk + r_all[j * SUB:(j + 1) * SUB, :] * v_i[j:j + 1, :]
            if blk > 0:
                b_ref = b[lo - 1:lo, :]
                q_hat = (q_i * jnp.exp(b_i - b_ref)).astype(BF16)
                k_hat = (k[0:lo, :] * jnp.exp(b_ref - b[0:lo, :])).astype(BF16)
                scores = _dot_nt(q_hat, k_hat)
                o_blk = o_blk + _dot(scores.astype(BF16), v_bf[0:lo, :])
            outs.append(o_blk)
        return jnp.concatenate(outs, axis=0)

    def one_chunk(r0, factored):
        rows = pl.ds(r0, c)
        hs = [slice(h * HEAD_DIM, (h + 1) * HEAD_DIM) for h in range(heads)]
        q = _silu(q_ref[rows, :].astype(F32))
        f = lb_all + (1.0 - lb_all) * jax.nn.sigmoid(f_ref[rows, :])
        v_bf = i_ref[rows, :]
        v = v_bf.astype(F32)
        lf = jnp.log(f)
        k = 1.0 - f
        lf_hi = lf.astype(BF16)
        lf_lo = (lf - lf_hi.astype(F32)).astype(BF16)
        b2 = _dot(tril, jnp.concatenate([lf_hi, lf_lo], axis=1))
        width = lf.shape[1]
        b = b2[:, :width] + b2[:, width:]
        b_last = b[c - 1:c, :]
        q_dec = (q * jnp.exp(b)).astype(BF16)
        k_dec = (k * jnp.exp(b_last - b)).astype(BF16)
        decay = jnp.exp(b_last)
        st = [st_ref[h] for h in range(heads)]
        st_bf = [s.astype(BF16) for s in st]
        if factored:
            k_grow = (k * jnp.exp(-b)).astype(BF16)
            both = [_dot_nt(q_dec[:, hs[h]], jnp.concatenate([st_bf[h], k_grow[:, hs[h]]], axis=0))
                    for h in range(heads)]
            scores = [jnp.where(row >= col, both[h][:, HEAD_DIM:], 0.0).astype(BF16)
                      for h in range(heads)]
            outs = [both[h][:, :HEAD_DIM] + _dot(scores[h], v_bf[:, hs[h]]) for h in range(heads)]
        else:
            outs = [intra_pairwise(q[:, hs[h]], k[:, hs[h]], v[:, hs[h]], v_bf[:, hs[h]],
                                   b[:, hs[h]], _dot_nt(q_dec[:, hs[h]], st_bf[h]))
                    for h in range(heads)]
        for h in range(heads):
            st_ref[h] = st[h] * decay[:, hs[h]] + _dot_tn(v_bf[:, hs[h]], k_dec[:, hs[h]])
        inv = [lax.rsqrt(jnp.mean(o * o, axis=-1, keepdims=True) + RMS_EPS) for o in outs]
        o = jnp.concatenate([outs[h] * inv[h] for h in range(heads)], axis=1)
        o_ref[rows, :] = (o * gn_all * _silu(g_ref[rows, :].astype(F32))).astype(o_ref.dtype)

    def run_chunks(factored):
        per_trip = 2 if (factored and n_chunks % 2 == 0) else 1

        def chunk_body(ci, carry):
            for u in range(per_trip):
                one_chunk(pl.multiple_of((ci * per_trip + u) * c, c), factored)
            return carry

        lax.fori_loop(0, n_chunks // per_trip, chunk_body, 0)

    bounded = jnp.min(lb_all) >= FACTORED_MIN_LB
    pl.when(bounded)(lambda: run_chunks(True))
    pl.when(jnp.logical_not(bounded))(lambda: run_chunks(False))


def _hgrn(proj3, f_pre3, lb_logits, norm_g, sb):
    bsz, s, _ = proj3.shape
    hg_width = norm_g.shape[0]
    heads = hg_width // HEAD_DIM

    def sect(k):
        return pl.BlockSpec((None, sb, hg_width), lambda b, j, k=k: (b, j, k))

    return pl.pallas_call(
        functools.partial(_hgrn_kernel, n_chunks=sb // CHUNK),
        grid=(bsz, s // sb),
        in_specs=[sect(0), sect(0), sect(2), sect(3),
                  pl.BlockSpec((lb_logits.shape[0], hg_width), lambda b, j: (0, 0)),
                  pl.BlockSpec((1, hg_width), lambda b, j: (0, 0))],
        out_specs=pl.BlockSpec((None, sb, hg_width), lambda b, j: (b, j, 0)),
        out_shape=jax.ShapeDtypeStruct((bsz, s, hg_width), BF16),
        scratch_shapes=[pltpu.VMEM((heads, HEAD_DIM, HEAD_DIM), F32)],
        compiler_params=_cparams(("parallel", "arbitrary")),
        name="hgrn",
    )(proj3, f_pre3, proj3, proj3, lb_logits, norm_g.reshape(1, hg_width))


def _gmlp_kernel(u_ref, v_ref, vg_ref, vb_ref, ws_ref, bs_ref, og_ref, o_ref, *, n_blocks):
    groups = ws_ref.shape[0]
    gdim = u_ref.shape[-1] // groups
    row = lax.broadcasted_iota(jnp.int32, (GM_BLOCK, GM_BLOCK), 0) // CHUNK
    col = lax.broadcasted_iota(jnp.int32, (GM_BLOCK, GM_BLOCK), 1) // CHUNK
    allowed = row >= col
    ws = [jnp.where(allowed, ws_ref[g], 0.0).astype(BF16) for g in range(groups)]
    for n in range(n_blocks):
        rows = pl.ds(n * GM_BLOCK, GM_BLOCK)
        u = _gelu(u_ref[rows, :].astype(F32))
        v = _layer_norm(_gelu(v_ref[rows, :].astype(F32)), vg_ref[...], vb_ref[...])
        v_bf = v.astype(BF16)
        mixed = jnp.concatenate(
            [_dot(ws[g], v_bf[:, g * gdim:(g + 1) * gdim]) for g in range(groups)], axis=1)
        y = u * (mixed + bs_ref[...])
        inv = lax.rsqrt(jnp.mean(y * y, axis=-1, keepdims=True) + RMS_EPS)
        o_ref[rows, :] = (y * inv * og_ref[...]).astype(o_ref.dtype)


def _gmlp(proj3, v_norm_g, v_norm_b, w_s, b_s, out_norm_g, u_blk, n_blocks):
    bsz, s, _ = proj3.shape
    gw = v_norm_g.shape[0]
    groups = w_s.shape[0]
    rows = n_blocks * GM_BLOCK
    bias_full = jnp.repeat(b_s.T, gw // groups, axis=1)
    return pl.pallas_call(
        functools.partial(_gmlp_kernel, n_blocks=n_blocks),
        grid=(bsz, s // rows),
        in_specs=[pl.BlockSpec((None, rows, gw), lambda b, j: (b, j, u_blk)),
                  pl.BlockSpec((None, rows, gw), lambda b, j: (b, j, u_blk + 1)),
                  pl.BlockSpec((1, gw), lambda b, j: (0, 0)),
                  pl.BlockSpec((1, gw), lambda b, j: (0, 0)),
                  pl.BlockSpec((groups, GM_BLOCK, GM_BLOCK), lambda b, j: (0, 0, 0)),
                  pl.BlockSpec((GM_BLOCK, gw), lambda b, j: (0, 0)),
                  pl.BlockSpec((1, gw), lambda b, j: (0, 0))],
        out_specs=pl.BlockSpec((None, rows, gw), lambda b, j: (b, j, 0)),
        out_shape=jax.ShapeDtypeStruct((bsz, s, gw), BF16),
        compiler_params=_cparams(("parallel", "parallel")),
        name="gmlp",
    )(proj3, proj3, v_norm_g.reshape(1, gw), v_norm_b.reshape(1, gw), w_s, bias_full,
      out_norm_g.reshape(1, gw))


def _out_proj_kernel(yh_ref, yg_ref, x_ref, g0_ref, b0_ref, wa_ref, wb_ref, g_ref, b_ref, wr_ref,
                     h1_ref, hp_ref, lg_ref, *, alpha):
    mix = _dot(yh_ref[...], wa_ref[...]) + _dot(yg_ref[...], wb_ref[...])
    h0 = _layer_norm(x_ref[...], g0_ref[...], b0_ref[...])
    h1 = _layer_norm(alpha * h0 + mix, g_ref[...], b_ref[...])
    h1_ref[...] = h1
    hp_ref[...] = _pack_halves(h1)
    lg_ref[...] = _dot_nt(wr_ref[...], h1, precision=lax.Precision.HIGHEST)


def _out_proj(y_hg, y_gm, x2, g0, b0, w_out_bf, g, b, w_router_t, alpha, tm):
    t, d = x2.shape
    hw = y_hg.shape[1]
    gw = y_gm.shape[1]
    e = w_router_t.shape[0]
    return pl.pallas_call(
        functools.partial(_out_proj_kernel, alpha=alpha),
        grid=(t // tm,),
        in_specs=[pl.BlockSpec((tm, hw), lambda i: (i, 0)),
                  pl.BlockSpec((tm, gw), lambda i: (i, 0)),
                  pl.BlockSpec((tm, d), lambda i: (i, 0)),
                  pl.BlockSpec((1, d), lambda i: (0, 0)),
                  pl.BlockSpec((1, d), lambda i: (0, 0)),
                  pl.BlockSpec((hw, d), lambda i: (0, 0)),
                  pl.BlockSpec((gw, d), lambda i: (0, 0)),
                  pl.BlockSpec((1, d), lambda i: (0, 0)),
                  pl.BlockSpec((1, d), lambda i: (0, 0)),
                  pl.BlockSpec((e, d), lambda i: (0, 0))],
        out_specs=[pl.BlockSpec((tm, d), lambda i: (i, 0)),
                   pl.BlockSpec((tm, d // 2), lambda i: (i, 0)),
                   pl.BlockSpec((e, tm), lambda i: (0, i))],
        out_shape=[jax.ShapeDtypeStruct((t, d), F32), jax.ShapeDtypeStruct((t, d // 2), jnp.uint32),
                   jax.ShapeDtypeStruct((e, t), F32)],
        compiler_params=_cparams(("parallel",)),
        name="out_proj",
    )(y_hg, y_gm, x2, g0.reshape(1, d), b0.reshape(1, d), w_out_bf[:hw], w_out_bf[hw:],
      g.reshape(1, d), b.reshape(1, d), w_router_t)


def _route_kernel(lg_ref, bias_ref, eidx_ref, w_ref, rank_ref, cnt_ref, carry_ref):
    @pl.when(pl.program_id(0) == 0)
    def _():
        carry_ref[...] = jnp.zeros_like(carry_ref)

    n_exp, tt = lg_ref.shape
    per_group = n_exp // N_GROUPS
    neg_inf = jnp.float32(-jnp.inf)
    scores = jax.nn.sigmoid(lg_ref[...])
    biased = scores + bias_ref[...]

    gio = lax.broadcasted_iota(jnp.int32, (per_group, tt), 0)
    blocks, gs_rows = [], []
    for g in range(N_GROUPS):
        blk = biased[g * per_group:(g + 1) * per_group, :]
        m1 = jnp.max(blk, axis=0, keepdims=True)
        first = jnp.min(jnp.where(blk == m1, gio, per_group), axis=0, keepdims=True)
        m2 = jnp.max(jnp.where(gio == first, neg_inf, blk), axis=0, keepdims=True)
        blocks.append(blk)
        gs_rows.append(m1 + m2)
    gs = jnp.concatenate(gs_rows, axis=0)
    gidx = lax.broadcasted_iota(jnp.int32, (N_GROUPS, tt), 0)
    beaten = jnp.zeros((N_GROUPS, tt), jnp.int32)
    for g in range(N_GROUPS):
        r = gs_rows[g]
        beaten = beaten + jnp.where((r > gs) | ((r == gs) & (g < gidx)), 1, 0)
    keep = beaten < TOPK_GROUPS
    cand = jnp.concatenate(
        [jnp.where(keep[g:g + 1, :], blocks[g], neg_inf) for g in range(N_GROUPS)], axis=0)

    eio = lax.broadcasted_iota(jnp.int32, (n_exp, tt), 0)
    idx_rows, w_rows = [], []
    chosen = jnp.zeros((n_exp, tt), F32)
    for _ in range(TOP_K):
        m = jnp.max(cand, axis=0, keepdims=True)
        idx = jnp.min(jnp.where(cand == m, eio, n_exp), axis=0, keepdims=True)
        sel = eio == idx
        w_rows.append(jnp.sum(jnp.where(sel, scores, 0.0), axis=0, keepdims=True))
        idx_rows.append(idx)
        cand = jnp.where(sel, neg_inf, cand)
        chosen = jnp.where(sel, 1.0, chosen)
    eidx = jnp.concatenate(idx_rows, axis=0)
    w = jnp.concatenate(w_rows, axis=0)
    eidx_ref[...] = eidx
    w_ref[...] = w / jnp.sum(w, axis=0, keepdims=True) * ROUTED_SCALE

    tr = lax.broadcasted_iota(jnp.int32, (tt, tt), 0)
    tc = lax.broadcasted_iota(jnp.int32, (tt, tt), 1)
    before = (tr < tc).astype(BF16)
    cum = _dot(chosen.astype(BF16), before) + carry_ref[...]
    rank_ref[...] = jnp.concatenate(
        [jnp.sum(jnp.where(eio == idx_rows[k], cum, 0.0), axis=0, keepdims=True)
         for k in range(TOP_K)], axis=0).astype(jnp.int32)
    total = carry_ref[...] + jnp.sum(chosen, axis=1, keepdims=True)
    carry_ref[...] = total
    cnt_ref[...] = total.astype(jnp.int32)


def _route(logits_t, router_bias, tt):
    e, t = logits_t.shape
    return pl.pallas_call(
        _route_kernel,
        grid=(t // tt,),
        in_specs=[pl.BlockSpec((e, tt), lambda i: (0, i)),
                  pl.BlockSpec((e, 1), lambda i: (0, 0))],
        out_specs=[pl.BlockSpec((TOP_K, tt), lambda i: (0, i)),
                   pl.BlockSpec((TOP_K, tt), lambda i: (0, i)),
                   pl.BlockSpec((TOP_K, tt), lambda i: (0, i)),
                   pl.BlockSpec((e, 1), lambda i: (0, 0))],
        out_shape=[jax.ShapeDtypeStruct((TOP_K, t), jnp.int32),
                   jax.ShapeDtypeStruct((TOP_K, t), F32),
                   jax.ShapeDtypeStruct((TOP_K, t), jnp.int32),
                   jax.ShapeDtypeStruct((e, 1), jnp.int32)],
        scratch_shapes=[pltpu.VMEM((e, 1), F32)],
        compiler_params=_cparams(("arbitrary",)),
        name="route",
    )(logits_t, router_bias.astype(F32).reshape(e, 1))


def _dest_kernel(ps_ref, eidx_ref, rank_ref, dest_ref, *, n_exp):
    eidx = eidx_ref[...]

    def body(e, acc):
        return acc + jnp.where(eidx == e, ps_ref[e], 0)

    dest_ref[...] = lax.fori_loop(0, n_exp, body, rank_ref[...])


def _dest(pad_start, eidx, rank, tt):
    k, t = eidx.shape
    n_exp = pad_start.shape[0]
    return pl.pallas_call(
        functools.partial(_dest_kernel, n_exp=n_exp),
        grid_spec=pltpu.PrefetchScalarGridSpec(
            num_scalar_prefetch=1,
            grid=(t // tt,),
            in_specs=[pl.BlockSpec((k, tt), lambda i, ps: (0, i)),
                      pl.BlockSpec((k, tt), lambda i, ps: (0, i))],
            out_specs=pl.BlockSpec((k, tt), lambda i, ps: (0, i))),
        out_shape=jax.ShapeDtypeStruct((k, t), jnp.int32),
        compiler_params=_cparams(("parallel",)),
        name="dest",
    )(pad_start, eidx, rank)


def _chunked(dest, chunk):
    k, t = dest.shape
    return dest.reshape(k, t // chunk, chunk).transpose(1, 0, 2)


def _sc_workers():
    info = plsc.get_sparse_core_info()
    return info.num_cores, info.num_cores * info.num_subcores


def _dispatch(dest, h1, n_rows, chunk):
    t, dh = h1.shape
    n_cores, n_workers = _sc_workers()
    per_worker = t // chunk // n_workers
    mesh = plsc.VectorSubcoreMesh(core_axis_name="c", subcore_axis_name="s")

    @functools.partial(
        pl.kernel, mesh=mesh,
        out_type=jax.ShapeDtypeStruct((n_rows, dh), h1.dtype),
        scratch_types=[pltpu.VMEM((TOP_K, chunk), jnp.int32),
                       pltpu.VMEM((chunk, dh), h1.dtype),
                       pltpu.SemaphoreType.DMA],
        name="dispatch",
    )
    def scatter_rows(x_hbm, dest_hbm, o_hbm, idx_v, rows_v, sem):
        wid = lax.axis_index("s") * n_cores + lax.axis_index("c")

        @pl.loop(0, per_worker)
        def _(j):
            c = wid * per_worker + j
            pltpu.sync_copy(dest_hbm.at[c], idx_v)
            pltpu.sync_copy(x_hbm.at[pl.ds(c * chunk, chunk)], rows_v)
            copies = [pltpu.async_copy(rows_v, o_hbm.at[idx_v.at[k]], sem) for k in range(TOP_K)]
            for cp in copies:
                cp.wait()

    return scatter_rows(h1, _chunked(dest, chunk))


def _gather_back(dest_chunks, ys):
    n_chunks, k_top, chunk = dest_chunks.shape
    t = n_chunks * chunk
    dh = ys.shape[1]
    n_cores, n_workers = _sc_workers()
    per_worker = n_chunks // n_workers
    mesh = plsc.VectorSubcoreMesh(core_axis_name="c", subcore_axis_name="s")

    @functools.partial(
        pl.kernel, mesh=mesh,
        out_type=jax.ShapeDtypeStruct((k_top, t, dh), ys.dtype),
        scratch_types=[pltpu.VMEM((k_top, chunk), jnp.int32),
                       pltpu.VMEM((k_top, chunk, dh), ys.dtype),
                       pltpu.SemaphoreType.DMA],
        name="gather_back",
    )
    def gather_rows(ys_hbm, dest_hbm, o_hbm, idx_v, rows_v, sem):
        wid = lax.axis_index("s") * n_cores + lax.axis_index("c")

        @pl.loop(0, per_worker)
        def _(j):
            c = wid * per_worker + j
            pltpu.sync_copy(dest_hbm.at[c], idx_v)
            copies = [pltpu.async_copy(ys_hbm.at[idx_v.at[k]], rows_v.at[k], sem)
                      for k in range(k_top)]
            for cp in copies:
                cp.wait()
            for k in range(k_top):
                pltpu.sync_copy(rows_v.at[k], o_hbm.at[k, pl.ds(c * chunk, chunk)])

    return gather_rows(ys, dest_chunks)


def _experts_kernel(bs_ref, cnt_ref, wgu_hbm, wd_hbm, xs_hbm, ys_hbm, wgu_buf, wd_buf, gu_bf, d_bf,
                    xbuf, ybuf, wsem, xsem, ysem):
    e = pl.program_id(0)
    n_exp = pl.num_programs(0)
    n_slots, mb, _ = xbuf.shape
    ahead = n_slots - 1
    w_slots = wgu_buf.shape[0]
    n_blocks = bs_ref[n_exp]
    b0 = bs_ref[e]
    b1 = bs_ref[e + 1]
    count = cnt_ref[e]

    def w_copies(ex):
        slot = lax.rem(ex, w_slots)
        return (pltpu.make_async_copy(wgu_hbm.at[ex], wgu_buf.at[slot], wsem.at[slot]),
                pltpu.make_async_copy(wd_hbm.at[ex], wd_buf.at[slot], wsem.at[slot]))

    @pl.when(e == 0)
    def _():
        for i in range(w_slots - 1):
            @pl.when(i < n_exp)
            def _():
                for cp in w_copies(i):
                    cp.start()

    @pl.when(e + w_slots - 1 < n_exp)
    def _():
        for cp in w_copies(e + w_slots - 1):
            cp.start()

    def x_copy(b):
        slot = lax.rem(b, n_slots)
        return pltpu.make_async_copy(xs_hbm.at[pl.ds(pl.multiple_of(b * mb, mb), mb)],
                                     xbuf.at[slot], xsem.at[slot])

    def y_copy(b):
        slot = lax.rem(b, n_slots)
        return pltpu.make_async_copy(ybuf.at[slot],
                                     ys_hbm.at[pl.ds(pl.multiple_of(b * mb, mb), mb)], ysem.at[slot])

    @pl.when(e == 0)
    def _():
        for i in range(ahead):
            @pl.when(i < n_blocks)
            def _():
                x_copy(i).start(priority=RING_DMA_PRIORITY)

    for cp in w_copies(e):
        cp.wait()
    w_slot = lax.rem(e, w_slots)

    @pl.when(b1 > b0)
    def _():
        gu_bf[...] = wgu_buf[w_slot].astype(BF16)
        d_bf[...] = wd_buf[w_slot].astype(BF16)

    kh = gu_bf.shape[0] // 2
    half = gu_bf.shape[1] // 2

    def block(b, carry):
        slot = lax.rem(b, n_slots)
        x_copy(b).wait()

        @pl.when(b + ahead < n_blocks)
        def _():
            x_copy(b + ahead).start(priority=RING_DMA_PRIORITY)

        @pl.when(b >= n_slots)
        def _():
            y_copy(b - n_slots).wait()

        rows = lax.broadcasted_iota(jnp.int32, (mb, xbuf.shape[2]), 0)
        valid = count - (b - b0) * mb
        words = jnp.where(rows < valid, xbuf[slot], jnp.uint32(0))
        xa, xb = _unpack_halves(words)
        gu = _dot(xa.astype(BF16), gu_bf[0:kh, :]) + _dot(xb.astype(BF16), gu_bf[kh:, :])
        hid = _silu(gu[:, :half]) * gu[:, half:]
        ybuf[slot] = _pack_halves(_dot(hid.astype(BF16), d_bf[...]))
        y_copy(b).start(priority=RING_DMA_PRIORITY)
        return carry

    lax.fori_loop(b0, b1, block, 0)

    @pl.when(e == n_exp - 1)
    def _():
        for i in range(1, n_slots + 1):
            @pl.when(n_blocks >= i)
            def _():
                y_copy(n_blocks - i).wait()


def _experts(block_start, counts, xs, w_gu, w_down):
    n_rows, dh = xs.shape
    n_exp, d, gu_w = w_gu.shape
    ed = w_down.shape[1]
    return pl.pallas_call(
        _experts_kernel,
        grid_spec=pltpu.PrefetchScalarGridSpec(
            num_scalar_prefetch=2,
            grid=(n_exp,),
            in_specs=[pl.BlockSpec(memory_space=pl.ANY),
                      pl.BlockSpec(memory_space=pl.ANY),
                      pl.BlockSpec(memory_space=pl.ANY)],
            out_specs=pl.BlockSpec(memory_space=pl.ANY),
            scratch_shapes=[pltpu.VMEM((WEIGHT_SLOTS, d, gu_w), w_gu.dtype),
                            pltpu.VMEM((WEIGHT_SLOTS, ed, d), w_down.dtype),
                            pltpu.VMEM((d, gu_w), BF16), pltpu.VMEM((ed, d), BF16),
                            pltpu.VMEM((RING_SLOTS, MOE_BLOCK, dh), jnp.uint32),
                            pltpu.VMEM((RING_SLOTS, MOE_BLOCK, dh), jnp.uint32),
                            pltpu.SemaphoreType.DMA((WEIGHT_SLOTS,)),
                            pltpu.SemaphoreType.DMA((RING_SLOTS,)),
                            pltpu.SemaphoreType.DMA((RING_SLOTS,))]),
        out_shape=jax.ShapeDtypeStruct((n_rows, dh), jnp.uint32),
        compiler_params=_cparams(("arbitrary",)),
        name="experts",
    )(block_start, counts, w_gu, w_down, xs)


def _shared_kernel(h_ref, wsg_ref, wsd_ref, o_ref):
    gu = _dot(h_ref[...].astype(BF16), wsg_ref[...])
    half = gu.shape[1] // 2
    o_ref[...] = _dot((_silu(gu[:, :half]) * gu[:, half:]).astype(BF16), wsd_ref[...]).astype(o_ref.dtype)


def _shared(h1, ws_gu_bf, ws_down_bf, tm):
    t, d = h1.shape
    sg = ws_gu_bf.shape[1]
    sd = ws_down_bf.shape[0]
    return pl.pallas_call(
        _shared_kernel,
        grid=(t // tm,),
        in_specs=[pl.BlockSpec((tm, d), lambda i: (i, 0)),
                  pl.BlockSpec((d, sg), lambda i: (0, 0)),
                  pl.BlockSpec((sd, d), lambda i: (0, 0))],
        out_specs=pl.BlockSpec((tm, d), lambda i: (i, 0)),
        out_shape=jax.ShapeDtypeStruct((t, d), BF16),
        compiler_params=_cparams(("parallel",)),
        name="shared",
    )(h1, ws_gu_bf, ws_down_bf)


def _combine_kernel(h_ref, sh_ref, w_ref, g_ref, b_ref, yg_ref, *rest, alpha):
    o_ref = rest[-1]
    acc = alpha * h_ref[...] + sh_ref[...].astype(F32)
    dh = acc.shape[1] // 2
    acc_a = acc[:, :dh]
    acc_b = acc[:, dh:]
    w = w_ref[...]
    for k in range(TOP_K):
        ya, yb = _unpack_halves(yg_ref[k])
        acc_a = acc_a + ya * w[:, k:k + 1]
        acc_b = acc_b + yb * w[:, k:k + 1]
    o_ref[...] = _layer_norm(jnp.concatenate([acc_a, acc_b], axis=1), g_ref[...], b_ref[...])


def _combine(h1, shared, w_tk, g, b, yg, prev, part, alpha, tc):
    t, d = h1.shape
    tiles = yg.shape[1] // tc
    off = part * tiles
    in_specs = [pl.BlockSpec((tc, d), lambda i: (i + off, 0)),
                pl.BlockSpec((tc, d), lambda i: (i + off, 0)),
                pl.BlockSpec((tc, TOP_K), lambda i: (i + off, 0)),
                pl.BlockSpec((1, d), lambda i: (0, 0)),
                pl.BlockSpec((1, d), lambda i: (0, 0)),
                pl.BlockSpec((TOP_K, tc, yg.shape[2]), lambda i: (0, i, 0))]
    args = [h1, shared, w_tk, g.reshape(1, d), b.reshape(1, d), yg]
    aliases = {}
    if prev is not None:
        in_specs.append(pl.BlockSpec(memory_space=pl.ANY))
        args.append(prev)
        aliases = {len(args) - 1: 0}
    return pl.pallas_call(
        functools.partial(_combine_kernel, alpha=alpha),
        grid=(tiles,),
        in_specs=in_specs,
        out_specs=pl.BlockSpec((tc, d), lambda i: (i + off, 0)),
        out_shape=jax.ShapeDtypeStruct((t, d), F32),
        input_output_aliases=aliases,
        compiler_params=_cparams(("parallel",)),
        name="combine",
    )(*args)


def _moe(h1, h1_packed, logits_t, router_bias, w_exp_gu, w_exp_down, w_shared_gu, w_shared_down,
         ln_g, ln_b, alpha):
    t, d = h1.shape
    n_exp = logits_t.shape[0]
    eidx, w_kt, rank, counts = _route(logits_t, router_bias, tt=min(512, t))

    counts = counts.reshape(n_exp)
    blocks_per_expert = (counts + MOE_BLOCK - 1) // MOE_BLOCK
    block_start = jnp.concatenate(
        [jnp.zeros((1,), jnp.int32), jnp.cumsum(blocks_per_expert).astype(jnp.int32)])
    pad_start = block_start[:n_exp] * MOE_BLOCK
    n_blocks = -(-(t * TOP_K + n_exp * (MOE_BLOCK - 1)) // MOE_BLOCK)

    dest = _dest(pad_start, eidx, rank, tt=min(2048, t))
    xs = _dispatch(dest, h1_packed, n_blocks * MOE_BLOCK, chunk=DISPATCH_CHUNK)
    shared = _shared(h1, w_shared_gu.astype(BF16), w_shared_down.astype(BF16), tm=min(512, t))
    ys = _experts(block_start, counts, xs, w_exp_gu, w_exp_down)
    dest_chunks = _chunked(dest, GATHER_CHUNK)
    per_part = dest_chunks.shape[0] // COMBINE_PARTS
    w_tk = w_kt.T
    out = None
    for part in range(COMBINE_PARTS):
        yg = _gather_back(dest_chunks[part * per_part:(part + 1) * per_part], ys)
        out = _combine(h1, shared, w_tk, ln_g, ln_b, yg, out, part, alpha, tc=min(256, t))
    return out


def kernel(x, ln_in_g, ln_in_b, w_in, hg_lb_logits, hg_norm_g, gm_v_norm_g, gm_v_norm_b, gm_w_s, gm_b_s, gm_out_norm_g, w_out, ln1_g, ln1_b, w_router, router_bias, w_exp_gu, w_exp_down, w_shared_gu, w_shared_down, ln2_g, ln2_b):
    bsz, s, d = x.shape
    depth = w_in.shape[0]
    assert depth == 1, "the lower-bound table row used in the hgrn kernel assumes one layer"
    alpha = (2.0 * depth) ** 0.25
    t = bsz * s
    hg_width = hg_norm_g.shape[1]
    gm_width = gm_v_norm_g.shape[1]
    tm = min(512, t)

    x2 = x.reshape(t, d)
    proj, f_pre = _in_proj(x2, ln_in_g, ln_in_b, w_in[0].astype(BF16), hg_width, hg_width, tm)
    proj3 = proj.reshape(bsz, s, proj.shape[1])
    y_hg = _hgrn(proj3, f_pre.reshape(bsz, s, hg_width), hg_lb_logits.astype(F32), hg_norm_g[0],
                 sb=min(512, s))
    y_gm = _gmlp(proj3, gm_v_norm_g[0], gm_v_norm_b[0], gm_w_s[0], gm_b_s[0], gm_out_norm_g[0],
                 u_blk=4 * hg_width // gm_width, n_blocks=min(4, s // GM_BLOCK))
    h1, h1_packed, logits_t = _out_proj(y_hg.reshape(t, hg_width), y_gm.reshape(t, gm_width), x2,
                                        ln_in_g, ln_in_b, w_out[0].astype(BF16), ln1_g[0], ln1_b[0],
                                        w_router[0].T, alpha, tm)
    out = _moe(h1, h1_packed, logits_t, router_bias[0], w_exp_gu[0], w_exp_down[0], w_shared_gu[0],
               w_shared_down[0], ln2_g[0], ln2_b[0], alpha)
    return out.reshape(bsz, s, d)
```

```python
import functools

import jax
import jax.numpy as jnp
from jax import lax
from jax.experimental import pallas as pl
from jax.experimental.pallas import tpu as pltpu
from jax.experimental.pallas import tpu_sc as plsc

F32 = jnp.float32
BF16 = jnp.bfloat16

LN_EPS = 1e-5
RMS_EPS = 1e-6
CHUNK = 64
SUB = 16
MAX_FACTORED_EXPONENT = 80.0
FACTORED_MIN_LB = float(2.718281828459045 ** (-MAX_FACTORED_EXPONENT / CHUNK))
HEAD_DIM = 128
GM_BLOCK = 128
TOP_K = 8
N_GROUPS = 8
TOPK_GROUPS = 4
ROUTED_SCALE = 2.5
MOE_BLOCK = 128
WEIGHT_SLOTS = 3
RING_SLOTS = 4
RING_DMA_PRIORITY = 1
DISPATCH_CHUNK = 64
COMBINE_PARTS = 4
GATHER_CHUNK = 16
VMEM_LIMIT = 48 * 1024 * 1024


def _cparams(sem):
    return pltpu.CompilerParams(dimension_semantics=sem, vmem_limit_bytes=VMEM_LIMIT)


def _layer_norm(x, g, b):
    mu = jnp.mean(x, axis=-1, keepdims=True)
    xc = x - mu
    var = jnp.mean(xc * xc, axis=-1, keepdims=True)
    return xc * lax.rsqrt(var + LN_EPS) * g + b


def _silu(x):
    return x * jax.nn.sigmoid(x)


def _gelu(x):
    return 0.5 * x * (1.0 + lax.erf(x * (2.0 ** -0.5)))


def _dot(a, b):
    return jnp.dot(a, b, preferred_element_type=F32)


def _dot_nt(a, b, precision=None):
    return lax.dot_general(a, b, (((1,), (1,)), ((), ())), preferred_element_type=F32,
                           precision=precision)


def _dot_tn(a, b):
    return lax.dot_general(a, b, (((0,), (0,)), ((), ())), preferred_element_type=F32)


def _split_bf16(x):
    hi = x.astype(BF16)
    return hi, (x - hi.astype(F32)).astype(BF16)


def _pack_halves(x):
    n = x.shape[1] // 2
    hi = lax.bitcast_convert_type(x[:, :n].astype(BF16).astype(F32), jnp.uint32)
    lo = lax.bitcast_convert_type(x[:, n:].astype(BF16).astype(F32), jnp.uint32)
    return hi | (lo >> 16)


def _unpack_halves(w):
    a = lax.bitcast_convert_type(w & jnp.uint32(0xFFFF0000), F32)
    b = lax.bitcast_convert_type(w << 16, F32)
    return a, b


def _in_proj_kernel(x_ref, g_ref, b_ref, w_ref, p_ref, f_ref, *, f_lo):
    h = _layer_norm(x_ref[...], g_ref[...], b_ref[...])
    p = _dot(h.astype(BF16), w_ref[...])
    p_ref[...] = p.astype(BF16)
    f_ref[...] = p[:, f_lo:f_lo + f_ref.shape[1]]


def _in_proj(x2, g, b, w_bf, f_lo, f_width, tm):
    t, d = x2.shape
    n = w_bf.shape[1]
    return pl.pallas_call(
        functools.partial(_in_proj_kernel, f_lo=f_lo),
        grid=(t // tm,),
        in_specs=[pl.BlockSpec((tm, d), lambda i: (i, 0)),
                  pl.BlockSpec((1, d), lambda i: (0, 0)),
                  pl.BlockSpec((1, d), lambda i: (0, 0)),
                  pl.BlockSpec((d, n), lambda i: (0, 0))],
        out_specs=[pl.BlockSpec((tm, n), lambda i: (i, 0)),
                   pl.BlockSpec((tm, f_width), lambda i: (i, 0))],
        out_shape=[jax.ShapeDtypeStruct((t, n), BF16), jax.ShapeDtypeStruct((t, f_width), F32)],
        compiler_params=_cparams(("parallel",)),
        name="in_proj",
    )(x2, g.reshape(1, d), b.reshape(1, d), w_bf)


def _hgrn_kernel(q_ref, f_ref, i_ref, g_ref, lbl_ref, gn_ref, o_ref, st_ref, *, n_chunks):
    @pl.when(pl.program_id(1) == 0)
    def _():
        st_ref[...] = jnp.zeros_like(st_ref)

    lg = lbl_ref[...]
    ex = jnp.exp(lg - jnp.max(lg, axis=0, keepdims=True))
    lb_all = ex[0:1, :] / jnp.sum(ex, axis=0, keepdims=True)
    gn_all = gn_ref[...]

    c = CHUNK
    heads = st_ref.shape[0]
    row = lax.broadcasted_iota(jnp.int32, (c, c), 0)
    col = lax.broadcasted_iota(jnp.int32, (c, c), 1)
    tril = (row >= col).astype(BF16)
    sub_row = lax.broadcasted_iota(jnp.int32, (SUB, HEAD_DIM), 0)
    ones = jnp.ones((HEAD_DIM, HEAD_DIM), BF16)
    neg_inf = jnp.float32(-jnp.inf)

    def intra_pairwise(q, k, v, v_bf, b, o_inter):
        outs = []
        for blk in range(c // SUB):
            lo = blk * SUB
            b_i = b[lo:lo + SUB, :]
            q_i = q[lo:lo + SUB, :]
            k_i = k[lo:lo + SUB, :]
            v_i = v[lo:lo + SUB, :]
            prods = []
            for j in range(SUB):
                diff = jnp.where(sub_row >= j, b_i - b_i[j:j + 1, :], neg_inf)
                prods.append(q_i * k_i[j:j + 1, :] * jnp.exp(diff))
            p_all = jnp.concatenate(prods, axis=0).astype(BF16)
            r_all = _dot(p_all, ones)
            o_blk = o_inter[lo:lo + SUB, :]
            for j in range(SUB):
                o_blk = o_blk + r_all[j * SUB:(j + 1) * SUB, :] * v_i[j:j + 1, :]
            if blk > 0:
                b_ref = b[lo - 1:lo, :]
                q_hat = (q_i * jnp.exp(b_i - b_ref)).astype(BF16)
                k_hat = (k[0:lo, :] * jnp.exp(b_ref - b[0:lo, :])).astype(BF16)
                scores = _dot_nt(q_hat, k_hat)
                o_blk = o_blk + _dot(scores.astype(BF16), v_bf[0:lo, :])
            outs.append(o_blk)
        return jnp.concatenate(outs, axis=0)

    def one_chunk(r0, factored):
        rows = pl.ds(r0, c)
        hs = [slice(h * HEAD_DIM, (h + 1) * HEAD_DIM) for h in range(heads)]
        q = _silu(q_ref[rows, :].astype(F32))
        f = lb_all + (1.0 - lb_all) * jax.nn.sigmoid(f_ref[rows, :])
        v_bf = i_ref[rows, :]
        v = v_bf.astype(F32)
        lf = jnp.log(f)
        k = 1.0 - f
        lf_hi, lf_lo = _split_bf16(lf)
        b2 = _dot(tril, jnp.concatenate([lf_hi, lf_lo], axis=1))
        width = lf.shape[1]
        b = b2[:, :width] + b2[:, width:]
        b_last = b[c - 1:c, :]
        q_dec = (q * jnp.exp(b)).astype(BF16)
        k_dec = (k * jnp.exp(b_last - b)).astype(BF16)
        decay = jnp.exp(b_last)
        st = [st_ref[h] for h in range(heads)]
        st_bf = [s.astype(BF16) for s in st]
        if factored:
            k_grow = (k * jnp.exp(-b)).astype(BF16)
            both = [_dot_nt(q_dec[:, hs[h]], jnp.concatenate([st_bf[h], k_grow[:, hs[h]]], axis=0))
                    for h in range(heads)]
            scores = [jnp.where(row >= col, both[h][:, HEAD_DIM:], 0.0).astype(BF16)
                      for h in range(heads)]
            outs = [both[h][:, :HEAD_DIM] + _dot(scores[h], v_bf[:, hs[h]]) for h in range(heads)]
        else:
            outs = [intra_pairwise(q[:, hs[h]], k[:, hs[h]], v[:, hs[h]], v_bf[:, hs[h]],
                                   b[:, hs[h]], _dot_nt(q_dec[:, hs[h]], st_bf[h]))
                    for h in range(heads)]
        for h in range(heads):
            st_ref[h] = st[h] * decay[:, hs[h]] + _dot_tn(v_bf[:, hs[h]], k_dec[:, hs[h]])
        inv = [lax.rsqrt(jnp.mean(o * o, axis=-1, keepdims=True) + RMS_EPS) for o in outs]
        o = jnp.concatenate([outs[h] * inv[h] for h in range(heads)], axis=1)
        o_ref[rows, :] = (o * gn_all * _silu(g_ref[rows, :].astype(F32))).astype(o_ref.dtype)

    def run_chunks(factored):
        per_trip = 2 if (factored and n_chunks % 2 == 0) else 1

        def chunk_body(ci, carry):
            for u in range(per_trip):
                one_chunk(pl.multiple_of((ci * per_trip + u) * c, c), factored)
            return carry

        lax.fori_loop(0, n_chunks // per_trip, chunk_body, 0)

    bounded = jnp.min(lb_all) >= FACTORED_MIN_LB
    pl.when(bounded)(lambda: run_chunks(True))
    pl.when(jnp.logical_not(bounded))(lambda: run_chunks(False))


def _hgrn(proj3, f_pre3, lb_logits, norm_g, sb):
    bsz, s, _ = proj3.shape
    hg_width = norm_g.shape[0]
    heads = hg_width // HEAD_DIM

    def sect(k):
        return pl.BlockSpec((None, sb, hg_width), lambda b, j, k=k: (b, j, k))

    return pl.pallas_call(
        functools.partial(_hgrn_kernel, n_chunks=sb // CHUNK),
        grid=(bsz, s // sb),
        in_specs=[sect(0), sect(0), sect(2), sect(3),
                  pl.BlockSpec((lb_logits.shape[0], hg_width), lambda b, j: (0, 0)),
                  pl.BlockSpec((1, hg_width), lambda b, j: (0, 0))],
        out_specs=pl.BlockSpec((None, sb, hg_width), lambda b, j: (b, j, 0)),
        out_shape=jax.ShapeDtypeStruct((bsz, s, hg_width), BF16),
        scratch_shapes=[pltpu.VMEM((heads, HEAD_DIM, HEAD_DIM), F32)],
        compiler_params=_cparams(("parallel", "arbitrary")),
        name="hgrn",
    )(proj3, f_pre3, proj3, proj3, lb_logits, norm_g.reshape(1, hg_width))


def _gmlp_kernel(u_ref, v_ref, vg_ref, vb_ref, ws_ref, bs_ref, og_ref, o_ref, *, n_blocks):
    groups = ws_ref.shape[0]
    gdim = u_ref.shape[-1] // groups
    row = lax.broadcasted_iota(jnp.int32, (GM_BLOCK, GM_BLOCK), 0) // CHUNK
    col = lax.broadcasted_iota(jnp.int32, (GM_BLOCK, GM_BLOCK), 1) // CHUNK
    allowed = row >= col
    ws = [jnp.where(allowed, ws_ref[g], 0.0).astype(BF16) for g in range(groups)]
    for n in range(n_blocks):
        rows = pl.ds(n * GM_BLOCK, GM_BLOCK)
        u = _gelu(u_ref[rows, :].astype(F32))
        v = _layer_norm(_gelu(v_ref[rows, :].astype(F32)), vg_ref[...], vb_ref[...])
        v_bf = v.astype(BF16)
        mixed = jnp.concatenate(
            [_dot(ws[g], v_bf[:, g * gdim:(g + 1) * gdim]) for g in range(groups)], axis=1)
        y = u * (mixed + bs_ref[...])
        inv = lax.rsqrt(jnp.mean(y * y, axis=-1, keepdims=True) + RMS_EPS)
        o_ref[rows, :] = (y * inv * og_ref[...]).astype(o_ref.dtype)


def _gmlp(proj3, v_norm_g, v_norm_b, w_s, b_s, out_norm_g, u_blk, n_blocks):
    bsz, s, _ = proj3.shape
    gw = v_norm_g.shape[0]
    groups = w_s.shape[0]
    rows = n_blocks * GM_BLOCK
    bias_full = jnp.repeat(b_s.T, gw // groups, axis=1)
    return pl.pallas_call(
        functools.partial(_gmlp_kernel, n_blocks=n_blocks),
        grid=(bsz, s // rows),
        in_specs=[pl.BlockSpec((None, rows, gw), lambda b, j: (b, j, u_blk)),
                  pl.BlockSpec((None, rows, gw), lambda b, j: (b, j, u_blk + 1)),
                  pl.BlockSpec((1, gw), lambda b, j: (0, 0)),
                  pl.BlockSpec((1, gw), lambda b, j: (0, 0)),
                  pl.BlockSpec((groups, GM_BLOCK, GM_BLOCK), lambda b, j: (0, 0, 0)),
                  pl.BlockSpec((GM_BLOCK, gw), lambda b, j: (0, 0)),
                  pl.BlockSpec((1, gw), lambda b, j: (0, 0))],
        out_specs=pl.BlockSpec((None, rows, gw), lambda b, j: (b, j, 0)),
        out_shape=jax.ShapeDtypeStruct((bsz, s, gw), BF16),
        compiler_params=_cparams(("parallel", "parallel")),
        name="gmlp",
    )(proj3, proj3, v_norm_g.reshape(1, gw), v_norm_b.reshape(1, gw), w_s, bias_full,
      out_norm_g.reshape(1, gw))


def _out_proj_kernel(yh_ref, yg_ref, x_ref, g0_ref, b0_ref, wa_ref, wb_ref, g_ref, b_ref, wr_ref,
                     h1_ref, hp_ref, lg_ref, *, alpha):
    mix = _dot(yh_ref[...], wa_ref[...]) + _dot(yg_ref[...], wb_ref[...])
    h0 = _layer_norm(x_ref[...], g0_ref[...], b0_ref[...])
    h1 = _layer_norm(alpha * h0 + mix, g_ref[...], b_ref[...])
    h1_ref[...] = h1
    hp_ref[...] = _pack_halves(h1)
    w_hi, w_lo = _split_bf16(wr_ref[...])
    h_hi, h_lo = _split_bf16(h1)
    lg_ref[...] = _dot_nt(w_hi, h_hi) + (_dot_nt(w_hi, h_lo) + _dot_nt(w_lo, h_hi))


def _out_proj(y_hg, y_gm, x2, g0, b0, w_out_bf, g, b, w_router_t, alpha, tm):
    t, d = x2.shape
    hw = y_hg.shape[1]
    gw = y_gm.shape[1]
    e = w_router_t.shape[0]
    return pl.pallas_call(
        functools.partial(_out_proj_kernel, alpha=alpha),
        grid=(t // tm,),
        in_specs=[pl.BlockSpec((tm, hw), lambda i: (i, 0)),
                  pl.BlockSpec((tm, gw), lambda i: (i, 0)),
                  pl.BlockSpec((tm, d), lambda i: (i, 0)),
                  pl.BlockSpec((1, d), lambda i: (0, 0)),
                  pl.BlockSpec((1, d), lambda i: (0, 0)),
                  pl.BlockSpec((hw, d), lambda i: (0, 0)),
                  pl.BlockSpec((gw, d), lambda i: (0, 0)),
                  pl.BlockSpec((1, d), lambda i: (0, 0)),
                  pl.BlockSpec((1, d), lambda i: (0, 0)),
                  pl.BlockSpec((e, d), lambda i: (0, 0))],
        out_specs=[pl.BlockSpec((tm, d), lambda i: (i, 0)),
                   pl.BlockSpec((tm, d // 2), lambda i: (i, 0)),
                   pl.BlockSpec((e, tm), lambda i: (0, i))],
        out_shape=[jax.ShapeDtypeStruct((t, d), F32), jax.ShapeDtypeStruct((t, d // 2), jnp.uint32),
                   jax.ShapeDtypeStruct((e, t), F32)],
        compiler_params=_cparams(("parallel",)),
        name="out_proj",
    )(y_hg, y_gm, x2, g0.reshape(1, d), b0.reshape(1, d), w_out_bf[:hw], w_out_bf[hw:],
      g.reshape(1, d), b.reshape(1, d), w_router_t)


def _route_kernel(lg_ref, bias_ref, eidx_ref, w_ref, rank_ref, cnt_ref, carry_ref):
    @pl.when(pl.program_id(0) == 0)
    def _():
        carry_ref[...] = jnp.zeros_like(carry_ref)

    n_exp, tt = lg_ref.shape
    per_group = n_exp // N_GROUPS
    neg_inf = jnp.float32(-jnp.inf)
    scores = jax.nn.sigmoid(lg_ref[...])
    biased = scores + bias_ref[...]

    gio = lax.broadcasted_iota(jnp.int32, (per_group, tt), 0)
    blocks, gs_rows = [], []
    for g in range(N_GROUPS):
        blk = biased[g * per_group:(g + 1) * per_group, :]
        m1 = jnp.max(blk, axis=0, keepdims=True)
        first = jnp.min(jnp.where(blk == m1, gio, per_group), axis=0, keepdims=True)
        m2 = jnp.max(jnp.where(gio == first, neg_inf, blk), axis=0, keepdims=True)
        blocks.append(blk)
        gs_rows.append(m1 + m2)
    gs = jnp.concatenate(gs_rows, axis=0)
    gidx = lax.broadcasted_iota(jnp.int32, (N_GROUPS, tt), 0)
    beaten = jnp.zeros((N_GROUPS, tt), jnp.int32)
    for g in range(N_GROUPS):
        r = gs_rows[g]
        beaten = beaten + jnp.where((r > gs) | ((r == gs) & (g < gidx)), 1, 0)
    keep = beaten < TOPK_GROUPS
    cand = jnp.concatenate(
        [jnp.where(keep[g:g + 1, :], blocks[g], neg_inf) for g in range(N_GROUPS)], axis=0)

    eio = lax.broadcasted_iota(jnp.int32, (n_exp, tt), 0)
    idx_rows, w_rows = [], []
    chosen = jnp.zeros((n_exp, tt), F32)
    for _ in range(TOP_K):
        m = jnp.max(cand, axis=0, keepdims=True)
        idx = jnp.min(jnp.where(cand == m, eio, n_exp), axis=0, keepdims=True)
        sel = eio == idx
        w_rows.append(jnp.sum(jnp.where(sel, scores, 0.0), axis=0, keepdims=True))
        idx_rows.append(idx)
        cand = jnp.where(sel, neg_inf, cand)
        chosen = jnp.where(sel, 1.0, chosen)
    eidx = jnp.concatenate(idx_rows, axis=0)
    w = jnp.concatenate(w_rows, axis=0)
    eidx_ref[...] = eidx
    w_ref[...] = w / jnp.sum(w, axis=0, keepdims=True) * ROUTED_SCALE

    tr = lax.broadcasted_iota(jnp.int32, (tt, tt), 0)
    tc = lax.broadcasted_iota(jnp.int32, (tt, tt), 1)
    before = (tr < tc).astype(BF16)
    cum = _dot(chosen.astype(BF16), before) + carry_ref[...]
    rank_ref[...] = jnp.concatenate(
        [jnp.sum(jnp.where(eio == idx_rows[k], cum, 0.0), axis=0, keepdims=True)
         for k in range(TOP_K)], axis=0).astype(jnp.int32)
    total = carry_ref[...] + jnp.sum(chosen, axis=1, keepdims=True)
    carry_ref[...] = total
    cnt_ref[...] = total.astype(jnp.int32)


def _route(logits_t, router_bias, tt):
    e, t = logits_t.shape
    return pl.pallas_call(
        _route_kernel,
        grid=(t // tt,),
        in_specs=[pl.BlockSpec((e, tt), lambda i: (0, i)),
                  pl.BlockSpec((e, 1), lambda i: (0, 0))],
        out_specs=[pl.BlockSpec((TOP_K, tt), lambda i: (0, i)),
                   pl.BlockSpec((TOP_K, tt), lambda i: (0, i)),
                   pl.BlockSpec((TOP_K, tt), lambda i: (0, i)),
                   pl.BlockSpec((e, 1), lambda i: (0, 0))],
        out_shape=[jax.ShapeDtypeStruct((TOP_K, t), jnp.int32),
                   jax.ShapeDtypeStruct((TOP_K, t), F32),
                   jax.ShapeDtypeStruct((TOP_K, t), jnp.int32),
                   jax.ShapeDtypeStruct((e, 1), jnp.int32)],
        scratch_shapes=[pltpu.VMEM((e, 1), F32)],
        compiler_params=_cparams(("arbitrary",)),
        name="route",
    )(logits_t, router_bias.astype(F32).reshape(e, 1))


def _dest_kernel(ps_ref, eidx_ref, rank_ref, dest_ref, *, n_exp):
    eidx = eidx_ref[...]

    def body(e, acc):
        return acc + jnp.where(eidx == e, ps_ref[e], 0)

    dest_ref[...] = lax.fori_loop(0, n_exp, body, rank_ref[...])


def _dest(pad_start, eidx, rank, tt):
    k, t = eidx.shape
    n_exp = pad_start.shape[0]
    return pl.pallas_call(
        functools.partial(_dest_kernel, n_exp=n_exp),
        grid_spec=pltpu.PrefetchScalarGridSpec(
            num_scalar_prefetch=1,
            grid=(t // tt,),
            in_specs=[pl.BlockSpec((k, tt), lambda i, ps: (0, i)),
                      pl.BlockSpec((k, tt), lambda i, ps: (0, i))],
            out_specs=pl.BlockSpec((k, tt), lambda i, ps: (0, i))),
        out_shape=jax.ShapeDtypeStruct((k, t), jnp.int32),
        compiler_params=_cparams(("parallel",)),
        name="dest",
    )(pad_start, eidx, rank)


def _chunked(dest, chunk):
    k, t = dest.shape
    return dest.reshape(k, t // chunk, chunk).transpose(1, 0, 2)


def _sc_workers():
    info = plsc.get_sparse_core_info()
    return info.num_cores, info.num_cores * info.num_subcores


def _dispatch(dest, h1, n_rows, chunk):
    t, dh = h1.shape
    n_cores, n_workers = _sc_workers()
    per_worker = t // chunk // n_workers
    mesh = plsc.VectorSubcoreMesh(core_axis_name="c", subcore_axis_name="s")

    @functools.partial(
        pl.kernel, mesh=mesh,
        out_type=jax.ShapeDtypeStruct((n_rows, dh), h1.dtype),
        scratch_types=[pltpu.VMEM((TOP_K, chunk), jnp.int32),
                       pltpu.VMEM((chunk, dh), h1.dtype),
                       pltpu.SemaphoreType.DMA],
        name="dispatch",
    )
    def scatter_rows(x_hbm, dest_hbm, o_hbm, idx_v, rows_v, sem):
        wid = lax.axis_index("s") * n_cores + lax.axis_index("c")

        @pl.loop(0, per_worker)
        def _(j):
            c = wid * per_worker + j
            pltpu.sync_copy(dest_hbm.at[c], idx_v)
            pltpu.sync_copy(x_hbm.at[pl.ds(c * chunk, chunk)], rows_v)
            copies = [pltpu.async_copy(rows_v, o_hbm.at[idx_v.at[k]], sem) for k in range(TOP_K)]
            for cp in copies:
                cp.wait()

    return scatter_rows(h1, _chunked(dest, chunk))


def _gather_back(dest_chunks, ys):
    n_chunks, k_top, chunk = dest_chunks.shape
    t = n_chunks * chunk
    dh = ys.shape[1]
    n_cores, n_workers = _sc_workers()
    per_worker = n_chunks // n_workers
    mesh = plsc.VectorSubcoreMesh(core_axis_name="c", subcore_axis_name="s")

    @functools.partial(
        pl.kernel, mesh=mesh,
        out_type=jax.ShapeDtypeStruct((k_top, t, dh), ys.dtype),
        scratch_types=[pltpu.VMEM((k_top, chunk), jnp.int32),
                       pltpu.VMEM((k_top, chunk, dh), ys.dtype),
                       pltpu.SemaphoreType.DMA],
        name="gather_back",
    )
    def gather_rows(ys_hbm, dest_hbm, o_hbm, idx_v, rows_v, sem):
        wid = lax.axis_index("s") * n_cores + lax.axis_index("c")

        @pl.loop(0, per_worker)
        def _(j):
            c = wid * per_worker + j
            pltpu.sync_copy(dest_hbm.at[c], idx_v)
            copies = [pltpu.async_copy(ys_hbm.at[idx_v.at[k]], rows_v.at[k], sem)
                      for k in range(k_top)]
            for cp in copies:
                cp.wait()
            for k in range(k_top):
                pltpu.sync_copy(rows_v.at[k], o_hbm.at[k, pl.ds(c * chunk, chunk)])

    return gather_rows(ys, dest_chunks)


def _experts_kernel(bs_ref, cnt_ref, wgu_hbm, wd_hbm, xs_hbm, ys_hbm, wgu_buf, wd_buf, gu_bf, d_bf,
                    xbuf, ybuf, wsem, xsem, ysem):
    e = pl.program_id(0)
    n_exp = pl.num_programs(0)
    n_slots, mb, _ = xbuf.shape
    ahead = n_slots - 1
    w_slots = wgu_buf.shape[0]
    n_blocks = bs_ref[n_exp]
    b0 = bs_ref[e]
    b1 = bs_ref[e + 1]
    count = cnt_ref[e]

    def w_copies(ex):
        slot = lax.rem(ex, w_slots)
        return (pltpu.make_async_copy(wgu_hbm.at[ex], wgu_buf.at[slot], wsem.at[slot]),
                pltpu.make_async_copy(wd_hbm.at[ex], wd_buf.at[slot], wsem.at[slot]))

    @pl.when(e == 0)
    def _():
        for i in range(w_slots - 1):
            @pl.when(i < n_exp)
            def _():
                for cp in w_copies(i):
                    cp.start()

    @pl.when(e + w_slots - 1 < n_exp)
    def _():
        for cp in w_copies(e + w_slots - 1):
            cp.start()

    def x_copy(b):
        slot = lax.rem(b, n_slots)
        return pltpu.make_async_copy(xs_hbm.at[pl.ds(pl.multiple_of(b * mb, mb), mb)],
                                     xbuf.at[slot], xsem.at[slot])

    def y_copy(b):
        slot = lax.rem(b, n_slots)
        return pltpu.make_async_copy(ybuf.at[slot],
                                     ys_hbm.at[pl.ds(pl.multiple_of(b * mb, mb), mb)], ysem.at[slot])

    @pl.when(e == 0)
    def _():
        for i in range(ahead):
            @pl.when(i < n_blocks)
            def _():
                x_copy(i).start(priority=RING_DMA_PRIORITY)

    for cp in w_copies(e):
        cp.wait()
    w_slot = lax.rem(e, w_slots)

    @pl.when(b1 > b0)
    def _():
        gu_bf[...] = wgu_buf[w_slot].astype(BF16)
        d_bf[...] = wd_buf[w_slot].astype(BF16)

    kh = gu_bf.shape[0] // 2
    half = gu_bf.shape[1] // 2

    def block(b, carry):
        slot = lax.rem(b, n_slots)
        x_copy(b).wait()

        @pl.when(b + ahead < n_blocks)
        def _():
            x_copy(b + ahead).start(priority=RING_DMA_PRIORITY)

        @pl.when(b >= n_slots)
        def _():
            y_copy(b - n_slots).wait()

        rows = lax.broadcasted_iota(jnp.int32, (mb, xbuf.shape[2]), 0)
        valid = count - (b - b0) * mb
        words = jnp.where(rows < valid, xbuf[slot], jnp.uint32(0))
        xa, xb = _unpack_halves(words)
        gu = _dot(xa.astype(BF16), gu_bf[0:kh, :]) + _dot(xb.astype(BF16), gu_bf[kh:, :])
        hid = _silu(gu[:, :half]) * gu[:, half:]
        ybuf[slot] = _pack_halves(_dot(hid.astype(BF16), d_bf[...]))
        y_copy(b).start(priority=RING_DMA_PRIORITY)
        return carry

    lax.fori_loop(b0, b1, block, 0)

    @pl.when(e == n_exp - 1)
    def _():
        for i in range(1, n_slots + 1):
            @pl.when(n_blocks >= i)
            def _():
                y_copy(n_blocks - i).wait()


def _experts(block_start, counts, xs, w_gu, w_down):
    n_rows, dh = xs.shape
    n_exp, d, gu_w = w_gu.shape
    ed = w_down.shape[1]
    return pl.pallas_call(
        _experts_kernel,
        grid_spec=pltpu.PrefetchScalarGridSpec(
            num_scalar_prefetch=2,
            grid=(n_exp,),
            in_specs=[pl.BlockSpec(memory_space=pl.ANY),
                      pl.BlockSpec(memory_space=pl.ANY),
                      pl.BlockSpec(memory_space=pl.ANY)],
            out_specs=pl.BlockSpec(memory_space=pl.ANY),
            scratch_shapes=[pltpu.VMEM((WEIGHT_SLOTS, d, gu_w), w_gu.dtype),
                            pltpu.VMEM((WEIGHT_SLOTS, ed, d), w_down.dtype),
                            pltpu.VMEM((d, gu_w), BF16), pltpu.VMEM((ed, d), BF16),
                            pltpu.VMEM((RING_SLOTS, MOE_BLOCK, dh), jnp.uint32),
                            pltpu.VMEM((RING_SLOTS, MOE_BLOCK, dh), jnp.uint32),
                            pltpu.SemaphoreType.DMA((WEIGHT_SLOTS,)),
                            pltpu.SemaphoreType.DMA((RING_SLOTS,)),
                            pltpu.SemaphoreType.DMA((RING_SLOTS,))]),
        out_shape=jax.ShapeDtypeStruct((n_rows, dh), jnp.uint32),
        compiler_params=_cparams(("arbitrary",)),
        name="experts",
    )(block_start, counts, w_gu, w_down, xs)


def _shared_kernel(h_ref, wsg_ref, wsd_ref, o_ref):
    gu = _dot(h_ref[...].astype(BF16), wsg_ref[...])
    half = gu.shape[1] // 2
    o_ref[...] = _dot((_silu(gu[:, :half]) * gu[:, half:]).astype(BF16), wsd_ref[...]).astype(o_ref.dtype)


def _shared(h1, ws_gu_bf, ws_down_bf, tm):
    t, d = h1.shape
    sg = ws_gu_bf.shape[1]
    sd = ws_down_bf.shape[0]
    return pl.pallas_call(
        _shared_kernel,
        grid=(t // tm,),
        in_specs=[pl.BlockSpec((tm, d), lambda i: (i, 0)),
                  pl.BlockSpec((d, sg), lambda i: (0, 0)),
                  pl.BlockSpec((sd, d), lambda i: (0, 0))],
        out_specs=pl.BlockSpec((tm, d), lambda i: (i, 0)),
        out_shape=jax.ShapeDtypeStruct((t, d), BF16),
        compiler_params=_cparams(("parallel",)),
        name="shared",
    )(h1, ws_gu_bf, ws_down_bf)


def _combine_kernel(h_ref, sh_ref, w_ref, g_ref, b_ref, yg_ref, *rest, alpha):
    o_ref = rest[-1]
    acc = alpha * h_ref[...] + sh_ref[...].astype(F32)
    dh = acc.shape[1] // 2
    acc_a = acc[:, :dh]
    acc_b = acc[:, dh:]
    w = w_ref[...]
    for k in range(TOP_K):
        ya, yb = _unpack_halves(yg_ref[k])
        acc_a = acc_a + ya * w[:, k:k + 1]
        acc_b = acc_b + yb * w[:, k:k + 1]
    o_ref[...] = _layer_norm(jnp.concatenate([acc_a, acc_b], axis=1), g_ref[...], b_ref[...])


def _combine(h1, shared, w_tk, g, b, yg, prev, part, alpha, tc):
    t, d = h1.shape
    tiles = yg.shape[1] // tc
    off = part * tiles
    in_specs = [pl.BlockSpec((tc, d), lambda i: (i + off, 0)),
                pl.BlockSpec((tc, d), lambda i: (i + off, 0)),
                pl.BlockSpec((tc, TOP_K), lambda i: (i + off, 0)),
                pl.BlockSpec((1, d), lambda i: (0, 0)),
                pl.BlockSpec((1, d), lambda i: (0, 0)),
                pl.BlockSpec((TOP_K, tc, yg.shape[2]), lambda i: (0, i, 0))]
    args = [h1, shared, w_tk, g.reshape(1, d), b.reshape(1, d), yg]
    aliases = {}
    if prev is not None:
        in_specs.append(pl.BlockSpec(memory_space=pl.ANY))
        args.append(prev)
        aliases = {len(args) - 1: 0}
    return pl.pallas_call(
        functools.partial(_combine_kernel, alpha=alpha),
        grid=(tiles,),
        in_specs=in_specs,
        out_specs=pl.BlockSpec((tc, d), lambda i: (i + off, 0)),
        out_shape=jax.ShapeDtypeStruct((t, d), F32),
        input_output_aliases=aliases,
        compiler_params=_cparams(("parallel",)),
        name="combine",
    )(*args)


def _moe(h1, h1_packed, logits_t, router_bias, w_exp_gu, w_exp_down, w_shared_gu, w_shared_down,
         ln_g, ln_b, alpha):
    t, d = h1.shape
    n_exp = logits_t.shape[0]
    eidx, w_kt, rank, counts = _route(logits_t, router_bias, tt=min(512, t))

    counts = counts.reshape(n_exp)
    blocks_per_expert = (counts + MOE_BLOCK - 1) // MOE_BLOCK
    block_start = jnp.concatenate(
        [jnp.zeros((1,), jnp.int32), jnp.cumsum(blocks_per_expert).astype(jnp.int32)])
    pad_start = block_start[:n_exp] * MOE_BLOCK
    n_blocks = -(-(t * TOP_K + n_exp * (MOE_BLOCK - 1)) // MOE_BLOCK)

    dest = _dest(pad_start, eidx, rank, tt=min(2048, t))
    xs = _dispatch(dest, h1_packed, n_blocks * MOE_BLOCK, chunk=DISPATCH_CHUNK)
    shared = _shared(h1, w_shared_gu.astype(BF16), w_shared_down.astype(BF16), tm=min(512, t))
    ys = _experts(block_start, counts, xs, w_exp_gu, w_exp_down)
    dest_chunks = _chunked(dest, GATHER_CHUNK)
    per_part = dest_chunks.shape[0] // COMBINE_PARTS
    w_tk = w_kt.T
    out = None
    for part in range(COMBINE_PARTS):
        yg = _gather_back(dest_chunks[part * per_part:(part + 1) * per_part], ys)
        out = _combine(h1, shared, w_tk, ln_g, ln_b, yg, out, part, alpha, tc=min(256, t))
    return out


def kernel(x, ln_in_g, ln_in_b, w_in, hg_lb_logits, hg_norm_g, gm_v_norm_g, gm_v_norm_b, gm_w_s, gm_b_s, gm_out_norm_g, w_out, ln1_g, ln1_b, w_router, router_bias, w_exp_gu, w_exp_down, w_shared_gu, w_shared_down, ln2_g, ln2_b):
    bsz, s, d = x.shape
    depth = w_in.shape[0]
    assert depth == 1, "the lower-bound table row used in the hgrn kernel assumes one layer"
    alpha = (2.0 * depth) ** 0.25
    t = bsz * s
    hg_width = hg_norm_g.shape[1]
    gm_width = gm_v_norm_g.shape[1]
    tm = min(512, t)

    x2 = x.reshape(t, d)
    proj, f_pre = _in_proj(x2, ln_in_g, ln_in_b, w_in[0].astype(BF16), hg_width, hg_width, tm)
    proj3 = proj.reshape(bsz, s, proj.shape[1])
    y_hg = _hgrn(proj3, f_pre.reshape(bsz, s, hg_width), hg_lb_logits.astype(F32), hg_norm_g[0],
                 sb=min(512, s))
    y_gm = _gmlp(proj3, gm_v_norm_g[0], gm_v_norm_b[0], gm_w_s[0], gm_b_s[0], gm_out_norm_g[0],
                 u_blk=4 * hg_width // gm_width, n_blocks=min(4, s // GM_BLOCK))
    h1, h1_packed, logits_t = _out_proj(y_hg.reshape(t, hg_width), y_gm.reshape(t, gm_width), x2,
                                        ln_in_g, ln_in_b, w_out[0].astype(BF16), ln1_g[0], ln1_b[0],
                                        w_router[0].T, alpha, tm)
    out = _moe(h1, h1_packed, logits_t, router_bias[0], w_exp_gu[0], w_exp_down[0], w_shared_gu[0],
               w_shared_down[0], ln2_g[0], ln2_b[0], alpha)
    return out.reshape(bsz, s, d)
```

```python
import functools

import jax
import jax.numpy as jnp
from jax import lax
from jax.experimental import pallas as pl
from jax.experimental.pallas import tpu as pltpu
from jax.experimental.pallas import tpu_sc as plsc

F32 = jnp.float32
BF16 = jnp.bfloat16

LN_EPS = 1e-5
RMS_EPS = 1e-6
CHUNK = 64
SUB = 16
MAX_FACTORED_EXPONENT = 80.0
FACTORED_MIN_LB = float(2.718281828459045 ** (-MAX_FACTORED_EXPONENT / CHUNK))
HEAD_DIM = 128
GM_BLOCK = 128
TOP_K = 8
N_GROUPS = 8
TOPK_GROUPS = 4
ROUTED_SCALE = 2.5
MOE_BLOCK = 128
WEIGHT_SLOTS = 3
RING_SLOTS = 8
RING_DMA_PRIORITY = 1
DISPATCH_CHUNK = 64
COMBINE_PARTS = 4
GATHER_CHUNK = 16
VMEM_LIMIT = 48 * 1024 * 1024


def _cparams(sem):
    return pltpu.CompilerParams(dimension_semantics=sem, vmem_limit_bytes=VMEM_LIMIT)


def _layer_norm(x, g, b):
    mu = jnp.mean(x, axis=-1, keepdims=True)
    xc = x - mu
    var = jnp.mean(xc * xc, axis=-1, keepdims=True)
    return xc * lax.rsqrt(var + LN_EPS) * g + b


def _silu(x):
    return x * jax.nn.sigmoid(x)


def _gelu(x):
    return 0.5 * x * (1.0 + lax.erf(x * (2.0 ** -0.5)))


def _dot(a, b):
    return jnp.dot(a, b, preferred_element_type=F32)


def _dot_nt(a, b, precision=None):
    return lax.dot_general(a, b, (((1,), (1,)), ((), ())), preferred_element_type=F32,
                           precision=precision)


def _dot_tn(a, b):
    return lax.dot_general(a, b, (((0,), (0,)), ((), ())), preferred_element_type=F32)


def _split_bf16(x):
    hi = x.astype(BF16)
    return hi, (x - hi.astype(F32)).astype(BF16)


def _pack_halves(x):
    n = x.shape[1] // 2
    hi = lax.bitcast_convert_type(x[:, :n].astype(BF16).astype(F32), jnp.uint32)
    lo = lax.bitcast_convert_type(x[:, n:].astype(BF16).astype(F32), jnp.uint32)
    return hi | (lo >> 16)


def _unpack_halves(w):
    a = lax.bitcast_convert_type(w & jnp.uint32(0xFFFF0000), F32)
    b = lax.bitcast_convert_type(w << 16, F32)
    return a, b


def _in_proj_kernel(x_ref, g_ref, b_ref, w_ref, p_ref, f_ref, *, f_lo):
    h = _layer_norm(x_ref[...], g_ref[...], b_ref[...])
    p = _dot(h.astype(BF16), w_ref[...])
    p_ref[...] = p.astype(BF16)
    f_ref[...] = p[:, f_lo:f_lo + f_ref.shape[1]]


def _in_proj(x2, g, b, w_bf, f_lo, f_width, tm):
    t, d = x2.shape
    n = w_bf.shape[1]
    return pl.pallas_call(
        functools.partial(_in_proj_kernel, f_lo=f_lo),
        grid=(t // tm,),
        in_specs=[pl.BlockSpec((tm, d), lambda i: (i, 0)),
                  pl.BlockSpec((1, d), lambda i: (0, 0)),
                  pl.BlockSpec((1, d), lambda i: (0, 0)),
                  pl.BlockSpec((d, n), lambda i: (0, 0))],
        out_specs=[pl.BlockSpec((tm, n), lambda i: (i, 0)),
                   pl.BlockSpec((tm, f_width), lambda i: (i, 0))],
        out_shape=[jax.ShapeDtypeStruct((t, n), BF16), jax.ShapeDtypeStruct((t, f_width), F32)],
        compiler_params=_cparams(("parallel",)),
        name="in_proj",
    )(x2, g.reshape(1, d), b.reshape(1, d), w_bf)


def _hgrn_kernel(q_ref, f_ref, i_ref, g_ref, lbl_ref, gn_ref, o_ref, st_ref, *, n_chunks):
    @pl.when(pl.program_id(1) == 0)
    def _():
        st_ref[...] = jnp.zeros_like(st_ref)

    lg = lbl_ref[...]
    ex = jnp.exp(lg - jnp.max(lg, axis=0, keepdims=True))
    lb_all = ex[0:1, :] / jnp.sum(ex, axis=0, keepdims=True)
    gn_all = gn_ref[...]

    c = CHUNK
    heads = st_ref.shape[0]
    row = lax.broadcasted_iota(jnp.int32, (c, c), 0)
    col = lax.broadcasted_iota(jnp.int32, (c, c), 1)
    tril = (row >= col).astype(BF16)
    sub_row = lax.broadcasted_iota(jnp.int32, (SUB, HEAD_DIM), 0)
    ones = jnp.ones((HEAD_DIM, HEAD_DIM), BF16)
    neg_inf = jnp.float32(-jnp.inf)

    def intra_pairwise(q, k, v, v_bf, b, o_inter):
        outs = []
        for blk in range(c // SUB):
            lo = blk * SUB
            b_i = b[lo:lo + SUB, :]
            q_i = q[lo:lo + SUB, :]
            k_i = k[lo:lo + SUB, :]
            v_i = v[lo:lo + SUB, :]
            prods = []
            for j in range(SUB):
                diff = jnp.where(sub_row >= j, b_i - b_i[j:j + 1, :], neg_inf)
                prods.append(q_i * k_i[j:j + 1, :] * jnp.exp(diff))
            p_all = jnp.concatenate(prods, axis=0).astype(BF16)
            r_all = _dot(p_all, ones)
            o_blk = o_inter[lo:lo + SUB, :]
            for j in range(SUB):
                o_blk = o_blk + r_all[j * SUB:(j + 1) * SUB, :] * v_i[j:j + 1, :]
            if blk > 0:
                b_ref = b[lo - 1:lo, :]
                q_hat = (q_i * jnp.exp(b_i - b_ref)).astype(BF16)
                k_hat = (k[0:lo, :] * jnp.exp(b_ref - b[0:lo, :])).astype(BF16)
                scores = _dot_nt(q_hat, k_hat)
                o_blk = o_blk + _dot(scores.astype(BF16), v_bf[0:lo, :])
            outs.append(o_blk)
        return jnp.concatenate(outs, axis=0)

    def one_chunk(r0, factored):
        rows = pl.ds(r0, c)
        hs = [slice(h * HEAD_DIM, (h + 1) * HEAD_DIM) for h in range(heads)]
        q = _silu(q_ref[rows, :].astype(F32))
        f = lb_all + (1.0 - lb_all) * jax.nn.sigmoid(f_ref[rows, :])
        v_bf = i_ref[rows, :]
        v = v_bf.astype(F32)
        lf = jnp.log(f)
        k = 1.0 - f
        lf_hi, lf_lo = _split_bf16(lf)
        b2 = _dot(tril, jnp.concatenate([lf_hi, lf_lo], axis=1))
        width = lf.shape[1]
        b = b2[:, :width] + b2[:, width:]
        b_last = b[c - 1:c, :]
        q_dec = (q * jnp.exp(b)).astype(BF16)
        k_dec = (k * jnp.exp(b_last - b)).astype(BF16)
        decay = jnp.exp(b_last)
        st = [st_ref[h] for h in range(heads)]
        st_bf = [s.astype(BF16) for s in st]
        if factored:
            k_grow = (k * jnp.exp(-b)).astype(BF16)
            both = [_dot_nt(q_dec[:, hs[h]], jnp.concatenate([st_bf[h], k_grow[:, hs[h]]], axis=0))
                    for h in range(heads)]
            scores = [jnp.where(row >= col, both[h][:, HEAD_DIM:], 0.0).astype(BF16)
                      for h in range(heads)]
            outs = [both[h][:, :HEAD_DIM] + _dot(scores[h], v_bf[:, hs[h]]) for h in range(heads)]
        else:
            outs = [intra_pairwise(q[:, hs[h]], k[:, hs[h]], v[:, hs[h]], v_bf[:, hs[h]],
                                   b[:, hs[h]], _dot_nt(q_dec[:, hs[h]], st_bf[h]))
                    for h in range(heads)]
        for h in range(heads):
            st_ref[h] = st[h] * decay[:, hs[h]] + _dot_tn(v_bf[:, hs[h]], k_dec[:, hs[h]])
        inv = [lax.rsqrt(jnp.mean(o * o, axis=-1, keepdims=True) + RMS_EPS) for o in outs]
        o = jnp.concatenate([outs[h] * inv[h] for h in range(heads)], axis=1)
        o_ref[rows, :] = (o * gn_all * _silu(g_ref[rows, :].astype(F32))).astype(o_ref.dtype)

    def run_chunks(factored):
        per_trip = 4 if (factored and n_chunks % 4 == 0) else 1

        def chunk_body(ci, carry):
            for u in range(per_trip):
                one_chunk(pl.multiple_of((ci * per_trip + u) * c, c), factored)
            return carry

        lax.fori_loop(0, n_chunks // per_trip, chunk_body, 0)

    bounded = jnp.min(lb_all) >= FACTORED_MIN_LB
    pl.when(bounded)(lambda: run_chunks(True))
    pl.when(jnp.logical_not(bounded))(lambda: run_chunks(False))


def _hgrn(proj3, f_pre3, lb_logits, norm_g, sb):
    bsz, s, _ = proj3.shape
    hg_width = norm_g.shape[0]
    heads = hg_width // HEAD_DIM

    def sect(k):
        return pl.BlockSpec((None, sb, hg_width), lambda b, j, k=k: (b, j, k))

    return pl.pallas_call(
        functools.partial(_hgrn_kernel, n_chunks=sb // CHUNK),
        grid=(bsz, s // sb),
        in_specs=[sect(0), sect(0), sect(2), sect(3),
                  pl.BlockSpec((lb_logits.shape[0], hg_width), lambda b, j: (0, 0)),
                  pl.BlockSpec((1, hg_width), lambda b, j: (0, 0))],
        out_specs=pl.BlockSpec((None, sb, hg_width), lambda b, j: (b, j, 0)),
        out_shape=jax.ShapeDtypeStruct((bsz, s, hg_width), BF16),
        scratch_shapes=[pltpu.VMEM((heads, HEAD_DIM, HEAD_DIM), F32)],
        compiler_params=_cparams(("parallel", "arbitrary")),
        name="hgrn",
    )(proj3, f_pre3, proj3, proj3, lb_logits, norm_g.reshape(1, hg_width))


def _gmlp_kernel(u_ref, v_ref, vg_ref, vb_ref, ws_ref, bs_ref, og_ref, o_ref, *, n_blocks):
    groups = ws_ref.shape[0]
    gdim = u_ref.shape[-1] // groups
    row = lax.broadcasted_iota(jnp.int32, (GM_BLOCK, GM_BLOCK), 0) // CHUNK
    col = lax.broadcasted_iota(jnp.int32, (GM_BLOCK, GM_BLOCK), 1) // CHUNK
    allowed = row >= col
    ws = [jnp.where(allowed, ws_ref[g], 0.0).astype(BF16) for g in range(groups)]
    for n in range(n_blocks):
        rows = pl.ds(n * GM_BLOCK, GM_BLOCK)
        u = _gelu(u_ref[rows, :].astype(F32))
        v = _layer_norm(_gelu(v_ref[rows, :].astype(F32)), vg_ref[...], vb_ref[...])
        v_bf = v.astype(BF16)
        mixed = jnp.concatenate(
            [_dot(ws[g], v_bf[:, g * gdim:(g + 1) * gdim]) for g in range(groups)], axis=1)
        y = u * (mixed + bs_ref[...])
        inv = lax.rsqrt(jnp.mean(y * y, axis=-1, keepdims=True) + RMS_EPS)
        o_ref[rows, :] = (y * inv * og_ref[...]).astype(o_ref.dtype)


def _gmlp(proj3, v_norm_g, v_norm_b, w_s, b_s, out_norm_g, u_blk, n_blocks):
    bsz, s, _ = proj3.shape
    gw = v_norm_g.shape[0]
    groups = w_s.shape[0]
    rows = n_blocks * GM_BLOCK
    bias_full = jnp.repeat(b_s.T, gw // groups, axis=1)
    return pl.pallas_call(
        functools.partial(_gmlp_kernel, n_blocks=n_blocks),
        grid=(bsz, s // rows),
        in_specs=[pl.BlockSpec((None, rows, gw), lambda b, j: (b, j, u_blk)),
                  pl.BlockSpec((None, rows, gw), lambda b, j: (b, j, u_blk + 1)),
                  pl.BlockSpec((1, gw), lambda b, j: (0, 0)),
                  pl.BlockSpec((1, gw), lambda b, j: (0, 0)),
                  pl.BlockSpec((groups, GM_BLOCK, GM_BLOCK), lambda b, j: (0, 0, 0)),
                  pl.BlockSpec((GM_BLOCK, gw), lambda b, j: (0, 0)),
                  pl.BlockSpec((1, gw), lambda b, j: (0, 0))],
        out_specs=pl.BlockSpec((None, rows, gw), lambda b, j: (b, j, 0)),
        out_shape=jax.ShapeDtypeStruct((bsz, s, gw), BF16),
        compiler_params=_cparams(("parallel", "parallel")),
        name="gmlp",
    )(proj3, proj3, v_norm_g.reshape(1, gw), v_norm_b.reshape(1, gw), w_s, bias_full,
      out_norm_g.reshape(1, gw))


def _out_proj_kernel(yh_ref, yg_ref, x_ref, g0_ref, b0_ref, wa_ref, wb_ref, g_ref, b_ref, wr_ref,
                     h1_ref, hp_ref, lg_ref, *, alpha):
    mix = _dot(yh_ref[...], wa_ref[...]) + _dot(yg_ref[...], wb_ref[...])
    h0 = _layer_norm(x_ref[...], g0_ref[...], b0_ref[...])
    h1 = _layer_norm(alpha * h0 + mix, g_ref[...], b_ref[...])
    h1_ref[...] = h1
    hp_ref[...] = _pack_halves(h1)
    w_hi, w_lo = _split_bf16(wr_ref[...])
    h_hi, h_lo = _split_bf16(h1)
    lg_ref[...] = _dot_nt(w_hi, h_hi) + (_dot_nt(w_hi, h_lo) + _dot_nt(w_lo, h_hi))


def _out_proj(y_hg, y_gm, x2, g0, b0, w_out_bf, g, b, w_router_t, alpha, tm):
    t, d = x2.shape
    hw = y_hg.shape[1]
    gw = y_gm.shape[1]
    e = w_router_t.shape[0]
    return pl.pallas_call(
        functools.partial(_out_proj_kernel, alpha=alpha),
        grid=(t // tm,),
        in_specs=[pl.BlockSpec((tm, hw), lambda i: (i, 0)),
                  pl.BlockSpec((tm, gw), lambda i: (i, 0)),
                  pl.BlockSpec((tm, d), lambda i: (i, 0)),
                  pl.BlockSpec((1, d), lambda i: (0, 0)),
                  pl.BlockSpec((1, d), lambda i: (0, 0)),
                  pl.BlockSpec((hw, d), lambda i: (0, 0)),
                  pl.BlockSpec((gw, d), lambda i: (0, 0)),
                  pl.BlockSpec((1, d), lambda i: (0, 0)),
                  pl.BlockSpec((1, d), lambda i: (0, 0)),
                  pl.BlockSpec((e, d), lambda i: (0, 0))],
        out_specs=[pl.BlockSpec((tm, d), lambda i: (i, 0)),
                   pl.BlockSpec((tm, d // 2), lambda i: (i, 0)),
                   pl.BlockSpec((e, tm), lambda i: (0, i))],
        out_shape=[jax.ShapeDtypeStruct((t, d), F32), jax.ShapeDtypeStruct((t, d // 2), jnp.uint32),
                   jax.ShapeDtypeStruct((e, t), F32)],
        compiler_params=_cparams(("parallel",)),
        name="out_proj",
    )(y_hg, y_gm, x2, g0.reshape(1, d), b0.reshape(1, d), w_out_bf[:hw], w_out_bf[hw:],
      g.reshape(1, d), b.reshape(1, d), w_router_t)


def _route_kernel(lg_ref, bias_ref, eidx_ref, w_ref, rank_ref, cnt_ref, carry_ref):
    @pl.when(pl.program_id(0) == 0)
    def _():
        carry_ref[...] = jnp.zeros_like(carry_ref)

    n_exp, tt = lg_ref.shape
    per_group = n_exp // N_GROUPS
    neg_inf = jnp.float32(-jnp.inf)
    scores = jax.nn.sigmoid(lg_ref[...])
    biased = scores + bias_ref[...]

    gio = lax.broadcasted_iota(jnp.int32, (per_group, tt), 0)
    blocks, gs_rows = [], []
    for g in range(N_GROUPS):
        blk = biased[g * per_group:(g + 1) * per_group, :]
        m1 = jnp.max(blk, axis=0, keepdims=True)
        first = jnp.min(jnp.where(blk == m1, gio, per_group), axis=0, keepdims=True)
        m2 = jnp.max(jnp.where(gio == first, neg_inf, blk), axis=0, keepdims=True)
        blocks.append(blk)
        gs_rows.append(m1 + m2)
    gs = jnp.concatenate(gs_rows, axis=0)
    gidx = lax.broadcasted_iota(jnp.int32, (N_GROUPS, tt), 0)
    beaten = jnp.zeros((N_GROUPS, tt), jnp.int32)
    for g in range(N_GROUPS):
        r = gs_rows[g]
        beaten = beaten + jnp.where((r > gs) | ((r == gs) & (g < gidx)), 1, 0)
    keep = beaten < TOPK_GROUPS
    cand = jnp.concatenate(
        [jnp.where(keep[g:g + 1, :], blocks[g], neg_inf) for g in range(N_GROUPS)], axis=0)

    eio = lax.broadcasted_iota(jnp.int32, (n_exp, tt), 0)
    idx_rows, w_rows = [], []
    chosen = jnp.zeros((n_exp, tt), F32)
    for _ in range(TOP_K):
        m = jnp.max(cand, axis=0, keepdims=True)
        idx = jnp.min(jnp.where(cand == m, eio, n_exp), axis=0, keepdims=True)
        sel = eio == idx
        w_rows.append(jnp.sum(jnp.where(sel, scores, 0.0), axis=0, keepdims=True))
        idx_rows.append(idx)
        cand = jnp.where(sel, neg_inf, cand)
        chosen = jnp.where(sel, 1.0, chosen)
    eidx = jnp.concatenate(idx_rows, axis=0)
    w = jnp.concatenate(w_rows, axis=0)
    eidx_ref[...] = eidx
    w_ref[...] = w / jnp.sum(w, axis=0, keepdims=True) * ROUTED_SCALE

    tr = lax.broadcasted_iota(jnp.int32, (tt, tt), 0)
    tc = lax.broadcasted_iota(jnp.int32, (tt, tt), 1)
    before = (tr < tc).astype(BF16)
    cum = _dot(chosen.astype(BF16), before) + carry_ref[...]
    rank_ref[...] = jnp.concatenate(
        [jnp.sum(jnp.where(eio == idx_rows[k], cum, 0.0), axis=0, keepdims=True)
         for k in range(TOP_K)], axis=0).astype(jnp.int32)
    total = carry_ref[...] + jnp.sum(chosen, axis=1, keepdims=True)
    carry_ref[...] = total
    cnt_ref[...] = total.astype(jnp.int32)


def _route(logits_t, router_bias, tt):
    e, t = logits_t.shape
    return pl.pallas_call(
        _route_kernel,
        grid=(t // tt,),
        in_specs=[pl.BlockSpec((e, tt), lambda i: (0, i)),
                  pl.BlockSpec((e, 1), lambda i: (0, 0))],
        out_specs=[pl.BlockSpec((TOP_K, tt), lambda i: (0, i)),
                   pl.BlockSpec((TOP_K, tt), lambda i: (0, i)),
                   pl.BlockSpec((TOP_K, tt), lambda i: (0, i)),
                   pl.BlockSpec((e, 1), lambda i: (0, 0))],
        out_shape=[jax.ShapeDtypeStruct((TOP_K, t), jnp.int32),
                   jax.ShapeDtypeStruct((TOP_K, t), F32),
                   jax.ShapeDtypeStruct((TOP_K, t), jnp.int32),
                   jax.ShapeDtypeStruct((e, 1), jnp.int32)],
        scratch_shapes=[pltpu.VMEM((e, 1), F32)],
        compiler_params=_cparams(("arbitrary",)),
        name="route",
    )(logits_t, router_bias.astype(F32).reshape(e, 1))


def _dest_kernel(ps_ref, eidx_ref, rank_ref, dest_ref, *, n_exp):
    eidx = eidx_ref[...]

    def body(e, acc):
        return acc + jnp.where(eidx == e, ps_ref[e], 0)

    dest_ref[...] = lax.fori_loop(0, n_exp, body, rank_ref[...])


def _dest(pad_start, eidx, rank, tt):
    k, t = eidx.shape
    n_exp = pad_start.shape[0]
    return pl.pallas_call(
        functools.partial(_dest_kernel, n_exp=n_exp),
        grid_spec=pltpu.PrefetchScalarGridSpec(
            num_scalar_prefetch=1,
            grid=(t // tt,),
            in_specs=[pl.BlockSpec((k, tt), lambda i, ps: (0, i)),
                      pl.BlockSpec((k, tt), lambda i, ps: (0, i))],
            out_specs=pl.BlockSpec((k, tt), lambda i, ps: (0, i))),
        out_shape=jax.ShapeDtypeStruct((k, t), jnp.int32),
        compiler_params=_cparams(("parallel",)),
        name="dest",
    )(pad_start, eidx, rank)


def _chunked(dest, chunk):
    k, t = dest.shape
    return dest.reshape(k, t // chunk, chunk).transpose(1, 0, 2)


def _sc_workers():
    info = plsc.get_sparse_core_info()
    return info.num_cores, info.num_cores * info.num_subcores


def _dispatch(dest, h1, n_rows, chunk):
    t, dh = h1.shape
    n_cores, n_workers = _sc_workers()
    per_worker = t // chunk // n_workers
    mesh = plsc.VectorSubcoreMesh(core_axis_name="c", subcore_axis_name="s")

    @functools.partial(
        pl.kernel, mesh=mesh,
        out_type=jax.ShapeDtypeStruct((n_rows, dh), h1.dtype),
        scratch_types=[pltpu.VMEM((TOP_K, chunk), jnp.int32),
                       pltpu.VMEM((chunk, dh), h1.dtype),
                       pltpu.SemaphoreType.DMA],
        name="dispatch",
    )
    def scatter_rows(x_hbm, dest_hbm, o_hbm, idx_v, rows_v, sem):
        wid = lax.axis_index("s") * n_cores + lax.axis_index("c")

        @pl.loop(0, per_worker)
        def _(j):
            c = wid * per_worker + j
            pltpu.sync_copy(dest_hbm.at[c], idx_v)
            pltpu.sync_copy(x_hbm.at[pl.ds(c * chunk, chunk)], rows_v)
            copies = [pltpu.async_copy(rows_v, o_hbm.at[idx_v.at[k]], sem) for k in range(TOP_K)]
            for cp in copies:
                cp.wait()

    return scatter_rows(h1, _chunked(dest, chunk))


def _gather_back(dest_chunks, ys):
    n_chunks, k_top, chunk = dest_chunks.shape
    t = n_chunks * chunk
    dh = ys.shape[1]
    n_cores, n_workers = _sc_workers()
    per_worker = n_chunks // n_workers
    mesh = plsc.VectorSubcoreMesh(core_axis_name="c", subcore_axis_name="s")

    @functools.partial(
        pl.kernel, mesh=mesh,
        out_type=jax.ShapeDtypeStruct((k_top, t, dh), ys.dtype),
        scratch_types=[pltpu.VMEM((k_top, chunk), jnp.int32),
                       pltpu.VMEM((k_top, chunk, dh), ys.dtype),
                       pltpu.SemaphoreType.DMA],
        name="gather_back",
    )
    def gather_rows(ys_hbm, dest_hbm, o_hbm, idx_v, rows_v, sem):
        wid = lax.axis_index("s") * n_cores + lax.axis_index("c")

        @pl.loop(0, per_worker)
        def _(j):
            c = wid * per_worker + j
            pltpu.sync_copy(dest_hbm.at[c], idx_v)
            copies = [pltpu.async_copy(ys_hbm.at[idx_v.at[k]], rows_v.at[k], sem)
                      for k in range(k_top)]
            for cp in copies:
                cp.wait()
            for k in range(k_top):
                pltpu.sync_copy(rows_v.at[k], o_hbm.at[k, pl.ds(c * chunk, chunk)])

    return gather_rows(ys, dest_chunks)


def _experts_kernel(bs_ref, cnt_ref, wgu_hbm, wd_hbm, xs_hbm, ys_hbm, wgu_buf, wd_buf, gu_bf, d_bf,
                    xbuf, ybuf, wsem, xsem, ysem):
    e = pl.program_id(0)
    n_exp = pl.num_programs(0)
    n_slots, mb, _ = xbuf.shape
    ahead = n_slots - 1
    w_slots = wgu_buf.shape[0]
    n_blocks = bs_ref[n_exp]
    b0 = bs_ref[e]
    b1 = bs_ref[e + 1]
    count = cnt_ref[e]

    def w_copies(ex):
        slot = lax.rem(ex, w_slots)
        return (pltpu.make_async_copy(wgu_hbm.at[ex], wgu_buf.at[slot], wsem.at[slot]),
                pltpu.make_async_copy(wd_hbm.at[ex], wd_buf.at[slot], wsem.at[slot]))

    @pl.when(e == 0)
    def _():
        for i in range(w_slots - 1):
            @pl.when(i < n_exp)
            def _():
                for cp in w_copies(i):
                    cp.start()

    @pl.when(e + w_slots - 1 < n_exp)
    def _():
        for cp in w_copies(e + w_slots - 1):
            cp.start()

    def x_copy(b):
        slot = lax.rem(b, n_slots)
        return pltpu.make_async_copy(xs_hbm.at[pl.ds(pl.multiple_of(b * mb, mb), mb)],
                                     xbuf.at[slot], xsem.at[slot])

    def y_copy(b):
        slot = lax.rem(b, n_slots)
        return pltpu.make_async_copy(ybuf.at[slot],
                                     ys_hbm.at[pl.ds(pl.multiple_of(b * mb, mb), mb)], ysem.at[slot])

    @pl.when(e == 0)
    def _():
        for i in range(ahead):
            @pl.when(i < n_blocks)
            def _():
                x_copy(i).start(priority=RING_DMA_PRIORITY)

    for cp in w_copies(e):
        cp.wait()
    w_slot = lax.rem(e, w_slots)

    @pl.when(b1 > b0)
    def _():
        gu_bf[...] = wgu_buf[w_slot].astype(BF16)
        d_bf[...] = wd_buf[w_slot].astype(BF16)

    kh = gu_bf.shape[0] // 2
    half = gu_bf.shape[1] // 2

    def block(b, carry):
        slot = lax.rem(b, n_slots)
        x_copy(b).wait()

        @pl.when(b + ahead < n_blocks)
        def _():
            x_copy(b + ahead).start(priority=RING_DMA_PRIORITY)

        @pl.when(b >= n_slots)
        def _():
            y_copy(b - n_slots).wait()

        rows = lax.broadcasted_iota(jnp.int32, (mb, xbuf.shape[2]), 0)
        valid = count - (b - b0) * mb
        words = jnp.where(rows < valid, xbuf[slot], jnp.uint32(0))
        xa, xb = _unpack_halves(words)
        gu = _dot(xa.astype(BF16), gu_bf[0:kh, :]) + _dot(xb.astype(BF16), gu_bf[kh:, :])
        hid = _silu(gu[:, :half]) * gu[:, half:]
        ybuf[slot] = _pack_halves(_dot(hid.astype(BF16), d_bf[...]))
        y_copy(b).start(priority=RING_DMA_PRIORITY)
        return carry

    lax.fori_loop(b0, b1, block, 0)

    @pl.when(e == n_exp - 1)
    def _():
        for i in range(1, n_slots + 1):
            @pl.when(n_blocks >= i)
            def _():
                y_copy(n_blocks - i).wait()


def _experts(block_start, counts, xs, w_gu, w_down):
    n_rows, dh = xs.shape
    n_exp, d, gu_w = w_gu.shape
    ed = w_down.shape[1]
    return pl.pallas_call(
        _experts_kernel,
        grid_spec=pltpu.PrefetchScalarGridSpec(
            num_scalar_prefetch=2,
            grid=(n_exp,),
            in_specs=[pl.BlockSpec(memory_space=pl.ANY),
                      pl.BlockSpec(memory_space=pl.ANY),
                      pl.BlockSpec(memory_space=pl.ANY)],
            out_specs=pl.BlockSpec(memory_space=pl.ANY),
            scratch_shapes=[pltpu.VMEM((WEIGHT_SLOTS, d, gu_w), w_gu.dtype),
                            pltpu.VMEM((WEIGHT_SLOTS, ed, d), w_down.dtype),
                            pltpu.VMEM((d, gu_w), BF16), pltpu.VMEM((ed, d), BF16),
                            pltpu.VMEM((RING_SLOTS, MOE_BLOCK, dh), jnp.uint32),
                            pltpu.VMEM((RING_SLOTS, MOE_BLOCK, dh), jnp.uint32),
                            pltpu.SemaphoreType.DMA((WEIGHT_SLOTS,)),
                            pltpu.SemaphoreType.DMA((RING_SLOTS,)),
                            pltpu.SemaphoreType.DMA((RING_SLOTS,))]),
        out_shape=jax.ShapeDtypeStruct((n_rows, dh), jnp.uint32),
        compiler_params=_cparams(("arbitrary",)),
        name="experts",
    )(block_start, counts, w_gu, w_down, xs)


def _shared_kernel(h_ref, wsg_ref, wsd_ref, o_ref):
    gu = _dot(h_ref[...].astype(BF16), wsg_ref[...])
    half = gu.shape[1] // 2
    o_ref[...] = _dot((_silu(gu[:, :half]) * gu[:, half:]).astype(BF16), wsd_ref[...]).astype(o_ref.dtype)


def _shared(h1, ws_gu_bf, ws_down_bf, tm):
    t, d = h1.shape
    sg = ws_gu_bf.shape[1]
    sd = ws_down_bf.shape[0]
    return pl.pallas_call(
        _shared_kernel,
        grid=(t // tm,),
        in_specs=[pl.BlockSpec((tm, d), lambda i: (i, 0)),
                  pl.BlockSpec((d, sg), lambda i: (0, 0)),
                  pl.BlockSpec((sd, d), lambda i: (0, 0))],
        out_specs=pl.BlockSpec((tm, d), lambda i: (i, 0)),
        out_shape=jax.ShapeDtypeStruct((t, d), BF16),
        compiler_params=_cparams(("parallel",)),
        name="shared",
    )(h1, ws_gu_bf, ws_down_bf)


def _combine_kernel(h_ref, sh_ref, w_ref, g_ref, b_ref, yg_ref, *rest, alpha):
    o_ref = rest[-1]
    acc = alpha * h_ref[...] + sh_ref[...].astype(F32)
    dh = acc.shape[1] // 2
    acc_a = acc[:, :dh]
    acc_b = acc[:, dh:]
    w = w_ref[...]
    for k in range(TOP_K):
        ya, yb = _unpack_halves(yg_ref[k])
        acc_a = acc_a + ya * w[:, k:k + 1]
        acc_b = acc_b + yb * w[:, k:k + 1]
    o_ref[...] = _layer_norm(jnp.concatenate([acc_a, acc_b], axis=1), g_ref[...], b_ref[...])


def _combine(h1, shared, w_tk, g, b, yg, prev, part, alpha, tc):
    t, d = h1.shape
    tiles = yg.shape[1] // tc
    off = part * tiles
    in_specs = [pl.BlockSpec((tc, d), lambda i: (i + off, 0)),
                pl.BlockSpec((tc, d), lambda i: (i + off, 0)),
                pl.BlockSpec((tc, TOP_K), lambda i: (i + off, 0)),
                pl.BlockSpec((1, d), lambda i: (0, 0)),
                pl.BlockSpec((1, d), lambda i: (0, 0)),
                pl.BlockSpec((TOP_K, tc, yg.shape[2]), lambda i: (0, i, 0))]
    args = [h1, shared, w_tk, g.reshape(1, d), b.reshape(1, d), yg]
    aliases = {}
    if prev is not None:
        in_specs.append(pl.BlockSpec(memory_space=pl.ANY))
        args.append(prev)
        aliases = {len(args) - 1: 0}
    return pl.pallas_call(
        functools.partial(_combine_kernel, alpha=alpha),
        grid=(tiles,),
        in_specs=in_specs,
        out_specs=pl.BlockSpec((tc, d), lambda i: (i + off, 0)),
        out_shape=jax.ShapeDtypeStruct((t, d), F32),
        input_output_aliases=aliases,
        compiler_params=_cparams(("parallel",)),
        name="combine",
    )(*args)


def _moe(h1, h1_packed, logits_t, router_bias, w_exp_gu, w_exp_down, w_shared_gu, w_shared_down,
         ln_g, ln_b, alpha):
    t, d = h1.shape
    n_exp = logits_t.shape[0]
    eidx, w_kt, rank, counts = _route(logits_t, router_bias, tt=min(512, t))

    counts = counts.reshape(n_exp)
    blocks_per_expert = (counts + MOE_BLOCK - 1) // MOE_BLOCK
    block_start = jnp.concatenate(
        [jnp.zeros((1,), jnp.int32), jnp.cumsum(blocks_per_expert).astype(jnp.int32)])
    pad_start = block_start[:n_exp] * MOE_BLOCK
    n_blocks = -(-(t * TOP_K + n_exp * (MOE_BLOCK - 1)) // MOE_BLOCK)

    dest = _dest(pad_start, eidx, rank, tt=min(2048, t))
    xs = _dispatch(dest, h1_packed, n_blocks * MOE_BLOCK, chunk=DISPATCH_CHUNK)
    shared = _shared(h1, w_shared_gu.astype(BF16), w_shared_down.astype(BF16), tm=min(512, t))
    ys = _experts(block_start, counts, xs, w_exp_gu, w_exp_down)
    dest_chunks = _chunked(dest, GATHER_CHUNK)
    per_part = dest_chunks.shape[0] // COMBINE_PARTS
    w_tk = w_kt.T
    out = None
    for part in range(COMBINE_PARTS):
        yg = _gather_back(dest_chunks[part * per_part:(part + 1) * per_part], ys)
        out = _combine(h1, shared, w_tk, ln_g, ln_b, yg, out, part, alpha, tc=min(256, t))
    return out


def kernel(x, ln_in_g, ln_in_b, w_in, hg_lb_logits, hg_norm_g, gm_v_norm_g, gm_v_norm_b, gm_w_s, gm_b_s, gm_out_norm_g, w_out, ln1_g, ln1_b, w_router, router_bias, w_exp_gu, w_exp_down, w_shared_gu, w_shared_down, ln2_g, ln2_b):
    bsz, s, d = x.shape
    depth = w_in.shape[0]
    assert depth == 1, "the lower-bound table row used in the hgrn kernel assumes one layer"
    alpha = (2.0 * depth) ** 0.25
    t = bsz * s
    hg_width = hg_norm_g.shape[1]
    gm_width = gm_v_norm_g.shape[1]
    tm = min(512, t)

    x2 = x.reshape(t, d)
    proj, f_pre = _in_proj(x2, ln_in_g, ln_in_b, w_in[0].astype(BF16), hg_width, hg_width, tm)
    proj3 = proj.reshape(bsz, s, proj.shape[1])
    y_hg = _hgrn(proj3, f_pre.reshape(bsz, s, hg_width), hg_lb_logits.astype(F32), hg_norm_g[0],
                 sb=min(512, s))
    y_gm = _gmlp(proj3, gm_v_norm_g[0], gm_v_norm_b[0], gm_w_s[0], gm_b_s[0], gm_out_norm_g[0],
                 u_blk=4 * hg_width // gm_width, n_blocks=min(4, s // GM_BLOCK))
    h1, h1_packed, logits_t = _out_proj(y_hg.reshape(t, hg_width), y_gm.reshape(t, gm_width), x2,
                                        ln_in_g, ln_in_b, w_out[0].astype(BF16), ln1_g[0], ln1_b[0],
                                        w_router[0].T, alpha, tm)
    out = _moe(h1, h1_packed, logits_t, router_bias[0], w_exp_gu[0], w_exp_down[0], w_shared_gu[0],
               w_shared_down[0], ln2_g[0], ln2_b[0], alpha)
    return out.reshape(bsz, s, d)
```

```python
import functools

import jax
import jax.numpy as jnp
from jax import lax
from jax.experimental import pallas as pl
from jax.experimental.pallas import tpu as pltpu
from jax.experimental.pallas import tpu_sc as plsc

F32 = jnp.float32
BF16 = jnp.bfloat16

LN_EPS = 1e-5
RMS_EPS = 1e-6
CHUNK = 64
SUB = 16
MAX_FACTORED_EXPONENT = 80.0
FACTORED_MIN_LB = float(2.718281828459045 ** (-MAX_FACTORED_EXPONENT / CHUNK))
HEAD_DIM = 128
GM_BLOCK = 128
TOP_K = 8
N_GROUPS = 8
TOPK_GROUPS = 4
ROUTED_SCALE = 2.5
MOE_BLOCK = 128
WEIGHT_SLOTS = 3
RING_SLOTS = 8
RING_DMA_PRIORITY = 1
DISPATCH_CHUNK = 64
ROUTE_SUBTILES = 2
COMBINE_PARTS = 4
GATHER_CHUNK = 16
VMEM_LIMIT = 48 * 1024 * 1024


def _cparams(sem):
    return pltpu.CompilerParams(dimension_semantics=sem, vmem_limit_bytes=VMEM_LIMIT)


def _layer_norm(x, g, b):
    mu = jnp.mean(x, axis=-1, keepdims=True)
    xc = x - mu
    var = jnp.mean(xc * xc, axis=-1, keepdims=True)
    return xc * lax.rsqrt(var + LN_EPS) * g + b


def _silu(x):
    return x * jax.nn.sigmoid(x)


def _gelu(x):
    return 0.5 * x * (1.0 + lax.erf(x * (2.0 ** -0.5)))


def _dot(a, b):
    return jnp.dot(a, b, preferred_element_type=F32)


def _dot_nt(a, b, precision=None):
    return lax.dot_general(a, b, (((1,), (1,)), ((), ())), preferred_element_type=F32,
                           precision=precision)


def _dot_tn(a, b):
    return lax.dot_general(a, b, (((0,), (0,)), ((), ())), preferred_element_type=F32)


def _split_bf16(x):
    hi = x.astype(BF16)
    return hi, (x - hi.astype(F32)).astype(BF16)


def _pack_halves(x):
    n = x.shape[1] // 2
    hi = lax.bitcast_convert_type(x[:, :n].astype(BF16).astype(F32), jnp.uint32)
    lo = lax.bitcast_convert_type(x[:, n:].astype(BF16).astype(F32), jnp.uint32)
    return hi | (lo >> 16)


def _unpack_halves(w):
    a = lax.bitcast_convert_type(w & jnp.uint32(0xFFFF0000), F32)
    b = lax.bitcast_convert_type(w << 16, F32)
    return a, b


def _in_proj_kernel(x_ref, g_ref, b_ref, w_ref, h_ref, p_ref, f_ref, *, f_lo):
    h = _layer_norm(x_ref[...], g_ref[...], b_ref[...])
    h_ref[...] = h
    p = _dot(h.astype(BF16), w_ref[...])
    p_ref[...] = p.astype(BF16)
    f_ref[...] = p[:, f_lo:f_lo + f_ref.shape[1]]


def _in_proj(x2, g, b, w_bf, f_lo, f_width, tm):
    t, d = x2.shape
    n = w_bf.shape[1]
    return pl.pallas_call(
        functools.partial(_in_proj_kernel, f_lo=f_lo),
        grid=(t // tm,),
        in_specs=[pl.BlockSpec((tm, d), lambda i: (i, 0)),
                  pl.BlockSpec((1, d), lambda i: (0, 0)),
                  pl.BlockSpec((1, d), lambda i: (0, 0)),
                  pl.BlockSpec((d, n), lambda i: (0, 0))],
        out_specs=[pl.BlockSpec((tm, d), lambda i: (i, 0)),
                   pl.BlockSpec((tm, n), lambda i: (i, 0)),
                   pl.BlockSpec((tm, f_width), lambda i: (i, 0))],
        out_shape=[jax.ShapeDtypeStruct((t, d), F32), jax.ShapeDtypeStruct((t, n), BF16),
                   jax.ShapeDtypeStruct((t, f_width), F32)],
        compiler_params=_cparams(("parallel",)),
        name="in_proj",
    )(x2, g.reshape(1, d), b.reshape(1, d), w_bf)


def _hgrn_kernel(q_ref, f_ref, i_ref, g_ref, lbl_ref, gn_ref, o_ref, st_ref, *, n_chunks):
    @pl.when(pl.program_id(1) == 0)
    def _():
        st_ref[...] = jnp.zeros_like(st_ref)

    lg = lbl_ref[...]
    ex = jnp.exp(lg - jnp.max(lg, axis=0, keepdims=True))
    lb_all = ex[0:1, :] / jnp.sum(ex, axis=0, keepdims=True)
    gn_all = gn_ref[...]

    c = CHUNK
    heads = st_ref.shape[0]
    row = lax.broadcasted_iota(jnp.int32, (c, c), 0)
    col = lax.broadcasted_iota(jnp.int32, (c, c), 1)
    tril = (row >= col).astype(BF16)
    sub_row = lax.broadcasted_iota(jnp.int32, (SUB, HEAD_DIM), 0)
    ones = jnp.ones((HEAD_DIM, HEAD_DIM), BF16)
    neg_inf = jnp.float32(-jnp.inf)

    def intra_pairwise(q, k, v, v_bf, b, o_inter):
        outs = []
        for blk in range(c // SUB):
            lo = blk * SUB
            b_i = b[lo:lo + SUB, :]
            q_i = q[lo:lo + SUB, :]
            k_i = k[lo:lo + SUB, :]
            v_i = v[lo:lo + SUB, :]
            prods = []
            for j in range(SUB):
                diff = jnp.where(sub_row >= j, b_i - b_i[j:j + 1, :], neg_inf)
                prods.append(q_i * k_i[j:j + 1, :] * jnp.exp(diff))
            p_all = jnp.concatenate(prods, axis=0).astype(BF16)
            r_all = _dot(p_all, ones)
            o_blk = o_inter[lo:lo + SUB, :]
            for j in range(SUB):
                o_blk = o_blk + r_all[j * SUB:(j + 1) * SUB, :] * v_i[j:j + 1, :]
            if blk > 0:
                b_ref = b[lo - 1:lo, :]
                q_hat = (q_i * jnp.exp(b_i - b_ref)).astype(BF16)
                k_hat = (k[0:lo, :] * jnp.exp(b_ref - b[0:lo, :])).astype(BF16)
                scores = _dot_nt(q_hat, k_hat)
                o_blk = o_blk + _dot(scores.astype(BF16), v_bf[0:lo, :])
            outs.append(o_blk)
        return jnp.concatenate(outs, axis=0)

    def one_chunk(r0, factored):
        rows = pl.ds(r0, c)
        hs = [slice(h * HEAD_DIM, (h + 1) * HEAD_DIM) for h in range(heads)]
        q = _silu(q_ref[rows, :].astype(F32))
        f = lb_all + (1.0 - lb_all) * jax.nn.sigmoid(f_ref[rows, :])
        v_bf = i_ref[rows, :]
        v = v_bf.astype(F32)
        lf = jnp.log(f)
        k = 1.0 - f
        lf_hi, lf_lo = _split_bf16(lf)
        b2 = _dot(tril, jnp.concatenate([lf_hi, lf_lo], axis=1))
        width = lf.shape[1]
        b = b2[:, :width] + b2[:, width:]
        b_last = b[c - 1:c, :]
        q_dec = (q * jnp.exp(b)).astype(BF16)
        k_dec = (k * jnp.exp(b_last - b)).astype(BF16)
        decay = jnp.exp(b_last)
        st = [st_ref[h] for h in range(heads)]
        st_bf = [s.astype(BF16) for s in st]
        if factored:
            k_grow = (k * jnp.exp(-b)).astype(BF16)
            both = [_dot_nt(q_dec[:, hs[h]], jnp.concatenate([st_bf[h], k_grow[:, hs[h]]], axis=0))
                    for h in range(heads)]
            scores = [jnp.where(row >= col, both[h][:, HEAD_DIM:], 0.0).astype(BF16)
                      for h in range(heads)]
            outs = [both[h][:, :HEAD_DIM] + _dot(scores[h], v_bf[:, hs[h]]) for h in range(heads)]
        else:
            outs = [intra_pairwise(q[:, hs[h]], k[:, hs[h]], v[:, hs[h]], v_bf[:, hs[h]],
                                   b[:, hs[h]], _dot_nt(q_dec[:, hs[h]], st_bf[h]))
                    for h in range(heads)]
        for h in range(heads):
            st_ref[h] = st[h] * decay[:, hs[h]] + _dot_tn(v_bf[:, hs[h]], k_dec[:, hs[h]])
        inv = [lax.rsqrt(jnp.mean(o * o, axis=-1, keepdims=True) + RMS_EPS) for o in outs]
        o = jnp.concatenate([outs[h] * inv[h] for h in range(heads)], axis=1)
        o_ref[rows, :] = (o * gn_all * _silu(g_ref[rows, :].astype(F32))).astype(o_ref.dtype)

    def run_chunks(factored):
        per_trip = 4 if (factored and n_chunks % 4 == 0) else 1

        def chunk_body(ci, carry):
            for u in range(per_trip):
                one_chunk(pl.multiple_of((ci * per_trip + u) * c, c), factored)
            return carry

        lax.fori_loop(0, n_chunks // per_trip, chunk_body, 0)

    bounded = jnp.min(lb_all) >= FACTORED_MIN_LB
    pl.when(bounded)(lambda: run_chunks(True))
    pl.when(jnp.logical_not(bounded))(lambda: run_chunks(False))


def _hgrn(proj3, f_pre3, lb_logits, norm_g, sb):
    bsz, s, _ = proj3.shape
    hg_width = norm_g.shape[0]
    heads = hg_width // HEAD_DIM

    def sect(k):
        return pl.BlockSpec((None, sb, hg_width), lambda b, j, k=k: (b, j, k))

    return pl.pallas_call(
        functools.partial(_hgrn_kernel, n_chunks=sb // CHUNK),
        grid=(bsz, s // sb),
        in_specs=[sect(0), sect(0), sect(2), sect(3),
                  pl.BlockSpec((lb_logits.shape[0], hg_width), lambda b, j: (0, 0)),
                  pl.BlockSpec((1, hg_width), lambda b, j: (0, 0))],
        out_specs=pl.BlockSpec((None, sb, hg_width), lambda b, j: (b, j, 0)),
        out_shape=jax.ShapeDtypeStruct((bsz, s, hg_width), BF16),
        scratch_shapes=[pltpu.VMEM((heads, HEAD_DIM, HEAD_DIM), F32)],
        compiler_params=_cparams(("parallel", "arbitrary")),
        name="hgrn",
    )(proj3, f_pre3, proj3, proj3, lb_logits, norm_g.reshape(1, hg_width))


def _gmlp_kernel(u_ref, v_ref, vg_ref, vb_ref, ws_ref, bs_ref, og_ref, o_ref, *, n_blocks):
    groups = ws_ref.shape[0]
    gdim = u_ref.shape[-1] // groups
    row = lax.broadcasted_iota(jnp.int32, (GM_BLOCK, GM_BLOCK), 0) // CHUNK
    col = lax.broadcasted_iota(jnp.int32, (GM_BLOCK, GM_BLOCK), 1) // CHUNK
    allowed = row >= col
    ws = [jnp.where(allowed, ws_ref[g], 0.0).astype(BF16) for g in range(groups)]
    for n in range(n_blocks):
        rows = pl.ds(n * GM_BLOCK, GM_BLOCK)
        u = _gelu(u_ref[rows, :].astype(F32))
        v = _layer_norm(_gelu(v_ref[rows, :].astype(F32)), vg_ref[...], vb_ref[...])
        v_bf = v.astype(BF16)
        mixed = jnp.concatenate(
            [_dot(ws[g], v_bf[:, g * gdim:(g + 1) * gdim]) for g in range(groups)], axis=1)
        y = u * (mixed + bs_ref[...])
        inv = lax.rsqrt(jnp.mean(y * y, axis=-1, keepdims=True) + RMS_EPS)
        o_ref[rows, :] = (y * inv * og_ref[...]).astype(o_ref.dtype)


def _gmlp(proj3, v_norm_g, v_norm_b, w_s, b_s, out_norm_g, u_blk, n_blocks):
    bsz, s, _ = proj3.shape
    gw = v_norm_g.shape[0]
    groups = w_s.shape[0]
    rows = n_blocks * GM_BLOCK
    bias_full = jnp.repeat(b_s.T, gw // groups, axis=1)
    return pl.pallas_call(
        functools.partial(_gmlp_kernel, n_blocks=n_blocks),
        grid=(bsz, s // rows),
        in_specs=[pl.BlockSpec((None, rows, gw), lambda b, j: (b, j, u_blk)),
                  pl.BlockSpec((None, rows, gw), lambda b, j: (b, j, u_blk + 1)),
                  pl.BlockSpec((1, gw), lambda b, j: (0, 0)),
                  pl.BlockSpec((1, gw), lambda b, j: (0, 0)),
                  pl.BlockSpec((groups, GM_BLOCK, GM_BLOCK), lambda b, j: (0, 0, 0)),
                  pl.BlockSpec((GM_BLOCK, gw), lambda b, j: (0, 0)),
                  pl.BlockSpec((1, gw), lambda b, j: (0, 0))],
        out_specs=pl.BlockSpec((None, rows, gw), lambda b, j: (b, j, 0)),
        out_shape=jax.ShapeDtypeStruct((bsz, s, gw), BF16),
        compiler_params=_cparams(("parallel", "parallel")),
        name="gmlp",
    )(proj3, proj3, v_norm_g.reshape(1, gw), v_norm_b.reshape(1, gw), w_s, bias_full,
      out_norm_g.reshape(1, gw))


def _out_proj_kernel(yh_ref, yg_ref, h0_ref, wa_ref, wb_ref, g_ref, b_ref, wr_ref, rb_ref,
                     wsg_ref, wsd_ref, h1_ref, hp_ref, sh_ref, eidx_ref, gate_ref, rank_ref, cnt_ref,
                     carry_ref, *, alpha):
    @pl.when(pl.program_id(0) == 0)
    def _():
        carry_ref[...] = jnp.zeros_like(carry_ref)

    tm = h0_ref.shape[0]
    sub = tm // ROUTE_SUBTILES
    w_hi, w_lo = _split_bf16(wr_ref[...])
    logits = []
    for s in range(ROUTE_SUBTILES):
        rows = pl.ds(s * sub, sub)
        mix = _dot(yh_ref[rows, :], wa_ref[...]) + _dot(yg_ref[rows, :], wb_ref[...])
        h1 = _layer_norm(alpha * h0_ref[rows, :] + mix, g_ref[...], b_ref[...])
        h1_ref[rows, :] = h1
        hp_ref[rows, :] = _pack_halves(h1)
        h_hi, h_lo = _split_bf16(h1)
        logits.append(_dot_nt(w_hi, h_hi) + (_dot_nt(w_hi, h_lo) + _dot_nt(w_lo, h_hi)))
        gu = _dot(h_hi, wsg_ref[...])
        half = gu.shape[1] // 2
        hid = (_silu(gu[:, :half]) * gu[:, half:]).astype(BF16)
        sh_ref[rows, :] = _dot(hid, wsd_ref[...]).astype(sh_ref.dtype)
    total = carry_ref[...]
    for s in range(ROUTE_SUBTILES):
        lanes = pl.ds(s * sub, sub)
        eidx, gates, rank, total = _route_tile(logits[s], rb_ref[...], total)
        eidx_ref[:, lanes] = eidx
        gate_ref[:, lanes] = gates
        rank_ref[:, lanes] = rank
    carry_ref[...] = total
    cnt_ref[...] = total.astype(jnp.int32)


def _out_proj(y_hg, y_gm, h0, w_out_bf, g, b, w_router_t, router_bias, ws_gu_bf, ws_down_bf, alpha, tm):
    t, d = h0.shape
    sg = ws_gu_bf.shape[1]
    sd = ws_down_bf.shape[0]
    hw = y_hg.shape[1]
    gw = y_gm.shape[1]
    e = w_router_t.shape[0]
    return pl.pallas_call(
        functools.partial(_out_proj_kernel, alpha=alpha),
        grid=(t // tm,),
        in_specs=[pl.BlockSpec((tm, hw), lambda i: (i, 0)),
                  pl.BlockSpec((tm, gw), lambda i: (i, 0)),
                  pl.BlockSpec((tm, d), lambda i: (i, 0)),
                  pl.BlockSpec((hw, d), lambda i: (0, 0)),
                  pl.BlockSpec((gw, d), lambda i: (0, 0)),
                  pl.BlockSpec((1, d), lambda i: (0, 0)),
                  pl.BlockSpec((1, d), lambda i: (0, 0)),
                  pl.BlockSpec((e, d), lambda i: (0, 0)),
                  pl.BlockSpec((e, 1), lambda i: (0, 0)),
                  pl.BlockSpec((d, sg), lambda i: (0, 0)),
                  pl.BlockSpec((sd, d), lambda i: (0, 0))],
        out_specs=[pl.BlockSpec((tm, d), lambda i: (i, 0)),
                   pl.BlockSpec((tm, d // 2), lambda i: (i, 0)),
                   pl.BlockSpec((tm, d), lambda i: (i, 0)),
                   pl.BlockSpec((TOP_K, tm), lambda i: (0, i)),
                   pl.BlockSpec((TOP_K, tm), lambda i: (0, i)),
                   pl.BlockSpec((TOP_K, tm), lambda i: (0, i)),
                   pl.BlockSpec((e, 1), lambda i: (0, 0))],
        out_shape=[jax.ShapeDtypeStruct((t, d), F32), jax.ShapeDtypeStruct((t, d // 2), jnp.uint32),
                   jax.ShapeDtypeStruct((t, d), BF16),
                   jax.ShapeDtypeStruct((TOP_K, t), jnp.int32),
                   jax.ShapeDtypeStruct((TOP_K, t), F32),
                   jax.ShapeDtypeStruct((TOP_K, t), jnp.int32),
                   jax.ShapeDtypeStruct((e, 1), jnp.int32)],
        scratch_shapes=[pltpu.VMEM((e, 1), F32)],
        compiler_params=_cparams(("arbitrary",)),
        name="out_proj",
    )(y_hg, y_gm, h0, w_out_bf[:hw], w_out_bf[hw:], g.reshape(1, d), b.reshape(1, d), w_router_t,
      router_bias.astype(F32).reshape(e, 1), ws_gu_bf, ws_down_bf)


def _route_tile(logits, bias, carry):
    n_exp, tt = logits.shape
    per_group = n_exp // N_GROUPS
    neg_inf = jnp.float32(-jnp.inf)
    scores = jax.nn.sigmoid(logits)
    biased = scores + bias

    gio = lax.broadcasted_iota(jnp.int32, (per_group, tt), 0)
    blocks, gs_rows = [], []
    for g in range(N_GROUPS):
        blk = biased[g * per_group:(g + 1) * per_group, :]
        m1 = jnp.max(blk, axis=0, keepdims=True)
        first = jnp.min(jnp.where(blk == m1, gio, per_group), axis=0, keepdims=True)
        m2 = jnp.max(jnp.where(gio == first, neg_inf, blk), axis=0, keepdims=True)
        blocks.append(blk)
        gs_rows.append(m1 + m2)
    gs = jnp.concatenate(gs_rows, axis=0)
    gidx = lax.broadcasted_iota(jnp.int32, (N_GROUPS, tt), 0)
    beaten = jnp.zeros((N_GROUPS, tt), jnp.int32)
    for g in range(N_GROUPS):
        r = gs_rows[g]
        beaten = beaten + jnp.where((r > gs) | ((r == gs) & (g < gidx)), 1, 0)
    keep = beaten < TOPK_GROUPS
    cand = jnp.concatenate(
        [jnp.where(keep[g:g + 1, :], blocks[g], neg_inf) for g in range(N_GROUPS)], axis=0)

    eio = lax.broadcasted_iota(jnp.int32, (n_exp, tt), 0)
    idx_rows, w_rows = [], []
    chosen = jnp.zeros((n_exp, tt), F32)
    for _ in range(TOP_K):
        m = jnp.max(cand, axis=0, keepdims=True)
        idx = jnp.min(jnp.where(cand == m, eio, n_exp), axis=0, keepdims=True)
        sel = eio == idx
        w_rows.append(jnp.sum(jnp.where(sel, scores, 0.0), axis=0, keepdims=True))
        idx_rows.append(idx)
        cand = jnp.where(sel, neg_inf, cand)
        chosen = jnp.where(sel, 1.0, chosen)
    eidx = jnp.concatenate(idx_rows, axis=0)
    w = jnp.concatenate(w_rows, axis=0)
    gates = w / jnp.sum(w, axis=0, keepdims=True) * ROUTED_SCALE

    tr = lax.broadcasted_iota(jnp.int32, (tt, tt), 0)
    tc = lax.broadcasted_iota(jnp.int32, (tt, tt), 1)
    before = (tr < tc).astype(BF16)
    cum = _dot(chosen.astype(BF16), before) + carry
    rank = jnp.concatenate(
        [jnp.sum(jnp.where(eio == idx_rows[k], cum, 0.0), axis=0, keepdims=True)
         for k in range(TOP_K)], axis=0).astype(jnp.int32)
    total = carry + jnp.sum(chosen, axis=1, keepdims=True)
    return eidx, gates, rank, total


def _dest_kernel(ps_ref, eidx_ref, rank_ref, dest_ref, *, n_exp):
    eidx = eidx_ref[...]

    def body(e, acc):
        return acc + jnp.where(eidx == e, ps_ref[e], 0)

    dest_ref[...] = lax.fori_loop(0, n_exp, body, rank_ref[...])


def _dest(pad_start, eidx, rank, tt):
    k, t = eidx.shape
    n_exp = pad_start.shape[0]
    return pl.pallas_call(
        functools.partial(_dest_kernel, n_exp=n_exp),
        grid_spec=pltpu.PrefetchScalarGridSpec(
            num_scalar_prefetch=1,
            grid=(t // tt,),
            in_specs=[pl.BlockSpec((k, tt), lambda i, ps: (0, i)),
                      pl.BlockSpec((k, tt), lambda i, ps: (0, i))],
            out_specs=pl.BlockSpec((k, tt), lambda i, ps: (0, i))),
        out_shape=jax.ShapeDtypeStruct((k, t), jnp.int32),
        compiler_params=_cparams(("parallel",)),
        name="dest",
    )(pad_start, eidx, rank)


def _chunked(dest, chunk):
    k, t = dest.shape
    return dest.reshape(k, t // chunk, chunk).transpose(1, 0, 2)


def _sc_workers():
    info = plsc.get_sparse_core_info()
    return info.num_cores, info.num_cores * info.num_subcores


def _dispatch(dest, h1, n_rows, chunk):
    t, dh = h1.shape
    n_cores, n_workers = _sc_workers()
    per_worker = t // chunk // n_workers
    mesh = plsc.VectorSubcoreMesh(core_axis_name="c", subcore_axis_name="s")

    @functools.partial(
        pl.kernel, mesh=mesh,
        out_type=jax.ShapeDtypeStruct((n_rows, dh), h1.dtype),
        scratch_types=[pltpu.VMEM((TOP_K, chunk), jnp.int32),
                       pltpu.VMEM((chunk, dh), h1.dtype),
                       pltpu.SemaphoreType.DMA],
        name="dispatch",
    )
    def scatter_rows(x_hbm, dest_hbm, o_hbm, idx_v, rows_v, sem):
        wid = lax.axis_index("s") * n_cores + lax.axis_index("c")

        @pl.loop(0, per_worker)
        def _(j):
            c = wid * per_worker + j
            pltpu.sync_copy(dest_hbm.at[c], idx_v)
            pltpu.sync_copy(x_hbm.at[pl.ds(c * chunk, chunk)], rows_v)
            copies = [pltpu.async_copy(rows_v, o_hbm.at[idx_v.at[k]], sem) for k in range(TOP_K)]
            for cp in copies:
                cp.wait()

    return scatter_rows(h1, _chunked(dest, chunk))


def _gather_back(dest_chunks, ys):
    n_chunks, k_top, chunk = dest_chunks.shape
    t = n_chunks * chunk
    dh = ys.shape[1]
    n_cores, n_workers = _sc_workers()
    per_worker = n_chunks // n_workers
    mesh = plsc.VectorSubcoreMesh(core_axis_name="c", subcore_axis_name="s")

    @functools.partial(
        pl.kernel, mesh=mesh,
        out_type=jax.ShapeDtypeStruct((k_top, t, dh), ys.dtype),
        scratch_types=[pltpu.VMEM((k_top, chunk), jnp.int32),
                       pltpu.VMEM((k_top, chunk, dh), ys.dtype),
                       pltpu.SemaphoreType.DMA],
        name="gather_back",
    )
    def gather_rows(ys_hbm, dest_hbm, o_hbm, idx_v, rows_v, sem):
        wid = lax.axis_index("s") * n_cores + lax.axis_index("c")

        @pl.loop(0, per_worker)
        def _(j):
            c = wid * per_worker + j
            pltpu.sync_copy(dest_hbm.at[c], idx_v)
            copies = [pltpu.async_copy(ys_hbm.at[idx_v.at[k]], rows_v.at[k], sem)
                      for k in range(k_top)]
            for cp in copies:
                cp.wait()
            for k in range(k_top):
                pltpu.sync_copy(rows_v.at[k], o_hbm.at[k, pl.ds(c * chunk, chunk)])

    return gather_rows(ys, dest_chunks)


def _experts_kernel(bs_ref, cnt_ref, wgu_hbm, wd_hbm, xs_hbm, ys_hbm, wgu_buf, wd_buf, gu_bf, d_bf,
                    xbuf, ybuf, wsem, xsem, ysem):
    e = pl.program_id(0)
    n_exp = pl.num_programs(0)
    n_slots, mb, _ = xbuf.shape
    ahead = n_slots - 1
    w_slots = wgu_buf.shape[0]
    n_blocks = bs_ref[n_exp]
    b0 = bs_ref[e]
    b1 = bs_ref[e + 1]
    count = cnt_ref[e]

    def w_copies(ex):
        slot = lax.rem(ex, w_slots)
        return (pltpu.make_async_copy(wgu_hbm.at[ex], wgu_buf.at[slot], wsem.at[slot]),
                pltpu.make_async_copy(wd_hbm.at[ex], wd_buf.at[slot], wsem.at[slot]))

    @pl.when(e == 0)
    def _():
        for i in range(w_slots - 1):
            @pl.when(i < n_exp)
            def _():
                for cp in w_copies(i):
                    cp.start()

    @pl.when(e + w_slots - 1 < n_exp)
    def _():
        for cp in w_copies(e + w_slots - 1):
            cp.start()

    def x_copy(b):
        slot = lax.rem(b, n_slots)
        return pltpu.make_async_copy(xs_hbm.at[pl.ds(pl.multiple_of(b * mb, mb), mb)],
                                     xbuf.at[slot], xsem.at[slot])

    def y_copy(b):
        slot = lax.rem(b, n_slots)
        return pltpu.make_async_copy(ybuf.at[slot],
                                     ys_hbm.at[pl.ds(pl.multiple_of(b * mb, mb), mb)], ysem.at[slot])

    @pl.when(e == 0)
    def _():
        for i in range(ahead):
            @pl.when(i < n_blocks)
            def _():
                x_copy(i).start(priority=RING_DMA_PRIORITY)

    for cp in w_copies(e):
        cp.wait()
    w_slot = lax.rem(e, w_slots)

    @pl.when(b1 > b0)
    def _():
        gu_bf[...] = wgu_buf[w_slot].astype(BF16)
        d_bf[...] = wd_buf[w_slot].astype(BF16)

    kh = gu_bf.shape[0] // 2
    half = gu_bf.shape[1] // 2

    def block(b, carry):
        slot = lax.rem(b, n_slots)
        x_copy(b).wait()

        @pl.when(b + ahead < n_blocks)
        def _():
            x_copy(b + ahead).start(priority=RING_DMA_PRIORITY)

        @pl.when(b >= n_slots)
        def _():
            y_copy(b - n_slots).wait()

        rows = lax.broadcasted_iota(jnp.int32, (mb, xbuf.shape[2]), 0)
        valid = count - (b - b0) * mb
        words = jnp.where(rows < valid, xbuf[slot], jnp.uint32(0))
        xa, xb = _unpack_halves(words)
        gu = _dot(xa.astype(BF16), gu_bf[0:kh, :]) + _dot(xb.astype(BF16), gu_bf[kh:, :])
        hid = _silu(gu[:, :half]) * gu[:, half:]
        ybuf[slot] = _pack_halves(_dot(hid.astype(BF16), d_bf[...]))
        y_copy(b).start(priority=RING_DMA_PRIORITY)
        return carry

    lax.fori_loop(b0, b1, block, 0)

    @pl.when(e == n_exp - 1)
    def _():
        for i in range(1, n_slots + 1):
            @pl.when(n_blocks >= i)
            def _():
                y_copy(n_blocks - i).wait()


def _experts(block_start, counts, xs, w_gu, w_down):
    n_rows, dh = xs.shape
    n_exp, d, gu_w = w_gu.shape
    ed = w_down.shape[1]
    return pl.pallas_call(
        _experts_kernel,
        grid_spec=pltpu.PrefetchScalarGridSpec(
            num_scalar_prefetch=2,
            grid=(n_exp,),
            in_specs=[pl.BlockSpec(memory_space=pl.ANY),
                      pl.BlockSpec(memory_space=pl.ANY),
                      pl.BlockSpec(memory_space=pl.ANY)],
            out_specs=pl.BlockSpec(memory_space=pl.ANY),
            scratch_shapes=[pltpu.VMEM((WEIGHT_SLOTS, d, gu_w), w_gu.dtype),
                            pltpu.VMEM((WEIGHT_SLOTS, ed, d), w_down.dtype),
                            pltpu.VMEM((d, gu_w), BF16), pltpu.VMEM((ed, d), BF16),
                            pltpu.VMEM((RING_SLOTS, MOE_BLOCK, dh), jnp.uint32),
                            pltpu.VMEM((RING_SLOTS, MOE_BLOCK, dh), jnp.uint32),
                            pltpu.SemaphoreType.DMA((WEIGHT_SLOTS,)),
                            pltpu.SemaphoreType.DMA((RING_SLOTS,)),
                            pltpu.SemaphoreType.DMA((RING_SLOTS,))]),
        out_shape=jax.ShapeDtypeStruct((n_rows, dh), jnp.uint32),
        compiler_params=_cparams(("arbitrary",)),
        name="experts",
    )(block_start, counts, w_gu, w_down, xs)


def _combine_kernel(h_ref, sh_ref, w_ref, g_ref, b_ref, yg_ref, *rest, alpha):
    o_ref = rest[-1]
    acc = alpha * h_ref[...] + sh_ref[...].astype(F32)
    dh = acc.shape[1] // 2
    acc_a = acc[:, :dh]
    acc_b = acc[:, dh:]
    w = w_ref[...]
    for k in range(TOP_K):
        ya, yb = _unpack_halves(yg_ref[k])
        acc_a = acc_a + ya * w[:, k:k + 1]
        acc_b = acc_b + yb * w[:, k:k + 1]
    o_ref[...] = _layer_norm(jnp.concatenate([acc_a, acc_b], axis=1), g_ref[...], b_ref[...])


def _combine(h1, shared, w_tk, g, b, yg, prev, part, alpha, tc):
    t, d = h1.shape
    tiles = yg.shape[1] // tc
    off = part * tiles
    in_specs = [pl.BlockSpec((tc, d), lambda i: (i + off, 0)),
                pl.BlockSpec((tc, d), lambda i: (i + off, 0)),
                pl.BlockSpec((tc, TOP_K), lambda i: (i + off, 0)),
                pl.BlockSpec((1, d), lambda i: (0, 0)),
                pl.BlockSpec((1, d), lambda i: (0, 0)),
                pl.BlockSpec((TOP_K, tc, yg.shape[2]), lambda i: (0, i, 0))]
    args = [h1, shared, w_tk, g.reshape(1, d), b.reshape(1, d), yg]
    aliases = {}
    if prev is not None:
        in_specs.append(pl.BlockSpec(memory_space=pl.ANY))
        args.append(prev)
        aliases = {len(args) - 1: 0}
    return pl.pallas_call(
        functools.partial(_combine_kernel, alpha=alpha),
        grid=(tiles,),
        in_specs=in_specs,
        out_specs=pl.BlockSpec((tc, d), lambda i: (i + off, 0)),
        out_shape=jax.ShapeDtypeStruct((t, d), F32),
        input_output_aliases=aliases,
        compiler_params=_cparams(("parallel",)),
        name="combine",
    )(*args)


def _moe(h1, h1_packed, shared, eidx, w_kt, rank, counts, w_exp_gu, w_exp_down, ln_g, ln_b, alpha):
    t, d = h1.shape
    n_exp = counts.shape[0]

    counts = counts.reshape(n_exp)
    blocks_per_expert = (counts + MOE_BLOCK - 1) // MOE_BLOCK
    block_start = jnp.concatenate(
        [jnp.zeros((1,), jnp.int32), jnp.cumsum(blocks_per_expert).astype(jnp.int32)])
    pad_start = block_start[:n_exp] * MOE_BLOCK
    n_blocks = -(-(t * TOP_K + n_exp * (MOE_BLOCK - 1)) // MOE_BLOCK)

    dest = _dest(pad_start, eidx, rank, tt=min(2048, t))
    xs = _dispatch(dest, h1_packed, n_blocks * MOE_BLOCK, chunk=DISPATCH_CHUNK)
    ys = _experts(block_start, counts, xs, w_exp_gu, w_exp_down)
    dest_chunks = _chunked(dest, GATHER_CHUNK)
    per_part = dest_chunks.shape[0] // COMBINE_PARTS
    w_tk = w_kt.T
    out = None
    for part in range(COMBINE_PARTS):
        yg = _gather_back(dest_chunks[part * per_part:(part + 1) * per_part], ys)
        out = _combine(h1, shared, w_tk, ln_g, ln_b, yg, out, part, alpha, tc=min(256, t))
    return out


def kernel(x, ln_in_g, ln_in_b, w_in, hg_lb_logits, hg_norm_g, gm_v_norm_g, gm_v_norm_b, gm_w_s, gm_b_s, gm_out_norm_g, w_out, ln1_g, ln1_b, w_router, router_bias, w_exp_gu, w_exp_down, w_shared_gu, w_shared_down, ln2_g, ln2_b):
    bsz, s, d = x.shape
    depth = w_in.shape[0]
    assert depth == 1, "the lower-bound table row used in the hgrn kernel assumes one layer"
    alpha = (2.0 * depth) ** 0.25
    t = bsz * s
    hg_width = hg_norm_g.shape[1]
    gm_width = gm_v_norm_g.shape[1]
    tm = min(512, t)

    x2 = x.reshape(t, d)
    h0, proj, f_pre = _in_proj(x2, ln_in_g, ln_in_b, w_in[0].astype(BF16), hg_width, hg_width, tm)
    proj3 = proj.reshape(bsz, s, proj.shape[1])
    y_hg = _hgrn(proj3, f_pre.reshape(bsz, s, hg_width), hg_lb_logits.astype(F32), hg_norm_g[0],
                 sb=min(512, s))
    y_gm = _gmlp(proj3, gm_v_norm_g[0], gm_v_norm_b[0], gm_w_s[0], gm_b_s[0], gm_out_norm_g[0],
                 u_blk=4 * hg_width // gm_width, n_blocks=min(4, s // GM_BLOCK))
    h1, h1_packed, shared, eidx, gates, rank, counts = _out_proj(
        y_hg.reshape(t, hg_width), y_gm.reshape(t, gm_width), h0, w_out[0].astype(BF16),
        ln1_g[0], ln1_b[0], w_router[0].T, router_bias[0], w_shared_gu[0].astype(BF16),
        w_shared_down[0].astype(BF16), alpha, tm)
    out = _moe(h1, h1_packed, shared, eidx, gates, rank, counts, w_exp_gu[0], w_exp_down[0],
               ln2_g[0], ln2_b[0], alpha)
    return out.reshape(bsz, s, d)
```

```python
import functools

import jax
import jax.numpy as jnp
from jax import lax
from jax.experimental import pallas as pl
from jax.experimental.pallas import tpu as pltpu
from jax.experimental.pallas import tpu_sc as plsc

F32 = jnp.float32
BF16 = jnp.bfloat16

LN_EPS = 1e-5
RMS_EPS = 1e-6
CHUNK = 64
SUB = 16
MAX_FACTORED_EXPONENT = 80.0
FACTORED_MIN_LB = float(2.718281828459045 ** (-MAX_FACTORED_EXPONENT / CHUNK))
HEAD_DIM = 128
GM_BLOCK = 128
TOP_K = 8
N_GROUPS = 8
TOPK_GROUPS = 4
ROUTED_SCALE = 2.5
MOE_BLOCK = 128
WEIGHT_SLOTS = 3
RING_SLOTS = 8
RING_DMA_PRIORITY = 1
DISPATCH_CHUNK = 64
ROUTE_SUBTILES = 2
COMBINE_PARTS = 4
GATHER_CHUNK = 16
VMEM_LIMIT = 48 * 1024 * 1024


def _cparams(sem):
    return pltpu.CompilerParams(dimension_semantics=sem, vmem_limit_bytes=VMEM_LIMIT)


def _layer_norm(x, g, b):
    mu = jnp.mean(x, axis=-1, keepdims=True)
    xc = x - mu
    var = jnp.mean(xc * xc, axis=-1, keepdims=True)
    return xc * lax.rsqrt(var + LN_EPS) * g + b


def _silu(x):
    return x * jax.nn.sigmoid(x)


def _gelu(x):
    return 0.5 * x * (1.0 + lax.erf(x * (2.0 ** -0.5)))


def _dot(a, b):
    return jnp.dot(a, b, preferred_element_type=F32)


def _dot_nt(a, b, precision=None):
    return lax.dot_general(a, b, (((1,), (1,)), ((), ())), preferred_element_type=F32,
                           precision=precision)


def _dot_tn(a, b):
    return lax.dot_general(a, b, (((0,), (0,)), ((), ())), preferred_element_type=F32)


def _split_bf16(x):
    hi = x.astype(BF16)
    return hi, (x - hi.astype(F32)).astype(BF16)


def _pack_halves(x):
    n = x.shape[1] // 2
    hi = lax.bitcast_convert_type(x[:, :n].astype(BF16).astype(F32), jnp.uint32)
    lo = lax.bitcast_convert_type(x[:, n:].astype(BF16).astype(F32), jnp.uint32)
    return hi | (lo >> 16)


def _unpack_halves(w):
    a = lax.bitcast_convert_type(w & jnp.uint32(0xFFFF0000), F32)
    b = lax.bitcast_convert_type(w << 16, F32)
    return a, b


def _in_proj_kernel(x_ref, g_ref, b_ref, w_ref, h_ref, p_ref, f_ref, *, f_lo):
    h = _layer_norm(x_ref[...], g_ref[...], b_ref[...])
    h_ref[...] = h
    p = _dot(h.astype(BF16), w_ref[...])
    p_ref[...] = p.astype(BF16)
    f_ref[...] = p[:, f_lo:f_lo + f_ref.shape[1]]


def _in_proj(x2, g, b, w_bf, f_lo, f_width, tm):
    t, d = x2.shape
    n = w_bf.shape[1]
    return pl.pallas_call(
        functools.partial(_in_proj_kernel, f_lo=f_lo),
        grid=(t // tm,),
        in_specs=[pl.BlockSpec((tm, d), lambda i: (i, 0)),
                  pl.BlockSpec((1, d), lambda i: (0, 0)),
                  pl.BlockSpec((1, d), lambda i: (0, 0)),
                  pl.BlockSpec((d, n), lambda i: (0, 0))],
        out_specs=[pl.BlockSpec((tm, d), lambda i: (i, 0)),
                   pl.BlockSpec((tm, n), lambda i: (i, 0)),
                   pl.BlockSpec((tm, f_width), lambda i: (i, 0))],
        out_shape=[jax.ShapeDtypeStruct((t, d), F32), jax.ShapeDtypeStruct((t, n), BF16),
                   jax.ShapeDtypeStruct((t, f_width), F32)],
        compiler_params=_cparams(("parallel",)),
        name="in_proj",
    )(x2, g.reshape(1, d), b.reshape(1, d), w_bf)


def _mixer_kernel(q_ref, f_ref, i_ref, g_ref, lbl_ref, gn_ref, u_ref, v_ref, vg_ref, vb_ref, ws_ref,
                  bs_ref, og_ref, o_ref, ogm_ref, st_ref, *, n_chunks):
    @pl.when(pl.program_id(1) == 0)
    def _():
        st_ref[...] = jnp.zeros_like(st_ref)

    lg = lbl_ref[...]
    ex = jnp.exp(lg - jnp.max(lg, axis=0, keepdims=True))
    lb_all = ex[0:1, :] / jnp.sum(ex, axis=0, keepdims=True)
    gn_all = gn_ref[...]

    c = CHUNK
    heads = st_ref.shape[0]
    row = lax.broadcasted_iota(jnp.int32, (c, c), 0)
    col = lax.broadcasted_iota(jnp.int32, (c, c), 1)
    tril = (row >= col).astype(BF16)
    sub_row = lax.broadcasted_iota(jnp.int32, (SUB, HEAD_DIM), 0)
    ones = jnp.ones((HEAD_DIM, HEAD_DIM), BF16)
    neg_inf = jnp.float32(-jnp.inf)

    def intra_pairwise(q, k, v, v_bf, b, o_inter):
        outs = []
        for blk in range(c // SUB):
            lo = blk * SUB
            b_i = b[lo:lo + SUB, :]
            q_i = q[lo:lo + SUB, :]
            k_i = k[lo:lo + SUB, :]
            v_i = v[lo:lo + SUB, :]
            prods = []
            for j in range(SUB):
                diff = jnp.where(sub_row >= j, b_i - b_i[j:j + 1, :], neg_inf)
                prods.append(q_i * k_i[j:j + 1, :] * jnp.exp(diff))
            p_all = jnp.concatenate(prods, axis=0).astype(BF16)
            r_all = _dot(p_all, ones)
            o_blk = o_inter[lo:lo + SUB, :]
            for j in range(SUB):
                o_blk = o_blk + r_all[j * SUB:(j + 1) * SUB, :] * v_i[j:j + 1, :]
            if blk > 0:
                b_ref = b[lo - 1:lo, :]
                q_hat = (q_i * jnp.exp(b_i - b_ref)).astype(BF16)
                k_hat = (k[0:lo, :] * jnp.exp(b_ref - b[0:lo, :])).astype(BF16)
                scores = _dot_nt(q_hat, k_hat)
                o_blk = o_blk + _dot(scores.astype(BF16), v_bf[0:lo, :])
            outs.append(o_blk)
        return jnp.concatenate(outs, axis=0)

    def one_chunk(r0, factored):
        rows = pl.ds(r0, c)
        hs = [slice(h * HEAD_DIM, (h + 1) * HEAD_DIM) for h in range(heads)]
        q = _silu(q_ref[rows, :].astype(F32))
        f = lb_all + (1.0 - lb_all) * jax.nn.sigmoid(f_ref[rows, :])
        v_bf = i_ref[rows, :]
        v = v_bf.astype(F32)
        lf = jnp.log(f)
        k = 1.0 - f
        lf_hi, lf_lo = _split_bf16(lf)
        b2 = _dot(tril, jnp.concatenate([lf_hi, lf_lo], axis=1))
        width = lf.shape[1]
        b = b2[:, :width] + b2[:, width:]
        b_last = b[c - 1:c, :]
        q_dec = (q * jnp.exp(b)).astype(BF16)
        k_dec = (k * jnp.exp(b_last - b)).astype(BF16)
        decay = jnp.exp(b_last)
        st = [st_ref[h] for h in range(heads)]
        st_bf = [s.astype(BF16) for s in st]
        if factored:
            k_grow = (k * jnp.exp(-b)).astype(BF16)
            both = [_dot_nt(q_dec[:, hs[h]], jnp.concatenate([st_bf[h], k_grow[:, hs[h]]], axis=0))
                    for h in range(heads)]
            scores = [jnp.where(row >= col, both[h][:, HEAD_DIM:], 0.0).astype(BF16)
                      for h in range(heads)]
            outs = [both[h][:, :HEAD_DIM] + _dot(scores[h], v_bf[:, hs[h]]) for h in range(heads)]
        else:
            outs = [intra_pairwise(q[:, hs[h]], k[:, hs[h]], v[:, hs[h]], v_bf[:, hs[h]],
                                   b[:, hs[h]], _dot_nt(q_dec[:, hs[h]], st_bf[h]))
                    for h in range(heads)]
        for h in range(heads):
            st_ref[h] = st[h] * decay[:, hs[h]] + _dot_tn(v_bf[:, hs[h]], k_dec[:, hs[h]])
        inv = [lax.rsqrt(jnp.mean(o * o, axis=-1, keepdims=True) + RMS_EPS) for o in outs]
        o = jnp.concatenate([outs[h] * inv[h] for h in range(heads)], axis=1)
        o_ref[rows, :] = (o * gn_all * _silu(g_ref[rows, :].astype(F32))).astype(o_ref.dtype)

    groups = ws_ref.shape[0]
    gdim = u_ref.shape[-1] // groups
    gm_row = lax.broadcasted_iota(jnp.int32, (GM_BLOCK, GM_BLOCK), 0) // CHUNK
    gm_col = lax.broadcasted_iota(jnp.int32, (GM_BLOCK, GM_BLOCK), 1) // CHUNK
    ws = [jnp.where(gm_row >= gm_col, ws_ref[g], 0.0).astype(BF16) for g in range(groups)]

    def gmlp_block(r0):
        rows = pl.ds(r0, GM_BLOCK)
        u = _gelu(u_ref[rows, :].astype(F32))
        v = _layer_norm(_gelu(v_ref[rows, :].astype(F32)), vg_ref[...], vb_ref[...])
        v_bf = v.astype(BF16)
        mixed = jnp.concatenate(
            [_dot(ws[g], v_bf[:, g * gdim:(g + 1) * gdim]) for g in range(groups)], axis=1)
        y = u * (mixed + bs_ref[...])
        inv = lax.rsqrt(jnp.mean(y * y, axis=-1, keepdims=True) + RMS_EPS)
        ogm_ref[rows, :] = (y * inv * og_ref[...]).astype(ogm_ref.dtype)

    def run_chunks(factored):
        per_trip = 4 if (factored and n_chunks % 4 == 0) else 1
        trip_rows = per_trip * c
        inline_gmlp = trip_rows % GM_BLOCK == 0

        def chunk_body(ci, carry):
            for u in range(per_trip):
                one_chunk(pl.multiple_of((ci * per_trip + u) * c, c), factored)
            if inline_gmlp:
                for n in range(trip_rows // GM_BLOCK):
                    gmlp_block(pl.multiple_of(ci * trip_rows + n * GM_BLOCK, GM_BLOCK))
            return carry

        lax.fori_loop(0, n_chunks // per_trip, chunk_body, 0)
        if not inline_gmlp:
            for n in range(n_chunks * c // GM_BLOCK):
                gmlp_block(n * GM_BLOCK)

    bounded = jnp.min(lb_all) >= FACTORED_MIN_LB
    pl.when(bounded)(lambda: run_chunks(True))
    pl.when(jnp.logical_not(bounded))(lambda: run_chunks(False))


def _mixer(proj3, f_pre3, lb_logits, norm_g, v_norm_g, v_norm_b, w_s, b_s, out_norm_g, sb):
    bsz, s, _ = proj3.shape
    hg_width = norm_g.shape[0]
    heads = hg_width // HEAD_DIM
    gw = v_norm_g.shape[0]
    assert gw == hg_width and sb % GM_BLOCK == 0
    groups = w_s.shape[0]
    bias_full = jnp.repeat(b_s.T, gw // groups, axis=1)

    def sect(k):
        return pl.BlockSpec((None, sb, hg_width), lambda b, j, k=k: (b, j, k))

    row_vec = pl.BlockSpec((1, hg_width), lambda b, j: (0, 0))
    return pl.pallas_call(
        functools.partial(_mixer_kernel, n_chunks=sb // CHUNK),
        grid=(bsz, s // sb),
        in_specs=[sect(0), sect(0), sect(2), sect(3),
                  pl.BlockSpec((lb_logits.shape[0], hg_width), lambda b, j: (0, 0)),
                  row_vec, sect(4), sect(5), row_vec, row_vec,
                  pl.BlockSpec((groups, GM_BLOCK, GM_BLOCK), lambda b, j: (0, 0, 0)),
                  pl.BlockSpec((GM_BLOCK, gw), lambda b, j: (0, 0)),
                  row_vec],
        out_specs=[pl.BlockSpec((None, sb, hg_width), lambda b, j: (b, j, 0)),
                   pl.BlockSpec((None, sb, gw), lambda b, j: (b, j, 0))],
        out_shape=[jax.ShapeDtypeStruct((bsz, s, hg_width), BF16),
                   jax.ShapeDtypeStruct((bsz, s, gw), BF16)],
        scratch_shapes=[pltpu.VMEM((heads, HEAD_DIM, HEAD_DIM), F32)],
        compiler_params=_cparams(("parallel", "arbitrary")),
        name="mixer",
    )(proj3, f_pre3, proj3, proj3, lb_logits, norm_g.reshape(1, hg_width), proj3, proj3,
      v_norm_g.reshape(1, gw), v_norm_b.reshape(1, gw), w_s, bias_full, out_norm_g.reshape(1, gw))


def _out_proj_kernel(yh_ref, yg_ref, h0_ref, wa_ref, wb_ref, g_ref, b_ref, wr_ref, rb_ref,
                     wsg_ref, wsd_ref, base_ref, hp_ref, eidx_ref, gate_ref, rank_ref, cnt_ref,
                     carry_ref, *, alpha):
    @pl.when(pl.program_id(0) == 0)
    def _():
        carry_ref[...] = jnp.zeros_like(carry_ref)

    tm = h0_ref.shape[0]
    sub = tm // ROUTE_SUBTILES
    w_hi, w_lo = _split_bf16(wr_ref[...])
    logits = []
    for s in range(ROUTE_SUBTILES):
        rows = pl.ds(s * sub, sub)
        mix = _dot(yh_ref[rows, :], wa_ref[...]) + _dot(yg_ref[rows, :], wb_ref[...])
        h1 = _layer_norm(alpha * h0_ref[rows, :] + mix, g_ref[...], b_ref[...])
        hp_ref[rows, :] = _pack_halves(h1)
        h_hi, h_lo = _split_bf16(h1)
        logits.append(_dot_nt(w_hi, h_hi) + (_dot_nt(w_hi, h_lo) + _dot_nt(w_lo, h_hi)))
        gu = _dot(h_hi, wsg_ref[...])
        half = gu.shape[1] // 2
        hid = (_silu(gu[:, :half]) * gu[:, half:]).astype(BF16)
        base_ref[rows, :] = alpha * h1 + _dot(hid, wsd_ref[...])
    total = carry_ref[...]
    for s in range(ROUTE_SUBTILES):
        lanes = pl.ds(s * sub, sub)
        eidx, gates, rank, total = _route_tile(logits[s], rb_ref[...], total)
        eidx_ref[:, lanes] = eidx
        gate_ref[:, lanes] = gates
        rank_ref[:, lanes] = rank
    carry_ref[...] = total
    cnt_ref[...] = total.astype(jnp.int32)


def _out_proj(y_hg, y_gm, h0, w_out_bf, g, b, w_router_t, router_bias, ws_gu_bf, ws_down_bf, alpha, tm):
    t, d = h0.shape
    sg = ws_gu_bf.shape[1]
    sd = ws_down_bf.shape[0]
    hw = y_hg.shape[1]
    gw = y_gm.shape[1]
    e = w_router_t.shape[0]
    return pl.pallas_call(
        functools.partial(_out_proj_kernel, alpha=alpha),
        grid=(t // tm,),
        in_specs=[pl.BlockSpec((tm, hw), lambda i: (i, 0)),
                  pl.BlockSpec((tm, gw), lambda i: (i, 0)),
                  pl.BlockSpec((tm, d), lambda i: (i, 0)),
                  pl.BlockSpec((hw, d), lambda i: (0, 0)),
                  pl.BlockSpec((gw, d), lambda i: (0, 0)),
                  pl.BlockSpec((1, d), lambda i: (0, 0)),
                  pl.BlockSpec((1, d), lambda i: (0, 0)),
                  pl.BlockSpec((e, d), lambda i: (0, 0)),
                  pl.BlockSpec((e, 1), lambda i: (0, 0)),
                  pl.BlockSpec((d, sg), lambda i: (0, 0)),
                  pl.BlockSpec((sd, d), lambda i: (0, 0))],
        out_specs=[pl.BlockSpec((tm, d), lambda i: (i, 0)),
                   pl.BlockSpec((tm, d // 2), lambda i: (i, 0)),
                   pl.BlockSpec((TOP_K, tm), lambda i: (0, i)),
                   pl.BlockSpec((TOP_K, tm), lambda i: (0, i)),
                   pl.BlockSpec((TOP_K, tm), lambda i: (0, i)),
                   pl.BlockSpec((e, 1), lambda i: (0, 0))],
        out_shape=[jax.ShapeDtypeStruct((t, d), F32), jax.ShapeDtypeStruct((t, d // 2), jnp.uint32),
                   jax.ShapeDtypeStruct((TOP_K, t), jnp.int32),
                   jax.ShapeDtypeStruct((TOP_K, t), F32),
                   jax.ShapeDtypeStruct((TOP_K, t), jnp.int32),
                   jax.ShapeDtypeStruct((e, 1), jnp.int32)],
        scratch_shapes=[pltpu.VMEM((e, 1), F32)],
        compiler_params=_cparams(("arbitrary",)),
        name="out_proj",
    )(y_hg, y_gm, h0, w_out_bf[:hw], w_out_bf[hw:], g.reshape(1, d), b.reshape(1, d), w_router_t,
      router_bias.astype(F32).reshape(e, 1), ws_gu_bf, ws_down_bf)


def _route_tile(logits, bias, carry):
    n_exp, tt = logits.shape
    per_group = n_exp // N_GROUPS
    neg_inf = jnp.float32(-jnp.inf)
    scores = jax.nn.sigmoid(logits)
    biased = scores + bias

    gio = lax.broadcasted_iota(jnp.int32, (per_group, tt), 0)
    blocks, gs_rows = [], []
    for g in range(N_GROUPS):
        blk = biased[g * per_group:(g + 1) * per_group, :]
        m1 = jnp.max(blk, axis=0, keepdims=True)
        first = jnp.min(jnp.where(blk == m1, gio, per_group), axis=0, keepdims=True)
        m2 = jnp.max(jnp.where(gio == first, neg_inf, blk), axis=0, keepdims=True)
        blocks.append(blk)
        gs_rows.append(m1 + m2)
    gs = jnp.concatenate(gs_rows, axis=0)
    gidx = lax.broadcasted_iota(jnp.int32, (N_GROUPS, tt), 0)
    beaten = jnp.zeros((N_GROUPS, tt), jnp.int32)
    for g in range(N_GROUPS):
        r = gs_rows[g]
        beaten = beaten + jnp.where((r > gs) | ((r == gs) & (g < gidx)), 1, 0)
    keep = beaten < TOPK_GROUPS
    cand = jnp.concatenate(
        [jnp.where(keep[g:g + 1, :], blocks[g], neg_inf) for g in range(N_GROUPS)], axis=0)

    eio = lax.broadcasted_iota(jnp.int32, (n_exp, tt), 0)
    idx_rows, w_rows = [], []
    chosen = jnp.zeros((n_exp, tt), F32)
    for _ in range(TOP_K):
        m = jnp.max(cand, axis=0, keepdims=True)
        idx = jnp.min(jnp.where(cand == m, eio, n_exp), axis=0, keepdims=True)
        sel = eio == idx
        w_rows.append(jnp.sum(jnp.where(sel, scores, 0.0), axis=0, keepdims=True))
        idx_rows.append(idx)
        cand = jnp.where(sel, neg_inf, cand)
        chosen = jnp.where(sel, 1.0, chosen)
    eidx = jnp.concatenate(idx_rows, axis=0)
    w = jnp.concatenate(w_rows, axis=0)
    gates = w / jnp.sum(w, axis=0, keepdims=True) * ROUTED_SCALE

    tr = lax.broadcasted_iota(jnp.int32, (tt, tt), 0)
    tc = lax.broadcasted_iota(jnp.int32, (tt, tt), 1)
    before = (tr < tc).astype(BF16)
    cum = _dot(chosen.astype(BF16), before) + carry
    rank = jnp.concatenate(
        [jnp.sum(jnp.where(eio == idx_rows[k], cum, 0.0), axis=0, keepdims=True)
         for k in range(TOP_K)], axis=0).astype(jnp.int32)
    total = carry + jnp.sum(chosen, axis=1, keepdims=True)
    return eidx, gates, rank, total


def _dest_kernel(ps_ref, eidx_ref, rank_ref, dest_ref, *, n_exp):
    eidx = eidx_ref[...]

    def body(e, acc):
        return acc + jnp.where(eidx == e, ps_ref[e], 0)

    dest_ref[...] = lax.fori_loop(0, n_exp, body, rank_ref[...])


def _dest(pad_start, eidx, rank, tt):
    k, t = eidx.shape
    n_exp = pad_start.shape[0]
    return pl.pallas_call(
        functools.partial(_dest_kernel, n_exp=n_exp),
        grid_spec=pltpu.PrefetchScalarGridSpec(
            num_scalar_prefetch=1,
            grid=(t // tt,),
            in_specs=[pl.BlockSpec((k, tt), lambda i, ps: (0, i)),
                      pl.BlockSpec((k, tt), lambda i, ps: (0, i))],
            out_specs=pl.BlockSpec((k, tt), lambda i, ps: (0, i))),
        out_shape=jax.ShapeDtypeStruct((k, t), jnp.int32),
        compiler_params=_cparams(("parallel",)),
        name="dest",
    )(pad_start, eidx, rank)


def _chunked(dest, chunk):
    k, t = dest.shape
    return dest.reshape(k, t // chunk, chunk).transpose(1, 0, 2)


def _sc_workers():
    info = plsc.get_sparse_core_info()
    return info.num_cores, info.num_cores * info.num_subcores


def _dispatch(dest, h1, n_rows, chunk):
    t, dh = h1.shape
    n_cores, n_workers = _sc_workers()
    per_worker = t // chunk // n_workers
    mesh = plsc.VectorSubcoreMesh(core_axis_name="c", subcore_axis_name="s")

    @functools.partial(
        pl.kernel, mesh=mesh,
        out_type=jax.ShapeDtypeStruct((n_rows, dh), h1.dtype),
        scratch_types=[pltpu.VMEM((TOP_K, chunk), jnp.int32),
                       pltpu.VMEM((chunk, dh), h1.dtype),
                       pltpu.SemaphoreType.DMA],
        name="dispatch",
    )
    def scatter_rows(x_hbm, dest_hbm, o_hbm, idx_v, rows_v, sem):
        wid = lax.axis_index("s") * n_cores + lax.axis_index("c")

        @pl.loop(0, per_worker)
        def _(j):
            c = wid * per_worker + j
            pltpu.sync_copy(dest_hbm.at[c], idx_v)
            pltpu.sync_copy(x_hbm.at[pl.ds(c * chunk, chunk)], rows_v)
            copies = [pltpu.async_copy(rows_v, o_hbm.at[idx_v.at[k]], sem) for k in range(TOP_K)]
            for cp in copies:
                cp.wait()

    return scatter_rows(h1, _chunked(dest, chunk))


def _gather_back(dest_chunks, ys):
    n_chunks, k_top, chunk = dest_chunks.shape
    t = n_chunks * chunk
    dh = ys.shape[1]
    n_cores, n_workers = _sc_workers()
    per_worker = n_chunks // n_workers
    mesh = plsc.VectorSubcoreMesh(core_axis_name="c", subcore_axis_name="s")

    @functools.partial(
        pl.kernel, mesh=mesh,
        out_type=jax.ShapeDtypeStruct((k_top, t, dh), ys.dtype),
        scratch_types=[pltpu.VMEM((k_top, chunk), jnp.int32),
                       pltpu.VMEM((k_top, chunk, dh), ys.dtype),
                       pltpu.SemaphoreType.DMA],
        name="gather_back",
    )
    def gather_rows(ys_hbm, dest_hbm, o_hbm, idx_v, rows_v, sem):
        wid = lax.axis_index("s") * n_cores + lax.axis_index("c")

        @pl.loop(0, per_worker)
        def _(j):
            c = wid * per_worker + j
            pltpu.sync_copy(dest_hbm.at[c], idx_v)
            copies = [pltpu.async_copy(ys_hbm.at[idx_v.at[k]], rows_v.at[k], sem)
                      for k in range(k_top)]
            for cp in copies:
                cp.wait()
            for k in range(k_top):
                pltpu.sync_copy(rows_v.at[k], o_hbm.at[k, pl.ds(c * chunk, chunk)])

    return gather_rows(ys, dest_chunks)


def _experts_kernel(bs_ref, cnt_ref, wgu_hbm, wd_hbm, xs_hbm, ys_hbm, wgu_buf, wd_buf, gu_bf, d_bf,
                    xbuf, ybuf, wsem, xsem, ysem):
    e = pl.program_id(0)
    n_exp = pl.num_programs(0)
    n_slots, mb, _ = xbuf.shape
    ahead = n_slots - 1
    w_slots = wgu_buf.shape[0]
    n_blocks = bs_ref[n_exp]
    b0 = bs_ref[e]
    b1 = bs_ref[e + 1]
    count = cnt_ref[e]

    def w_copies(ex):
        slot = lax.rem(ex, w_slots)
        return (pltpu.make_async_copy(wgu_hbm.at[ex], wgu_buf.at[slot], wsem.at[slot]),
                pltpu.make_async_copy(wd_hbm.at[ex], wd_buf.at[slot], wsem.at[slot]))

    @pl.when(e == 0)
    def _():
        for i in range(w_slots - 1):
            @pl.when(i < n_exp)
            def _():
                for cp in w_copies(i):
                    cp.start()

    @pl.when(e + w_slots - 1 < n_exp)
    def _():
        for cp in w_copies(e + w_slots - 1):
            cp.start()

    def x_copy(b):
        slot = lax.rem(b, n_slots)
        return pltpu.make_async_copy(xs_hbm.at[pl.ds(pl.multiple_of(b * mb, mb), mb)],
                                     xbuf.at[slot], xsem.at[slot])

    def y_copy(b):
        slot = lax.rem(b, n_slots)
        return pltpu.make_async_copy(ybuf.at[slot],
                                     ys_hbm.at[pl.ds(pl.multiple_of(b * mb, mb), mb)], ysem.at[slot])

    @pl.when(e == 0)
    def _():
        for i in range(ahead):
            @pl.when(i < n_blocks)
            def _():
                x_copy(i).start(priority=RING_DMA_PRIORITY)

    for cp in w_copies(e):
        cp.wait()
    w_slot = lax.rem(e, w_slots)

    @pl.when(b1 > b0)
    def _():
        gu_bf[...] = wgu_buf[w_slot].astype(BF16)
        d_bf[...] = wd_buf[w_slot].astype(BF16)

    kh = gu_bf.shape[0] // 2
    half = gu_bf.shape[1] // 2

    def block(b, carry):
        slot = lax.rem(b, n_slots)
        x_copy(b).wait()

        @pl.when(b + ahead < n_blocks)
        def _():
            x_copy(b + ahead).start(priority=RING_DMA_PRIORITY)

        @pl.when(b >= n_slots)
        def _():
            y_copy(b - n_slots).wait()

        rows = lax.broadcasted_iota(jnp.int32, (mb, xbuf.shape[2]), 0)
        valid = count - (b - b0) * mb
        words = jnp.where(rows < valid, xbuf[slot], jnp.uint32(0))
        xa, xb = _unpack_halves(words)
        gu = _dot(xa.astype(BF16), gu_bf[0:kh, :]) + _dot(xb.astype(BF16), gu_bf[kh:, :])
        hid = _silu(gu[:, :half]) * gu[:, half:]
        ybuf[slot] = _pack_halves(_dot(hid.astype(BF16), d_bf[...]))
        y_copy(b).start(priority=RING_DMA_PRIORITY)
        return carry

    lax.fori_loop(b0, b1, block, 0)

    @pl.when(e == n_exp - 1)
    def _():
        for i in range(1, n_slots + 1):
            @pl.when(n_blocks >= i)
            def _():
                y_copy(n_blocks - i).wait()


def _experts(block_start, counts, xs, w_gu, w_down):
    n_rows, dh = xs.shape
    n_exp, d, gu_w = w_gu.shape
    ed = w_down.shape[1]
    return pl.pallas_call(
        _experts_kernel,
        grid_spec=pltpu.PrefetchScalarGridSpec(
            num_scalar_prefetch=2,
            grid=(n_exp,),
            in_specs=[pl.BlockSpec(memory_space=pl.ANY),
                      pl.BlockSpec(memory_space=pl.ANY),
                      pl.BlockSpec(memory_space=pl.ANY)],
            out_specs=pl.BlockSpec(memory_space=pl.ANY),
            scratch_shapes=[pltpu.VMEM((WEIGHT_SLOTS, d, gu_w), w_gu.dtype),
                            pltpu.VMEM((WEIGHT_SLOTS, ed, d), w_down.dtype),
                            pltpu.VMEM((d, gu_w), BF16), pltpu.VMEM((ed, d), BF16),
                            pltpu.VMEM((RING_SLOTS, MOE_BLOCK, dh), jnp.uint32),
                            pltpu.VMEM((RING_SLOTS, MOE_BLOCK, dh), jnp.uint32),
                            pltpu.SemaphoreType.DMA((WEIGHT_SLOTS,)),
                            pltpu.SemaphoreType.DMA((RING_SLOTS,)),
                            pltpu.SemaphoreType.DMA((RING_SLOTS,))]),
        out_shape=jax.ShapeDtypeStruct((n_rows, dh), jnp.uint32),
        compiler_params=_cparams(("arbitrary",)),
        name="experts",
    )(block_start, counts, w_gu, w_down, xs)


def _combine_kernel(base_ref, w_ref, g_ref, b_ref, yg_ref, *rest):
    o_ref = rest[-1]
    acc = base_ref[...]
    dh = acc.shape[1] // 2
    acc_a = acc[:, :dh]
    acc_b = acc[:, dh:]
    w = w_ref[...]
    for k in range(TOP_K):
        ya, yb = _unpack_halves(yg_ref[k])
        acc_a = acc_a + ya * w[:, k:k + 1]
        acc_b = acc_b + yb * w[:, k:k + 1]
    o_ref[...] = _layer_norm(jnp.concatenate([acc_a, acc_b], axis=1), g_ref[...], b_ref[...])


def _combine(base, w_tk, g, b, yg, prev, part, tc):
    t, d = base.shape
    tiles = yg.shape[1] // tc
    off = part * tiles
    in_specs = [pl.BlockSpec((tc, d), lambda i: (i + off, 0)),
                pl.BlockSpec((tc, TOP_K), lambda i: (i + off, 0)),
                pl.BlockSpec((1, d), lambda i: (0, 0)),
                pl.BlockSpec((1, d), lambda i: (0, 0)),
                pl.BlockSpec((TOP_K, tc, yg.shape[2]), lambda i: (0, i, 0))]
    args = [base, w_tk, g.reshape(1, d), b.reshape(1, d), yg]
    aliases = {}
    if prev is not None:
        in_specs.append(pl.BlockSpec(memory_space=pl.ANY))
        args.append(prev)
        aliases = {len(args) - 1: 0}
    return pl.pallas_call(
        _combine_kernel,
        grid=(tiles,),
        in_specs=in_specs,
        out_specs=pl.BlockSpec((tc, d), lambda i: (i + off, 0)),
        out_shape=jax.ShapeDtypeStruct((t, d), F32),
        input_output_aliases=aliases,
        compiler_params=_cparams(("parallel",)),
        name="combine",
    )(*args)


def _moe(base, h1_packed, eidx, w_kt, rank, counts, w_exp_gu, w_exp_down, ln_g, ln_b):
    t, d = base.shape
    n_exp = counts.shape[0]

    counts = counts.reshape(n_exp)
    blocks_per_expert = (counts + MOE_BLOCK - 1) // MOE_BLOCK
    block_start = jnp.concatenate(
        [jnp.zeros((1,), jnp.int32), jnp.cumsum(blocks_per_expert).astype(jnp.int32)])
    pad_start = block_start[:n_exp] * MOE_BLOCK
    n_blocks = -(-(t * TOP_K + n_exp * (MOE_BLOCK - 1)) // MOE_BLOCK)

    dest = _dest(pad_start, eidx, rank, tt=min(2048, t))
    xs = _dispatch(dest, h1_packed, n_blocks * MOE_BLOCK, chunk=DISPATCH_CHUNK)
    ys = _experts(block_start, counts, xs, w_exp_gu, w_exp_down)
    dest_chunks = _chunked(dest, GATHER_CHUNK)
    per_part = dest_chunks.shape[0] // COMBINE_PARTS
    w_tk = w_kt.T
    out = None
    for part in range(COMBINE_PARTS):
        yg = _gather_back(dest_chunks[part * per_part:(part + 1) * per_part], ys)
        out = _combine(base, w_tk, ln_g, ln_b, yg, out, part, tc=min(256, t))
    return out


def kernel(x, ln_in_g, ln_in_b, w_in, hg_lb_logits, hg_norm_g, gm_v_norm_g, gm_v_norm_b, gm_w_s, gm_b_s, gm_out_norm_g, w_out, ln1_g, ln1_b, w_router, router_bias, w_exp_gu, w_exp_down, w_shared_gu, w_shared_down, ln2_g, ln2_b):
    bsz, s, d = x.shape
    depth = w_in.shape[0]
    assert depth == 1, "the lower-bound table row used in the hgrn kernel assumes one layer"
    alpha = (2.0 * depth) ** 0.25
    t = bsz * s
    hg_width = hg_norm_g.shape[1]
    gm_width = gm_v_norm_g.shape[1]
    tm = min(512, t)

    x2 = x.reshape(t, d)
    h0, proj, f_pre = _in_proj(x2, ln_in_g, ln_in_b, w_in[0].astype(BF16), hg_width, hg_width, tm)
    proj3 = proj.reshape(bsz, s, proj.shape[1])
    y_hg, y_gm = _mixer(proj3, f_pre.reshape(bsz, s, hg_width), hg_lb_logits.astype(F32),
                        hg_norm_g[0], gm_v_norm_g[0], gm_v_norm_b[0], gm_w_s[0], gm_b_s[0],
                        gm_out_norm_g[0], sb=min(512, s))
    base, h1_packed, eidx, gates, rank, counts = _out_proj(
        y_hg.reshape(t, hg_width), y_gm.reshape(t, gm_width), h0, w_out[0].astype(BF16),
        ln1_g[0], ln1_b[0], w_router[0].T, router_bias[0], w_shared_gu[0].astype(BF16),
        w_shared_down[0].astype(BF16), alpha, tm)
    out = _moe(base, h1_packed, eidx, gates, rank, counts, w_exp_gu[0], w_exp_down[0],
               ln2_g[0], ln2_b[0])
    return out.reshape(bsz, s, d)
```

```python
import functools

import jax
import jax.numpy as jnp
from jax import lax
from jax.experimental import pallas as pl
from jax.experimental.pallas import tpu as pltpu
from jax.experimental.pallas import tpu_sc as plsc

F32 = jnp.float32
BF16 = jnp.bfloat16

LN_EPS = 1e-5
RMS_EPS = 1e-6
CHUNK = 64
SUB = 16
MAX_FACTORED_EXPONENT = 80.0
FACTORED_MIN_LB = float(2.718281828459045 ** (-MAX_FACTORED_EXPONENT / CHUNK))
HEAD_DIM = 128
GM_BLOCK = 128
TOP_K = 8
N_GROUPS = 8
TOPK_GROUPS = 4
ROUTED_SCALE = 2.5
MOE_BLOCK = 128
WEIGHT_SLOTS = 3
RING_SLOTS = 8
RING_DMA_PRIORITY = 1
DISPATCH_CHUNK = 64
ROUTE_SUBTILES = 2
COMBINE_PARTS = 4
GATHER_CHUNK = 16
VMEM_LIMIT = 48 * 1024 * 1024


def _cparams(sem):
    return pltpu.CompilerParams(dimension_semantics=sem, vmem_limit_bytes=VMEM_LIMIT)


def _layer_norm(x, g, b):
    mu = jnp.mean(x, axis=-1, keepdims=True)
    xc = x - mu
    var = jnp.mean(xc * xc, axis=-1, keepdims=True)
    return xc * lax.rsqrt(var + LN_EPS) * g + b


def _silu(x):
    return x * jax.nn.sigmoid(x)


def _gelu(x):
    return 0.5 * x * (1.0 + lax.erf(x * (2.0 ** -0.5)))


def _dot(a, b):
    return jnp.dot(a, b, preferred_element_type=F32)


def _dot_nt(a, b, precision=None):
    return lax.dot_general(a, b, (((1,), (1,)), ((), ())), preferred_element_type=F32,
                           precision=precision)


def _dot_tn(a, b):
    return lax.dot_general(a, b, (((0,), (0,)), ((), ())), preferred_element_type=F32)


def _split_bf16(x):
    hi = x.astype(BF16)
    return hi, (x - hi.astype(F32)).astype(BF16)


def _pack_halves(x):
    n = x.shape[1] // 2
    hi = lax.bitcast_convert_type(x[:, :n].astype(BF16).astype(F32), jnp.uint32)
    lo = lax.bitcast_convert_type(x[:, n:].astype(BF16).astype(F32), jnp.uint32)
    return hi | (lo >> 16)


def _unpack_halves(w):
    a = lax.bitcast_convert_type(w & jnp.uint32(0xFFFF0000), F32)
    b = lax.bitcast_convert_type(w << 16, F32)
    return a, b


def _in_proj_kernel(x_ref, g_ref, b_ref, w_ref, h_ref, p_ref, f_ref, *, f_lo):
    h = _layer_norm(x_ref[...], g_ref[...], b_ref[...])
    h_ref[...] = h
    p = _dot(h.astype(BF16), w_ref[...])
    p_ref[...] = p.astype(BF16)
    f_ref[...] = p[:, f_lo:f_lo + f_ref.shape[1]]


def _in_proj(x2, g, b, w_bf, f_lo, f_width, tm):
    t, d = x2.shape
    n = w_bf.shape[1]
    return pl.pallas_call(
        functools.partial(_in_proj_kernel, f_lo=f_lo),
        grid=(t // tm,),
        in_specs=[pl.BlockSpec((tm, d), lambda i: (i, 0)),
                  pl.BlockSpec((1, d), lambda i: (0, 0)),
                  pl.BlockSpec((1, d), lambda i: (0, 0)),
                  pl.BlockSpec((d, n), lambda i: (0, 0))],
        out_specs=[pl.BlockSpec((tm, d), lambda i: (i, 0)),
                   pl.BlockSpec((tm, n), lambda i: (i, 0)),
                   pl.BlockSpec((tm, f_width), lambda i: (i, 0))],
        out_shape=[jax.ShapeDtypeStruct((t, d), F32), jax.ShapeDtypeStruct((t, n), BF16),
                   jax.ShapeDtypeStruct((t, f_width), F32)],
        compiler_params=_cparams(("parallel",)),
        name="in_proj",
    )(x2, g.reshape(1, d), b.reshape(1, d), w_bf)


def _mixer_kernel(q_ref, f_ref, i_ref, g_ref, lbl_ref, gn_ref, u_ref, v_ref, vg_ref, vb_ref, ws_ref,
                  bs_ref, og_ref, o_ref, ogm_ref, st_ref, *, n_chunks):
    @pl.when(pl.program_id(1) == 0)
    def _():
        st_ref[...] = jnp.zeros_like(st_ref)

    lg = lbl_ref[...]
    ex = jnp.exp(lg - jnp.max(lg, axis=0, keepdims=True))
    lb_all = ex[0:1, :] / jnp.sum(ex, axis=0, keepdims=True)
    gn_all = gn_ref[...]

    c = CHUNK
    heads = st_ref.shape[0]
    row = lax.broadcasted_iota(jnp.int32, (c, c), 0)
    col = lax.broadcasted_iota(jnp.int32, (c, c), 1)
    tril = (row >= col).astype(BF16)
    sub_row = lax.broadcasted_iota(jnp.int32, (SUB, HEAD_DIM), 0)
    ones = jnp.ones((HEAD_DIM, HEAD_DIM), BF16)
    neg_inf = jnp.float32(-jnp.inf)

    def intra_pairwise(q, k, v, v_bf, b, o_inter):
        outs = []
        for blk in range(c // SUB):
            lo = blk * SUB
            b_i = b[lo:lo + SUB, :]
            q_i = q[lo:lo + SUB, :]
            k_i = k[lo:lo + SUB, :]
            v_i = v[lo:lo + SUB, :]
            prods = []
            for j in range(SUB):
                diff = jnp.where(sub_row >= j, b_i - b_i[j:j + 1, :], neg_inf)
                prods.append(q_i * k_i[j:j + 1, :] * jnp.exp(diff))
            p_all = jnp.concatenate(prods, axis=0).astype(BF16)
            r_all = _dot(p_all, ones)
            o_blk = o_inter[lo:lo + SUB, :]
            for j in range(SUB):
                o_blk = o_blk + r_all[j * SUB:(j + 1) * SUB, :] * v_i[j:j + 1, :]
            if blk > 0:
                b_ref = b[lo - 1:lo, :]
                q_hat = (q_i * jnp.exp(b_i - b_ref)).astype(BF16)
                k_hat = (k[0:lo, :] * jnp.exp(b_ref - b[0:lo, :])).astype(BF16)
                scores = _dot_nt(q_hat, k_hat)
                o_blk = o_blk + _dot(scores.astype(BF16), v_bf[0:lo, :])
            outs.append(o_blk)
        return jnp.concatenate(outs, axis=0)

    def one_chunk(r0, factored):
        rows = pl.ds(r0, c)
        hs = [slice(h * HEAD_DIM, (h + 1) * HEAD_DIM) for h in range(heads)]
        q = _silu(q_ref[rows, :].astype(F32))
        f = lb_all + (1.0 - lb_all) * jax.nn.sigmoid(f_ref[rows, :])
        v_bf = i_ref[rows, :]
        v = v_bf.astype(F32)
        lf = jnp.log(f)
        k = 1.0 - f
        lf_hi, lf_lo = _split_bf16(lf)
        b2 = _dot(tril, jnp.concatenate([lf_hi, lf_lo], axis=1))
        width = lf.shape[1]
        b = b2[:, :width] + b2[:, width:]
        b_last = b[c - 1:c, :]
        q_dec = (q * jnp.exp(b)).astype(BF16)
        k_dec = (k * jnp.exp(b_last - b)).astype(BF16)
        decay = jnp.exp(b_last)
        st = [st_ref[h] for h in range(heads)]
        st_bf = [s.astype(BF16) for s in st]
        if factored:
            k_grow = (k * jnp.exp(-b)).astype(BF16)
            both = [_dot_nt(q_dec[:, hs[h]], jnp.concatenate([st_bf[h], k_grow[:, hs[h]]], axis=0))
                    for h in range(heads)]
            scores = [jnp.where(row >= col, both[h][:, HEAD_DIM:], 0.0).astype(BF16)
                      for h in range(heads)]
            outs = [both[h][:, :HEAD_DIM] + _dot(scores[h], v_bf[:, hs[h]]) for h in range(heads)]
        else:
            outs = [intra_pairwise(q[:, hs[h]], k[:, hs[h]], v[:, hs[h]], v_bf[:, hs[h]],
                                   b[:, hs[h]], _dot_nt(q_dec[:, hs[h]], st_bf[h]))
                    for h in range(heads)]
        for h in range(heads):
            st_ref[h] = st[h] * decay[:, hs[h]] + _dot_tn(v_bf[:, hs[h]], k_dec[:, hs[h]])
        inv = [lax.rsqrt(jnp.mean(o * o, axis=-1, keepdims=True) + RMS_EPS) for o in outs]
        o = jnp.concatenate([outs[h] * inv[h] for h in range(heads)], axis=1)
        o_ref[rows, :] = (o * gn_all * _silu(g_ref[rows, :].astype(F32))).astype(o_ref.dtype)

    groups = ws_ref.shape[0]
    gdim = u_ref.shape[-1] // groups
    gm_row = lax.broadcasted_iota(jnp.int32, (GM_BLOCK, GM_BLOCK), 0) // CHUNK
    gm_col = lax.broadcasted_iota(jnp.int32, (GM_BLOCK, GM_BLOCK), 1) // CHUNK
    ws = [jnp.where(gm_row >= gm_col, ws_ref[g], 0.0).astype(BF16) for g in range(groups)]

    def gmlp_block(r0):
        rows = pl.ds(r0, GM_BLOCK)
        u = _gelu(u_ref[rows, :].astype(F32))
        v = _layer_norm(_gelu(v_ref[rows, :].astype(F32)), vg_ref[...], vb_ref[...])
        v_bf = v.astype(BF16)
        mixed = jnp.concatenate(
            [_dot(ws[g], v_bf[:, g * gdim:(g + 1) * gdim]) for g in range(groups)], axis=1)
        y = u * (mixed + bs_ref[...])
        inv = lax.rsqrt(jnp.mean(y * y, axis=-1, keepdims=True) + RMS_EPS)
        ogm_ref[rows, :] = (y * inv * og_ref[...]).astype(ogm_ref.dtype)

    def run_chunks(factored):
        per_trip = 4 if (factored and n_chunks % 4 == 0) else 1
        trip_rows = per_trip * c
        inline_gmlp = trip_rows % GM_BLOCK == 0

        def chunk_body(ci, carry):
            for u in range(per_trip):
                one_chunk(pl.multiple_of((ci * per_trip + u) * c, c), factored)
            if inline_gmlp:
                for n in range(trip_rows // GM_BLOCK):
                    gmlp_block(pl.multiple_of(ci * trip_rows + n * GM_BLOCK, GM_BLOCK))
            return carry

        lax.fori_loop(0, n_chunks // per_trip, chunk_body, 0)
        if not inline_gmlp:
            for n in range(n_chunks * c // GM_BLOCK):
                gmlp_block(n * GM_BLOCK)

    bounded = jnp.min(lb_all) >= FACTORED_MIN_LB
    pl.when(bounded)(lambda: run_chunks(True))
    pl.when(jnp.logical_not(bounded))(lambda: run_chunks(False))


def _mixer(proj3, f_pre3, lb_logits, norm_g, v_norm_g, v_norm_b, w_s, b_s, out_norm_g, sb):
    bsz, s, _ = proj3.shape
    hg_width = norm_g.shape[0]
    heads = hg_width // HEAD_DIM
    gw = v_norm_g.shape[0]
    assert gw == hg_width and sb % GM_BLOCK == 0
    groups = w_s.shape[0]
    bias_full = jnp.repeat(b_s.T, gw // groups, axis=1)

    def sect(k):
        return pl.BlockSpec((None, sb, hg_width), lambda b, j, k=k: (b, j, k))

    row_vec = pl.BlockSpec((1, hg_width), lambda b, j: (0, 0))
    return pl.pallas_call(
        functools.partial(_mixer_kernel, n_chunks=sb // CHUNK),
        grid=(bsz, s // sb),
        in_specs=[sect(0), sect(0), sect(2), sect(3),
                  pl.BlockSpec((lb_logits.shape[0], hg_width), lambda b, j: (0, 0)),
                  row_vec, sect(4), sect(5), row_vec, row_vec,
                  pl.BlockSpec((groups, GM_BLOCK, GM_BLOCK), lambda b, j: (0, 0, 0)),
                  pl.BlockSpec((GM_BLOCK, gw), lambda b, j: (0, 0)),
                  row_vec],
        out_specs=[pl.BlockSpec((None, sb, hg_width), lambda b, j: (b, j, 0)),
                   pl.BlockSpec((None, sb, gw), lambda b, j: (b, j, 0))],
        out_shape=[jax.ShapeDtypeStruct((bsz, s, hg_width), BF16),
                   jax.ShapeDtypeStruct((bsz, s, gw), BF16)],
        scratch_shapes=[pltpu.VMEM((heads, HEAD_DIM, HEAD_DIM), F32)],
        compiler_params=_cparams(("parallel", "arbitrary")),
        name="mixer",
    )(proj3, f_pre3, proj3, proj3, lb_logits, norm_g.reshape(1, hg_width), proj3, proj3,
      v_norm_g.reshape(1, gw), v_norm_b.reshape(1, gw), w_s, bias_full, out_norm_g.reshape(1, gw))


def _out_proj_kernel(yh_ref, yg_ref, h0_ref, wa_ref, wb_ref, g_ref, b_ref, wr_ref, rb_ref,
                     wsg_ref, wsd_ref, base_ref, hp_ref, eidx_ref, gate_ref, rank_ref, cnt_ref,
                     carry_ref, *, alpha):
    @pl.when(pl.program_id(0) == 0)
    def _():
        carry_ref[...] = jnp.zeros_like(carry_ref)

    tm = h0_ref.shape[0]
    sub = tm // ROUTE_SUBTILES
    w_hi, w_lo = _split_bf16(wr_ref[...])
    logits = []
    for s in range(ROUTE_SUBTILES):
        rows = pl.ds(s * sub, sub)
        mix = _dot(yh_ref[rows, :], wa_ref[...]) + _dot(yg_ref[rows, :], wb_ref[...])
        h1 = _layer_norm(alpha * h0_ref[rows, :] + mix, g_ref[...], b_ref[...])
        hp_ref[rows, :] = _pack_halves(h1)
        h_hi, h_lo = _split_bf16(h1)
        logits.append(_dot_nt(w_hi, h_hi) + (_dot_nt(w_hi, h_lo) + _dot_nt(w_lo, h_hi)))
        gu = _dot(h_hi, wsg_ref[...])
        half = gu.shape[1] // 2
        hid = (_silu(gu[:, :half]) * gu[:, half:]).astype(BF16)
        base_ref[rows, :] = alpha * h1 + _dot(hid, wsd_ref[...])
    total = carry_ref[...]
    for s in range(ROUTE_SUBTILES):
        lanes = pl.ds(s * sub, sub)
        eidx, gates, rank, total = _route_tile(logits[s], rb_ref[...], total)
        eidx_ref[:, lanes] = eidx
        gate_ref[:, lanes] = gates
        rank_ref[:, lanes] = rank
    carry_ref[...] = total
    cnt_ref[...] = total.astype(jnp.int32)


def _out_proj(y_hg, y_gm, h0, w_out_bf, g, b, w_router_t, router_bias, ws_gu_bf, ws_down_bf, alpha, tm):
    t, d = h0.shape
    sg = ws_gu_bf.shape[1]
    sd = ws_down_bf.shape[0]
    hw = y_hg.shape[1]
    gw = y_gm.shape[1]
    e = w_router_t.shape[0]
    return pl.pallas_call(
        functools.partial(_out_proj_kernel, alpha=alpha),
        grid=(t // tm,),
        in_specs=[pl.BlockSpec((tm, hw), lambda i: (i, 0)),
                  pl.BlockSpec((tm, gw), lambda i: (i, 0)),
                  pl.BlockSpec((tm, d), lambda i: (i, 0)),
                  pl.BlockSpec((hw, d), lambda i: (0, 0)),
                  pl.BlockSpec((gw, d), lambda i: (0, 0)),
                  pl.BlockSpec((1, d), lambda i: (0, 0)),
                  pl.BlockSpec((1, d), lambda i: (0, 0)),
                  pl.BlockSpec((e, d), lambda i: (0, 0)),
                  pl.BlockSpec((e, 1), lambda i: (0, 0)),
                  pl.BlockSpec((d, sg), lambda i: (0, 0)),
                  pl.BlockSpec((sd, d), lambda i: (0, 0))],
        out_specs=[pl.BlockSpec((tm, d), lambda i: (i, 0)),
                   pl.BlockSpec((tm, d // 2), lambda i: (i, 0)),
                   pl.BlockSpec((TOP_K, tm), lambda i: (0, i)),
                   pl.BlockSpec((TOP_K, tm), lambda i: (0, i)),
                   pl.BlockSpec((TOP_K, tm), lambda i: (0, i)),
                   pl.BlockSpec((e, 1), lambda i: (0, 0))],
        out_shape=[jax.ShapeDtypeStruct((t, d), F32), jax.ShapeDtypeStruct((t, d // 2), jnp.uint32),
                   jax.ShapeDtypeStruct((TOP_K, t), jnp.int32),
                   jax.ShapeDtypeStruct((TOP_K, t), F32),
                   jax.ShapeDtypeStruct((TOP_K, t), jnp.int32),
                   jax.ShapeDtypeStruct((e, 1), jnp.int32)],
        scratch_shapes=[pltpu.VMEM((e, 1), F32)],
        compiler_params=_cparams(("arbitrary",)),
        name="out_proj",
    )(y_hg, y_gm, h0, w_out_bf[:hw], w_out_bf[hw:], g.reshape(1, d), b.reshape(1, d), w_router_t,
      router_bias.astype(F32).reshape(e, 1), ws_gu_bf, ws_down_bf)


def _route_tile(logits, bias, carry):
    n_exp, tt = logits.shape
    per_group = n_exp // N_GROUPS
    neg_inf = jnp.float32(-jnp.inf)
    scores = jax.nn.sigmoid(logits)
    biased = scores + bias

    gio = lax.broadcasted_iota(jnp.int32, (per_group, tt), 0)
    blocks, gs_rows = [], []
    for g in range(N_GROUPS):
        blk = biased[g * per_group:(g + 1) * per_group, :]
        m1 = jnp.max(blk, axis=0, keepdims=True)
        first = jnp.min(jnp.where(blk == m1, gio, per_group), axis=0, keepdims=True)
        m2 = jnp.max(jnp.where(gio == first, neg_inf, blk), axis=0, keepdims=True)
        blocks.append(blk)
        gs_rows.append(m1 + m2)
    gs = jnp.concatenate(gs_rows, axis=0)
    gidx = lax.broadcasted_iota(jnp.int32, (N_GROUPS, tt), 0)
    beaten = jnp.zeros((N_GROUPS, tt), jnp.int32)
    for g in range(N_GROUPS):
        r = gs_rows[g]
        beaten = beaten + jnp.where((r > gs) | ((r == gs) & (g < gidx)), 1, 0)
    keep = beaten < TOPK_GROUPS
    cand = jnp.concatenate(
        [jnp.where(keep[g:g + 1, :], blocks[g], neg_inf) for g in range(N_GROUPS)], axis=0)

    eio = lax.broadcasted_iota(jnp.int32, (n_exp, tt), 0)
    idx_rows, w_rows = [], []
    chosen = jnp.zeros((n_exp, tt), F32)
    for _ in range(TOP_K):
        m = jnp.max(cand, axis=0, keepdims=True)
        idx = jnp.min(jnp.where(cand == m, eio, n_exp), axis=0, keepdims=True)
        sel = eio == idx
        w_rows.append(jnp.sum(jnp.where(sel, scores, 0.0), axis=0, keepdims=True))
        idx_rows.append(idx)
        cand = jnp.where(sel, neg_inf, cand)
        chosen = jnp.where(sel, 1.0, chosen)
    eidx = jnp.concatenate(idx_rows, axis=0)
    w = jnp.concatenate(w_rows, axis=0)
    gates = w / jnp.sum(w, axis=0, keepdims=True) * ROUTED_SCALE

    tr = lax.broadcasted_iota(jnp.int32, (tt, tt), 0)
    tc = lax.broadcasted_iota(jnp.int32, (tt, tt), 1)
    before = (tr < tc).astype(BF16)
    cum = _dot(chosen.astype(BF16), before) + carry
    rank = jnp.concatenate(
        [jnp.sum(jnp.where(eio == idx_rows[k], cum, 0.0), axis=0, keepdims=True)
         for k in range(TOP_K)], axis=0).astype(jnp.int32)
    total = carry + jnp.sum(chosen, axis=1, keepdims=True)
    return eidx, gates, rank, total


def _chunked(table, chunk):
    k, t = table.shape
    return table.reshape(k, t // chunk, chunk).transpose(1, 0, 2)


def _sc_workers():
    info = plsc.get_sparse_core_info()
    return info.num_cores, info.num_cores * info.num_subcores


def _dispatch(eidx, rank, pad_start, h1, n_rows, chunk):
    t, dh = h1.shape
    n_exp = pad_start.shape[0]
    n_cores, n_workers = _sc_workers()
    n_chunks = t // chunk
    per_worker = n_chunks // n_workers
    lanes = plsc.get_sparse_core_info().num_lanes
    mesh = plsc.VectorSubcoreMesh(core_axis_name="c", subcore_axis_name="s")

    @functools.partial(
        pl.kernel, mesh=mesh,
        out_type=(jax.ShapeDtypeStruct((n_rows, dh), h1.dtype),
                  jax.ShapeDtypeStruct((n_chunks, TOP_K, chunk), jnp.int32)),
        scratch_types=[pltpu.VMEM((n_exp,), jnp.int32),
                       pltpu.VMEM((TOP_K, chunk), jnp.int32),
                       pltpu.VMEM((TOP_K, chunk), jnp.int32),
                       pltpu.VMEM((TOP_K, chunk), jnp.int32),
                       pltpu.VMEM((chunk, dh), h1.dtype),
                       pltpu.SemaphoreType.DMA],
        compiler_params=pltpu.CompilerParams(needs_layout_passes=False),
        name="dispatch",
    )
    def scatter_rows(x_hbm, e_hbm, r_hbm, ps_hbm, o_hbm, d_hbm, ps_v, e_v, r_v, idx_v, rows_v, sem):
        wid = lax.axis_index("s") * n_cores + lax.axis_index("c")
        pltpu.sync_copy(ps_hbm, ps_v)

        @pl.loop(0, per_worker)
        def _(j):
            c = wid * per_worker + j
            pltpu.sync_copy(e_hbm.at[c], e_v)
            pltpu.sync_copy(r_hbm.at[c], r_v)
            pltpu.sync_copy(x_hbm.at[pl.ds(c * chunk, chunk)], rows_v)
            for k in range(TOP_K):
                for s in range(chunk // lanes):
                    sl = pl.ds(s * lanes, lanes)
                    idx_v[k, sl] = plsc.load_gather(ps_v, [e_v[k, sl]]) + r_v[k, sl]
            pltpu.sync_copy(idx_v, d_hbm.at[c])
            copies = [pltpu.async_copy(rows_v, o_hbm.at[idx_v.at[k]], sem) for k in range(TOP_K)]
            for cp in copies:
                cp.wait()

    return scatter_rows(h1, _chunked(eidx, chunk), _chunked(rank, chunk), pad_start)


def _gather_back(dest_chunks, ys, sub):
    n_chunks, k_top, chunk = dest_chunks.shape
    t = n_chunks * chunk
    dh = ys.shape[1]
    n_cores, n_workers = _sc_workers()
    per_worker = n_chunks // n_workers
    mesh = plsc.VectorSubcoreMesh(core_axis_name="c", subcore_axis_name="s")

    @functools.partial(
        pl.kernel, mesh=mesh,
        out_type=jax.ShapeDtypeStruct((k_top, t, dh), ys.dtype),
        scratch_types=[pltpu.VMEM((k_top, chunk), jnp.int32),
                       pltpu.VMEM((k_top, sub, dh), ys.dtype),
                       pltpu.SemaphoreType.DMA],
        name="gather_back",
    )
    def gather_rows(ys_hbm, dest_hbm, o_hbm, idx_v, rows_v, sem):
        wid = lax.axis_index("s") * n_cores + lax.axis_index("c")

        @pl.loop(0, per_worker)
        def _(j):
            c = wid * per_worker + j
            pltpu.sync_copy(dest_hbm.at[c], idx_v)
            for q in range(chunk // sub):
                copies = [pltpu.async_copy(ys_hbm.at[idx_v.at[k, pl.ds(q * sub, sub)]], rows_v.at[k], sem)
                          for k in range(k_top)]
                for cp in copies:
                    cp.wait()
                for k in range(k_top):
                    pltpu.sync_copy(rows_v.at[k], o_hbm.at[k, pl.ds(c * chunk + q * sub, sub)])

    return gather_rows(ys, dest_chunks)


def _experts_kernel(bs_ref, cnt_ref, wgu_hbm, wd_hbm, xs_hbm, ys_hbm, wgu_buf, wd_buf, gu_bf, d_bf,
                    xbuf, ybuf, wsem, xsem, ysem):
    e = pl.program_id(0)
    n_exp = pl.num_programs(0)
    n_slots, mb, _ = xbuf.shape
    ahead = n_slots - 1
    w_slots = wgu_buf.shape[0]
    n_blocks = bs_ref[n_exp]
    b0 = bs_ref[e]
    b1 = bs_ref[e + 1]
    count = cnt_ref[e]

    def w_copies(ex):
        slot = lax.rem(ex, w_slots)
        return (pltpu.make_async_copy(wgu_hbm.at[ex], wgu_buf.at[slot], wsem.at[slot]),
                pltpu.make_async_copy(wd_hbm.at[ex], wd_buf.at[slot], wsem.at[slot]))

    @pl.when(e == 0)
    def _():
        for i in range(w_slots - 1):
            @pl.when(i < n_exp)
            def _():
                for cp in w_copies(i):
                    cp.start()

    @pl.when(e + w_slots - 1 < n_exp)
    def _():
        for cp in w_copies(e + w_slots - 1):
            cp.start()

    def x_copy(b):
        slot = lax.rem(b, n_slots)
        return pltpu.make_async_copy(xs_hbm.at[pl.ds(pl.multiple_of(b * mb, mb), mb)],
                                     xbuf.at[slot], xsem.at[slot])

    def y_copy(b):
        slot = lax.rem(b, n_slots)
        return pltpu.make_async_copy(ybuf.at[slot],
                                     ys_hbm.at[pl.ds(pl.multiple_of(b * mb, mb), mb)], ysem.at[slot])

    @pl.when(e == 0)
    def _():
        for i in range(ahead):
            @pl.when(i < n_blocks)
            def _():
                x_copy(i).start(priority=RING_DMA_PRIORITY)

    for cp in w_copies(e):
        cp.wait()
    w_slot = lax.rem(e, w_slots)

    @pl.when(b1 > b0)
    def _():
        gu_bf[...] = wgu_buf[w_slot].astype(BF16)
        d_bf[...] = wd_buf[w_slot].astype(BF16)

    kh = gu_bf.shape[0] // 2
    half = gu_bf.shape[1] // 2

    def block(b, carry):
        slot = lax.rem(b, n_slots)
        x_copy(b).wait()

        @pl.when(b + ahead < n_blocks)
        def _():
            x_copy(b + ahead).start(priority=RING_DMA_PRIORITY)

        @pl.when(b >= n_slots)
        def _():
            y_copy(b - n_slots).wait()

        rows = lax.broadcasted_iota(jnp.int32, (mb, xbuf.shape[2]), 0)
        valid = count - (b - b0) * mb
        words = jnp.where(rows < valid, xbuf[slot], jnp.uint32(0))
        xa, xb = _unpack_halves(words)
        gu = _dot(xa.astype(BF16), gu_bf[0:kh, :]) + _dot(xb.astype(BF16), gu_bf[kh:, :])
        hid = _silu(gu[:, :half]) * gu[:, half:]
        ybuf[slot] = _pack_halves(_dot(hid.astype(BF16), d_bf[...]))
        y_copy(b).start(priority=RING_DMA_PRIORITY)
        return carry

    lax.fori_loop(b0, b1, block, 0)

    @pl.when(e == n_exp - 1)
    def _():
        for i in range(1, n_slots + 1):
            @pl.when(n_blocks >= i)
            def _():
                y_copy(n_blocks - i).wait()


def _experts(block_start, counts, xs, w_gu, w_down):
    n_rows, dh = xs.shape
    n_exp, d, gu_w = w_gu.shape
    ed = w_down.shape[1]
    return pl.pallas_call(
        _experts_kernel,
        grid_spec=pltpu.PrefetchScalarGridSpec(
            num_scalar_prefetch=2,
            grid=(n_exp,),
            in_specs=[pl.BlockSpec(memory_space=pl.ANY),
                      pl.BlockSpec(memory_space=pl.ANY),
                      pl.BlockSpec(memory_space=pl.ANY)],
            out_specs=pl.BlockSpec(memory_space=pl.ANY),
            scratch_shapes=[pltpu.VMEM((WEIGHT_SLOTS, d, gu_w), w_gu.dtype),
                            pltpu.VMEM((WEIGHT_SLOTS, ed, d), w_down.dtype),
                            pltpu.VMEM((d, gu_w), BF16), pltpu.VMEM((ed, d), BF16),
                            pltpu.VMEM((RING_SLOTS, MOE_BLOCK, dh), jnp.uint32),
                            pltpu.VMEM((RING_SLOTS, MOE_BLOCK, dh), jnp.uint32),
                            pltpu.SemaphoreType.DMA((WEIGHT_SLOTS,)),
                            pltpu.SemaphoreType.DMA((RING_SLOTS,)),
                            pltpu.SemaphoreType.DMA((RING_SLOTS,))]),
        out_shape=jax.ShapeDtypeStruct((n_rows, dh), jnp.uint32),
        compiler_params=_cparams(("arbitrary",)),
        name="experts",
    )(block_start, counts, w_gu, w_down, xs)


def _combine_kernel(base_ref, w_ref, g_ref, b_ref, yg_ref, *rest):
    o_ref = rest[-1]
    acc = base_ref[...]
    dh = acc.shape[1] // 2
    acc_a = acc[:, :dh]
    acc_b = acc[:, dh:]
    w = w_ref[...]
    for k in range(TOP_K):
        ya, yb = _unpack_halves(yg_ref[k])
        acc_a = acc_a + ya * w[:, k:k + 1]
        acc_b = acc_b + yb * w[:, k:k + 1]
    o_ref[...] = _layer_norm(jnp.concatenate([acc_a, acc_b], axis=1), g_ref[...], b_ref[...])


def _combine(base, w_tk, g, b, yg, prev, part, tc):
    t, d = base.shape
    tiles = yg.shape[1] // tc
    off = part * tiles
    in_specs = [pl.BlockSpec((tc, d), lambda i: (i + off, 0)),
                pl.BlockSpec((tc, TOP_K), lambda i: (i + off, 0)),
                pl.BlockSpec((1, d), lambda i: (0, 0)),
                pl.BlockSpec((1, d), lambda i: (0, 0)),
                pl.BlockSpec((TOP_K, tc, yg.shape[2]), lambda i: (0, i, 0))]
    args = [base, w_tk, g.reshape(1, d), b.reshape(1, d), yg]
    aliases = {}
    if prev is not None:
        in_specs.append(pl.BlockSpec(memory_space=pl.ANY))
        args.append(prev)
        aliases = {len(args) - 1: 0}
    return pl.pallas_call(
        _combine_kernel,
        grid=(tiles,),
        in_specs=in_specs,
        out_specs=pl.BlockSpec((tc, d), lambda i: (i + off, 0)),
        out_shape=jax.ShapeDtypeStruct((t, d), F32),
        input_output_aliases=aliases,
        compiler_params=_cparams(("parallel",)),
        name="combine",
    )(*args)


def _moe(base, h1_packed, eidx, w_kt, rank, counts, w_exp_gu, w_exp_down, ln_g, ln_b):
    t, d = base.shape
    n_exp = counts.shape[0]

    counts = counts.reshape(n_exp)
    blocks_per_expert = (counts + MOE_BLOCK - 1) // MOE_BLOCK
    block_start = jnp.concatenate(
        [jnp.zeros((1,), jnp.int32), jnp.cumsum(blocks_per_expert).astype(jnp.int32)])
    pad_start = block_start[:n_exp] * MOE_BLOCK
    n_blocks = -(-(t * TOP_K + n_exp * (MOE_BLOCK - 1)) // MOE_BLOCK)

    xs, dest_chunks = _dispatch(eidx, rank, pad_start, h1_packed, n_blocks * MOE_BLOCK,
                                chunk=DISPATCH_CHUNK)
    ys = _experts(block_start, counts, xs, w_exp_gu, w_exp_down)
    per_part = dest_chunks.shape[0] // COMBINE_PARTS
    w_tk = w_kt.T
    out = None
    for part in range(COMBINE_PARTS):
        yg = _gather_back(dest_chunks[part * per_part:(part + 1) * per_part], ys, sub=GATHER_CHUNK)
        out = _combine(base, w_tk, ln_g, ln_b, yg, out, part, tc=min(256, t))
    return out


def kernel(x, ln_in_g, ln_in_b, w_in, hg_lb_logits, hg_norm_g, gm_v_norm_g, gm_v_norm_b, gm_w_s, gm_b_s, gm_out_norm_g, w_out, ln1_g, ln1_b, w_router, router_bias, w_exp_gu, w_exp_down, w_shared_gu, w_shared_down, ln2_g, ln2_b):
    bsz, s, d = x.shape
    depth = w_in.shape[0]
    assert depth == 1, "the lower-bound table row used in the hgrn kernel assumes one layer"
    alpha = (2.0 * depth) ** 0.25
    t = bsz * s
    hg_width = hg_norm_g.shape[1]
    gm_width = gm_v_norm_g.shape[1]
    tm = min(512, t)

    x2 = x.reshape(t, d)
    h0, proj, f_pre = _in_proj(x2, ln_in_g, ln_in_b, w_in[0].astype(BF16), hg_width, hg_width, tm)
    proj3 = proj.reshape(bsz, s, proj.shape[1])
    y_hg, y_gm = _mixer(proj3, f_pre.reshape(bsz, s, hg_width), hg_lb_logits.astype(F32),
                        hg_norm_g[0], gm_v_norm_g[0], gm_v_norm_b[0], gm_w_s[0], gm_b_s[0],
                        gm_out_norm_g[0], sb=min(512, s))
    base, h1_packed, eidx, gates, rank, counts = _out_proj(
        y_hg.reshape(t, hg_width), y_gm.reshape(t, gm_width), h0, w_out[0].astype(BF16),
        ln1_g[0], ln1_b[0], w_router[0].T, router_bias[0], w_shared_gu[0].astype(BF16),
        w_shared_down[0].astype(BF16), alpha, tm)
    out = _moe(base, h1_packed, eidx, gates, rank, counts, w_exp_gu[0], w_exp_down[0],
               ln2_g[0], ln2_b[0])
    return out.reshape(bsz, s, d)
```

```python
import functools

import jax
import jax.numpy as jnp
from jax import lax
from jax.experimental import pallas as pl
from jax.experimental.pallas import tpu as pltpu
from jax.experimental.pallas import tpu_sc as plsc

F32 = jnp.float32
BF16 = jnp.bfloat16

LN_EPS = 1e-5
RMS_EPS = 1e-6
CHUNK = 64
SUB = 16
MAX_FACTORED_EXPONENT = 80.0
FACTORED_MIN_LB = float(2.718281828459045 ** (-MAX_FACTORED_EXPONENT / CHUNK))
HEAD_DIM = 128
GM_BLOCK = 128
TOP_K = 8
N_GROUPS = 8
TOPK_GROUPS = 4
ROUTED_SCALE = 2.5
MOE_BLOCK = 128
WEIGHT_SLOTS = 3
RING_SLOTS = 8
RING_DMA_PRIORITY = 1
TOKEN_SLOTS = 4
TOKEN_ROW = 128
SLOT_CHUNK = 64
ROUTE_SUBTILES = 2
COMBINE_PARTS = 4
GATHER_CHUNK = 16
VMEM_LIMIT = 48 * 1024 * 1024
EXPERTS_VMEM_LIMIT = 56 * 1024 * 1024


def _cparams(sem):
    return pltpu.CompilerParams(dimension_semantics=sem, vmem_limit_bytes=VMEM_LIMIT)


def _layer_norm(x, g, b):
    mu = jnp.mean(x, axis=-1, keepdims=True)
    xc = x - mu
    var = jnp.mean(xc * xc, axis=-1, keepdims=True)
    return xc * lax.rsqrt(var + LN_EPS) * g + b


def _silu(x):
    return x * jax.nn.sigmoid(x)


def _gelu(x):
    return 0.5 * x * (1.0 + lax.erf(x * (2.0 ** -0.5)))


def _dot(a, b):
    return jnp.dot(a, b, preferred_element_type=F32)


def _dot_nt(a, b, precision=None):
    return lax.dot_general(a, b, (((1,), (1,)), ((), ())), preferred_element_type=F32,
                           precision=precision)


def _dot_tn(a, b):
    return lax.dot_general(a, b, (((0,), (0,)), ((), ())), preferred_element_type=F32)


def _split_bf16(x):
    hi = x.astype(BF16)
    return hi, (x - hi.astype(F32)).astype(BF16)


def _pack_halves(x):
    n = x.shape[1] // 2
    hi = lax.bitcast_convert_type(x[:, :n].astype(BF16).astype(F32), jnp.uint32)
    lo = lax.bitcast_convert_type(x[:, n:].astype(BF16).astype(F32), jnp.uint32)
    return hi | (lo >> 16)


def _unpack_halves(w):
    a = lax.bitcast_convert_type(w & jnp.uint32(0xFFFF0000), F32)
    b = lax.bitcast_convert_type(w << 16, F32)
    return a, b


def _in_proj_kernel(x_ref, g_ref, b_ref, w_ref, h_ref, p_ref, f_ref, *, f_lo):
    h = _layer_norm(x_ref[...], g_ref[...], b_ref[...])
    h_ref[...] = h
    p = _dot(h.astype(BF16), w_ref[...])
    p_ref[...] = p.astype(BF16)
    f_ref[...] = p[:, f_lo:f_lo + f_ref.shape[1]]


def _in_proj(x2, g, b, w_bf, f_lo, f_width, tm):
    t, d = x2.shape
    n = w_bf.shape[1]
    return pl.pallas_call(
        functools.partial(_in_proj_kernel, f_lo=f_lo),
        grid=(t // tm,),
        in_specs=[pl.BlockSpec((tm, d), lambda i: (i, 0)),
                  pl.BlockSpec((1, d), lambda i: (0, 0)),
                  pl.BlockSpec((1, d), lambda i: (0, 0)),
                  pl.BlockSpec((d, n), lambda i: (0, 0))],
        out_specs=[pl.BlockSpec((tm, d), lambda i: (i, 0)),
                   pl.BlockSpec((tm, n), lambda i: (i, 0)),
                   pl.BlockSpec((tm, f_width), lambda i: (i, 0))],
        out_shape=[jax.ShapeDtypeStruct((t, d), F32), jax.ShapeDtypeStruct((t, n), BF16),
                   jax.ShapeDtypeStruct((t, f_width), F32)],
        compiler_params=_cparams(("parallel",)),
        name="in_proj",
    )(x2, g.reshape(1, d), b.reshape(1, d), w_bf)


def _mixer_kernel(q_ref, f_ref, i_ref, g_ref, lbl_ref, gn_ref, u_ref, v_ref, vg_ref, vb_ref, ws_ref,
                  bs_ref, og_ref, o_ref, ogm_ref, st_ref, *, n_chunks):
    @pl.when(pl.program_id(1) == 0)
    def _():
        st_ref[...] = jnp.zeros_like(st_ref)

    lg = lbl_ref[...]
    ex = jnp.exp(lg - jnp.max(lg, axis=0, keepdims=True))
    lb_all = ex[0:1, :] / jnp.sum(ex, axis=0, keepdims=True)
    gn_all = gn_ref[...]

    c = CHUNK
    heads = st_ref.shape[0]
    row = lax.broadcasted_iota(jnp.int32, (c, c), 0)
    col = lax.broadcasted_iota(jnp.int32, (c, c), 1)
    tril = (row >= col).astype(BF16)
    sub_row = lax.broadcasted_iota(jnp.int32, (SUB, HEAD_DIM), 0)
    ones = jnp.ones((HEAD_DIM, HEAD_DIM), BF16)
    neg_inf = jnp.float32(-jnp.inf)

    def intra_pairwise(q, k, v, v_bf, b, o_inter):
        outs = []
        for blk in range(c // SUB):
            lo = blk * SUB
            b_i = b[lo:lo + SUB, :]
            q_i = q[lo:lo + SUB, :]
            k_i = k[lo:lo + SUB, :]
            v_i = v[lo:lo + SUB, :]
            prods = []
            for j in range(SUB):
                diff = jnp.where(sub_row >= j, b_i - b_i[j:j + 1, :], neg_inf)
                prods.append(q_i * k_i[j:j + 1, :] * jnp.exp(diff))
            p_all = jnp.concatenate(prods, axis=0).astype(BF16)
            r_all = _dot(p_all, ones)
            o_blk = o_inter[lo:lo + SUB, :]
            for j in range(SUB):
                o_blk = o_blk + r_all[j * SUB:(j + 1) * SUB, :] * v_i[j:j + 1, :]
            if blk > 0:
                b_ref = b[lo - 1:lo, :]
                q_hat = (q_i * jnp.exp(b_i - b_ref)).astype(BF16)
                k_hat = (k[0:lo, :] * jnp.exp(b_ref - b[0:lo, :])).astype(BF16)
                scores = _dot_nt(q_hat, k_hat)
                o_blk = o_blk + _dot(scores.astype(BF16), v_bf[0:lo, :])
            outs.append(o_blk)
        return jnp.concatenate(outs, axis=0)

    def one_chunk(r0, factored):
        rows = pl.ds(r0, c)
        hs = [slice(h * HEAD_DIM, (h + 1) * HEAD_DIM) for h in range(heads)]
        q = _silu(q_ref[rows, :].astype(F32))
        f = lb_all + (1.0 - lb_all) * jax.nn.sigmoid(f_ref[rows, :])
        v_bf = i_ref[rows, :]
        v = v_bf.astype(F32)
        lf = jnp.log(f)
        k = 1.0 - f
        lf_hi, lf_lo = _split_bf16(lf)
        b2 = _dot(tril, jnp.concatenate([lf_hi, lf_lo], axis=1))
        width = lf.shape[1]
        b = b2[:, :width] + b2[:, width:]
        b_last = b[c - 1:c, :]
        q_dec = (q * jnp.exp(b)).astype(BF16)
        k_dec = (k * jnp.exp(b_last - b)).astype(BF16)
        decay = jnp.exp(b_last)
        st = [st_ref[h] for h in range(heads)]
        st_bf = [s.astype(BF16) for s in st]
        if factored:
            k_grow = (k * jnp.exp(-b)).astype(BF16)
            both = [_dot_nt(q_dec[:, hs[h]], jnp.concatenate([st_bf[h], k_grow[:, hs[h]]], axis=0))
                    for h in range(heads)]
            scores = [jnp.where(row >= col, both[h][:, HEAD_DIM:], 0.0).astype(BF16)
                      for h in range(heads)]
            outs = [both[h][:, :HEAD_DIM] + _dot(scores[h], v_bf[:, hs[h]]) for h in range(heads)]
        else:
            outs = [intra_pairwise(q[:, hs[h]], k[:, hs[h]], v[:, hs[h]], v_bf[:, hs[h]],
                                   b[:, hs[h]], _dot_nt(q_dec[:, hs[h]], st_bf[h]))
                    for h in range(heads)]
        for h in range(heads):
            st_ref[h] = st[h] * decay[:, hs[h]] + _dot_tn(v_bf[:, hs[h]], k_dec[:, hs[h]])
        inv = [lax.rsqrt(jnp.mean(o * o, axis=-1, keepdims=True) + RMS_EPS) for o in outs]
        o = jnp.concatenate([outs[h] * inv[h] for h in range(heads)], axis=1)
        o_ref[rows, :] = (o * gn_all * _silu(g_ref[rows, :].astype(F32))).astype(o_ref.dtype)

    groups = ws_ref.shape[0]
    gdim = u_ref.shape[-1] // groups
    gm_row = lax.broadcasted_iota(jnp.int32, (GM_BLOCK, GM_BLOCK), 0) // CHUNK
    gm_col = lax.broadcasted_iota(jnp.int32, (GM_BLOCK, GM_BLOCK), 1) // CHUNK
    ws = [jnp.where(gm_row >= gm_col, ws_ref[g], 0.0).astype(BF16) for g in range(groups)]

    def gmlp_block(r0):
        rows = pl.ds(r0, GM_BLOCK)
        u = _gelu(u_ref[rows, :].astype(F32))
        v = _layer_norm(_gelu(v_ref[rows, :].astype(F32)), vg_ref[...], vb_ref[...])
        v_bf = v.astype(BF16)
        mixed = jnp.concatenate(
            [_dot(ws[g], v_bf[:, g * gdim:(g + 1) * gdim]) for g in range(groups)], axis=1)
        y = u * (mixed + bs_ref[...])
        inv = lax.rsqrt(jnp.mean(y * y, axis=-1, keepdims=True) + RMS_EPS)
        ogm_ref[rows, :] = (y * inv * og_ref[...]).astype(ogm_ref.dtype)

    def run_chunks(factored):
        per_trip = 4 if (factored and n_chunks % 4 == 0) else 1
        trip_rows = per_trip * c
        inline_gmlp = trip_rows % GM_BLOCK == 0

        def chunk_body(ci, carry):
            for u in range(per_trip):
                one_chunk(pl.multiple_of((ci * per_trip + u) * c, c), factored)
            if inline_gmlp:
                for n in range(trip_rows // GM_BLOCK):
                    gmlp_block(pl.multiple_of(ci * trip_rows + n * GM_BLOCK, GM_BLOCK))
            return carry

        lax.fori_loop(0, n_chunks // per_trip, chunk_body, 0)
        if not inline_gmlp:
            for n in range(n_chunks * c // GM_BLOCK):
                gmlp_block(n * GM_BLOCK)

    bounded = jnp.min(lb_all) >= FACTORED_MIN_LB
    pl.when(bounded)(lambda: run_chunks(True))
    pl.when(jnp.logical_not(bounded))(lambda: run_chunks(False))


def _mixer(proj3, f_pre3, lb_logits, norm_g, v_norm_g, v_norm_b, w_s, b_s, out_norm_g, sb):
    bsz, s, _ = proj3.shape
    hg_width = norm_g.shape[0]
    heads = hg_width // HEAD_DIM
    gw = v_norm_g.shape[0]
    assert gw == hg_width and sb % GM_BLOCK == 0
    groups = w_s.shape[0]
    bias_full = jnp.repeat(b_s.T, gw // groups, axis=1)

    def sect(k):
        return pl.BlockSpec((None, sb, hg_width), lambda b, j, k=k: (b, j, k))

    row_vec = pl.BlockSpec((1, hg_width), lambda b, j: (0, 0))
    return pl.pallas_call(
        functools.partial(_mixer_kernel, n_chunks=sb // CHUNK),
        grid=(bsz, s // sb),
        in_specs=[sect(0), sect(0), sect(2), sect(3),
                  pl.BlockSpec((lb_logits.shape[0], hg_width), lambda b, j: (0, 0)),
                  row_vec, sect(4), sect(5), row_vec, row_vec,
                  pl.BlockSpec((groups, GM_BLOCK, GM_BLOCK), lambda b, j: (0, 0, 0)),
                  pl.BlockSpec((GM_BLOCK, gw), lambda b, j: (0, 0)),
                  row_vec],
        out_specs=[pl.BlockSpec((None, sb, hg_width), lambda b, j: (b, j, 0)),
                   pl.BlockSpec((None, sb, gw), lambda b, j: (b, j, 0))],
        out_shape=[jax.ShapeDtypeStruct((bsz, s, hg_width), BF16),
                   jax.ShapeDtypeStruct((bsz, s, gw), BF16)],
        scratch_shapes=[pltpu.VMEM((heads, HEAD_DIM, HEAD_DIM), F32)],
        compiler_params=_cparams(("parallel", "arbitrary")),
        name="mixer",
    )(proj3, f_pre3, proj3, proj3, lb_logits, norm_g.reshape(1, hg_width), proj3, proj3,
      v_norm_g.reshape(1, gw), v_norm_b.reshape(1, gw), w_s, bias_full, out_norm_g.reshape(1, gw))


def _out_proj_kernel(yh_ref, yg_ref, h0_ref, wa_ref, wb_ref, g_ref, b_ref, wr_ref, rb_ref,
                     wsg_ref, wsd_ref, base_ref, hp_ref, eidx_ref, gate_ref, rank_ref, cnt_ref,
                     carry_ref, *, alpha):
    @pl.when(pl.program_id(0) == 0)
    def _():
        carry_ref[...] = jnp.zeros_like(carry_ref)

    tm = h0_ref.shape[0]
    sub = tm // ROUTE_SUBTILES
    w_hi, w_lo = _split_bf16(wr_ref[...])
    logits = []
    for s in range(ROUTE_SUBTILES):
        rows = pl.ds(s * sub, sub)
        mix = _dot(yh_ref[rows, :], wa_ref[...]) + _dot(yg_ref[rows, :], wb_ref[...])
        h1 = _layer_norm(alpha * h0_ref[rows, :] + mix, g_ref[...], b_ref[...])
        hp_ref[rows, :] = _pack_halves(h1)
        h_hi, h_lo = _split_bf16(h1)
        logits.append(_dot_nt(w_hi, h_hi) + (_dot_nt(w_hi, h_lo) + _dot_nt(w_lo, h_hi)))
        gu = _dot(h_hi, wsg_ref[...])
        half = gu.shape[1] // 2
        hid = (_silu(gu[:, :half]) * gu[:, half:]).astype(BF16)
        base_ref[rows, :] = alpha * h1 + _dot(hid, wsd_ref[...])
    total = carry_ref[...]
    for s in range(ROUTE_SUBTILES):
        lanes = pl.ds(s * sub, sub)
        eidx, gates, rank, total = _route_tile(logits[s], rb_ref[...], total)
        eidx_ref[:, lanes] = eidx
        gate_ref[:, lanes] = gates
        rank_ref[:, lanes] = rank
    carry_ref[...] = total
    cnt_ref[...] = total.astype(jnp.int32)


def _out_proj(y_hg, y_gm, h0, w_out_bf, g, b, w_router_t, router_bias, ws_gu_bf, ws_down_bf, alpha, tm):
    t, d = h0.shape
    sg = ws_gu_bf.shape[1]
    sd = ws_down_bf.shape[0]
    hw = y_hg.shape[1]
    gw = y_gm.shape[1]
    e = w_router_t.shape[0]
    return pl.pallas_call(
        functools.partial(_out_proj_kernel, alpha=alpha),
        grid=(t // tm,),
        in_specs=[pl.BlockSpec((tm, hw), lambda i: (i, 0)),
                  pl.BlockSpec((tm, gw), lambda i: (i, 0)),
                  pl.BlockSpec((tm, d), lambda i: (i, 0)),
                  pl.BlockSpec((hw, d), lambda i: (0, 0)),
                  pl.BlockSpec((gw, d), lambda i: (0, 0)),
                  pl.BlockSpec((1, d), lambda i: (0, 0)),
                  pl.BlockSpec((1, d), lambda i: (0, 0)),
                  pl.BlockSpec((e, d), lambda i: (0, 0)),
                  pl.BlockSpec((e, 1), lambda i: (0, 0)),
                  pl.BlockSpec((d, sg), lambda i: (0, 0)),
                  pl.BlockSpec((sd, d), lambda i: (0, 0))],
        out_specs=[pl.BlockSpec((tm, d), lambda i: (i, 0)),
                   pl.BlockSpec((tm, d // 2), lambda i: (i, 0)),
                   pl.BlockSpec((TOP_K, tm), lambda i: (0, i)),
                   pl.BlockSpec((TOP_K, tm), lambda i: (0, i)),
                   pl.BlockSpec((TOP_K, tm), lambda i: (0, i)),
                   pl.BlockSpec((e, 1), lambda i: (0, 0))],
        out_shape=[jax.ShapeDtypeStruct((t, d), F32), jax.ShapeDtypeStruct((t, d // 2), jnp.uint32),
                   jax.ShapeDtypeStruct((TOP_K, t), jnp.int32),
                   jax.ShapeDtypeStruct((TOP_K, t), F32),
                   jax.ShapeDtypeStruct((TOP_K, t), jnp.int32),
                   jax.ShapeDtypeStruct((e, 1), jnp.int32)],
        scratch_shapes=[pltpu.VMEM((e, 1), F32)],
        compiler_params=_cparams(("arbitrary",)),
        name="out_proj",
    )(y_hg, y_gm, h0, w_out_bf[:hw], w_out_bf[hw:], g.reshape(1, d), b.reshape(1, d), w_router_t,
      router_bias.astype(F32).reshape(e, 1), ws_gu_bf, ws_down_bf)


def _route_tile(logits, bias, carry):
    n_exp, tt = logits.shape
    per_group = n_exp // N_GROUPS
    neg_inf = jnp.float32(-jnp.inf)
    scores = jax.nn.sigmoid(logits)
    biased = scores + bias

    gio = lax.broadcasted_iota(jnp.int32, (per_group, tt), 0)
    blocks, gs_rows = [], []
    for g in range(N_GROUPS):
        blk = biased[g * per_group:(g + 1) * per_group, :]
        m1 = jnp.max(blk, axis=0, keepdims=True)
        first = jnp.min(jnp.where(blk == m1, gio, per_group), axis=0, keepdims=True)
        m2 = jnp.max(jnp.where(gio == first, neg_inf, blk), axis=0, keepdims=True)
        blocks.append(blk)
        gs_rows.append(m1 + m2)
    gs = jnp.concatenate(gs_rows, axis=0)
    gidx = lax.broadcasted_iota(jnp.int32, (N_GROUPS, tt), 0)
    beaten = jnp.zeros((N_GROUPS, tt), jnp.int32)
    for g in range(N_GROUPS):
        r = gs_rows[g]
        beaten = beaten + jnp.where((r > gs) | ((r == gs) & (g < gidx)), 1, 0)
    keep = beaten < TOPK_GROUPS
    cand = jnp.concatenate(
        [jnp.where(keep[g:g + 1, :], blocks[g], neg_inf) for g in range(N_GROUPS)], axis=0)

    eio = lax.broadcasted_iota(jnp.int32, (n_exp, tt), 0)
    idx_rows, w_rows = [], []
    chosen = jnp.zeros((n_exp, tt), F32)
    for _ in range(TOP_K):
        m = jnp.max(cand, axis=0, keepdims=True)
        idx = jnp.min(jnp.where(cand == m, eio, n_exp), axis=0, keepdims=True)
        sel = eio == idx
        w_rows.append(jnp.sum(jnp.where(sel, scores, 0.0), axis=0, keepdims=True))
        idx_rows.append(idx)
        cand = jnp.where(sel, neg_inf, cand)
        chosen = jnp.where(sel, 1.0, chosen)
    eidx = jnp.concatenate(idx_rows, axis=0)
    w = jnp.concatenate(w_rows, axis=0)
    gates = w / jnp.sum(w, axis=0, keepdims=True) * ROUTED_SCALE

    tr = lax.broadcasted_iota(jnp.int32, (tt, tt), 0)
    tc = lax.broadcasted_iota(jnp.int32, (tt, tt), 1)
    before = (tr < tc).astype(BF16)
    cum = _dot(chosen.astype(BF16), before) + carry
    rank = jnp.concatenate(
        [jnp.sum(jnp.where(eio == idx_rows[k], cum, 0.0), axis=0, keepdims=True)
         for k in range(TOP_K)], axis=0).astype(jnp.int32)
    total = carry + jnp.sum(chosen, axis=1, keepdims=True)
    return eidx, gates, rank, total


def _chunked(table, chunk):
    k, t = table.shape
    return table.reshape(k, t // chunk, chunk).transpose(1, 0, 2)


def _sc_workers():
    info = plsc.get_sparse_core_info()
    return info.num_cores, info.num_cores * info.num_subcores


def _slot_tables(eidx, rank, pad_start, n_rows, chunk):
    k_top, t = eidx.shape
    n_exp = pad_start.shape[0]
    n_cores, n_workers = _sc_workers()
    n_chunks = t // chunk
    per_worker = n_chunks // n_workers
    lanes = plsc.get_sparse_core_info().num_lanes
    mesh = plsc.VectorSubcoreMesh(core_axis_name="c", subcore_axis_name="s")

    @functools.partial(
        pl.kernel, mesh=mesh,
        out_type=(jax.ShapeDtypeStruct((n_rows, TOKEN_ROW), jnp.int32),
                  jax.ShapeDtypeStruct((n_chunks, k_top, chunk), jnp.int32)),
        scratch_types=[pltpu.VMEM((n_exp,), jnp.int32),
                       pltpu.VMEM((k_top, chunk), jnp.int32),
                       pltpu.VMEM((k_top, chunk), jnp.int32),
                       pltpu.VMEM((k_top, chunk), jnp.int32),
                       pltpu.VMEM((chunk, TOKEN_ROW), jnp.int32),
                       pltpu.SemaphoreType.DMA],
        compiler_params=pltpu.CompilerParams(needs_layout_passes=False),
        name="slot_tables",
    )
    def build(e_hbm, r_hbm, ps_hbm, tok_hbm, d_hbm, ps_v, e_v, r_v, idx_v, rows_v, sem):
        wid = lax.axis_index("s") * n_cores + lax.axis_index("c")
        pltpu.sync_copy(ps_hbm, ps_v)

        @pl.loop(0, per_worker)
        def _(j):
            c = wid * per_worker + j
            pltpu.sync_copy(e_hbm.at[c], e_v)
            pltpu.sync_copy(r_hbm.at[c], r_v)
            for k in range(k_top):
                for s in range(chunk // lanes):
                    sl = pl.ds(s * lanes, lanes)
                    idx_v[k, sl] = plsc.load_gather(ps_v, [e_v[k, sl]]) + r_v[k, sl]
            for i in range(chunk):
                rows_v[i, pl.ds(0, lanes)] = jnp.full((lanes,), c * chunk + i, jnp.int32)
            pltpu.sync_copy(idx_v, d_hbm.at[c])
            copies = [pltpu.async_copy(rows_v, tok_hbm.at[idx_v.at[k]], sem) for k in range(k_top)]
            for cp in copies:
                cp.wait()

    return build(_chunked(eidx, chunk), _chunked(rank, chunk), pad_start)


def _gather_back(dest_chunks, ys, sub):
    n_chunks, k_top, chunk = dest_chunks.shape
    t = n_chunks * chunk
    dh = ys.shape[1]
    n_cores, n_workers = _sc_workers()
    per_worker = n_chunks // n_workers
    mesh = plsc.VectorSubcoreMesh(core_axis_name="c", subcore_axis_name="s")

    @functools.partial(
        pl.kernel, mesh=mesh,
        out_type=jax.ShapeDtypeStruct((k_top, t, dh), ys.dtype),
        scratch_types=[pltpu.VMEM((k_top, chunk), jnp.int32),
                       pltpu.VMEM((k_top, sub, dh), ys.dtype),
                       pltpu.SemaphoreType.DMA],
        name="gather_back",
    )
    def gather_rows(ys_hbm, dest_hbm, o_hbm, idx_v, rows_v, sem):
        wid = lax.axis_index("s") * n_cores + lax.axis_index("c")

        @pl.loop(0, per_worker)
        def _(j):
            c = wid * per_worker + j
            pltpu.sync_copy(dest_hbm.at[c], idx_v)
            for q in range(chunk // sub):
                copies = [pltpu.async_copy(ys_hbm.at[idx_v.at[k, pl.ds(q * sub, sub)]], rows_v.at[k], sem)
                          for k in range(k_top)]
                for cp in copies:
                    cp.wait()
                for k in range(k_top):
                    pltpu.sync_copy(rows_v.at[k], o_hbm.at[k, pl.ds(c * chunk + q * sub, sub)])

    return gather_rows(ys, dest_chunks)


def _experts_kernel(bs_ref, cnt_ref, wgu_hbm, wd_hbm, h_hbm, tok_hbm, ys_hbm, wgu_buf, wd_buf, gu_bf,
                    d_bf, h_v, xg, ybuf, tokbuf, wsem, hsem, tsem, ysem):
    e = pl.program_id(0)
    n_exp = pl.num_programs(0)
    n_slots, mb, _ = ybuf.shape
    w_slots = wgu_buf.shape[0]
    t_slots = tokbuf.shape[0]
    n_tok = h_v.shape[0]
    n_blocks = bs_ref[n_exp]
    b0 = bs_ref[e]
    b1 = bs_ref[e + 1]
    count = cnt_ref[e]

    def w_copies(ex):
        slot = lax.rem(ex, w_slots)
        return (pltpu.make_async_copy(wgu_hbm.at[ex], wgu_buf.at[slot], wsem.at[slot]),
                pltpu.make_async_copy(wd_hbm.at[ex], wd_buf.at[slot], wsem.at[slot]))

    def tok_copy(b):
        src = jnp.minimum(b, n_blocks - 1)
        slot = lax.rem(b, t_slots)
        return pltpu.make_async_copy(tok_hbm.at[pl.ds(pl.multiple_of(src * mb, mb), mb)],
                                     tokbuf.at[slot], tsem.at[slot])

    def y_copy(b):
        slot = lax.rem(b, n_slots)
        return pltpu.make_async_copy(ybuf.at[slot],
                                     ys_hbm.at[pl.ds(pl.multiple_of(b * mb, mb), mb)], ysem.at[slot])

    def gather_rows(b):
        ids = lax.rem(b, t_slots)
        dst = lax.rem(b, 2)
        for i in range(mb):
            t = jnp.clip(tokbuf[ids, i, 0], 0, n_tok - 1)
            xg[dst, pl.ds(i, 1), :] = h_v[pl.ds(t, 1), :]

    @pl.when(e == 0)
    def _():
        resident = pltpu.make_async_copy(h_hbm, h_v, hsem)
        resident.start()
        for i in range(w_slots - 1):
            @pl.when(i < n_exp)
            def _():
                for cp in w_copies(i):
                    cp.start()
        for i in range(t_slots - 1):
            tok_copy(i).start()
        resident.wait()
        tok_copy(0).wait()
        gather_rows(0)

    @pl.when(e + w_slots - 1 < n_exp)
    def _():
        for cp in w_copies(e + w_slots - 1):
            cp.start()

    for cp in w_copies(e):
        cp.wait()
    w_slot = lax.rem(e, w_slots)

    @pl.when(b1 > b0)
    def _():
        gu_bf[...] = wgu_buf[w_slot].astype(BF16)
        d_bf[...] = wd_buf[w_slot].astype(BF16)

    kh = gu_bf.shape[0] // 2
    half = gu_bf.shape[1] // 2

    def block(b, carry):
        slot = lax.rem(b, n_slots)
        tok_copy(b + t_slots - 1).start()
        tok_copy(b + 1).wait()

        @pl.when(b >= n_slots)
        def _():
            y_copy(b - n_slots).wait()

        rows = lax.broadcasted_iota(jnp.int32, (mb, xg.shape[2]), 0)
        valid = count - (b - b0) * mb
        words = jnp.where(rows < valid, xg[lax.rem(b, 2)], jnp.uint32(0))
        xa, xb = _unpack_halves(words)
        gu = _dot(xa.astype(BF16), gu_bf[0:kh, :]) + _dot(xb.astype(BF16), gu_bf[kh:, :])
        hid = _silu(gu[:, :half]) * gu[:, half:]
        ybuf[slot] = _pack_halves(_dot(hid.astype(BF16), d_bf[...]))
        gather_rows(b + 1)
        y_copy(b).start(priority=RING_DMA_PRIORITY)
        return carry

    lax.fori_loop(b0, b1, block, 0)

    @pl.when(e == n_exp - 1)
    def _():
        for i in range(1, n_slots + 1):
            @pl.when(n_blocks >= i)
            def _():
                y_copy(n_blocks - i).wait()
        for i in range(1, t_slots - 1):
            tok_copy(n_blocks + i).wait()


def _experts(block_start, counts, h1_packed, tok_rows, w_gu, w_down):
    t, dh = h1_packed.shape
    n_rows = tok_rows.shape[0]
    n_exp, d, gu_w = w_gu.shape
    ed = w_down.shape[1]
    return pl.pallas_call(
        _experts_kernel,
        grid_spec=pltpu.PrefetchScalarGridSpec(
            num_scalar_prefetch=2,
            grid=(n_exp,),
            in_specs=[pl.BlockSpec(memory_space=pl.ANY)] * 4,
            out_specs=pl.BlockSpec(memory_space=pl.ANY),
            scratch_shapes=[pltpu.VMEM((WEIGHT_SLOTS, d, gu_w), w_gu.dtype),
                            pltpu.VMEM((WEIGHT_SLOTS, ed, d), w_down.dtype),
                            pltpu.VMEM((d, gu_w), BF16), pltpu.VMEM((ed, d), BF16),
                            pltpu.VMEM((t, dh), h1_packed.dtype),
                            pltpu.VMEM((2, MOE_BLOCK, dh), jnp.uint32),
                            pltpu.VMEM((RING_SLOTS, MOE_BLOCK, dh), jnp.uint32),
                            pltpu.SMEM((TOKEN_SLOTS, MOE_BLOCK, TOKEN_ROW), jnp.int32),
                            pltpu.SemaphoreType.DMA((WEIGHT_SLOTS,)),
                            pltpu.SemaphoreType.DMA,
                            pltpu.SemaphoreType.DMA((TOKEN_SLOTS,)),
                            pltpu.SemaphoreType.DMA((RING_SLOTS,))]),
        out_shape=jax.ShapeDtypeStruct((n_rows, dh), jnp.uint32),
        compiler_params=pltpu.CompilerParams(dimension_semantics=("arbitrary",),
                                             vmem_limit_bytes=EXPERTS_VMEM_LIMIT),
        name="experts",
    )(block_start, counts, w_gu, w_down, h1_packed, tok_rows)


def _combine_kernel(base_ref, w_ref, g_ref, b_ref, yg_ref, *rest):
    o_ref = rest[-1]
    acc = base_ref[...]
    dh = acc.shape[1] // 2
    acc_a = acc[:, :dh]
    acc_b = acc[:, dh:]
    w = w_ref[...]
    for k in range(TOP_K):
        ya, yb = _unpack_halves(yg_ref[k])
        acc_a = acc_a + ya * w[:, k:k + 1]
        acc_b = acc_b + yb * w[:, k:k + 1]
    o_ref[...] = _layer_norm(jnp.concatenate([acc_a, acc_b], axis=1), g_ref[...], b_ref[...])


def _combine(base, w_tk, g, b, yg, prev, part, tc):
    t, d = base.shape
    tiles = yg.shape[1] // tc
    off = part * tiles
    in_specs = [pl.BlockSpec((tc, d), lambda i: (i + off, 0)),
                pl.BlockSpec((tc, TOP_K), lambda i: (i + off, 0)),
                pl.BlockSpec((1, d), lambda i: (0, 0)),
                pl.BlockSpec((1, d), lambda i: (0, 0)),
                pl.BlockSpec((TOP_K, tc, yg.shape[2]), lambda i: (0, i, 0))]
    args = [base, w_tk, g.reshape(1, d), b.reshape(1, d), yg]
    aliases = {}
    if prev is not None:
        in_specs.append(pl.BlockSpec(memory_space=pl.ANY))
        args.append(prev)
        aliases = {len(args) - 1: 0}
    return pl.pallas_call(
        _combine_kernel,
        grid=(tiles,),
        in_specs=in_specs,
        out_specs=pl.BlockSpec((tc, d), lambda i: (i + off, 0)),
        out_shape=jax.ShapeDtypeStruct((t, d), F32),
        input_output_aliases=aliases,
        compiler_params=_cparams(("parallel",)),
        name="combine",
    )(*args)


def _moe(base, h1_packed, eidx, w_kt, rank, counts, w_exp_gu, w_exp_down, ln_g, ln_b):
    t, d = base.shape
    n_exp = counts.shape[0]

    counts = counts.reshape(n_exp)
    blocks_per_expert = (counts + MOE_BLOCK - 1) // MOE_BLOCK
    block_start = jnp.concatenate(
        [jnp.zeros((1,), jnp.int32), jnp.cumsum(blocks_per_expert).astype(jnp.int32)])
    pad_start = block_start[:n_exp] * MOE_BLOCK
    n_blocks = -(-(t * TOP_K + n_exp * (MOE_BLOCK - 1)) // MOE_BLOCK)

    tok_rows, dest_chunks = _slot_tables(eidx, rank, pad_start, n_blocks * MOE_BLOCK, chunk=SLOT_CHUNK)
    ys = _experts(block_start, counts, h1_packed, tok_rows, w_exp_gu, w_exp_down)
    per_part = dest_chunks.shape[0] // COMBINE_PARTS
    w_tk = w_kt.T
    out = None
    for part in range(COMBINE_PARTS):
        yg = _gather_back(dest_chunks[part * per_part:(part + 1) * per_part], ys, sub=GATHER_CHUNK)
        out = _combine(base, w_tk, ln_g, ln_b, yg, out, part, tc=min(256, t))
    return out


def kernel(x, ln_in_g, ln_in_b, w_in, hg_lb_logits, hg_norm_g, gm_v_norm_g, gm_v_norm_b, gm_w_s, gm_b_s, gm_out_norm_g, w_out, ln1_g, ln1_b, w_router, router_bias, w_exp_gu, w_exp_down, w_shared_gu, w_shared_down, ln2_g, ln2_b):
    bsz, s, d = x.shape
    depth = w_in.shape[0]
    assert depth == 1, "the lower-bound table row used in the hgrn kernel assumes one layer"
    alpha = (2.0 * depth) ** 0.25
    t = bsz * s
    hg_width = hg_norm_g.shape[1]
    gm_width = gm_v_norm_g.shape[1]
    tm = min(512, t)

    x2 = x.reshape(t, d)
    h0, proj, f_pre = _in_proj(x2, ln_in_g, ln_in_b, w_in[0].astype(BF16), hg_width, hg_width, tm)
    proj3 = proj.reshape(bsz, s, proj.shape[1])
    y_hg, y_gm = _mixer(proj3, f_pre.reshape(bsz, s, hg_width), hg_lb_logits.astype(F32),
                        hg_norm_g[0], gm_v_norm_g[0], gm_v_norm_b[0], gm_w_s[0], gm_b_s[0],
                        gm_out_norm_g[0], sb=min(512, s))
    base, h1_packed, eidx, gates, rank, counts = _out_proj(
        y_hg.reshape(t, hg_width), y_gm.reshape(t, gm_width), h0, w_out[0].astype(BF16),
        ln1_g[0], ln1_b[0], w_router[0].T, router_bias[0], w_shared_gu[0].astype(BF16),
        w_shared_down[0].astype(BF16), alpha, tm)
    out = _moe(base, h1_packed, eidx, gates, rank, counts, w_exp_gu[0], w_exp_down[0],
               ln2_g[0], ln2_b[0])
    return out.reshape(bsz, s, d)
```

```python
import functools

import jax
import jax.numpy as jnp
from jax import lax
from jax.experimental import pallas as pl
from jax.experimental.pallas import tpu as pltpu
from jax.experimental.pallas import tpu_sc as plsc

F32 = jnp.float32
BF16 = jnp.bfloat16

LN_EPS = 1e-5
RMS_EPS = 1e-6
CHUNK = 64
SUB = 16
MAX_FACTORED_EXPONENT = 80.0
FACTORED_MIN_LB = float(2.718281828459045 ** (-MAX_FACTORED_EXPONENT / CHUNK))
HEAD_DIM = 128
GM_BLOCK = 128
TOP_K = 8
N_GROUPS = 8
TOPK_GROUPS = 4
ROUTED_SCALE = 2.5
MOE_BLOCK = 128
WEIGHT_SLOTS = 3
RING_SLOTS = 8
RING_DMA_PRIORITY = 1
TOKEN_SLOTS = 4
TOKEN_ROW = 128
SLOT_CHUNK = 64
ROUTE_SUBTILES = 2
COMBINE_PARTS = 4
GATHER_CHUNK = 16
VMEM_LIMIT = 48 * 1024 * 1024
EXPERTS_VMEM_LIMIT = 56 * 1024 * 1024


def _cparams(sem):
    return pltpu.CompilerParams(dimension_semantics=sem, vmem_limit_bytes=VMEM_LIMIT)


def _layer_norm(x, g, b):
    mu = jnp.mean(x, axis=-1, keepdims=True)
    xc = x - mu
    var = jnp.mean(xc * xc, axis=-1, keepdims=True)
    return xc * lax.rsqrt(var + LN_EPS) * g + b


def _silu(x):
    return x * jax.nn.sigmoid(x)


def _gelu(x):
    return 0.5 * x * (1.0 + lax.erf(x * (2.0 ** -0.5)))


def _dot(a, b):
    return jnp.dot(a, b, preferred_element_type=F32)


def _dot_nt(a, b, precision=None):
    return lax.dot_general(a, b, (((1,), (1,)), ((), ())), preferred_element_type=F32,
                           precision=precision)


def _dot_tn(a, b):
    return lax.dot_general(a, b, (((0,), (0,)), ((), ())), preferred_element_type=F32)


def _split_bf16(x):
    hi = x.astype(BF16)
    return hi, (x - hi.astype(F32)).astype(BF16)


def _pack_halves(x):
    n = x.shape[1] // 2
    hi = lax.bitcast_convert_type(x[:, :n].astype(BF16).astype(F32), jnp.uint32)
    lo = lax.bitcast_convert_type(x[:, n:].astype(BF16).astype(F32), jnp.uint32)
    return hi | (lo >> 16)


def _unpack_halves(w):
    a = lax.bitcast_convert_type(w & jnp.uint32(0xFFFF0000), F32)
    b = lax.bitcast_convert_type(w << 16, F32)
    return a, b


def _in_proj_kernel(x_ref, g_ref, b_ref, w_ref, h_ref, p_ref, f_ref, *, f_lo):
    h = _layer_norm(x_ref[...], g_ref[...], b_ref[...])
    h_ref[...] = h
    p = _dot(h.astype(BF16), w_ref[...])
    p_ref[...] = p.astype(BF16)
    f_ref[...] = p[:, f_lo:f_lo + f_ref.shape[1]]


def _in_proj(x2, g, b, w_bf, f_lo, f_width, tm):
    t, d = x2.shape
    n = w_bf.shape[1]
    return pl.pallas_call(
        functools.partial(_in_proj_kernel, f_lo=f_lo),
        grid=(t // tm,),
        in_specs=[pl.BlockSpec((tm, d), lambda i: (i, 0)),
                  pl.BlockSpec((1, d), lambda i: (0, 0)),
                  pl.BlockSpec((1, d), lambda i: (0, 0)),
                  pl.BlockSpec((d, n), lambda i: (0, 0))],
        out_specs=[pl.BlockSpec((tm, d), lambda i: (i, 0)),
                   pl.BlockSpec((tm, n), lambda i: (i, 0)),
                   pl.BlockSpec((tm, f_width), lambda i: (i, 0))],
        out_shape=[jax.ShapeDtypeStruct((t, d), F32), jax.ShapeDtypeStruct((t, n), BF16),
                   jax.ShapeDtypeStruct((t, f_width), F32)],
        compiler_params=_cparams(("parallel",)),
        name="in_proj",
    )(x2, g.reshape(1, d), b.reshape(1, d), w_bf)


def _mixer_kernel(q_ref, f_ref, i_ref, g_ref, lbl_ref, gn_ref, u_ref, v_ref, vg_ref, vb_ref, ws_ref,
                  bs_ref, og_ref, o_ref, ogm_ref, st_ref, *, n_chunks):
    @pl.when(pl.program_id(1) == 0)
    def _():
        st_ref[...] = jnp.zeros_like(st_ref)

    lg = lbl_ref[...]
    ex = jnp.exp(lg - jnp.max(lg, axis=0, keepdims=True))
    lb_all = ex[0:1, :] / jnp.sum(ex, axis=0, keepdims=True)
    gn_all = gn_ref[...]

    c = CHUNK
    heads = st_ref.shape[0]
    row = lax.broadcasted_iota(jnp.int32, (c, c), 0)
    col = lax.broadcasted_iota(jnp.int32, (c, c), 1)
    tril = (row >= col).astype(BF16)
    sub_row = lax.broadcasted_iota(jnp.int32, (SUB, HEAD_DIM), 0)
    ones = jnp.ones((HEAD_DIM, HEAD_DIM), BF16)
    neg_inf = jnp.float32(-jnp.inf)

    def intra_pairwise(q, k, v, v_bf, b, o_inter):
        outs = []
        for blk in range(c // SUB):
            lo = blk * SUB
            b_i = b[lo:lo + SUB, :]
            q_i = q[lo:lo + SUB, :]
            k_i = k[lo:lo + SUB, :]
            v_i = v[lo:lo + SUB, :]
            prods = []
            for j in range(SUB):
                diff = jnp.where(sub_row >= j, b_i - b_i[j:j + 1, :], neg_inf)
                prods.append(q_i * k_i[j:j + 1, :] * jnp.exp(diff))
            p_all = jnp.concatenate(prods, axis=0).astype(BF16)
            r_all = _dot(p_all, ones)
            o_blk = o_inter[lo:lo + SUB, :]
            for j in range(SUB):
                o_blk = o_blk + r_all[j * SUB:(j + 1) * SUB, :] * v_i[j:j + 1, :]
            if blk > 0:
                b_ref = b[lo - 1:lo, :]
                q_hat = (q_i * jnp.exp(b_i - b_ref)).astype(BF16)
                k_hat = (k[0:lo, :] * jnp.exp(b_ref - b[0:lo, :])).astype(BF16)
                scores = _dot_nt(q_hat, k_hat)
                o_blk = o_blk + _dot(scores.astype(BF16), v_bf[0:lo, :])
            outs.append(o_blk)
        return jnp.concatenate(outs, axis=0)

    def one_chunk(r0, factored):
        rows = pl.ds(r0, c)
        hs = [slice(h * HEAD_DIM, (h + 1) * HEAD_DIM) for h in range(heads)]
        q = _silu(q_ref[rows, :].astype(F32))
        f = lb_all + (1.0 - lb_all) * jax.nn.sigmoid(f_ref[rows, :])
        v_bf = i_ref[rows, :]
        v = v_bf.astype(F32)
        lf = jnp.log(f)
        k = 1.0 - f
        lf_hi, lf_lo = _split_bf16(lf)
        b2 = _dot(tril, jnp.concatenate([lf_hi, lf_lo], axis=1))
        width = lf.shape[1]
        b = b2[:, :width] + b2[:, width:]
        b_last = b[c - 1:c, :]
        q_dec = (q * jnp.exp(b)).astype(BF16)
        k_dec = (k * jnp.exp(b_last - b)).astype(BF16)
        decay = jnp.exp(b_last)
        st = [st_ref[h] for h in range(heads)]
        st_bf = [s.astype(BF16) for s in st]
        if factored:
            k_grow = (k * jnp.exp(-b)).astype(BF16)
            both = [_dot_nt(q_dec[:, hs[h]], jnp.concatenate([st_bf[h], k_grow[:, hs[h]]], axis=0))
                    for h in range(heads)]
            scores = [jnp.where(row >= col, both[h][:, HEAD_DIM:], 0.0).astype(BF16)
                      for h in range(heads)]
            outs = [both[h][:, :HEAD_DIM] + _dot(scores[h], v_bf[:, hs[h]]) for h in range(heads)]
        else:
            outs = [intra_pairwise(q[:, hs[h]], k[:, hs[h]], v[:, hs[h]], v_bf[:, hs[h]],
                                   b[:, hs[h]], _dot_nt(q_dec[:, hs[h]], st_bf[h]))
                    for h in range(heads)]
        for h in range(heads):
            st_ref[h] = st[h] * decay[:, hs[h]] + _dot_tn(v_bf[:, hs[h]], k_dec[:, hs[h]])
        inv = [lax.rsqrt(jnp.mean(o * o, axis=-1, keepdims=True) + RMS_EPS) for o in outs]
        o = jnp.concatenate([outs[h] * inv[h] for h in range(heads)], axis=1)
        o_ref[rows, :] = (o * gn_all * _silu(g_ref[rows, :].astype(F32))).astype(o_ref.dtype)

    groups = ws_ref.shape[0]
    gdim = u_ref.shape[-1] // groups
    gm_row = lax.broadcasted_iota(jnp.int32, (GM_BLOCK, GM_BLOCK), 0) // CHUNK
    gm_col = lax.broadcasted_iota(jnp.int32, (GM_BLOCK, GM_BLOCK), 1) // CHUNK
    ws = [jnp.where(gm_row >= gm_col, ws_ref[g], 0.0).astype(BF16) for g in range(groups)]

    def gmlp_block(r0):
        rows = pl.ds(r0, GM_BLOCK)
        u = _gelu(u_ref[rows, :].astype(F32))
        v = _layer_norm(_gelu(v_ref[rows, :].astype(F32)), vg_ref[...], vb_ref[...])
        v_bf = v.astype(BF16)
        mixed = jnp.concatenate(
            [_dot(ws[g], v_bf[:, g * gdim:(g + 1) * gdim]) for g in range(groups)], axis=1)
        y = u * (mixed + bs_ref[...])
        inv = lax.rsqrt(jnp.mean(y * y, axis=-1, keepdims=True) + RMS_EPS)
        ogm_ref[rows, :] = (y * inv * og_ref[...]).astype(ogm_ref.dtype)

    def run_chunks(factored):
        per_trip = 4 if (factored and n_chunks % 4 == 0) else 1
        trip_rows = per_trip * c
        inline_gmlp = trip_rows % GM_BLOCK == 0

        def chunk_body(ci, carry):
            for u in range(per_trip):
                one_chunk(pl.multiple_of((ci * per_trip + u) * c, c), factored)
            if inline_gmlp:
                for n in range(trip_rows // GM_BLOCK):
                    gmlp_block(pl.multiple_of(ci * trip_rows + n * GM_BLOCK, GM_BLOCK))
            return carry

        lax.fori_loop(0, n_chunks // per_trip, chunk_body, 0)
        if not inline_gmlp:
            for n in range(n_chunks * c // GM_BLOCK):
                gmlp_block(n * GM_BLOCK)

    bounded = jnp.min(lb_all) >= FACTORED_MIN_LB
    pl.when(bounded)(lambda: run_chunks(True))
    pl.when(jnp.logical_not(bounded))(lambda: run_chunks(False))


def _mixer(proj3, f_pre3, lb_logits, norm_g, v_norm_g, v_norm_b, w_s, b_s, out_norm_g, sb):
    bsz, s, _ = proj3.shape
    hg_width = norm_g.shape[0]
    heads = hg_width // HEAD_DIM
    gw = v_norm_g.shape[0]
    assert gw == hg_width and sb % GM_BLOCK == 0
    groups = w_s.shape[0]
    bias_full = jnp.repeat(b_s.T, gw // groups, axis=1)

    def sect(k):
        return pl.BlockSpec((None, sb, hg_width), lambda b, j, k=k: (b, j, k))

    row_vec = pl.BlockSpec((1, hg_width), lambda b, j: (0, 0))
    return pl.pallas_call(
        functools.partial(_mixer_kernel, n_chunks=sb // CHUNK),
        grid=(bsz, s // sb),
        in_specs=[sect(0), sect(0), sect(2), sect(3),
                  pl.BlockSpec((lb_logits.shape[0], hg_width), lambda b, j: (0, 0)),
                  row_vec, sect(4), sect(5), row_vec, row_vec,
                  pl.BlockSpec((groups, GM_BLOCK, GM_BLOCK), lambda b, j: (0, 0, 0)),
                  pl.BlockSpec((GM_BLOCK, gw), lambda b, j: (0, 0)),
                  row_vec],
        out_specs=[pl.BlockSpec((None, sb, hg_width), lambda b, j: (b, j, 0)),
                   pl.BlockSpec((None, sb, gw), lambda b, j: (b, j, 0))],
        out_shape=[jax.ShapeDtypeStruct((bsz, s, hg_width), BF16),
                   jax.ShapeDtypeStruct((bsz, s, gw), BF16)],
        scratch_shapes=[pltpu.VMEM((heads, HEAD_DIM, HEAD_DIM), F32)],
        compiler_params=_cparams(("parallel", "arbitrary")),
        name="mixer",
    )(proj3, f_pre3, proj3, proj3, lb_logits, norm_g.reshape(1, hg_width), proj3, proj3,
      v_norm_g.reshape(1, gw), v_norm_b.reshape(1, gw), w_s, bias_full, out_norm_g.reshape(1, gw))


def _out_proj_kernel(yh_ref, yg_ref, h0_ref, wa_ref, wb_ref, g_ref, b_ref, wr_ref, rb_ref,
                     wsg_ref, wsd_ref, base_ref, hp_ref, eidx_ref, gate_ref, rank_ref, cnt_ref,
                     carry_ref, *, alpha):
    @pl.when(pl.program_id(0) == 0)
    def _():
        carry_ref[...] = jnp.zeros_like(carry_ref)

    tm = h0_ref.shape[0]
    sub = tm // ROUTE_SUBTILES
    w_hi, w_lo = _split_bf16(wr_ref[...])
    logits = []
    for s in range(ROUTE_SUBTILES):
        rows = pl.ds(s * sub, sub)
        mix = _dot(yh_ref[rows, :], wa_ref[...]) + _dot(yg_ref[rows, :], wb_ref[...])
        h1 = _layer_norm(alpha * h0_ref[rows, :] + mix, g_ref[...], b_ref[...])
        hp_ref[rows, :] = _pack_halves(h1)
        h_hi, h_lo = _split_bf16(h1)
        logits.append(_dot_nt(w_hi, h_hi) + (_dot_nt(w_hi, h_lo) + _dot_nt(w_lo, h_hi)))
        gu = _dot(h_hi, wsg_ref[...])
        half = gu.shape[1] // 2
        hid = (_silu(gu[:, :half]) * gu[:, half:]).astype(BF16)
        base_ref[rows, :] = alpha * h1 + _dot(hid, wsd_ref[...])
    total = carry_ref[...]
    for s in range(ROUTE_SUBTILES):
        lanes = pl.ds(s * sub, sub)
        eidx, gates, rank, total = _route_tile(logits[s], rb_ref[...], total)
        eidx_ref[:, lanes] = eidx
        gate_ref[:, lanes] = gates
        rank_ref[:, lanes] = rank
    carry_ref[...] = total
    cnt_ref[...] = total.astype(jnp.int32)


def _out_proj(y_hg, y_gm, h0, w_out_bf, g, b, w_router_t, router_bias, ws_gu_bf, ws_down_bf, alpha, tm):
    t, d = h0.shape
    sg = ws_gu_bf.shape[1]
    sd = ws_down_bf.shape[0]
    hw = y_hg.shape[1]
    gw = y_gm.shape[1]
    e = w_router_t.shape[0]
    return pl.pallas_call(
        functools.partial(_out_proj_kernel, alpha=alpha),
        grid=(t // tm,),
        in_specs=[pl.BlockSpec((tm, hw), lambda i: (i, 0)),
                  pl.BlockSpec((tm, gw), lambda i: (i, 0)),
                  pl.BlockSpec((tm, d), lambda i: (i, 0)),
                  pl.BlockSpec((hw, d), lambda i: (0, 0)),
                  pl.BlockSpec((gw, d), lambda i: (0, 0)),
                  pl.BlockSpec((1, d), lambda i: (0, 0)),
                  pl.BlockSpec((1, d), lambda i: (0, 0)),
                  pl.BlockSpec((e, d), lambda i: (0, 0)),
                  pl.BlockSpec((e, 1), lambda i: (0, 0)),
                  pl.BlockSpec((d, sg), lambda i: (0, 0)),
                  pl.BlockSpec((sd, d), lambda i: (0, 0))],
        out_specs=[pl.BlockSpec((tm, d), lambda i: (i, 0)),
                   pl.BlockSpec((tm, d // 2), lambda i: (i, 0)),
                   pl.BlockSpec((TOP_K, tm), lambda i: (0, i)),
                   pl.BlockSpec((TOP_K, tm), lambda i: (0, i)),
                   pl.BlockSpec((TOP_K, tm), lambda i: (0, i)),
                   pl.BlockSpec((e, 1), lambda i: (0, 0))],
        out_shape=[jax.ShapeDtypeStruct((t, d), F32), jax.ShapeDtypeStruct((t, d // 2), jnp.uint32),
                   jax.ShapeDtypeStruct((TOP_K, t), jnp.int32),
                   jax.ShapeDtypeStruct((TOP_K, t), F32),
                   jax.ShapeDtypeStruct((TOP_K, t), jnp.int32),
                   jax.ShapeDtypeStruct((e, 1), jnp.int32)],
        scratch_shapes=[pltpu.VMEM((e, 1), F32)],
        compiler_params=_cparams(("arbitrary",)),
        name="out_proj",
    )(y_hg, y_gm, h0, w_out_bf[:hw], w_out_bf[hw:], g.reshape(1, d), b.reshape(1, d), w_router_t,
      router_bias.astype(F32).reshape(e, 1), ws_gu_bf, ws_down_bf)


def _route_tile(logits, bias, carry):
    n_exp, tt = logits.shape
    per_group = n_exp // N_GROUPS
    neg_inf = jnp.float32(-jnp.inf)
    scores = jax.nn.sigmoid(logits)
    biased = scores + bias

    gio = lax.broadcasted_iota(jnp.int32, (per_group, tt), 0)
    blocks, gs_rows = [], []
    for g in range(N_GROUPS):
        blk = biased[g * per_group:(g + 1) * per_group, :]
        m1 = jnp.max(blk, axis=0, keepdims=True)
        first = jnp.min(jnp.where(blk == m1, gio, per_group), axis=0, keepdims=True)
        m2 = jnp.max(jnp.where(gio == first, neg_inf, blk), axis=0, keepdims=True)
        blocks.append(blk)
        gs_rows.append(m1 + m2)
    gs = jnp.concatenate(gs_rows, axis=0)
    gidx = lax.broadcasted_iota(jnp.int32, (N_GROUPS, tt), 0)
    beaten = jnp.zeros((N_GROUPS, tt), jnp.int32)
    for g in range(N_GROUPS):
        r = gs_rows[g]
        beaten = beaten + jnp.where((r > gs) | ((r == gs) & (g < gidx)), 1, 0)
    keep = beaten < TOPK_GROUPS
    cand = jnp.concatenate(
        [jnp.where(keep[g:g + 1, :], blocks[g], neg_inf) for g in range(N_GROUPS)], axis=0)

    eio = lax.broadcasted_iota(jnp.int32, (n_exp, tt), 0)
    idx_rows, w_rows = [], []
    chosen = jnp.zeros((n_exp, tt), F32)
    for _ in range(TOP_K):
        m = jnp.max(cand, axis=0, keepdims=True)
        idx = jnp.min(jnp.where(cand == m, eio, n_exp), axis=0, keepdims=True)
        sel = eio == idx
        w_rows.append(jnp.sum(jnp.where(sel, scores, 0.0), axis=0, keepdims=True))
        idx_rows.append(idx)
        cand = jnp.where(sel, neg_inf, cand)
        chosen = jnp.where(sel, 1.0, chosen)
    eidx = jnp.concatenate(idx_rows, axis=0)
    w = jnp.concatenate(w_rows, axis=0)
    gates = w / jnp.sum(w, axis=0, keepdims=True) * ROUTED_SCALE

    tr = lax.broadcasted_iota(jnp.int32, (tt, tt), 0)
    tc = lax.broadcasted_iota(jnp.int32, (tt, tt), 1)
    before = (tr < tc).astype(BF16)
    cum = _dot(chosen.astype(BF16), before) + carry
    rank = jnp.concatenate(
        [jnp.sum(jnp.where(eio == idx_rows[k], cum, 0.0), axis=0, keepdims=True)
         for k in range(TOP_K)], axis=0).astype(jnp.int32)
    total = carry + jnp.sum(chosen, axis=1, keepdims=True)
    return eidx, gates, rank, total


def _chunked(table, chunk):
    k, t = table.shape
    return table.reshape(k, t // chunk, chunk).transpose(1, 0, 2)


def _sc_workers():
    info = plsc.get_sparse_core_info()
    return info.num_cores, info.num_cores * info.num_subcores


def _slot_tables(eidx, rank, pad_start, n_rows, chunk):
    k_top, t = eidx.shape
    n_exp = pad_start.shape[0]
    n_cores, n_workers = _sc_workers()
    n_chunks = t // chunk
    per_worker = n_chunks // n_workers
    lanes = plsc.get_sparse_core_info().num_lanes
    mesh = plsc.VectorSubcoreMesh(core_axis_name="c", subcore_axis_name="s")

    @functools.partial(
        pl.kernel, mesh=mesh,
        out_type=(jax.ShapeDtypeStruct((n_rows, TOKEN_ROW), jnp.int32),
                  jax.ShapeDtypeStruct((n_chunks, k_top, chunk), jnp.int32)),
        scratch_types=[pltpu.VMEM((n_exp,), jnp.int32),
                       pltpu.VMEM((k_top, chunk), jnp.int32),
                       pltpu.VMEM((k_top, chunk), jnp.int32),
                       pltpu.VMEM((k_top, chunk), jnp.int32),
                       pltpu.VMEM((chunk, TOKEN_ROW), jnp.int32),
                       pltpu.SemaphoreType.DMA],
        compiler_params=pltpu.CompilerParams(needs_layout_passes=False),
        name="slot_tables",
    )
    def build(e_hbm, r_hbm, ps_hbm, tok_hbm, d_hbm, ps_v, e_v, r_v, idx_v, rows_v, sem):
        wid = lax.axis_index("s") * n_cores + lax.axis_index("c")
        pltpu.sync_copy(ps_hbm, ps_v)

        @pl.loop(0, per_worker)
        def _(j):
            c = wid * per_worker + j
            pltpu.sync_copy(e_hbm.at[c], e_v)
            pltpu.sync_copy(r_hbm.at[c], r_v)
            for k in range(k_top):
                for s in range(chunk // lanes):
                    sl = pl.ds(s * lanes, lanes)
                    idx_v[k, sl] = plsc.load_gather(ps_v, [e_v[k, sl]]) + r_v[k, sl]
            for i in range(chunk):
                rows_v[i, pl.ds(0, lanes)] = jnp.full((lanes,), c * chunk + i, jnp.int32)
            pltpu.sync_copy(idx_v, d_hbm.at[c])
            copies = [pltpu.async_copy(rows_v, tok_hbm.at[idx_v.at[k]], sem) for k in range(k_top)]
            for cp in copies:
                cp.wait()

    return build(_chunked(eidx, chunk), _chunked(rank, chunk), pad_start)


def _gather_back(dest_chunks, ys, sub):
    n_chunks, k_top, chunk = dest_chunks.shape
    t = n_chunks * chunk
    dh = ys.shape[1]
    n_cores, n_workers = _sc_workers()
    per_worker = n_chunks // n_workers
    mesh = plsc.VectorSubcoreMesh(core_axis_name="c", subcore_axis_name="s")

    @functools.partial(
        pl.kernel, mesh=mesh,
        out_type=jax.ShapeDtypeStruct((k_top, t, dh), ys.dtype),
        scratch_types=[pltpu.VMEM((k_top, chunk), jnp.int32),
                       pltpu.VMEM((k_top, sub, dh), ys.dtype),
                       pltpu.SemaphoreType.DMA],
        name="gather_back",
    )
    def gather_rows(ys_hbm, dest_hbm, o_hbm, idx_v, rows_v, sem):
        wid = lax.axis_index("s") * n_cores + lax.axis_index("c")

        @pl.loop(0, per_worker)
        def _(j):
            c = wid * per_worker + j
            pltpu.sync_copy(dest_hbm.at[c], idx_v)
            for q in range(chunk // sub):
                copies = [pltpu.async_copy(ys_hbm.at[idx_v.at[k, pl.ds(q * sub, sub)]], rows_v.at[k], sem)
                          for k in range(k_top)]
                for cp in copies:
                    cp.wait()
                for k in range(k_top):
                    pltpu.sync_copy(rows_v.at[k], o_hbm.at[k, pl.ds(c * chunk + q * sub, sub)])

    return gather_rows(ys, dest_chunks)


def _experts_kernel(bs_ref, cnt_ref, wgu_hbm, wd_hbm, h_hbm, tok_hbm, ys_hbm, wgu_buf, wd_buf, gu_bf,
                    d_bf, h_v, xg, ybuf, tokbuf, wsem, hsem, tsem, ysem):
    e = pl.program_id(0)
    n_exp = pl.num_programs(0)
    n_slots, mb, _ = ybuf.shape
    w_slots = wgu_buf.shape[0]
    t_slots = tokbuf.shape[0]
    n_tok = h_v.shape[0]
    n_blocks = bs_ref[n_exp]
    b0 = bs_ref[e]
    b1 = bs_ref[e + 1]
    count = cnt_ref[e]

    def w_copies(ex):
        slot = lax.rem(ex, w_slots)
        return (pltpu.make_async_copy(wgu_hbm.at[ex], wgu_buf.at[slot], wsem.at[slot]),
                pltpu.make_async_copy(wd_hbm.at[ex], wd_buf.at[slot], wsem.at[slot]))

    def tok_copy(b):
        src = jnp.minimum(b, n_blocks - 1)
        slot = lax.rem(b, t_slots)
        return pltpu.make_async_copy(tok_hbm.at[pl.ds(src, 1)], tokbuf.at[slot], tsem.at[slot])

    def y_copy(b):
        slot = lax.rem(b, n_slots)
        return pltpu.make_async_copy(ybuf.at[slot],
                                     ys_hbm.at[pl.ds(pl.multiple_of(b * mb, mb), mb)], ysem.at[slot])

    def gather_rows(b):
        ids = lax.rem(b, t_slots)
        dst = lax.rem(b, 2)
        for i in range(mb):
            t = jnp.clip(tokbuf[ids, 0, i], 0, n_tok - 1)
            xg[dst, pl.ds(i, 1), :] = h_v[pl.ds(t, 1), :]

    @pl.when(e == 0)
    def _():
        resident = pltpu.make_async_copy(h_hbm, h_v, hsem)
        resident.start()
        for i in range(w_slots - 1):
            @pl.when(i < n_exp)
            def _():
                for cp in w_copies(i):
                    cp.start()
        for i in range(t_slots - 1):
            tok_copy(i).start()
        resident.wait()
        tok_copy(0).wait()
        gather_rows(0)

    @pl.when(e + w_slots - 1 < n_exp)
    def _():
        for cp in w_copies(e + w_slots - 1):
            cp.start()

    for cp in w_copies(e):
        cp.wait()
    w_slot = lax.rem(e, w_slots)

    @pl.when(b1 > b0)
    def _():
        gu_bf[...] = wgu_buf[w_slot].astype(BF16)
        d_bf[...] = wd_buf[w_slot].astype(BF16)

    kh = gu_bf.shape[0] // 2
    half = gu_bf.shape[1] // 2

    def block(b, carry):
        slot = lax.rem(b, n_slots)
        tok_copy(b + t_slots - 1).start()
        tok_copy(b + 1).wait()

        @pl.when(b >= n_slots)
        def _():
            y_copy(b - n_slots).wait()

        rows = lax.broadcasted_iota(jnp.int32, (mb, xg.shape[2]), 0)
        valid = count - (b - b0) * mb
        words = jnp.where(rows < valid, xg[lax.rem(b, 2)], jnp.uint32(0))
        xa, xb = _unpack_halves(words)
        gu = _dot(xa.astype(BF16), gu_bf[0:kh, :]) + _dot(xb.astype(BF16), gu_bf[kh:, :])
        hid = _silu(gu[:, :half]) * gu[:, half:]
        ybuf[slot] = _pack_halves(_dot(hid.astype(BF16), d_bf[...]))
        gather_rows(b + 1)
        y_copy(b).start(priority=RING_DMA_PRIORITY)
        return carry

    lax.fori_loop(b0, b1, block, 0)

    @pl.when(e == n_exp - 1)
    def _():
        for i in range(1, n_slots + 1):
            @pl.when(n_blocks >= i)
            def _():
                y_copy(n_blocks - i).wait()
        for i in range(1, t_slots - 1):
            tok_copy(n_blocks + i).wait()


def _experts(block_start, counts, h1_packed, tok_ids, w_gu, w_down):
    t, dh = h1_packed.shape
    n_rows = tok_ids.shape[0] * tok_ids.shape[1]
    n_exp, d, gu_w = w_gu.shape
    ed = w_down.shape[1]
    return pl.pallas_call(
        _experts_kernel,
        grid_spec=pltpu.PrefetchScalarGridSpec(
            num_scalar_prefetch=2,
            grid=(n_exp,),
            in_specs=[pl.BlockSpec(memory_space=pl.ANY)] * 4,
            out_specs=pl.BlockSpec(memory_space=pl.ANY),
            scratch_shapes=[pltpu.VMEM((WEIGHT_SLOTS, d, gu_w), w_gu.dtype),
                            pltpu.VMEM((WEIGHT_SLOTS, ed, d), w_down.dtype),
                            pltpu.VMEM((d, gu_w), BF16), pltpu.VMEM((ed, d), BF16),
                            pltpu.VMEM((t, dh), h1_packed.dtype),
                            pltpu.VMEM((2, MOE_BLOCK, dh), jnp.uint32),
                            pltpu.VMEM((RING_SLOTS, MOE_BLOCK, dh), jnp.uint32),
                            pltpu.SMEM((TOKEN_SLOTS, 1, MOE_BLOCK), jnp.int32),
                            pltpu.SemaphoreType.DMA((WEIGHT_SLOTS,)),
                            pltpu.SemaphoreType.DMA,
                            pltpu.SemaphoreType.DMA((TOKEN_SLOTS,)),
                            pltpu.SemaphoreType.DMA((RING_SLOTS,))]),
        out_shape=jax.ShapeDtypeStruct((n_rows, dh), jnp.uint32),
        compiler_params=pltpu.CompilerParams(dimension_semantics=("arbitrary",),
                                             vmem_limit_bytes=EXPERTS_VMEM_LIMIT),
        name="experts",
    )(block_start, counts, w_gu, w_down, h1_packed, tok_ids)


def _combine_kernel(base_ref, w_ref, g_ref, b_ref, yg_ref, *rest):
    o_ref = rest[-1]
    acc = base_ref[...]
    dh = acc.shape[1] // 2
    acc_a = acc[:, :dh]
    acc_b = acc[:, dh:]
    w = w_ref[...]
    for k in range(TOP_K):
        ya, yb = _unpack_halves(yg_ref[k])
        acc_a = acc_a + ya * w[:, k:k + 1]
        acc_b = acc_b + yb * w[:, k:k + 1]
    o_ref[...] = _layer_norm(jnp.concatenate([acc_a, acc_b], axis=1), g_ref[...], b_ref[...])


def _combine(base, w_tk, g, b, yg, prev, part, tc):
    t, d = base.shape
    tiles = yg.shape[1] // tc
    off = part * tiles
    in_specs = [pl.BlockSpec((tc, d), lambda i: (i + off, 0)),
                pl.BlockSpec((tc, TOP_K), lambda i: (i + off, 0)),
                pl.BlockSpec((1, d), lambda i: (0, 0)),
                pl.BlockSpec((1, d), lambda i: (0, 0)),
                pl.BlockSpec((TOP_K, tc, yg.shape[2]), lambda i: (0, i, 0))]
    args = [base, w_tk, g.reshape(1, d), b.reshape(1, d), yg]
    aliases = {}
    if prev is not None:
        in_specs.append(pl.BlockSpec(memory_space=pl.ANY))
        args.append(prev)
        aliases = {len(args) - 1: 0}
    return pl.pallas_call(
        _combine_kernel,
        grid=(tiles,),
        in_specs=in_specs,
        out_specs=pl.BlockSpec((tc, d), lambda i: (i + off, 0)),
        out_shape=jax.ShapeDtypeStruct((t, d), F32),
        input_output_aliases=aliases,
        compiler_params=_cparams(("parallel",)),
        name="combine",
    )(*args)


def _moe(base, h1_packed, eidx, w_kt, rank, counts, w_exp_gu, w_exp_down, ln_g, ln_b):
    t, d = base.shape
    n_exp = counts.shape[0]

    counts = counts.reshape(n_exp)
    blocks_per_expert = (counts + MOE_BLOCK - 1) // MOE_BLOCK
    block_start = jnp.concatenate(
        [jnp.zeros((1,), jnp.int32), jnp.cumsum(blocks_per_expert).astype(jnp.int32)])
    pad_start = block_start[:n_exp] * MOE_BLOCK
    n_blocks = -(-(t * TOP_K + n_exp * (MOE_BLOCK - 1)) // MOE_BLOCK)

    tok_rows, dest_chunks = _slot_tables(eidx, rank, pad_start, n_blocks * MOE_BLOCK, chunk=SLOT_CHUNK)
    tok_ids = tok_rows[:, 0].reshape(n_blocks, MOE_BLOCK)
    ys = _experts(block_start, counts, h1_packed, tok_ids, w_exp_gu, w_exp_down)
    per_part = dest_chunks.shape[0] // COMBINE_PARTS
    w_tk = w_kt.T
    out = None
    for part in range(COMBINE_PARTS):
        yg = _gather_back(dest_chunks[part * per_part:(part + 1) * per_part], ys, sub=GATHER_CHUNK)
        out = _combine(base, w_tk, ln_g, ln_b, yg, out, part, tc=min(256, t))
    return out


def kernel(x, ln_in_g, ln_in_b, w_in, hg_lb_logits, hg_norm_g, gm_v_norm_g, gm_v_norm_b, gm_w_s, gm_b_s, gm_out_norm_g, w_out, ln1_g, ln1_b, w_router, router_bias, w_exp_gu, w_exp_down, w_shared_gu, w_shared_down, ln2_g, ln2_b):
    bsz, s, d = x.shape
    depth = w_in.shape[0]
    assert depth == 1, "the lower-bound table row used in the hgrn kernel assumes one layer"
    alpha = (2.0 * depth) ** 0.25
    t = bsz * s
    hg_width = hg_norm_g.shape[1]
    gm_width = gm_v_norm_g.shape[1]
    tm = min(512, t)

    x2 = x.reshape(t, d)
    h0, proj, f_pre = _in_proj(x2, ln_in_g, ln_in_b, w_in[0].astype(BF16), hg_width, hg_width, tm)
    proj3 = proj.reshape(bsz, s, proj.shape[1])
    y_hg, y_gm = _mixer(proj3, f_pre.reshape(bsz, s, hg_width), hg_lb_logits.astype(F32),
                        hg_norm_g[0], gm_v_norm_g[0], gm_v_norm_b[0], gm_w_s[0], gm_b_s[0],
                        gm_out_norm_g[0], sb=min(512, s))
    base, h1_packed, eidx, gates, rank, counts = _out_proj(
        y_hg.reshape(t, hg_width), y_gm.reshape(t, gm_width), h0, w_out[0].astype(BF16),
        ln1_g[0], ln1_b[0], w_router[0].T, router_bias[0], w_shared_gu[0].astype(BF16),
        w_shared_down[0].astype(BF16), alpha, tm)
    out = _moe(base, h1_packed, eidx, gates, rank, counts, w_exp_gu[0], w_exp_down[0],
               ln2_g[0], ln2_b[0])
    return out.reshape(bsz, s, d)
```

```python
import functools

import jax
import jax.numpy as jnp
from jax import lax
from jax.experimental import pallas as pl
from jax.experimental.pallas import tpu as pltpu
from jax.experimental.pallas import tpu_sc as plsc

F32 = jnp.float32
BF16 = jnp.bfloat16

LN_EPS = 1e-5
RMS_EPS = 1e-6
CHUNK = 64
SUB = 16
MAX_FACTORED_EXPONENT = 80.0
FACTORED_MIN_LB = float(2.718281828459045 ** (-MAX_FACTORED_EXPONENT / CHUNK))
HEAD_DIM = 128
GM_BLOCK = 128
TOP_K = 8
N_GROUPS = 8
TOPK_GROUPS = 4
ROUTED_SCALE = 2.5
MOE_BLOCK = 128
WEIGHT_SLOTS = 3
RING_SLOTS = 8
RING_DMA_PRIORITY = 1
TOKEN_SLOTS = 4
TOKEN_ROW = 128
SLOT_CHUNK = 64
COMBINE_PARTS = 2
GATHER_CHUNK = 16
VMEM_LIMIT = 48 * 1024 * 1024
EXPERTS_VMEM_LIMIT = 56 * 1024 * 1024


def _cparams(sem):
    return pltpu.CompilerParams(dimension_semantics=sem, vmem_limit_bytes=VMEM_LIMIT)


def _layer_norm(x, g, b):
    mu = jnp.mean(x, axis=-1, keepdims=True)
    xc = x - mu
    var = jnp.mean(xc * xc, axis=-1, keepdims=True)
    return xc * lax.rsqrt(var + LN_EPS) * g + b


def _silu(x):
    return x * jax.nn.sigmoid(x)


def _gelu(x):
    return 0.5 * x * (1.0 + lax.erf(x * (2.0 ** -0.5)))


def _dot(a, b):
    return jnp.dot(a, b, preferred_element_type=F32)


def _dot_nt(a, b, precision=None):
    return lax.dot_general(a, b, (((1,), (1,)), ((), ())), preferred_element_type=F32,
                           precision=precision)


def _dot_tn(a, b):
    return lax.dot_general(a, b, (((0,), (0,)), ((), ())), preferred_element_type=F32)


def _split_bf16(x):
    hi = x.astype(BF16)
    return hi, (x - hi.astype(F32)).astype(BF16)


def _pack_halves(x):
    n = x.shape[1] // 2
    hi = lax.bitcast_convert_type(x[:, :n].astype(BF16).astype(F32), jnp.uint32)
    lo = lax.bitcast_convert_type(x[:, n:].astype(BF16).astype(F32), jnp.uint32)
    return hi | (lo >> 16)


def _unpack_halves(w):
    a = lax.bitcast_convert_type(w & jnp.uint32(0xFFFF0000), F32)
    b = lax.bitcast_convert_type(w << 16, F32)
    return a, b


def _in_proj_kernel(x_ref, g_ref, b_ref, w_ref, h_ref, p_ref, f_ref, *, f_lo):
    h = _layer_norm(x_ref[...], g_ref[...], b_ref[...])
    h_ref[...] = h
    p = _dot(h.astype(BF16), w_ref[...])
    p_ref[...] = p.astype(BF16)
    f_ref[...] = p[:, f_lo:f_lo + f_ref.shape[1]]


def _in_proj(x2, g, b, w_bf, f_lo, f_width, tm):
    t, d = x2.shape
    n = w_bf.shape[1]
    return pl.pallas_call(
        functools.partial(_in_proj_kernel, f_lo=f_lo),
        grid=(t // tm,),
        in_specs=[pl.BlockSpec((tm, d), lambda i: (i, 0)),
                  pl.BlockSpec((1, d), lambda i: (0, 0)),
                  pl.BlockSpec((1, d), lambda i: (0, 0)),
                  pl.BlockSpec((d, n), lambda i: (0, 0))],
        out_specs=[pl.BlockSpec((tm, d), lambda i: (i, 0)),
                   pl.BlockSpec((tm, n), lambda i: (i, 0)),
                   pl.BlockSpec((tm, f_width), lambda i: (i, 0))],
        out_shape=[jax.ShapeDtypeStruct((t, d), F32), jax.ShapeDtypeStruct((t, n), BF16),
                   jax.ShapeDtypeStruct((t, f_width), F32)],
        compiler_params=_cparams(("parallel",)),
        name="in_proj",
    )(x2, g.reshape(1, d), b.reshape(1, d), w_bf)


def _mixer_kernel(q_ref, f_ref, i_ref, g_ref, lbl_ref, gn_ref, u_ref, v_ref, vg_ref, vb_ref, ws_ref,
                  bs_ref, og_ref, o_ref, ogm_ref, st_ref, *, n_chunks):
    @pl.when(pl.program_id(1) == 0)
    def _():
        st_ref[...] = jnp.zeros_like(st_ref)

    lg = lbl_ref[...]
    ex = jnp.exp(lg - jnp.max(lg, axis=0, keepdims=True))
    lb_all = ex[0:1, :] / jnp.sum(ex, axis=0, keepdims=True)
    gn_all = gn_ref[...]

    c = CHUNK
    heads = st_ref.shape[0]
    row = lax.broadcasted_iota(jnp.int32, (c, c), 0)
    col = lax.broadcasted_iota(jnp.int32, (c, c), 1)
    tril = (row >= col).astype(BF16)
    sub_row = lax.broadcasted_iota(jnp.int32, (SUB, HEAD_DIM), 0)
    ones = jnp.ones((HEAD_DIM, HEAD_DIM), BF16)
    neg_inf = jnp.float32(-jnp.inf)

    def intra_pairwise(q, k, v, v_bf, b, o_inter):
        outs = []
        for blk in range(c // SUB):
            lo = blk * SUB
            b_i = b[lo:lo + SUB, :]
            q_i = q[lo:lo + SUB, :]
            k_i = k[lo:lo + SUB, :]
            v_i = v[lo:lo + SUB, :]
            prods = []
            for j in range(SUB):
                diff = jnp.where(sub_row >= j, b_i - b_i[j:j + 1, :], neg_inf)
                prods.append(q_i * k_i[j:j + 1, :] * jnp.exp(diff))
            p_all = jnp.concatenate(prods, axis=0).astype(BF16)
            r_all = _dot(p_all, ones)
            o_blk = o_inter[lo:lo + SUB, :]
            for j in range(SUB):
                o_blk = o_blk + r_all[j * SUB:(j + 1) * SUB, :] * v_i[j:j + 1, :]
            if blk > 0:
                b_ref = b[lo - 1:lo, :]
                q_hat = (q_i * jnp.exp(b_i - b_ref)).astype(BF16)
                k_hat = (k[0:lo, :] * jnp.exp(b_ref - b[0:lo, :])).astype(BF16)
                scores = _dot_nt(q_hat, k_hat)
                o_blk = o_blk + _dot(scores.astype(BF16), v_bf[0:lo, :])
            outs.append(o_blk)
        return jnp.concatenate(outs, axis=0)

    def one_chunk(r0, factored):
        rows = pl.ds(r0, c)
        hs = [slice(h * HEAD_DIM, (h + 1) * HEAD_DIM) for h in range(heads)]
        q = _silu(q_ref[rows, :].astype(F32))
        f = lb_all + (1.0 - lb_all) * jax.nn.sigmoid(f_ref[rows, :])
        v_bf = i_ref[rows, :]
        v = v_bf.astype(F32)
        lf = jnp.log(f)
        k = 1.0 - f
        lf_hi, lf_lo = _split_bf16(lf)
        b2 = _dot(tril, jnp.concatenate([lf_hi, lf_lo], axis=1))
        width = lf.shape[1]
        b = b2[:, :width] + b2[:, width:]
        b_last = b[c - 1:c, :]
        q_dec = (q * jnp.exp(b)).astype(BF16)
        k_dec = (k * jnp.exp(b_last - b)).astype(BF16)
        decay = jnp.exp(b_last)
        st = [st_ref[h] for h in range(heads)]
        st_bf = [s.astype(BF16) for s in st]
        if factored:
            k_grow = (k * jnp.exp(-b)).astype(BF16)
            both = [_dot_nt(q_dec[:, hs[h]], jnp.concatenate([st_bf[h], k_grow[:, hs[h]]], axis=0))
                    for h in range(heads)]
            scores = [jnp.where(row >= col, both[h][:, HEAD_DIM:], 0.0).astype(BF16)
                      for h in range(heads)]
            outs = [both[h][:, :HEAD_DIM] + _dot(scores[h], v_bf[:, hs[h]]) for h in range(heads)]
        else:
            outs = [intra_pairwise(q[:, hs[h]], k[:, hs[h]], v[:, hs[h]], v_bf[:, hs[h]],
                                   b[:, hs[h]], _dot_nt(q_dec[:, hs[h]], st_bf[h]))
                    for h in range(heads)]
        for h in range(heads):
            st_ref[h] = st[h] * decay[:, hs[h]] + _dot_tn(v_bf[:, hs[h]], k_dec[:, hs[h]])
        inv = [lax.rsqrt(jnp.mean(o * o, axis=-1, keepdims=True) + RMS_EPS) for o in outs]
        o = jnp.concatenate([outs[h] * inv[h] for h in range(heads)], axis=1)
        o_ref[rows, :] = (o * gn_all * _silu(g_ref[rows, :].astype(F32))).astype(o_ref.dtype)

    groups = ws_ref.shape[0]
    gdim = u_ref.shape[-1] // groups
    gm_row = lax.broadcasted_iota(jnp.int32, (GM_BLOCK, GM_BLOCK), 0) // CHUNK
    gm_col = lax.broadcasted_iota(jnp.int32, (GM_BLOCK, GM_BLOCK), 1) // CHUNK
    ws = [jnp.where(gm_row >= gm_col, ws_ref[g], 0.0).astype(BF16) for g in range(groups)]

    def gmlp_block(r0):
        rows = pl.ds(r0, GM_BLOCK)
        u = _gelu(u_ref[rows, :].astype(F32))
        v = _layer_norm(_gelu(v_ref[rows, :].astype(F32)), vg_ref[...], vb_ref[...])
        v_bf = v.astype(BF16)
        mixed = jnp.concatenate(
            [_dot(ws[g], v_bf[:, g * gdim:(g + 1) * gdim]) for g in range(groups)], axis=1)
        y = u * (mixed + bs_ref[...])
        inv = lax.rsqrt(jnp.mean(y * y, axis=-1, keepdims=True) + RMS_EPS)
        ogm_ref[rows, :] = (y * inv * og_ref[...]).astype(ogm_ref.dtype)

    def run_chunks(factored):
        per_trip = 4 if (factored and n_chunks % 4 == 0) else 1
        trip_rows = per_trip * c
        inline_gmlp = trip_rows % GM_BLOCK == 0

        def chunk_body(ci, carry):
            for u in range(per_trip):
                one_chunk(pl.multiple_of((ci * per_trip + u) * c, c), factored)
            if inline_gmlp:
                for n in range(trip_rows // GM_BLOCK):
                    gmlp_block(pl.multiple_of(ci * trip_rows + n * GM_BLOCK, GM_BLOCK))
            return carry

        lax.fori_loop(0, n_chunks // per_trip, chunk_body, 0)
        if not inline_gmlp:
            for n in range(n_chunks * c // GM_BLOCK):
                gmlp_block(n * GM_BLOCK)

    bounded = jnp.min(lb_all) >= FACTORED_MIN_LB
    pl.when(bounded)(lambda: run_chunks(True))
    pl.when(jnp.logical_not(bounded))(lambda: run_chunks(False))


def _mixer(proj3, f_pre3, lb_logits, norm_g, v_norm_g, v_norm_b, w_s, b_s, out_norm_g, sb):
    bsz, s, _ = proj3.shape
    hg_width = norm_g.shape[0]
    heads = hg_width // HEAD_DIM
    gw = v_norm_g.shape[0]
    assert gw == hg_width and sb % GM_BLOCK == 0
    groups = w_s.shape[0]
    bias_full = jnp.repeat(b_s.T, gw // groups, axis=1)

    def sect(k):
        return pl.BlockSpec((None, sb, hg_width), lambda b, j, k=k: (b, j, k))

    row_vec = pl.BlockSpec((1, hg_width), lambda b, j: (0, 0))
    return pl.pallas_call(
        functools.partial(_mixer_kernel, n_chunks=sb // CHUNK),
        grid=(bsz, s // sb),
        in_specs=[sect(0), sect(0), sect(2), sect(3),
                  pl.BlockSpec((lb_logits.shape[0], hg_width), lambda b, j: (0, 0)),
                  row_vec, sect(4), sect(5), row_vec, row_vec,
                  pl.BlockSpec((groups, GM_BLOCK, GM_BLOCK), lambda b, j: (0, 0, 0)),
                  pl.BlockSpec((GM_BLOCK, gw), lambda b, j: (0, 0)),
                  row_vec],
        out_specs=[pl.BlockSpec((None, sb, hg_width), lambda b, j: (b, j, 0)),
                   pl.BlockSpec((None, sb, gw), lambda b, j: (b, j, 0))],
        out_shape=[jax.ShapeDtypeStruct((bsz, s, hg_width), BF16),
                   jax.ShapeDtypeStruct((bsz, s, gw), BF16)],
        scratch_shapes=[pltpu.VMEM((heads, HEAD_DIM, HEAD_DIM), F32)],
        compiler_params=_cparams(("parallel", "arbitrary")),
        name="mixer",
    )(proj3, f_pre3, proj3, proj3, lb_logits, norm_g.reshape(1, hg_width), proj3, proj3,
      v_norm_g.reshape(1, gw), v_norm_b.reshape(1, gw), w_s, bias_full, out_norm_g.reshape(1, gw))


def _out_proj_kernel(yh_ref, yg_ref, h0_ref, wa_ref, wb_ref, g_ref, b_ref, wr_ref, rb_ref,
                     wsg_ref, wsd_ref, base_ref, hp_ref, eidx_ref, gate_ref, rank_ref, cnt_ref,
                     carry_ref, *, alpha):
    @pl.when(pl.program_id(0) == 0)
    def _():
        carry_ref[...] = jnp.zeros_like(carry_ref)

    mix = _dot(yh_ref[...], wa_ref[...]) + _dot(yg_ref[...], wb_ref[...])
    h1 = _layer_norm(alpha * h0_ref[...] + mix, g_ref[...], b_ref[...])
    hp_ref[...] = _pack_halves(h1)
    w_hi, w_lo = _split_bf16(wr_ref[...])
    h_hi, h_lo = _split_bf16(h1)
    logits = _dot_nt(w_hi, h_hi) + (_dot_nt(w_hi, h_lo) + _dot_nt(w_lo, h_hi))
    gu = _dot(h_hi, wsg_ref[...])
    half = gu.shape[1] // 2
    hid = (_silu(gu[:, :half]) * gu[:, half:]).astype(BF16)
    base_ref[...] = alpha * h1 + _dot(hid, wsd_ref[...])
    eidx, gates, rank, total = _route_tile(logits, rb_ref[...], carry_ref[...])
    eidx_ref[...] = eidx
    gate_ref[...] = gates
    rank_ref[...] = rank
    carry_ref[...] = total
    cnt_ref[...] = total.astype(jnp.int32)


def _out_proj(y_hg, y_gm, h0, w_out_bf, g, b, w_router_t, router_bias, ws_gu_bf, ws_down_bf, alpha, tm):
    t, d = h0.shape
    sg = ws_gu_bf.shape[1]
    sd = ws_down_bf.shape[0]
    hw = y_hg.shape[1]
    gw = y_gm.shape[1]
    e = w_router_t.shape[0]
    return pl.pallas_call(
        functools.partial(_out_proj_kernel, alpha=alpha),
        grid=(t // tm,),
        in_specs=[pl.BlockSpec((tm, hw), lambda i: (i, 0)),
                  pl.BlockSpec((tm, gw), lambda i: (i, 0)),
                  pl.BlockSpec((tm, d), lambda i: (i, 0)),
                  pl.BlockSpec((hw, d), lambda i: (0, 0)),
                  pl.BlockSpec((gw, d), lambda i: (0, 0)),
                  pl.BlockSpec((1, d), lambda i: (0, 0)),
                  pl.BlockSpec((1, d), lambda i: (0, 0)),
                  pl.BlockSpec((e, d), lambda i: (0, 0)),
                  pl.BlockSpec((e, 1), lambda i: (0, 0)),
                  pl.BlockSpec((d, sg), lambda i: (0, 0)),
                  pl.BlockSpec((sd, d), lambda i: (0, 0))],
        out_specs=[pl.BlockSpec((tm, d), lambda i: (i, 0)),
                   pl.BlockSpec((tm, d // 2), lambda i: (i, 0)),
                   pl.BlockSpec((TOP_K, tm), lambda i: (0, i)),
                   pl.BlockSpec((TOP_K, tm), lambda i: (0, i)),
                   pl.BlockSpec((TOP_K, tm), lambda i: (0, i)),
                   pl.BlockSpec((e, 1), lambda i: (0, 0))],
        out_shape=[jax.ShapeDtypeStruct((t, d), F32), jax.ShapeDtypeStruct((t, d // 2), jnp.uint32),
                   jax.ShapeDtypeStruct((TOP_K, t), jnp.int32),
                   jax.ShapeDtypeStruct((TOP_K, t), F32),
                   jax.ShapeDtypeStruct((TOP_K, t), jnp.int32),
                   jax.ShapeDtypeStruct((e, 1), jnp.int32)],
        scratch_shapes=[pltpu.VMEM((e, 1), F32)],
        compiler_params=_cparams(("arbitrary",)),
        name="out_proj",
    )(y_hg, y_gm, h0, w_out_bf[:hw], w_out_bf[hw:], g.reshape(1, d), b.reshape(1, d), w_router_t,
      router_bias.astype(F32).reshape(e, 1), ws_gu_bf, ws_down_bf)


def _route_tile(logits, bias, carry):
    n_exp, tt = logits.shape
    per_group = n_exp // N_GROUPS
    neg_inf = jnp.float32(-jnp.inf)
    scores = jax.nn.sigmoid(logits)
    biased = scores + bias

    gio = lax.broadcasted_iota(jnp.int32, (per_group, tt), 0)
    blocks, gs_rows = [], []
    for g in range(N_GROUPS):
        blk = biased[g * per_group:(g + 1) * per_group, :]
        m1 = jnp.max(blk, axis=0, keepdims=True)
        first = jnp.min(jnp.where(blk == m1, gio, per_group), axis=0, keepdims=True)
        m2 = jnp.max(jnp.where(gio == first, neg_inf, blk), axis=0, keepdims=True)
        blocks.append(blk)
        gs_rows.append(m1 + m2)
    gs = jnp.concatenate(gs_rows, axis=0)
    gidx = lax.broadcasted_iota(jnp.int32, (N_GROUPS, tt), 0)
    beaten = jnp.zeros((N_GROUPS, tt), jnp.int32)
    for g in range(N_GROUPS):
        r = gs_rows[g]
        beaten = beaten + jnp.where((r > gs) | ((r == gs) & (g < gidx)), 1, 0)
    keep = beaten < TOPK_GROUPS
    cand = jnp.concatenate(
        [jnp.where(keep[g:g + 1, :], blocks[g], neg_inf) for g in range(N_GROUPS)], axis=0)

    eio = lax.broadcasted_iota(jnp.int32, (n_exp, tt), 0)
    idx_rows, w_rows = [], []
    chosen = jnp.zeros((n_exp, tt), F32)
    for _ in range(TOP_K):
        m = jnp.max(cand, axis=0, keepdims=True)
        idx = jnp.min(jnp.where(cand == m, eio, n_exp), axis=0, keepdims=True)
        sel = eio == idx
        w_rows.append(jnp.sum(jnp.where(sel, scores, 0.0), axis=0, keepdims=True))
        idx_rows.append(idx)
        cand = jnp.where(sel, neg_inf, cand)
        chosen = jnp.where(sel, 1.0, chosen)
    eidx = jnp.concatenate(idx_rows, axis=0)
    w = jnp.concatenate(w_rows, axis=0)
    gates = w / jnp.sum(w, axis=0, keepdims=True) * ROUTED_SCALE

    tr = lax.broadcasted_iota(jnp.int32, (tt, tt), 0)
    tc = lax.broadcasted_iota(jnp.int32, (tt, tt), 1)
    before = (tr < tc).astype(BF16)
    cum = _dot(chosen.astype(BF16), before) + carry
    rank = jnp.concatenate(
        [jnp.sum(jnp.where(eio == idx_rows[k], cum, 0.0), axis=0, keepdims=True)
         for k in range(TOP_K)], axis=0).astype(jnp.int32)
    total = carry + jnp.sum(chosen, axis=1, keepdims=True)
    return eidx, gates, rank, total


def _chunked(table, chunk):
    k, t = table.shape
    return table.reshape(k, t // chunk, chunk).transpose(1, 0, 2)


def _sc_workers():
    info = plsc.get_sparse_core_info()
    return info.num_cores, info.num_cores * info.num_subcores


def _slot_tables(eidx, rank, pad_start, n_rows, chunk):
    k_top, t = eidx.shape
    n_exp = pad_start.shape[0]
    n_cores, n_workers = _sc_workers()
    n_chunks = t // chunk
    per_worker = n_chunks // n_workers
    lanes = plsc.get_sparse_core_info().num_lanes
    mesh = plsc.VectorSubcoreMesh(core_axis_name="c", subcore_axis_name="s")

    @functools.partial(
        pl.kernel, mesh=mesh,
        out_type=(jax.ShapeDtypeStruct((n_rows, TOKEN_ROW), jnp.int32),
                  jax.ShapeDtypeStruct((n_chunks, k_top, chunk), jnp.int32)),
        scratch_types=[pltpu.VMEM((n_exp,), jnp.int32),
                       pltpu.VMEM((k_top, chunk), jnp.int32),
                       pltpu.VMEM((k_top, chunk), jnp.int32),
                       pltpu.VMEM((k_top, chunk), jnp.int32),
                       pltpu.VMEM((chunk, TOKEN_ROW), jnp.int32),
                       pltpu.SemaphoreType.DMA],
        compiler_params=pltpu.CompilerParams(needs_layout_passes=False),
        name="slot_tables",
    )
    def build(e_hbm, r_hbm, ps_hbm, tok_hbm, d_hbm, ps_v, e_v, r_v, idx_v, rows_v, sem):
        wid = lax.axis_index("s") * n_cores + lax.axis_index("c")
        pltpu.sync_copy(ps_hbm, ps_v)

        @pl.loop(0, per_worker)
        def _(j):
            c = wid * per_worker + j
            pltpu.sync_copy(e_hbm.at[c], e_v)
            pltpu.sync_copy(r_hbm.at[c], r_v)
            for k in range(k_top):
                for s in range(chunk // lanes):
                    sl = pl.ds(s * lanes, lanes)
                    idx_v[k, sl] = plsc.load_gather(ps_v, [e_v[k, sl]]) + r_v[k, sl]
            for i in range(chunk):
                rows_v[i, pl.ds(0, lanes)] = jnp.full((lanes,), c * chunk + i, jnp.int32)
            pltpu.sync_copy(idx_v, d_hbm.at[c])
            copies = [pltpu.async_copy(rows_v, tok_hbm.at[idx_v.at[k]], sem) for k in range(k_top)]
            for cp in copies:
                cp.wait()

    return build(_chunked(eidx, chunk), _chunked(rank, chunk), pad_start)


def _gather_back(dest_chunks, ys, sub):
    n_chunks, k_top, chunk = dest_chunks.shape
    t = n_chunks * chunk
    dh = ys.shape[1]
    n_cores, n_workers = _sc_workers()
    per_worker = n_chunks // n_workers
    mesh = plsc.VectorSubcoreMesh(core_axis_name="c", subcore_axis_name="s")

    @functools.partial(
        pl.kernel, mesh=mesh,
        out_type=jax.ShapeDtypeStruct((k_top, t, dh), ys.dtype),
        scratch_types=[pltpu.VMEM((k_top, chunk), jnp.int32),
                       pltpu.VMEM((k_top, sub, dh), ys.dtype),
                       pltpu.SemaphoreType.DMA],
        name="gather_back",
    )
    def gather_rows(ys_hbm, dest_hbm, o_hbm, idx_v, rows_v, sem):
        wid = lax.axis_index("s") * n_cores + lax.axis_index("c")

        @pl.loop(0, per_worker)
        def _(j):
            c = wid * per_worker + j
            pltpu.sync_copy(dest_hbm.at[c], idx_v)
            for q in range(chunk // sub):
                copies = [pltpu.async_copy(ys_hbm.at[idx_v.at[k, pl.ds(q * sub, sub)]], rows_v.at[k], sem)
                          for k in range(k_top)]
                for cp in copies:
                    cp.wait()
                for k in range(k_top):
                    pltpu.sync_copy(rows_v.at[k], o_hbm.at[k, pl.ds(c * chunk + q * sub, sub)])

    return gather_rows(ys, dest_chunks)


def _experts_kernel(bs_ref, cnt_ref, wgu_hbm, wd_hbm, h_hbm, tok_hbm, ys_hbm, wgu_buf, wd_buf, gu_bf,
                    d_bf, h_v, xg, ybuf, tokbuf, wsem, hsem, tsem, ysem):
    e = pl.program_id(0)
    n_exp = pl.num_programs(0)
    n_slots, mb, _ = ybuf.shape
    w_slots = wgu_buf.shape[0]
    t_slots = tokbuf.shape[0]
    n_tok = h_v.shape[0]
    n_blocks = bs_ref[n_exp]
    b0 = bs_ref[e]
    b1 = bs_ref[e + 1]
    count = cnt_ref[e]

    def w_copies(ex):
        slot = lax.rem(ex, w_slots)
        return (pltpu.make_async_copy(wgu_hbm.at[ex], wgu_buf.at[slot], wsem.at[slot]),
                pltpu.make_async_copy(wd_hbm.at[ex], wd_buf.at[slot], wsem.at[slot]))

    def tok_copy(b):
        src = jnp.minimum(b, n_blocks - 1)
        slot = lax.rem(b, t_slots)
        return pltpu.make_async_copy(tok_hbm.at[pl.ds(src, 1)], tokbuf.at[slot], tsem.at[slot])

    def y_copy(b):
        slot = lax.rem(b, n_slots)
        return pltpu.make_async_copy(ybuf.at[slot],
                                     ys_hbm.at[pl.ds(pl.multiple_of(b * mb, mb), mb)], ysem.at[slot])

    def gather_rows(b):
        ids = lax.rem(b, t_slots)
        dst = lax.rem(b, 2)
        for i in range(mb):
            t = jnp.clip(tokbuf[ids, 0, i], 0, n_tok - 1)
            xg[dst, pl.ds(i, 1), :] = h_v[pl.ds(t, 1), :]

    @pl.when(e == 0)
    def _():
        resident = pltpu.make_async_copy(h_hbm, h_v, hsem)
        resident.start()
        for i in range(w_slots - 1):
            @pl.when(i < n_exp)
            def _():
                for cp in w_copies(i):
                    cp.start()
        for i in range(t_slots - 1):
            tok_copy(i).start()
        resident.wait()
        tok_copy(0).wait()
        gather_rows(0)

    @pl.when(e + w_slots - 1 < n_exp)
    def _():
        for cp in w_copies(e + w_slots - 1):
            cp.start()

    for cp in w_copies(e):
        cp.wait()
    w_slot = lax.rem(e, w_slots)

    @pl.when(b1 > b0)
    def _():
        gu_bf[...] = wgu_buf[w_slot].astype(BF16)
        d_bf[...] = wd_buf[w_slot].astype(BF16)

    kh = gu_bf.shape[0] // 2
    half = gu_bf.shape[1] // 2

    def block(b, carry):
        slot = lax.rem(b, n_slots)
        tok_copy(b + t_slots - 1).start()
        tok_copy(b + 1).wait()

        @pl.when(b >= n_slots)
        def _():
            y_copy(b - n_slots).wait()

        rows = lax.broadcasted_iota(jnp.int32, (mb, xg.shape[2]), 0)
        valid = count - (b - b0) * mb
        words = jnp.where(rows < valid, xg[lax.rem(b, 2)], jnp.uint32(0))
        xa, xb = _unpack_halves(words)
        gu = _dot(xa.astype(BF16), gu_bf[0:kh, :]) + _dot(xb.astype(BF16), gu_bf[kh:, :])
        hid = _silu(gu[:, :half]) * gu[:, half:]
        ybuf[slot] = _pack_halves(_dot(hid.astype(BF16), d_bf[...]))
        gather_rows(b + 1)
        y_copy(b).start(priority=RING_DMA_PRIORITY)
        return carry

    lax.fori_loop(b0, b1, block, 0)

    @pl.when(e == n_exp - 1)
    def _():
        for i in range(1, n_slots + 1):
            @pl.when(n_blocks >= i)
            def _():
                y_copy(n_blocks - i).wait()
        for i in range(1, t_slots - 1):
            tok_copy(n_blocks + i).wait()


def _experts(block_start, counts, h1_packed, tok_ids, w_gu, w_down):
    t, dh = h1_packed.shape
    n_rows = tok_ids.shape[0] * tok_ids.shape[1]
    n_exp, d, gu_w = w_gu.shape
    ed = w_down.shape[1]
    return pl.pallas_call(
        _experts_kernel,
        grid_spec=pltpu.PrefetchScalarGridSpec(
            num_scalar_prefetch=2,
            grid=(n_exp,),
            in_specs=[pl.BlockSpec(memory_space=pl.ANY)] * 4,
            out_specs=pl.BlockSpec(memory_space=pl.ANY),
            scratch_shapes=[pltpu.VMEM((WEIGHT_SLOTS, d, gu_w), w_gu.dtype),
                            pltpu.VMEM((WEIGHT_SLOTS, ed, d), w_down.dtype),
                            pltpu.VMEM((d, gu_w), BF16), pltpu.VMEM((ed, d), BF16),
                            pltpu.VMEM((t, dh), h1_packed.dtype),
                            pltpu.VMEM((2, MOE_BLOCK, dh), jnp.uint32),
                            pltpu.VMEM((RING_SLOTS, MOE_BLOCK, dh), jnp.uint32),
                            pltpu.SMEM((TOKEN_SLOTS, 1, MOE_BLOCK), jnp.int32),
                            pltpu.SemaphoreType.DMA((WEIGHT_SLOTS,)),
                            pltpu.SemaphoreType.DMA,
                            pltpu.SemaphoreType.DMA((TOKEN_SLOTS,)),
                            pltpu.SemaphoreType.DMA((RING_SLOTS,))]),
        out_shape=jax.ShapeDtypeStruct((n_rows, dh), jnp.uint32),
        compiler_params=pltpu.CompilerParams(dimension_semantics=("arbitrary",),
                                             vmem_limit_bytes=EXPERTS_VMEM_LIMIT),
        name="experts",
    )(block_start, counts, w_gu, w_down, h1_packed, tok_ids)


def _combine_kernel(base_ref, w_ref, g_ref, b_ref, yg_ref, *rest):
    o_ref = rest[-1]
    acc = base_ref[...]
    dh = acc.shape[1] // 2
    acc_a = acc[:, :dh]
    acc_b = acc[:, dh:]
    w = w_ref[...]
    for k in range(TOP_K):
        ya, yb = _unpack_halves(yg_ref[k])
        acc_a = acc_a + ya * w[:, k:k + 1]
        acc_b = acc_b + yb * w[:, k:k + 1]
    o_ref[...] = _layer_norm(jnp.concatenate([acc_a, acc_b], axis=1), g_ref[...], b_ref[...])


def _combine(base, w_tk, g, b, yg, prev, part, tc):
    t, d = base.shape
    tiles = yg.shape[1] // tc
    off = part * tiles
    in_specs = [pl.BlockSpec((tc, d), lambda i: (i + off, 0)),
                pl.BlockSpec((tc, TOP_K), lambda i: (i + off, 0)),
                pl.BlockSpec((1, d), lambda i: (0, 0)),
                pl.BlockSpec((1, d), lambda i: (0, 0)),
                pl.BlockSpec((TOP_K, tc, yg.shape[2]), lambda i: (0, i, 0))]
    args = [base, w_tk, g.reshape(1, d), b.reshape(1, d), yg]
    aliases = {}
    if prev is not None:
        in_specs.append(pl.BlockSpec(memory_space=pl.ANY))
        args.append(prev)
        aliases = {len(args) - 1: 0}
    return pl.pallas_call(
        _combine_kernel,
        grid=(tiles,),
        in_specs=in_specs,
        out_specs=pl.BlockSpec((tc, d), lambda i: (i + off, 0)),
        out_shape=jax.ShapeDtypeStruct((t, d), F32),
        input_output_aliases=aliases,
        compiler_params=_cparams(("parallel",)),
        name="combine",
    )(*args)


def _moe(base, h1_packed, eidx, w_kt, rank, counts, w_exp_gu, w_exp_down, ln_g, ln_b):
    t, d = base.shape
    n_exp = counts.shape[0]

    counts = counts.reshape(n_exp)
    blocks_per_expert = (counts + MOE_BLOCK - 1) // MOE_BLOCK
    block_start = jnp.concatenate(
        [jnp.zeros((1,), jnp.int32), jnp.cumsum(blocks_per_expert).astype(jnp.int32)])
    pad_start = block_start[:n_exp] * MOE_BLOCK
    n_blocks = -(-(t * TOP_K + n_exp * (MOE_BLOCK - 1)) // MOE_BLOCK)

    tok_rows, dest_chunks = _slot_tables(eidx, rank, pad_start, n_blocks * MOE_BLOCK, chunk=SLOT_CHUNK)
    tok_ids = tok_rows[:, 0].reshape(n_blocks, MOE_BLOCK)
    ys = _experts(block_start, counts, h1_packed, tok_ids, w_exp_gu, w_exp_down)
    per_part = dest_chunks.shape[0] // COMBINE_PARTS
    w_tk = w_kt.T
    out = None
    for part in range(COMBINE_PARTS):
        yg = _gather_back(dest_chunks[part * per_part:(part + 1) * per_part], ys, sub=GATHER_CHUNK)
        out = _combine(base, w_tk, ln_g, ln_b, yg, out, part, tc=min(256, t))
    return out


def kernel(x, ln_in_g, ln_in_b, w_in, hg_lb_logits, hg_norm_g, gm_v_norm_g, gm_v_norm_b, gm_w_s, gm_b_s, gm_out_norm_g, w_out, ln1_g, ln1_b, w_router, router_bias, w_exp_gu, w_exp_down, w_shared_gu, w_shared_down, ln2_g, ln2_b):
    bsz, s, d = x.shape
    depth = w_in.shape[0]
    assert depth == 1, "the lower-bound table row used in the hgrn kernel assumes one layer"
    alpha = (2.0 * depth) ** 0.25
    t = bsz * s
    hg_width = hg_norm_g.shape[1]
    gm_width = gm_v_norm_g.shape[1]
    tm = min(512, t)

    x2 = x.reshape(t, d)
    h0, proj, f_pre = _in_proj(x2, ln_in_g, ln_in_b, w_in[0].astype(BF16), hg_width, hg_width, tm)
    proj3 = proj.reshape(bsz, s, proj.shape[1])
    y_hg, y_gm = _mixer(proj3, f_pre.reshape(bsz, s, hg_width), hg_lb_logits.astype(F32),
                        hg_norm_g[0], gm_v_norm_g[0], gm_v_norm_b[0], gm_w_s[0], gm_b_s[0],
                        gm_out_norm_g[0], sb=min(512, s))
    base, h1_packed, eidx, gates, rank, counts = _out_proj(
        y_hg.reshape(t, hg_width), y_gm.reshape(t, gm_width), h0, w_out[0].astype(BF16),
        ln1_g[0], ln1_b[0], w_router[0].T, router_bias[0], w_shared_gu[0].astype(BF16),
        w_shared_down[0].astype(BF16), alpha, tm)
    out = _moe(base, h1_packed, eidx, gates, rank, counts, w_exp_gu[0], w_exp_down[0],
               ln2_g[0], ln2_b[0])
    return out.reshape(bsz, s, d)
```

```python
import functools

import jax
import jax.numpy as jnp
from jax import lax
from jax.experimental import pallas as pl
from jax.experimental.pallas import tpu as pltpu
from jax.experimental.pallas import tpu_sc as plsc

F32 = jnp.float32
BF16 = jnp.bfloat16

LN_EPS = 1e-5
RMS_EPS = 1e-6
CHUNK = 64
SUB = 16
MAX_FACTORED_EXPONENT = 80.0
FACTORED_MIN_LB = float(2.718281828459045 ** (-MAX_FACTORED_EXPONENT / CHUNK))
HEAD_DIM = 128
GM_BLOCK = 128
TOP_K = 8
N_GROUPS = 8
TOPK_GROUPS = 4
ROUTED_SCALE = 2.5
MOE_BLOCK = 128
WEIGHT_SLOTS = 3
RING_SLOTS = 8
RING_DMA_PRIORITY = 1
BLOCKS_PER_STEP = 2
GATHER_AHEAD = 2
TOKEN_SLOTS = 8
TOKEN_ROW = 128
SLOT_CHUNK = 64
COMBINE_PARTS = 2
GATHER_CHUNK = 16
VMEM_LIMIT = 48 * 1024 * 1024
EXPERTS_VMEM_LIMIT = 56 * 1024 * 1024


def _cparams(sem):
    return pltpu.CompilerParams(dimension_semantics=sem, vmem_limit_bytes=VMEM_LIMIT)


def _layer_norm(x, g, b):
    mu = jnp.mean(x, axis=-1, keepdims=True)
    xc = x - mu
    var = jnp.mean(xc * xc, axis=-1, keepdims=True)
    return xc * lax.rsqrt(var + LN_EPS) * g + b


def _silu(x):
    return x * jax.nn.sigmoid(x)


def _gelu(x):
    return 0.5 * x * (1.0 + lax.erf(x * (2.0 ** -0.5)))


def _dot(a, b):
    return jnp.dot(a, b, preferred_element_type=F32)


def _dot_nt(a, b, precision=None):
    return lax.dot_general(a, b, (((1,), (1,)), ((), ())), preferred_element_type=F32,
                           precision=precision)


def _dot_tn(a, b):
    return lax.dot_general(a, b, (((0,), (0,)), ((), ())), preferred_element_type=F32)


def _split_bf16(x):
    hi = x.astype(BF16)
    return hi, (x - hi.astype(F32)).astype(BF16)


def _pack_halves(x):
    n = x.shape[1] // 2
    hi = lax.bitcast_convert_type(x[:, :n].astype(BF16).astype(F32), jnp.uint32)
    lo = lax.bitcast_convert_type(x[:, n:].astype(BF16).astype(F32), jnp.uint32)
    return hi | (lo >> 16)


def _unpack_halves(w):
    a = lax.bitcast_convert_type(w & jnp.uint32(0xFFFF0000), F32)
    b = lax.bitcast_convert_type(w << 16, F32)
    return a, b


def _in_proj_kernel(x_ref, g_ref, b_ref, w_ref, h_ref, p_ref, f_ref, *, f_lo):
    h = _layer_norm(x_ref[...], g_ref[...], b_ref[...])
    h_ref[...] = h
    p = _dot(h.astype(BF16), w_ref[...])
    p_ref[...] = p.astype(BF16)
    f_ref[...] = p[:, f_lo:f_lo + f_ref.shape[1]]


def _in_proj(x2, g, b, w_bf, f_lo, f_width, tm):
    t, d = x2.shape
    n = w_bf.shape[1]
    return pl.pallas_call(
        functools.partial(_in_proj_kernel, f_lo=f_lo),
        grid=(t // tm,),
        in_specs=[pl.BlockSpec((tm, d), lambda i: (i, 0)),
                  pl.BlockSpec((1, d), lambda i: (0, 0)),
                  pl.BlockSpec((1, d), lambda i: (0, 0)),
                  pl.BlockSpec((d, n), lambda i: (0, 0))],
        out_specs=[pl.BlockSpec((tm, d), lambda i: (i, 0)),
                   pl.BlockSpec((tm, n), lambda i: (i, 0)),
                   pl.BlockSpec((tm, f_width), lambda i: (i, 0))],
        out_shape=[jax.ShapeDtypeStruct((t, d), F32), jax.ShapeDtypeStruct((t, n), BF16),
                   jax.ShapeDtypeStruct((t, f_width), F32)],
        compiler_params=_cparams(("parallel",)),
        name="in_proj",
    )(x2, g.reshape(1, d), b.reshape(1, d), w_bf)


def _mixer_kernel(q_ref, f_ref, i_ref, g_ref, lbl_ref, gn_ref, u_ref, v_ref, vg_ref, vb_ref, ws_ref,
                  bs_ref, og_ref, o_ref, ogm_ref, st_ref, *, n_chunks):
    @pl.when(pl.program_id(1) == 0)
    def _():
        st_ref[...] = jnp.zeros_like(st_ref)

    lg = lbl_ref[...]
    ex = jnp.exp(lg - jnp.max(lg, axis=0, keepdims=True))
    lb_all = ex[0:1, :] / jnp.sum(ex, axis=0, keepdims=True)
    gn_all = gn_ref[...]

    c = CHUNK
    heads = st_ref.shape[0]
    row = lax.broadcasted_iota(jnp.int32, (c, c), 0)
    col = lax.broadcasted_iota(jnp.int32, (c, c), 1)
    tril = (row >= col).astype(BF16)
    sub_row = lax.broadcasted_iota(jnp.int32, (SUB, HEAD_DIM), 0)
    ones = jnp.ones((HEAD_DIM, HEAD_DIM), BF16)
    neg_inf = jnp.float32(-jnp.inf)

    def intra_pairwise(q, k, v, v_bf, b, o_inter):
        outs = []
        for blk in range(c // SUB):
            lo = blk * SUB
            b_i = b[lo:lo + SUB, :]
            q_i = q[lo:lo + SUB, :]
            k_i = k[lo:lo + SUB, :]
            v_i = v[lo:lo + SUB, :]
            prods = []
            for j in range(SUB):
                diff = jnp.where(sub_row >= j, b_i - b_i[j:j + 1, :], neg_inf)
                prods.append(q_i * k_i[j:j + 1, :] * jnp.exp(diff))
            p_all = jnp.concatenate(prods, axis=0).astype(BF16)
            r_all = _dot(p_all, ones)
            o_blk = o_inter[lo:lo + SUB, :]
            for j in range(SUB):
                o_blk = o_blk + r_all[j * SUB:(j + 1) * SUB, :] * v_i[j:j + 1, :]
            if blk > 0:
                b_ref = b[lo - 1:lo, :]
                q_hat = (q_i * jnp.exp(b_i - b_ref)).astype(BF16)
                k_hat = (k[0:lo, :] * jnp.exp(b_ref - b[0:lo, :])).astype(BF16)
                scores = _dot_nt(q_hat, k_hat)
                o_blk = o_blk + _dot(scores.astype(BF16), v_bf[0:lo, :])
            outs.append(o_blk)
        return jnp.concatenate(outs, axis=0)

    def one_chunk(r0, factored):
        rows = pl.ds(r0, c)
        hs = [slice(h * HEAD_DIM, (h + 1) * HEAD_DIM) for h in range(heads)]
        q = _silu(q_ref[rows, :].astype(F32))
        f = lb_all + (1.0 - lb_all) * jax.nn.sigmoid(f_ref[rows, :])
        v_bf = i_ref[rows, :]
        v = v_bf.astype(F32)
        lf = jnp.log(f)
        k = 1.0 - f
        lf_hi, lf_lo = _split_bf16(lf)
        b2 = _dot(tril, jnp.concatenate([lf_hi, lf_lo], axis=1))
        width = lf.shape[1]
        b = b2[:, :width] + b2[:, width:]
        b_last = b[c - 1:c, :]
        q_dec = (q * jnp.exp(b)).astype(BF16)
        k_dec = (k * jnp.exp(b_last - b)).astype(BF16)
        decay = jnp.exp(b_last)
        st = [st_ref[h] for h in range(heads)]
        st_bf = [s.astype(BF16) for s in st]
        if factored:
            k_grow = (k * jnp.exp(-b)).astype(BF16)
            both = [_dot_nt(q_dec[:, hs[h]], jnp.concatenate([st_bf[h], k_grow[:, hs[h]]], axis=0))
                    for h in range(heads)]
            scores = [jnp.where(row >= col, both[h][:, HEAD_DIM:], 0.0).astype(BF16)
                      for h in range(heads)]
            outs = [both[h][:, :HEAD_DIM] + _dot(scores[h], v_bf[:, hs[h]]) for h in range(heads)]
        else:
            outs = [intra_pairwise(q[:, hs[h]], k[:, hs[h]], v[:, hs[h]], v_bf[:, hs[h]],
                                   b[:, hs[h]], _dot_nt(q_dec[:, hs[h]], st_bf[h]))
                    for h in range(heads)]
        for h in range(heads):
            st_ref[h] = st[h] * decay[:, hs[h]] + _dot_tn(v_bf[:, hs[h]], k_dec[:, hs[h]])
        inv = [lax.rsqrt(jnp.mean(o * o, axis=-1, keepdims=True) + RMS_EPS) for o in outs]
        o = jnp.concatenate([outs[h] * inv[h] for h in range(heads)], axis=1)
        o_ref[rows, :] = (o * gn_all * _silu(g_ref[rows, :].astype(F32))).astype(o_ref.dtype)

    groups = ws_ref.shape[0]
    gdim = u_ref.shape[-1] // groups
    gm_row = lax.broadcasted_iota(jnp.int32, (GM_BLOCK, GM_BLOCK), 0) // CHUNK
    gm_col = lax.broadcasted_iota(jnp.int32, (GM_BLOCK, GM_BLOCK), 1) // CHUNK
    ws = [jnp.where(gm_row >= gm_col, ws_ref[g], 0.0).astype(BF16) for g in range(groups)]

    def gmlp_block(r0):
        rows = pl.ds(r0, GM_BLOCK)
        u = _gelu(u_ref[rows, :].astype(F32))
        v = _layer_norm(_gelu(v_ref[rows, :].astype(F32)), vg_ref[...], vb_ref[...])
        v_bf = v.astype(BF16)
        mixed = jnp.concatenate(
            [_dot(ws[g], v_bf[:, g * gdim:(g + 1) * gdim]) for g in range(groups)], axis=1)
        y = u * (mixed + bs_ref[...])
        inv = lax.rsqrt(jnp.mean(y * y, axis=-1, keepdims=True) + RMS_EPS)
        ogm_ref[rows, :] = (y * inv * og_ref[...]).astype(ogm_ref.dtype)

    def run_chunks(factored):
        per_trip = 4 if (factored and n_chunks % 4 == 0) else 1
        trip_rows = per_trip * c
        inline_gmlp = trip_rows % GM_BLOCK == 0

        def chunk_body(ci, carry):
            for u in range(per_trip):
                one_chunk(pl.multiple_of((ci * per_trip + u) * c, c), factored)
            if inline_gmlp:
                for n in range(trip_rows // GM_BLOCK):
                    gmlp_block(pl.multiple_of(ci * trip_rows + n * GM_BLOCK, GM_BLOCK))
            return carry

        lax.fori_loop(0, n_chunks // per_trip, chunk_body, 0)
        if not inline_gmlp:
            for n in range(n_chunks * c // GM_BLOCK):
                gmlp_block(n * GM_BLOCK)

    bounded = jnp.min(lb_all) >= FACTORED_MIN_LB
    pl.when(bounded)(lambda: run_chunks(True))
    pl.when(jnp.logical_not(bounded))(lambda: run_chunks(False))


def _mixer(proj3, f_pre3, lb_logits, norm_g, v_norm_g, v_norm_b, w_s, b_s, out_norm_g, sb):
    bsz, s, _ = proj3.shape
    hg_width = norm_g.shape[0]
    heads = hg_width // HEAD_DIM
    gw = v_norm_g.shape[0]
    assert gw == hg_width and sb % GM_BLOCK == 0
    groups = w_s.shape[0]
    bias_full = jnp.repeat(b_s.T, gw // groups, axis=1)

    def sect(k):
        return pl.BlockSpec((None, sb, hg_width), lambda b, j, k=k: (b, j, k))

    row_vec = pl.BlockSpec((1, hg_width), lambda b, j: (0, 0))
    return pl.pallas_call(
        functools.partial(_mixer_kernel, n_chunks=sb // CHUNK),
        grid=(bsz, s // sb),
        in_specs=[sect(0), sect(0), sect(2), sect(3),
                  pl.BlockSpec((lb_logits.shape[0], hg_width), lambda b, j: (0, 0)),
                  row_vec, sect(4), sect(5), row_vec, row_vec,
                  pl.BlockSpec((groups, GM_BLOCK, GM_BLOCK), lambda b, j: (0, 0, 0)),
                  pl.BlockSpec((GM_BLOCK, gw), lambda b, j: (0, 0)),
                  row_vec],
        out_specs=[pl.BlockSpec((None, sb, hg_width), lambda b, j: (b, j, 0)),
                   pl.BlockSpec((None, sb, gw), lambda b, j: (b, j, 0))],
        out_shape=[jax.ShapeDtypeStruct((bsz, s, hg_width), BF16),
                   jax.ShapeDtypeStruct((bsz, s, gw), BF16)],
        scratch_shapes=[pltpu.VMEM((heads, HEAD_DIM, HEAD_DIM), F32)],
        compiler_params=_cparams(("parallel", "arbitrary")),
        name="mixer",
    )(proj3, f_pre3, proj3, proj3, lb_logits, norm_g.reshape(1, hg_width), proj3, proj3,
      v_norm_g.reshape(1, gw), v_norm_b.reshape(1, gw), w_s, bias_full, out_norm_g.reshape(1, gw))


def _out_proj_kernel(yh_ref, yg_ref, h0_ref, wa_ref, wb_ref, g_ref, b_ref, wr_ref, rb_ref,
                     wsg_ref, wsd_ref, base_ref, hp_ref, eidx_ref, gate_ref, rank_ref, cnt_ref,
                     carry_ref, *, alpha):
    @pl.when(pl.program_id(0) == 0)
    def _():
        carry_ref[...] = jnp.zeros_like(carry_ref)

    mix = _dot(yh_ref[...], wa_ref[...]) + _dot(yg_ref[...], wb_ref[...])
    h1 = _layer_norm(alpha * h0_ref[...] + mix, g_ref[...], b_ref[...])
    hp_ref[...] = _pack_halves(h1)
    w_hi, w_lo = _split_bf16(wr_ref[...])
    h_hi, h_lo = _split_bf16(h1)
    logits = _dot_nt(w_hi, h_hi) + (_dot_nt(w_hi, h_lo) + _dot_nt(w_lo, h_hi))
    gu = _dot(h_hi, wsg_ref[...])
    half = gu.shape[1] // 2
    hid = (_silu(gu[:, :half]) * gu[:, half:]).astype(BF16)
    base_ref[...] = alpha * h1 + _dot(hid, wsd_ref[...])
    eidx, gates, rank, total = _route_tile(logits, rb_ref[...], carry_ref[...])
    eidx_ref[...] = eidx
    gate_ref[...] = gates
    rank_ref[...] = rank
    carry_ref[...] = total
    cnt_ref[...] = total.astype(jnp.int32)


def _out_proj(y_hg, y_gm, h0, w_out_bf, g, b, w_router_t, router_bias, ws_gu_bf, ws_down_bf, alpha, tm):
    t, d = h0.shape
    sg = ws_gu_bf.shape[1]
    sd = ws_down_bf.shape[0]
    hw = y_hg.shape[1]
    gw = y_gm.shape[1]
    e = w_router_t.shape[0]
    return pl.pallas_call(
        functools.partial(_out_proj_kernel, alpha=alpha),
        grid=(t // tm,),
        in_specs=[pl.BlockSpec((tm, hw), lambda i: (i, 0)),
                  pl.BlockSpec((tm, gw), lambda i: (i, 0)),
                  pl.BlockSpec((tm, d), lambda i: (i, 0)),
                  pl.BlockSpec((hw, d), lambda i: (0, 0)),
                  pl.BlockSpec((gw, d), lambda i: (0, 0)),
                  pl.BlockSpec((1, d), lambda i: (0, 0)),
                  pl.BlockSpec((1, d), lambda i: (0, 0)),
                  pl.BlockSpec((e, d), lambda i: (0, 0)),
                  pl.BlockSpec((e, 1), lambda i: (0, 0)),
                  pl.BlockSpec((d, sg), lambda i: (0, 0)),
                  pl.BlockSpec((sd, d), lambda i: (0, 0))],
        out_specs=[pl.BlockSpec((tm, d), lambda i: (i, 0)),
                   pl.BlockSpec((tm, d // 2), lambda i: (i, 0)),
                   pl.BlockSpec((TOP_K, tm), lambda i: (0, i)),
                   pl.BlockSpec((TOP_K, tm), lambda i: (0, i)),
                   pl.BlockSpec((TOP_K, tm), lambda i: (0, i)),
                   pl.BlockSpec((e, 1), lambda i: (0, 0))],
        out_shape=[jax.ShapeDtypeStruct((t, d), F32), jax.ShapeDtypeStruct((t, d // 2), jnp.uint32),
                   jax.ShapeDtypeStruct((TOP_K, t), jnp.int32),
                   jax.ShapeDtypeStruct((TOP_K, t), F32),
                   jax.ShapeDtypeStruct((TOP_K, t), jnp.int32),
                   jax.ShapeDtypeStruct((e, 1), jnp.int32)],
        scratch_shapes=[pltpu.VMEM((e, 1), F32)],
        compiler_params=_cparams(("arbitrary",)),
        name="out_proj",
    )(y_hg, y_gm, h0, w_out_bf[:hw], w_out_bf[hw:], g.reshape(1, d), b.reshape(1, d), w_router_t,
      router_bias.astype(F32).reshape(e, 1), ws_gu_bf, ws_down_bf)


def _route_tile(logits, bias, carry):
    n_exp, tt = logits.shape
    per_group = n_exp // N_GROUPS
    neg_inf = jnp.float32(-jnp.inf)
    scores = jax.nn.sigmoid(logits)
    biased = scores + bias

    gio = lax.broadcasted_iota(jnp.int32, (per_group, tt), 0)
    blocks, gs_rows = [], []
    for g in range(N_GROUPS):
        blk = biased[g * per_group:(g + 1) * per_group, :]
        m1 = jnp.max(blk, axis=0, keepdims=True)
        first = jnp.min(jnp.where(blk == m1, gio, per_group), axis=0, keepdims=True)
        m2 = jnp.max(jnp.where(gio == first, neg_inf, blk), axis=0, keepdims=True)
        blocks.append(blk)
        gs_rows.append(m1 + m2)
    gs = jnp.concatenate(gs_rows, axis=0)
    gidx = lax.broadcasted_iota(jnp.int32, (N_GROUPS, tt), 0)
    beaten = jnp.zeros((N_GROUPS, tt), jnp.int32)
    for g in range(N_GROUPS):
        r = gs_rows[g]
        beaten = beaten + jnp.where((r > gs) | ((r == gs) & (g < gidx)), 1, 0)
    keep = beaten < TOPK_GROUPS
    cand = jnp.concatenate(
        [jnp.where(keep[g:g + 1, :], blocks[g], neg_inf) for g in range(N_GROUPS)], axis=0)

    eio = lax.broadcasted_iota(jnp.int32, (n_exp, tt), 0)
    idx_rows, w_rows = [], []
    chosen = jnp.zeros((n_exp, tt), F32)
    for _ in range(TOP_K):
        m = jnp.max(cand, axis=0, keepdims=True)
        idx = jnp.min(jnp.where(cand == m, eio, n_exp), axis=0, keepdims=True)
        sel = eio == idx
        w_rows.append(jnp.sum(jnp.where(sel, scores, 0.0), axis=0, keepdims=True))
        idx_rows.append(idx)
        cand = jnp.where(sel, neg_inf, cand)
        chosen = jnp.where(sel, 1.0, chosen)
    eidx = jnp.concatenate(idx_rows, axis=0)
    w = jnp.concatenate(w_rows, axis=0)
    gates = w / jnp.sum(w, axis=0, keepdims=True) * ROUTED_SCALE

    tr = lax.broadcasted_iota(jnp.int32, (tt, tt), 0)
    tc = lax.broadcasted_iota(jnp.int32, (tt, tt), 1)
    before = (tr < tc).astype(BF16)
    cum = _dot(chosen.astype(BF16), before) + carry
    rank = jnp.concatenate(
        [jnp.sum(jnp.where(eio == idx_rows[k], cum, 0.0), axis=0, keepdims=True)
         for k in range(TOP_K)], axis=0).astype(jnp.int32)
    total = carry + jnp.sum(chosen, axis=1, keepdims=True)
    return eidx, gates, rank, total


def _chunked(table, chunk):
    k, t = table.shape
    return table.reshape(k, t // chunk, chunk).transpose(1, 0, 2)


def _sc_workers():
    info = plsc.get_sparse_core_info()
    return info.num_cores, info.num_cores * info.num_subcores


def _slot_tables(eidx, rank, pad_start, n_rows, chunk):
    k_top, t = eidx.shape
    n_exp = pad_start.shape[0]
    n_cores, n_workers = _sc_workers()
    n_chunks = t // chunk
    per_worker = n_chunks // n_workers
    lanes = plsc.get_sparse_core_info().num_lanes
    mesh = plsc.VectorSubcoreMesh(core_axis_name="c", subcore_axis_name="s")

    @functools.partial(
        pl.kernel, mesh=mesh,
        out_type=(jax.ShapeDtypeStruct((n_rows, TOKEN_ROW), jnp.int32),
                  jax.ShapeDtypeStruct((n_chunks, k_top, chunk), jnp.int32)),
        scratch_types=[pltpu.VMEM((n_exp,), jnp.int32),
                       pltpu.VMEM((k_top, chunk), jnp.int32),
                       pltpu.VMEM((k_top, chunk), jnp.int32),
                       pltpu.VMEM((k_top, chunk), jnp.int32),
                       pltpu.VMEM((chunk, TOKEN_ROW), jnp.int32),
                       pltpu.SemaphoreType.DMA],
        compiler_params=pltpu.CompilerParams(needs_layout_passes=False),
        name="slot_tables",
    )
    def build(e_hbm, r_hbm, ps_hbm, tok_hbm, d_hbm, ps_v, e_v, r_v, idx_v, rows_v, sem):
        wid = lax.axis_index("s") * n_cores + lax.axis_index("c")
        pltpu.sync_copy(ps_hbm, ps_v)

        @pl.loop(0, per_worker)
        def _(j):
            c = wid * per_worker + j
            pltpu.sync_copy(e_hbm.at[c], e_v)
            pltpu.sync_copy(r_hbm.at[c], r_v)
            for k in range(k_top):
                for s in range(chunk // lanes):
                    sl = pl.ds(s * lanes, lanes)
                    idx_v[k, sl] = plsc.load_gather(ps_v, [e_v[k, sl]]) + r_v[k, sl]
            for i in range(chunk):
                rows_v[i, pl.ds(0, lanes)] = jnp.full((lanes,), c * chunk + i, jnp.int32)
            pltpu.sync_copy(idx_v, d_hbm.at[c])
            copies = [pltpu.async_copy(rows_v, tok_hbm.at[idx_v.at[k]], sem) for k in range(k_top)]
            for cp in copies:
                cp.wait()

    return build(_chunked(eidx, chunk), _chunked(rank, chunk), pad_start)


def _gather_back(dest_chunks, ys, sub):
    n_chunks, k_top, chunk = dest_chunks.shape
    t = n_chunks * chunk
    dh = ys.shape[1]
    n_cores, n_workers = _sc_workers()
    per_worker = n_chunks // n_workers
    mesh = plsc.VectorSubcoreMesh(core_axis_name="c", subcore_axis_name="s")

    @functools.partial(
        pl.kernel, mesh=mesh,
        out_type=jax.ShapeDtypeStruct((k_top, t, dh), ys.dtype),
        scratch_types=[pltpu.VMEM((k_top, chunk), jnp.int32),
                       pltpu.VMEM((k_top, sub, dh), ys.dtype),
                       pltpu.SemaphoreType.DMA],
        name="gather_back",
    )
    def gather_rows(ys_hbm, dest_hbm, o_hbm, idx_v, rows_v, sem):
        wid = lax.axis_index("s") * n_cores + lax.axis_index("c")

        @pl.loop(0, per_worker)
        def _(j):
            c = wid * per_worker + j
            pltpu.sync_copy(dest_hbm.at[c], idx_v)
            for q in range(chunk // sub):
                copies = [pltpu.async_copy(ys_hbm.at[idx_v.at[k, pl.ds(q * sub, sub)]], rows_v.at[k], sem)
                          for k in range(k_top)]
                for cp in copies:
                    cp.wait()
                for k in range(k_top):
                    pltpu.sync_copy(rows_v.at[k], o_hbm.at[k, pl.ds(c * chunk + q * sub, sub)])

    return gather_rows(ys, dest_chunks)


def _experts_kernel(bs_ref, cnt_ref, wgu_hbm, wd_hbm, h_hbm, tok_hbm, ys_hbm, wgu_buf, wd_buf, gu_bf,
                    d_bf, h_v, xg, ybuf, tokbuf, wsem, hsem, tsem, ysem):
    e = pl.program_id(0)
    n_exp = pl.num_programs(0)
    n_slots, mb, _ = ybuf.shape
    w_slots = wgu_buf.shape[0]
    t_slots = tokbuf.shape[0]
    n_tok = h_v.shape[0]
    n_blocks = bs_ref[n_exp]
    b0 = bs_ref[e]
    b1 = bs_ref[e + 1]
    count = cnt_ref[e]

    def w_copies(ex):
        slot = lax.rem(ex, w_slots)
        return (pltpu.make_async_copy(wgu_hbm.at[ex], wgu_buf.at[slot], wsem.at[slot]),
                pltpu.make_async_copy(wd_hbm.at[ex], wd_buf.at[slot], wsem.at[slot]))

    def tok_copy(b):
        src = jnp.minimum(b, n_blocks - 1)
        slot = lax.rem(b, t_slots)
        return pltpu.make_async_copy(tok_hbm.at[pl.ds(src, 1)], tokbuf.at[slot], tsem.at[slot])

    def y_copy(b):
        slot = lax.rem(b, n_slots)
        return pltpu.make_async_copy(ybuf.at[slot],
                                     ys_hbm.at[pl.ds(pl.multiple_of(b * mb, mb), mb)], ysem.at[slot])

    def gather_rows(b):
        ids = lax.rem(b, t_slots)
        dst = lax.rem(b, xg.shape[0])
        for i in range(mb):
            xg[dst, pl.ds(i, 1), :] = h_v[pl.ds(tokbuf[ids, 0, i], 1), :]

    @pl.when(e == 0)
    def _():
        resident = pltpu.make_async_copy(h_hbm, h_v, hsem)
        resident.start()
        for i in range(w_slots - 1):
            @pl.when(i < n_exp)
            def _():
                for cp in w_copies(i):
                    cp.start()
        for i in range(t_slots - 1):
            tok_copy(i).start()
        resident.wait()
        for i in range(GATHER_AHEAD):
            tok_copy(i).wait()
            gather_rows(i)

    @pl.when(e + w_slots - 1 < n_exp)
    def _():
        for cp in w_copies(e + w_slots - 1):
            cp.start()

    for cp in w_copies(e):
        cp.wait()
    w_slot = lax.rem(e, w_slots)

    @pl.when(b1 > b0)
    def _():
        gu_bf[...] = wgu_buf[w_slot].astype(BF16)
        d_bf[...] = wd_buf[w_slot].astype(BF16)

    kh = gu_bf.shape[0] // 2
    half = gu_bf.shape[1] // 2

    def compute(b):
        rows = lax.broadcasted_iota(jnp.int32, (mb, xg.shape[2]), 0)
        valid = count - (b - b0) * mb
        words = jnp.where(rows < valid, xg[lax.rem(b, xg.shape[0])], jnp.uint32(0))
        xa, xb = _unpack_halves(words)
        gu = _dot(xa.astype(BF16), gu_bf[0:kh, :]) + _dot(xb.astype(BF16), gu_bf[kh:, :])
        hid = _silu(gu[:, :half]) * gu[:, half:]
        ybuf[lax.rem(b, n_slots)] = _pack_halves(_dot(hid.astype(BF16), d_bf[...]))

    def step(b, width):
        blocks = [b + i for i in range(width)]
        for bb in blocks:
            tok_copy(bb + t_slots - 1).start()
            tok_copy(bb + GATHER_AHEAD).wait()
        for bb in blocks:
            @pl.when(bb >= n_slots)
            def _():
                y_copy(bb - n_slots).wait()
        for bb in blocks:
            compute(bb)
        for bb in blocks:
            gather_rows(bb + GATHER_AHEAD)
        for bb in blocks:
            y_copy(bb).start(priority=RING_DMA_PRIORITY)

    n_pairs = (b1 - b0) // 2

    def pair(p, carry):
        step(b0 + 2 * p, 2)
        return carry

    lax.fori_loop(0, n_pairs, pair, 0)

    @pl.when(b0 + 2 * n_pairs < b1)
    def _():
        step(b1 - 1, 1)

    @pl.when(e == n_exp - 1)
    def _():
        for i in range(1, n_slots + 1):
            @pl.when(n_blocks >= i)
            def _():
                y_copy(n_blocks - i).wait()
        for i in range(GATHER_AHEAD, t_slots - 1):
            tok_copy(n_blocks + i).wait()


def _experts(block_start, counts, h1_packed, tok_ids, w_gu, w_down):
    t, dh = h1_packed.shape
    n_rows = tok_ids.shape[0] * tok_ids.shape[1]
    n_exp, d, gu_w = w_gu.shape
    ed = w_down.shape[1]
    return pl.pallas_call(
        _experts_kernel,
        grid_spec=pltpu.PrefetchScalarGridSpec(
            num_scalar_prefetch=2,
            grid=(n_exp,),
            in_specs=[pl.BlockSpec(memory_space=pl.ANY)] * 4,
            out_specs=pl.BlockSpec(memory_space=pl.ANY),
            scratch_shapes=[pltpu.VMEM((WEIGHT_SLOTS, d, gu_w), w_gu.dtype),
                            pltpu.VMEM((WEIGHT_SLOTS, ed, d), w_down.dtype),
                            pltpu.VMEM((d, gu_w), BF16), pltpu.VMEM((ed, d), BF16),
                            pltpu.VMEM((t, dh), h1_packed.dtype),
                            pltpu.VMEM((GATHER_AHEAD + BLOCKS_PER_STEP, MOE_BLOCK, dh), jnp.uint32),
                            pltpu.VMEM((RING_SLOTS, MOE_BLOCK, dh), jnp.uint32),
                            pltpu.SMEM((TOKEN_SLOTS, 1, MOE_BLOCK), jnp.int32),
                            pltpu.SemaphoreType.DMA((WEIGHT_SLOTS,)),
                            pltpu.SemaphoreType.DMA,
                            pltpu.SemaphoreType.DMA((TOKEN_SLOTS,)),
                            pltpu.SemaphoreType.DMA((RING_SLOTS,))]),
        out_shape=jax.ShapeDtypeStruct((n_rows, dh), jnp.uint32),
        compiler_params=pltpu.CompilerParams(dimension_semantics=("arbitrary",),
                                             vmem_limit_bytes=EXPERTS_VMEM_LIMIT),
        name="experts",
    )(block_start, counts, w_gu, w_down, h1_packed, tok_ids)


def _combine_kernel(base_ref, w_ref, g_ref, b_ref, yg_ref, *rest):
    o_ref = rest[-1]
    acc = base_ref[...]
    dh = acc.shape[1] // 2
    acc_a = acc[:, :dh]
    acc_b = acc[:, dh:]
    w = w_ref[...]
    for k in range(TOP_K):
        ya, yb = _unpack_halves(yg_ref[k])
        acc_a = acc_a + ya * w[:, k:k + 1]
        acc_b = acc_b + yb * w[:, k:k + 1]
    o_ref[...] = _layer_norm(jnp.concatenate([acc_a, acc_b], axis=1), g_ref[...], b_ref[...])


def _combine(base, w_tk, g, b, yg, prev, part, tc):
    t, d = base.shape
    tiles = yg.shape[1] // tc
    off = part * tiles
    in_specs = [pl.BlockSpec((tc, d), lambda i: (i + off, 0)),
                pl.BlockSpec((tc, TOP_K), lambda i: (i + off, 0)),
                pl.BlockSpec((1, d), lambda i: (0, 0)),
                pl.BlockSpec((1, d), lambda i: (0, 0)),
                pl.BlockSpec((TOP_K, tc, yg.shape[2]), lambda i: (0, i, 0))]
    args = [base, w_tk, g.reshape(1, d), b.reshape(1, d), yg]
    aliases = {}
    if prev is not None:
        in_specs.append(pl.BlockSpec(memory_space=pl.ANY))
        args.append(prev)
        aliases = {len(args) - 1: 0}
    return pl.pallas_call(
        _combine_kernel,
        grid=(tiles,),
        in_specs=in_specs,
        out_specs=pl.BlockSpec((tc, d), lambda i: (i + off, 0)),
        out_shape=jax.ShapeDtypeStruct((t, d), F32),
        input_output_aliases=aliases,
        compiler_params=_cparams(("parallel",)),
        name="combine",
    )(*args)


def _moe(base, h1_packed, eidx, w_kt, rank, counts, w_exp_gu, w_exp_down, ln_g, ln_b):
    t, d = base.shape
    n_exp = counts.shape[0]

    counts = counts.reshape(n_exp)
    blocks_per_expert = (counts + MOE_BLOCK - 1) // MOE_BLOCK
    block_start = jnp.concatenate(
        [jnp.zeros((1,), jnp.int32), jnp.cumsum(blocks_per_expert).astype(jnp.int32)])
    pad_start = block_start[:n_exp] * MOE_BLOCK
    n_blocks = -(-(t * TOP_K + n_exp * (MOE_BLOCK - 1)) // MOE_BLOCK)

    tok_rows, dest_chunks = _slot_tables(eidx, rank, pad_start, n_blocks * MOE_BLOCK, chunk=SLOT_CHUNK)
    tok_ids = jnp.clip(tok_rows[:, 0], 0, t - 1).reshape(n_blocks, MOE_BLOCK)
    ys = _experts(block_start, counts, h1_packed, tok_ids, w_exp_gu, w_exp_down)
    per_part = dest_chunks.shape[0] // COMBINE_PARTS
    w_tk = w_kt.T
    out = None
    for part in range(COMBINE_PARTS):
        yg = _gather_back(dest_chunks[part * per_part:(part + 1) * per_part], ys, sub=GATHER_CHUNK)
        out = _combine(base, w_tk, ln_g, ln_b, yg, out, part, tc=min(256, t))
    return out


def kernel(x, ln_in_g, ln_in_b, w_in, hg_lb_logits, hg_norm_g, gm_v_norm_g, gm_v_norm_b, gm_w_s, gm_b_s, gm_out_norm_g, w_out, ln1_g, ln1_b, w_router, router_bias, w_exp_gu, w_exp_down, w_shared_gu, w_shared_down, ln2_g, ln2_b):
    bsz, s, d = x.shape
    depth = w_in.shape[0]
    assert depth == 1, "the lower-bound table row used in the hgrn kernel assumes one layer"
    alpha = (2.0 * depth) ** 0.25
    t = bsz * s
    hg_width = hg_norm_g.shape[1]
    gm_width = gm_v_norm_g.shape[1]
    tm = min(512, t)

    x2 = x.reshape(t, d)
    h0, proj, f_pre = _in_proj(x2, ln_in_g, ln_in_b, w_in[0].astype(BF16), hg_width, hg_width, tm)
    proj3 = proj.reshape(bsz, s, proj.shape[1])
    y_hg, y_gm = _mixer(proj3, f_pre.reshape(bsz, s, hg_width), hg_lb_logits.astype(F32),
                        hg_norm_g[0], gm_v_norm_g[0], gm_v_norm_b[0], gm_w_s[0], gm_b_s[0],
                        gm_out_norm_g[0], sb=min(512, s))
    base, h1_packed, eidx, gates, rank, counts = _out_proj(
        y_hg.reshape(t, hg_width), y_gm.reshape(t, gm_width), h0, w_out[0].astype(BF16),
        ln1_g[0], ln1_b[0], w_router[0].T, router_bias[0], w_shared_gu[0].astype(BF16),
        w_shared_down[0].astype(BF16), alpha, tm)
    out = _moe(base, h1_packed, eidx, gates, rank, counts, w_exp_gu[0], w_exp_down[0],
               ln2_g[0], ln2_b[0])
    return out.reshape(bsz, s, d)
```

```python
import functools

import jax
import jax.numpy as jnp
from jax import lax
from jax.experimental import pallas as pl
from jax.experimental.pallas import tpu as pltpu
from jax.experimental.pallas import tpu_sc as plsc

F32 = jnp.float32
BF16 = jnp.bfloat16

LN_EPS = 1e-5
RMS_EPS = 1e-6
CHUNK = 64
SUB = 16
MAX_FACTORED_EXPONENT = 80.0
FACTORED_MIN_LB = float(2.718281828459045 ** (-MAX_FACTORED_EXPONENT / CHUNK))
HEAD_DIM = 128
GM_BLOCK = 128
TOP_K = 8
N_GROUPS = 8
TOPK_GROUPS = 4
ROUTED_SCALE = 2.5
MOE_BLOCK = 128
WEIGHT_SLOTS = 3
RING_SLOTS = 8
RING_DMA_PRIORITY = 1
BLOCKS_PER_STEP = 2
GATHER_AHEAD = 2
TOKEN_SLOTS = 8
TOKEN_ROW = 128
SLOT_CHUNK = 64
COMBINE_PARTS = 2
GATHER_CHUNK = 16
VMEM_LIMIT = 48 * 1024 * 1024
EXPERTS_VMEM_LIMIT = 56 * 1024 * 1024


def _cparams(sem):
    return pltpu.CompilerParams(dimension_semantics=sem, vmem_limit_bytes=VMEM_LIMIT)


def _layer_norm(x, g, b):
    mu = jnp.mean(x, axis=-1, keepdims=True)
    xc = x - mu
    var = jnp.mean(xc * xc, axis=-1, keepdims=True)
    return xc * lax.rsqrt(var + LN_EPS) * g + b


def _silu(x):
    return x * jax.nn.sigmoid(x)


def _gelu(x):
    return 0.5 * x * (1.0 + lax.erf(x * (2.0 ** -0.5)))


def _dot(a, b):
    return jnp.dot(a, b, preferred_element_type=F32)


def _dot_nt(a, b, precision=None):
    return lax.dot_general(a, b, (((1,), (1,)), ((), ())), preferred_element_type=F32,
                           precision=precision)


def _dot_tn(a, b):
    return lax.dot_general(a, b, (((0,), (0,)), ((), ())), preferred_element_type=F32)


def _split_bf16(x):
    hi = x.astype(BF16)
    return hi, (x - hi.astype(F32)).astype(BF16)


def _pack_halves(x):
    n = x.shape[1] // 2
    hi = lax.bitcast_convert_type(x[:, :n].astype(BF16).astype(F32), jnp.uint32)
    lo = lax.bitcast_convert_type(x[:, n:].astype(BF16).astype(F32), jnp.uint32)
    return hi | (lo >> 16)


def _unpack_halves(w):
    a = lax.bitcast_convert_type(w & jnp.uint32(0xFFFF0000), F32)
    b = lax.bitcast_convert_type(w << 16, F32)
    return a, b


def _in_proj_kernel(x_ref, g_ref, b_ref, w_ref, h_ref, p_ref, f_ref, *, f_lo):
    h = _layer_norm(x_ref[...], g_ref[...], b_ref[...])
    h_ref[...] = h
    p = _dot(h.astype(BF16), w_ref[...])
    p_ref[...] = p.astype(BF16)
    f_ref[...] = p[:, f_lo:f_lo + f_ref.shape[1]]


def _in_proj(x2, g, b, w_bf, f_lo, f_width, tm):
    t, d = x2.shape
    n = w_bf.shape[1]
    return pl.pallas_call(
        functools.partial(_in_proj_kernel, f_lo=f_lo),
        grid=(t // tm,),
        in_specs=[pl.BlockSpec((tm, d), lambda i: (i, 0)),
                  pl.BlockSpec((1, d), lambda i: (0, 0)),
                  pl.BlockSpec((1, d), lambda i: (0, 0)),
                  pl.BlockSpec((d, n), lambda i: (0, 0))],
        out_specs=[pl.BlockSpec((tm, d), lambda i: (i, 0)),
                   pl.BlockSpec((tm, n), lambda i: (i, 0)),
                   pl.BlockSpec((tm, f_width), lambda i: (i, 0))],
        out_shape=[jax.ShapeDtypeStruct((t, d), F32), jax.ShapeDtypeStruct((t, n), BF16),
                   jax.ShapeDtypeStruct((t, f_width), F32)],
        compiler_params=_cparams(("parallel",)),
        name="in_proj",
    )(x2, g.reshape(1, d), b.reshape(1, d), w_bf)


def _mixer_kernel(q_ref, f_ref, i_ref, g_ref, lbl_ref, gn_ref, u_ref, v_ref, vg_ref, vb_ref, ws_ref,
                  bs_ref, og_ref, o_ref, ogm_ref, st_ref, *, n_chunks):
    @pl.when(pl.program_id(1) == 0)
    def _():
        st_ref[...] = jnp.zeros_like(st_ref)

    lg = lbl_ref[...]
    ex = jnp.exp(lg - jnp.max(lg, axis=0, keepdims=True))
    lb_all = ex[0:1, :] / jnp.sum(ex, axis=0, keepdims=True)
    gn_all = gn_ref[...]

    c = CHUNK
    heads = st_ref.shape[0]
    row = lax.broadcasted_iota(jnp.int32, (c, c), 0)
    col = lax.broadcasted_iota(jnp.int32, (c, c), 1)
    tril = (row >= col).astype(BF16)
    sub_row = lax.broadcasted_iota(jnp.int32, (SUB, HEAD_DIM), 0)
    ones = jnp.ones((HEAD_DIM, HEAD_DIM), BF16)
    neg_inf = jnp.float32(-jnp.inf)

    def intra_pairwise(q, k, v, v_bf, b, o_inter):
        outs = []
        for blk in range(c // SUB):
            lo = blk * SUB
            b_i = b[lo:lo + SUB, :]
            q_i = q[lo:lo + SUB, :]
            k_i = k[lo:lo + SUB, :]
            v_i = v[lo:lo + SUB, :]
            prods = []
            for j in range(SUB):
                diff = jnp.where(sub_row >= j, b_i - b_i[j:j + 1, :], neg_inf)
                prods.append(q_i * k_i[j:j + 1, :] * jnp.exp(diff))
            p_all = jnp.concatenate(prods, axis=0).astype(BF16)
            r_all = _dot(p_all, ones)
            o_blk = o_inter[lo:lo + SUB, :]
            for j in range(SUB):
                o_blk = o_blk + r_all[j * SUB:(j + 1) * SUB, :] * v_i[j:j + 1, :]
            if blk > 0:
                b_ref = b[lo - 1:lo, :]
                q_hat = (q_i * jnp.exp(b_i - b_ref)).astype(BF16)
                k_hat = (k[0:lo, :] * jnp.exp(b_ref - b[0:lo, :])).astype(BF16)
                scores = _dot_nt(q_hat, k_hat)
                o_blk = o_blk + _dot(scores.astype(BF16), v_bf[0:lo, :])
            outs.append(o_blk)
        return jnp.concatenate(outs, axis=0)

    def one_chunk(r0, factored):
        rows = pl.ds(r0, c)
        hs = [slice(h * HEAD_DIM, (h + 1) * HEAD_DIM) for h in range(heads)]
        q = _silu(q_ref[rows, :].astype(F32))
        f = lb_all + (1.0 - lb_all) * jax.nn.sigmoid(f_ref[rows, :])
        v_bf = i_ref[rows, :]
        v = v_bf.astype(F32)
        lf = jnp.log(f)
        k = 1.0 - f
        lf_hi, lf_lo = _split_bf16(lf)
        b2 = _dot(tril, jnp.concatenate([lf_hi, lf_lo], axis=1))
        width = lf.shape[1]
        b = b2[:, :width] + b2[:, width:]
        b_last = b[c - 1:c, :]
        q_dec = (q * jnp.exp(b)).astype(BF16)
        k_dec = (k * jnp.exp(b_last - b)).astype(BF16)
        decay = jnp.exp(b_last)
        st = [st_ref[h] for h in range(heads)]
        st_bf = [s.astype(BF16) for s in st]
        if factored:
            k_grow = (k * jnp.exp(-b)).astype(BF16)
            both = [_dot_nt(q_dec[:, hs[h]], jnp.concatenate([st_bf[h], k_grow[:, hs[h]]], axis=0))
                    for h in range(heads)]
            scores = [jnp.where(row >= col, both[h][:, HEAD_DIM:], 0.0).astype(BF16)
                      for h in range(heads)]
            outs = [both[h][:, :HEAD_DIM] + _dot(scores[h], v_bf[:, hs[h]]) for h in range(heads)]
        else:
            outs = [intra_pairwise(q[:, hs[h]], k[:, hs[h]], v[:, hs[h]], v_bf[:, hs[h]],
                                   b[:, hs[h]], _dot_nt(q_dec[:, hs[h]], st_bf[h]))
                    for h in range(heads)]
        for h in range(heads):
            st_ref[h] = st[h] * decay[:, hs[h]] + _dot_tn(v_bf[:, hs[h]], k_dec[:, hs[h]])
        inv = [lax.rsqrt(jnp.mean(o * o, axis=-1, keepdims=True) + RMS_EPS) for o in outs]
        o = jnp.concatenate([outs[h] * inv[h] for h in range(heads)], axis=1)
        o_ref[rows, :] = (o * gn_all * _silu(g_ref[rows, :].astype(F32))).astype(o_ref.dtype)

    groups = ws_ref.shape[0]
    gdim = u_ref.shape[-1] // groups
    gm_row = lax.broadcasted_iota(jnp.int32, (GM_BLOCK, GM_BLOCK), 0) // CHUNK
    gm_col = lax.broadcasted_iota(jnp.int32, (GM_BLOCK, GM_BLOCK), 1) // CHUNK
    ws = [jnp.where(gm_row >= gm_col, ws_ref[g], 0.0).astype(BF16) for g in range(groups)]

    def gmlp_block(r0):
        rows = pl.ds(r0, GM_BLOCK)
        u = _gelu(u_ref[rows, :].astype(F32))
        v = _layer_norm(_gelu(v_ref[rows, :].astype(F32)), vg_ref[...], vb_ref[...])
        v_bf = v.astype(BF16)
        mixed = jnp.concatenate(
            [_dot(ws[g], v_bf[:, g * gdim:(g + 1) * gdim]) for g in range(groups)], axis=1)
        y = u * (mixed + bs_ref[...])
        inv = lax.rsqrt(jnp.mean(y * y, axis=-1, keepdims=True) + RMS_EPS)
        ogm_ref[rows, :] = (y * inv * og_ref[...]).astype(ogm_ref.dtype)

    def run_chunks(factored):
        per_trip = 4 if (factored and n_chunks % 4 == 0) else 1
        trip_rows = per_trip * c
        inline_gmlp = trip_rows % GM_BLOCK == 0

        def chunk_body(ci, carry):
            for u in range(per_trip):
                one_chunk(pl.multiple_of((ci * per_trip + u) * c, c), factored)
            if inline_gmlp:
                for n in range(trip_rows // GM_BLOCK):
                    gmlp_block(pl.multiple_of(ci * trip_rows + n * GM_BLOCK, GM_BLOCK))
            return carry

        lax.fori_loop(0, n_chunks // per_trip, chunk_body, 0)
        if not inline_gmlp:
            for n in range(n_chunks * c // GM_BLOCK):
                gmlp_block(n * GM_BLOCK)

    bounded = jnp.min(lb_all) >= FACTORED_MIN_LB
    pl.when(bounded)(lambda: run_chunks(True))
    pl.when(jnp.logical_not(bounded))(lambda: run_chunks(False))


def _mixer(proj3, f_pre3, lb_logits, norm_g, v_norm_g, v_norm_b, w_s, b_s, out_norm_g, sb):
    bsz, s, _ = proj3.shape
    hg_width = norm_g.shape[0]
    heads = hg_width // HEAD_DIM
    gw = v_norm_g.shape[0]
    assert gw == hg_width and sb % GM_BLOCK == 0
    groups = w_s.shape[0]
    bias_full = jnp.repeat(b_s.T, gw // groups, axis=1)

    def sect(k):
        return pl.BlockSpec((None, sb, hg_width), lambda b, j, k=k: (b, j, k))

    row_vec = pl.BlockSpec((1, hg_width), lambda b, j: (0, 0))
    return pl.pallas_call(
        functools.partial(_mixer_kernel, n_chunks=sb // CHUNK),
        grid=(bsz, s // sb),
        in_specs=[sect(0), sect(0), sect(2), sect(3),
                  pl.BlockSpec((lb_logits.shape[0], hg_width), lambda b, j: (0, 0)),
                  row_vec, sect(4), sect(5), row_vec, row_vec,
                  pl.BlockSpec((groups, GM_BLOCK, GM_BLOCK), lambda b, j: (0, 0, 0)),
                  pl.BlockSpec((GM_BLOCK, gw), lambda b, j: (0, 0)),
                  row_vec],
        out_specs=[pl.BlockSpec((None, sb, hg_width), lambda b, j: (b, j, 0)),
                   pl.BlockSpec((None, sb, gw), lambda b, j: (b, j, 0))],
        out_shape=[jax.ShapeDtypeStruct((bsz, s, hg_width), BF16),
                   jax.ShapeDtypeStruct((bsz, s, gw), BF16)],
        scratch_shapes=[pltpu.VMEM((heads, HEAD_DIM, HEAD_DIM), F32)],
        compiler_params=_cparams(("parallel", "arbitrary")),
        name="mixer",
    )(proj3, f_pre3, proj3, proj3, lb_logits, norm_g.reshape(1, hg_width), proj3, proj3,
      v_norm_g.reshape(1, gw), v_norm_b.reshape(1, gw), w_s, bias_full, out_norm_g.reshape(1, gw))


def _out_proj_kernel(yh_ref, yg_ref, h0_ref, wa_ref, wb_ref, g_ref, b_ref, wr_ref, rb_ref,
                     wsg_ref, wsd_ref, base_ref, hp_ref, eidx_ref, gate_ref, rank_ref, cnt_ref,
                     carry_ref, *, alpha):
    @pl.when(pl.program_id(0) == 0)
    def _():
        carry_ref[...] = jnp.zeros_like(carry_ref)

    mix = _dot(yh_ref[...], wa_ref[...]) + _dot(yg_ref[...], wb_ref[...])
    h1 = _layer_norm(alpha * h0_ref[...] + mix, g_ref[...], b_ref[...])
    hp_ref[...] = _pack_halves(h1)
    w_hi, w_lo = _split_bf16(wr_ref[...])
    h_hi, h_lo = _split_bf16(h1)
    logits = _dot_nt(w_hi, h_hi) + (_dot_nt(w_hi, h_lo) + _dot_nt(w_lo, h_hi))
    gu = _dot(h_hi, wsg_ref[...])
    half = gu.shape[1] // 2
    hid = (_silu(gu[:, :half]) * gu[:, half:]).astype(BF16)
    base_ref[...] = alpha * h1 + _dot(hid, wsd_ref[...])
    eidx, gates, rank, total = _route_tile(logits, rb_ref[...], carry_ref[...])
    eidx_ref[...] = eidx
    gate_ref[...] = gates
    rank_ref[...] = rank
    carry_ref[...] = total
    cnt_ref[...] = total.astype(jnp.int32)


def _out_proj(y_hg, y_gm, h0, w_out_bf, g, b, w_router_t, router_bias, ws_gu_bf, ws_down_bf, alpha, tm):
    t, d = h0.shape
    sg = ws_gu_bf.shape[1]
    sd = ws_down_bf.shape[0]
    hw = y_hg.shape[1]
    gw = y_gm.shape[1]
    e = w_router_t.shape[0]
    return pl.pallas_call(
        functools.partial(_out_proj_kernel, alpha=alpha),
        grid=(t // tm,),
        in_specs=[pl.BlockSpec((tm, hw), lambda i: (i, 0)),
                  pl.BlockSpec((tm, gw), lambda i: (i, 0)),
                  pl.BlockSpec((tm, d), lambda i: (i, 0)),
                  pl.BlockSpec((hw, d), lambda i: (0, 0)),
                  pl.BlockSpec((gw, d), lambda i: (0, 0)),
                  pl.BlockSpec((1, d), lambda i: (0, 0)),
                  pl.BlockSpec((1, d), lambda i: (0, 0)),
                  pl.BlockSpec((e, d), lambda i: (0, 0)),
                  pl.BlockSpec((e, 1), lambda i: (0, 0)),
                  pl.BlockSpec((d, sg), lambda i: (0, 0)),
                  pl.BlockSpec((sd, d), lambda i: (0, 0))],
        out_specs=[pl.BlockSpec((tm, d), lambda i: (i, 0)),
                   pl.BlockSpec((tm, d // 2), lambda i: (i, 0)),
                   pl.BlockSpec((TOP_K, tm), lambda i: (0, i)),
                   pl.BlockSpec((TOP_K, tm), lambda i: (0, i)),
                   pl.BlockSpec((TOP_K, tm), lambda i: (0, i)),
                   pl.BlockSpec((e, 1), lambda i: (0, 0))],
        out_shape=[jax.ShapeDtypeStruct((t, d), F32), jax.ShapeDtypeStruct((t, d // 2), jnp.uint32),
                   jax.ShapeDtypeStruct((TOP_K, t), jnp.int32),
                   jax.ShapeDtypeStruct((TOP_K, t), F32),
                   jax.ShapeDtypeStruct((TOP_K, t), jnp.int32),
                   jax.ShapeDtypeStruct((e, 1), jnp.int32)],
        scratch_shapes=[pltpu.VMEM((e, 1), F32)],
        compiler_params=_cparams(("arbitrary",)),
        name="out_proj",
    )(y_hg, y_gm, h0, w_out_bf[:hw], w_out_bf[hw:], g.reshape(1, d), b.reshape(1, d), w_router_t,
      router_bias.astype(F32).reshape(e, 1), ws_gu_bf, ws_down_bf)


def _route_tile(logits, bias, carry):
    n_exp, tt = logits.shape
    per_group = n_exp // N_GROUPS
    neg_inf = jnp.float32(-jnp.inf)
    scores = jax.nn.sigmoid(logits)
    biased = scores + bias

    gio = lax.broadcasted_iota(jnp.int32, (per_group, tt), 0)
    blocks, gs_rows = [], []
    for g in range(N_GROUPS):
        blk = biased[g * per_group:(g + 1) * per_group, :]
        m1 = jnp.max(blk, axis=0, keepdims=True)
        first = jnp.min(jnp.where(blk == m1, gio, per_group), axis=0, keepdims=True)
        m2 = jnp.max(jnp.where(gio == first, neg_inf, blk), axis=0, keepdims=True)
        blocks.append(blk)
        gs_rows.append(m1 + m2)
    gs = jnp.concatenate(gs_rows, axis=0)
    gidx = lax.broadcasted_iota(jnp.int32, (N_GROUPS, tt), 0)
    beaten = jnp.zeros((N_GROUPS, tt), jnp.int32)
    for g in range(N_GROUPS):
        r = gs_rows[g]
        beaten = beaten + jnp.where((r > gs) | ((r == gs) & (g < gidx)), 1, 0)
    keep = beaten < TOPK_GROUPS
    cand = jnp.concatenate(
        [jnp.where(keep[g:g + 1, :], blocks[g], neg_inf) for g in range(N_GROUPS)], axis=0)

    eio = lax.broadcasted_iota(jnp.int32, (n_exp, tt), 0)
    idx_rows, w_rows = [], []
    chosen = jnp.zeros((n_exp, tt), F32)
    for _ in range(TOP_K):
        m = jnp.max(cand, axis=0, keepdims=True)
        idx = jnp.min(jnp.where(cand == m, eio, n_exp), axis=0, keepdims=True)
        sel = eio == idx
        w_rows.append(jnp.sum(jnp.where(sel, scores, 0.0), axis=0, keepdims=True))
        idx_rows.append(idx)
        cand = jnp.where(sel, neg_inf, cand)
        chosen = jnp.where(sel, 1.0, chosen)
    eidx = jnp.concatenate(idx_rows, axis=0)
    w = jnp.concatenate(w_rows, axis=0)
    gates = w / jnp.sum(w, axis=0, keepdims=True) * ROUTED_SCALE

    tr = lax.broadcasted_iota(jnp.int32, (tt, tt), 0)
    tc = lax.broadcasted_iota(jnp.int32, (tt, tt), 1)
    before = (tr < tc).astype(BF16)
    cum = _dot(chosen.astype(BF16), before) + carry
    rank = jnp.concatenate(
        [jnp.sum(jnp.where(eio == idx_rows[k], cum, 0.0), axis=0, keepdims=True)
         for k in range(TOP_K)], axis=0).astype(jnp.int32)
    total = carry + jnp.sum(chosen, axis=1, keepdims=True)
    return eidx, gates, rank, total


def _chunked(table, chunk):
    k, t = table.shape
    return table.reshape(k, t // chunk, chunk).transpose(1, 0, 2)


def _sc_workers():
    info = plsc.get_sparse_core_info()
    return info.num_cores, info.num_cores * info.num_subcores


def _slot_tables(eidx, rank, pad_start, n_rows, chunk):
    k_top, t = eidx.shape
    n_exp = pad_start.shape[0]
    n_cores, n_workers = _sc_workers()
    n_chunks = t // chunk
    per_worker = n_chunks // n_workers
    lanes = plsc.get_sparse_core_info().num_lanes
    mesh = plsc.VectorSubcoreMesh(core_axis_name="c", subcore_axis_name="s")

    @functools.partial(
        pl.kernel, mesh=mesh,
        out_type=(jax.ShapeDtypeStruct((n_rows, TOKEN_ROW), jnp.int32),
                  jax.ShapeDtypeStruct((n_chunks, k_top, chunk), jnp.int32)),
        scratch_types=[pltpu.VMEM((n_exp,), jnp.int32),
                       pltpu.VMEM((k_top, chunk), jnp.int32),
                       pltpu.VMEM((k_top, chunk), jnp.int32),
                       pltpu.VMEM((k_top, chunk), jnp.int32),
                       pltpu.VMEM((chunk, TOKEN_ROW), jnp.int32),
                       pltpu.SemaphoreType.DMA],
        compiler_params=pltpu.CompilerParams(needs_layout_passes=False),
        name="slot_tables",
    )
    def build(e_hbm, r_hbm, ps_hbm, tok_hbm, d_hbm, ps_v, e_v, r_v, idx_v, rows_v, sem):
        wid = lax.axis_index("s") * n_cores + lax.axis_index("c")
        pltpu.sync_copy(ps_hbm, ps_v)

        @pl.loop(0, per_worker)
        def _(j):
            c = wid * per_worker + j
            pltpu.sync_copy(e_hbm.at[c], e_v)
            pltpu.sync_copy(r_hbm.at[c], r_v)
            for k in range(k_top):
                for s in range(chunk // lanes):
                    sl = pl.ds(s * lanes, lanes)
                    idx_v[k, sl] = plsc.load_gather(ps_v, [e_v[k, sl]]) + r_v[k, sl]
            for i in range(chunk):
                rows_v[i, pl.ds(0, lanes)] = jnp.full((lanes,), c * chunk + i, jnp.int32)
            pltpu.sync_copy(idx_v, d_hbm.at[c])
            copies = [pltpu.async_copy(rows_v, tok_hbm.at[idx_v.at[k]], sem) for k in range(k_top)]
            for cp in copies:
                cp.wait()

    return build(_chunked(eidx, chunk), _chunked(rank, chunk), pad_start)


def _gather_back(dest_chunks, ys, sub):
    n_chunks, k_top, chunk = dest_chunks.shape
    t = n_chunks * chunk
    dh = ys.shape[1]
    n_cores, n_workers = _sc_workers()
    per_worker = n_chunks // n_workers
    mesh = plsc.VectorSubcoreMesh(core_axis_name="c", subcore_axis_name="s")

    @functools.partial(
        pl.kernel, mesh=mesh,
        out_type=jax.ShapeDtypeStruct((k_top, t, dh), ys.dtype),
        scratch_types=[pltpu.VMEM((k_top, chunk), jnp.int32),
                       pltpu.VMEM((k_top, sub, dh), ys.dtype),
                       pltpu.SemaphoreType.DMA],
        name="gather_back",
    )
    def gather_rows(ys_hbm, dest_hbm, o_hbm, idx_v, rows_v, sem):
        wid = lax.axis_index("s") * n_cores + lax.axis_index("c")

        @pl.loop(0, per_worker)
        def _(j):
            c = wid * per_worker + j
            pltpu.sync_copy(dest_hbm.at[c], idx_v)
            for q in range(chunk // sub):
                copies = [pltpu.async_copy(ys_hbm.at[idx_v.at[k, pl.ds(q * sub, sub)]], rows_v.at[k], sem)
                          for k in range(k_top)]
                for cp in copies:
                    cp.wait()
                for k in range(k_top):
                    pltpu.sync_copy(rows_v.at[k], o_hbm.at[k, pl.ds(c * chunk + q * sub, sub)])

    return gather_rows(ys, dest_chunks)


def _experts_kernel(bs_ref, cnt_ref, wgu_hbm, wd_hbm, h_hbm, tok_hbm, ys_hbm, wgu_buf, wd_buf, gu_bf,
                    d_bf, h_v, xg, ybuf, tokbuf, wsem, hsem, tsem, ysem):
    e = pl.program_id(0)
    n_exp = pl.num_programs(0)
    n_slots, mb, _ = ybuf.shape
    w_slots = wgu_buf.shape[0]
    t_slots = tokbuf.shape[0]
    n_tok = h_v.shape[0]
    n_blocks = bs_ref[n_exp]
    b0 = bs_ref[e]
    b1 = bs_ref[e + 1]
    count = cnt_ref[e]

    def w_copies(ex):
        slot = lax.rem(ex, w_slots)
        return (pltpu.make_async_copy(wgu_hbm.at[ex], wgu_buf.at[slot], wsem.at[slot]),
                pltpu.make_async_copy(wd_hbm.at[ex], wd_buf.at[slot], wsem.at[slot]))

    def tok_copy(b):
        src = jnp.minimum(b, n_blocks - 1)
        slot = lax.rem(b, t_slots)
        return pltpu.make_async_copy(tok_hbm.at[pl.ds(src, 1)], tokbuf.at[slot], tsem.at[slot])

    def y_copy(b):
        slot = lax.rem(b, n_slots)
        return pltpu.make_async_copy(ybuf.at[slot],
                                     ys_hbm.at[pl.ds(pl.multiple_of(b * mb, mb), mb)], ysem.at[slot])

    def gather_rows(b):
        ids = lax.rem(b, t_slots)
        dst = lax.rem(b, xg.shape[0])
        for i in range(mb):
            xg[dst, pl.ds(i, 1), :] = h_v[pl.ds(tokbuf[ids, 0, i], 1), :]

    @pl.when(e == 0)
    def _():
        resident = pltpu.make_async_copy(h_hbm, h_v, hsem)
        resident.start()
        for i in range(w_slots - 1):
            @pl.when(i < n_exp)
            def _():
                for cp in w_copies(i):
                    cp.start()
        for i in range(t_slots - 1):
            tok_copy(i).start()
        resident.wait()
        for i in range(GATHER_AHEAD):
            tok_copy(i).wait()
            gather_rows(i)

    @pl.when(e + w_slots - 1 < n_exp)
    def _():
        for cp in w_copies(e + w_slots - 1):
            cp.start()

    for cp in w_copies(e):
        cp.wait()
    w_slot = lax.rem(e, w_slots)

    @pl.when(b1 > b0)
    def _():
        gu_bf[...] = wgu_buf[w_slot].astype(BF16)
        d_bf[...] = wd_buf[w_slot].astype(BF16)

    kh = gu_bf.shape[0] // 2
    half = gu_bf.shape[1] // 2

    def compute(blocks):
        rows = lax.broadcasted_iota(jnp.int32, (mb, xg.shape[2]), 0)
        halves = []
        for b in blocks:
            valid = count - (b - b0) * mb
            words = jnp.where(rows < valid, xg[lax.rem(b, xg.shape[0])], jnp.uint32(0))
            xa, xb = _unpack_halves(words)
            halves.append((xa.astype(BF16), xb.astype(BF16)))
        gus = [_dot(xa, gu_bf[0:kh, :]) + _dot(xb, gu_bf[kh:, :]) for xa, xb in halves]
        hids = [(_silu(gu[:, :half]) * gu[:, half:]).astype(BF16) for gu in gus]
        outs = [_dot(hid, d_bf[...]) for hid in hids]
        for b, y in zip(blocks, outs):
            ybuf[lax.rem(b, n_slots)] = _pack_halves(y)

    def step(b, width):
        blocks = [b + i for i in range(width)]
        for bb in blocks:
            tok_copy(bb + t_slots - 1).start()
            tok_copy(bb + GATHER_AHEAD).wait()
        for bb in blocks:
            @pl.when(bb >= n_slots)
            def _():
                y_copy(bb - n_slots).wait()
        compute(blocks)
        for bb in blocks:
            gather_rows(bb + GATHER_AHEAD)
        for bb in blocks:
            y_copy(bb).start(priority=RING_DMA_PRIORITY)

    n_pairs = (b1 - b0) // 2

    def pair(p, carry):
        step(b0 + 2 * p, 2)
        return carry

    lax.fori_loop(0, n_pairs, pair, 0)

    @pl.when(b0 + 2 * n_pairs < b1)
    def _():
        step(b1 - 1, 1)

    @pl.when(e == n_exp - 1)
    def _():
        for i in range(1, n_slots + 1):
            @pl.when(n_blocks >= i)
            def _():
                y_copy(n_blocks - i).wait()
        for i in range(GATHER_AHEAD, t_slots - 1):
            tok_copy(n_blocks + i).wait()


def _experts(block_start, counts, h1_packed, tok_ids, w_gu, w_down):
    t, dh = h1_packed.shape
    n_rows = tok_ids.shape[0] * tok_ids.shape[1]
    n_exp, d, gu_w = w_gu.shape
    ed = w_down.shape[1]
    return pl.pallas_call(
        _experts_kernel,
        grid_spec=pltpu.PrefetchScalarGridSpec(
            num_scalar_prefetch=2,
            grid=(n_exp,),
            in_specs=[pl.BlockSpec(memory_space=pl.ANY)] * 4,
            out_specs=pl.BlockSpec(memory_space=pl.ANY),
            scratch_shapes=[pltpu.VMEM((WEIGHT_SLOTS, d, gu_w), w_gu.dtype),
                            pltpu.VMEM((WEIGHT_SLOTS, ed, d), w_down.dtype),
                            pltpu.VMEM((d, gu_w), BF16), pltpu.VMEM((ed, d), BF16),
                            pltpu.VMEM((t, dh), h1_packed.dtype),
                            pltpu.VMEM((GATHER_AHEAD + BLOCKS_PER_STEP, MOE_BLOCK, dh), jnp.uint32),
                            pltpu.VMEM((RING_SLOTS, MOE_BLOCK, dh), jnp.uint32),
                            pltpu.SMEM((TOKEN_SLOTS, 1, MOE_BLOCK), jnp.int32),
                            pltpu.SemaphoreType.DMA((WEIGHT_SLOTS,)),
                            pltpu.SemaphoreType.DMA,
                            pltpu.SemaphoreType.DMA((TOKEN_SLOTS,)),
                            pltpu.SemaphoreType.DMA((RING_SLOTS,))]),
        out_shape=jax.ShapeDtypeStruct((n_rows, dh), jnp.uint32),
        compiler_params=pltpu.CompilerParams(dimension_semantics=("arbitrary",),
                                             vmem_limit_bytes=EXPERTS_VMEM_LIMIT),
        name="experts",
    )(block_start, counts, w_gu, w_down, h1_packed, tok_ids)


def _combine_kernel(base_ref, w_ref, g_ref, b_ref, yg_ref, *rest):
    o_ref = rest[-1]
    acc = base_ref[...]
    dh = acc.shape[1] // 2
    acc_a = acc[:, :dh]
    acc_b = acc[:, dh:]
    w = w_ref[...]
    for k in range(TOP_K):
        ya, yb = _unpack_halves(yg_ref[k])
        acc_a = acc_a + ya * w[:, k:k + 1]
        acc_b = acc_b + yb * w[:, k:k + 1]
    o_ref[...] = _layer_norm(jnp.concatenate([acc_a, acc_b], axis=1), g_ref[...], b_ref[...])


def _combine(base, w_tk, g, b, yg, prev, part, tc):
    t, d = base.shape
    tiles = yg.shape[1] // tc
    off = part * tiles
    in_specs = [pl.BlockSpec((tc, d), lambda i: (i + off, 0)),
                pl.BlockSpec((tc, TOP_K), lambda i: (i + off, 0)),
                pl.BlockSpec((1, d), lambda i: (0, 0)),
                pl.BlockSpec((1, d), lambda i: (0, 0)),
                pl.BlockSpec((TOP_K, tc, yg.shape[2]), lambda i: (0, i, 0))]
    args = [base, w_tk, g.reshape(1, d), b.reshape(1, d), yg]
    aliases = {}
    if prev is not None:
        in_specs.append(pl.BlockSpec(memory_space=pl.ANY))
        args.append(prev)
        aliases = {len(args) - 1: 0}
    return pl.pallas_call(
        _combine_kernel,
        grid=(tiles,),
        in_specs=in_specs,
        out_specs=pl.BlockSpec((tc, d), lambda i: (i + off, 0)),
        out_shape=jax.ShapeDtypeStruct((t, d), F32),
        input_output_aliases=aliases,
        compiler_params=_cparams(("parallel",)),
        name="combine",
    )(*args)


def _moe(base, h1_packed, eidx, w_kt, rank, counts, w_exp_gu, w_exp_down, ln_g, ln_b):
    t, d = base.shape
    n_exp = counts.shape[0]

    counts = counts.reshape(n_exp)
    blocks_per_expert = (counts + MOE_BLOCK - 1) // MOE_BLOCK
    block_start = jnp.concatenate(
        [jnp.zeros((1,), jnp.int32), jnp.cumsum(blocks_per_expert).astype(jnp.int32)])
    pad_start = block_start[:n_exp] * MOE_BLOCK
    n_blocks = -(-(t * TOP_K + n_exp * (MOE_BLOCK - 1)) // MOE_BLOCK)

    tok_rows, dest_chunks = _slot_tables(eidx, rank, pad_start, n_blocks * MOE_BLOCK, chunk=SLOT_CHUNK)
    tok_ids = jnp.clip(tok_rows[:, 0], 0, t - 1).reshape(n_blocks, MOE_BLOCK)
    ys = _experts(block_start, counts, h1_packed, tok_ids, w_exp_gu, w_exp_down)
    per_part = dest_chunks.shape[0] // COMBINE_PARTS
    w_tk = w_kt.T
    out = None
    for part in range(COMBINE_PARTS):
        yg = _gather_back(dest_chunks[part * per_part:(part + 1) * per_part], ys, sub=GATHER_CHUNK)
        out = _combine(base, w_tk, ln_g, ln_b, yg, out, part, tc=min(256, t))
    return out


def kernel(x, ln_in_g, ln_in_b, w_in, hg_lb_logits, hg_norm_g, gm_v_norm_g, gm_v_norm_b, gm_w_s, gm_b_s, gm_out_norm_g, w_out, ln1_g, ln1_b, w_router, router_bias, w_exp_gu, w_exp_down, w_shared_gu, w_shared_down, ln2_g, ln2_b):
    bsz, s, d = x.shape
    depth = w_in.shape[0]
    assert depth == 1, "the lower-bound table row used in the hgrn kernel assumes one layer"
    alpha = (2.0 * depth) ** 0.25
    t = bsz * s
    hg_width = hg_norm_g.shape[1]
    gm_width = gm_v_norm_g.shape[1]
    tm = min(512, t)

    x2 = x.reshape(t, d)
    h0, proj, f_pre = _in_proj(x2, ln_in_g, ln_in_b, w_in[0].astype(BF16), hg_width, hg_width, tm)
    proj3 = proj.reshape(bsz, s, proj.shape[1])
    y_hg, y_gm = _mixer(proj3, f_pre.reshape(bsz, s, hg_width), hg_lb_logits.astype(F32),
                        hg_norm_g[0], gm_v_norm_g[0], gm_v_norm_b[0], gm_w_s[0], gm_b_s[0],
                        gm_out_norm_g[0], sb=min(512, s))
    base, h1_packed, eidx, gates, rank, counts = _out_proj(
        y_hg.reshape(t, hg_width), y_gm.reshape(t, gm_width), h0, w_out[0].astype(BF16),
        ln1_g[0], ln1_b[0], w_router[0].T, router_bias[0], w_shared_gu[0].astype(BF16),
        w_shared_down[0].astype(BF16), alpha, tm)
    out = _moe(base, h1_packed, eidx, gates, rank, counts, w_exp_gu[0], w_exp_down[0],
               ln2_g[0], ln2_b[0])
    return out.reshape(bsz, s, d)
```

```python
import functools

import jax
import jax.numpy as jnp
from jax import lax
from jax.experimental import pallas as pl
from jax.experimental.pallas import tpu as pltpu
from jax.experimental.pallas import tpu_sc as plsc

F32 = jnp.float32
BF16 = jnp.bfloat16

LN_EPS = 1e-5
RMS_EPS = 1e-6
CHUNK = 64
SUB = 16
MAX_FACTORED_EXPONENT = 80.0
FACTORED_MIN_LB = float(2.718281828459045 ** (-MAX_FACTORED_EXPONENT / CHUNK))
HEAD_DIM = 128
GM_BLOCK = 128
TOP_K = 8
N_GROUPS = 8
TOPK_GROUPS = 4
ROUTED_SCALE = 2.5
MOE_BLOCK = 128
WEIGHT_SLOTS = 3
RING_SLOTS = 8
RING_DMA_PRIORITY = 1
BLOCKS_PER_STEP = 2
GATHER_AHEAD = 2
TOKEN_SLOTS = 8
TOKEN_ROW = 128
SLOT_CHUNK = 64
COMBINE_PARTS = 2
GATHER_CHUNK = 16
VMEM_LIMIT = 48 * 1024 * 1024
EXPERTS_VMEM_LIMIT = 56 * 1024 * 1024


def _cparams(sem):
    return pltpu.CompilerParams(dimension_semantics=sem, vmem_limit_bytes=VMEM_LIMIT)


def _layer_norm(x, g, b):
    mu = jnp.mean(x, axis=-1, keepdims=True)
    xc = x - mu
    var = jnp.mean(xc * xc, axis=-1, keepdims=True)
    return xc * lax.rsqrt(var + LN_EPS) * g + b


def _silu(x):
    return x * jax.nn.sigmoid(x)


def _gelu(x):
    return 0.5 * x * (1.0 + lax.erf(x * (2.0 ** -0.5)))


def _dot(a, b):
    return jnp.dot(a, b, preferred_element_type=F32)


def _dot_nt(a, b, precision=None):
    return lax.dot_general(a, b, (((1,), (1,)), ((), ())), preferred_element_type=F32,
                           precision=precision)


def _dot_tn(a, b):
    return lax.dot_general(a, b, (((0,), (0,)), ((), ())), preferred_element_type=F32)


def _split_bf16(x):
    hi = x.astype(BF16)
    return hi, (x - hi.astype(F32)).astype(BF16)


def _pack_halves(x):
    n = x.shape[1] // 2
    hi = lax.bitcast_convert_type(x[:, :n].astype(BF16).astype(F32), jnp.uint32)
    lo = lax.bitcast_convert_type(x[:, n:].astype(BF16).astype(F32), jnp.uint32)
    return hi | (lo >> 16)


def _unpack_halves(w):
    a = lax.bitcast_convert_type(w & jnp.uint32(0xFFFF0000), F32)
    b = lax.bitcast_convert_type(w << 16, F32)
    return a, b


def _in_proj_kernel(x_ref, g_ref, b_ref, w_ref, h_ref, p_ref, f_ref, *, f_lo):
    h = _layer_norm(x_ref[...], g_ref[...], b_ref[...])
    h_ref[...] = h
    p = _dot(h.astype(BF16), w_ref[...])
    p_ref[...] = p.astype(BF16)
    f_ref[...] = p[:, f_lo:f_lo + f_ref.shape[1]]


def _in_proj(x2, g, b, w_bf, f_lo, f_width, tm):
    t, d = x2.shape
    n = w_bf.shape[1]
    return pl.pallas_call(
        functools.partial(_in_proj_kernel, f_lo=f_lo),
        grid=(t // tm,),
        in_specs=[pl.BlockSpec((tm, d), lambda i: (i, 0)),
                  pl.BlockSpec((1, d), lambda i: (0, 0)),
                  pl.BlockSpec((1, d), lambda i: (0, 0)),
                  pl.BlockSpec((d, n), lambda i: (0, 0))],
        out_specs=[pl.BlockSpec((tm, d), lambda i: (i, 0)),
                   pl.BlockSpec((tm, n), lambda i: (i, 0)),
                   pl.BlockSpec((tm, f_width), lambda i: (i, 0))],
        out_shape=[jax.ShapeDtypeStruct((t, d), F32), jax.ShapeDtypeStruct((t, n), BF16),
                   jax.ShapeDtypeStruct((t, f_width), F32)],
        compiler_params=_cparams(("parallel",)),
        name="in_proj",
    )(x2, g.reshape(1, d), b.reshape(1, d), w_bf)


def _mixer_kernel(q_ref, f_ref, i_ref, g_ref, lbl_ref, gn_ref, u_ref, v_ref, vg_ref, vb_ref, ws_ref,
                  bs_ref, og_ref, o_ref, ogm_ref, st_ref, *, n_chunks):
    @pl.when(pl.program_id(1) == 0)
    def _():
        st_ref[...] = jnp.zeros_like(st_ref)

    lg = lbl_ref[...]
    ex = jnp.exp(lg - jnp.max(lg, axis=0, keepdims=True))
    lb_all = ex[0:1, :] / jnp.sum(ex, axis=0, keepdims=True)
    gn_all = gn_ref[...]

    c = CHUNK
    heads = st_ref.shape[0]
    row = lax.broadcasted_iota(jnp.int32, (c, c), 0)
    col = lax.broadcasted_iota(jnp.int32, (c, c), 1)
    tril = (row >= col).astype(BF16)
    sub_row = lax.broadcasted_iota(jnp.int32, (SUB, HEAD_DIM), 0)
    ones = jnp.ones((HEAD_DIM, HEAD_DIM), BF16)
    neg_inf = jnp.float32(-jnp.inf)

    def intra_pairwise(q, k, v, v_bf, b, o_inter):
        outs = []
        for blk in range(c // SUB):
            lo = blk * SUB
            b_i = b[lo:lo + SUB, :]
            q_i = q[lo:lo + SUB, :]
            k_i = k[lo:lo + SUB, :]
            v_i = v[lo:lo + SUB, :]
            prods = []
            for j in range(SUB):
                diff = jnp.where(sub_row >= j, b_i - b_i[j:j + 1, :], neg_inf)
                prods.append(q_i * k_i[j:j + 1, :] * jnp.exp(diff))
            p_all = jnp.concatenate(prods, axis=0).astype(BF16)
            r_all = _dot(p_all, ones)
            o_blk = o_inter[lo:lo + SUB, :]
            for j in range(SUB):
                o_blk = o_blk + r_all[j * SUB:(j + 1) * SUB, :] * v_i[j:j + 1, :]
            if blk > 0:
                b_ref = b[lo - 1:lo, :]
                q_hat = (q_i * jnp.exp(b_i - b_ref)).astype(BF16)
                k_hat = (k[0:lo, :] * jnp.exp(b_ref - b[0:lo, :])).astype(BF16)
                scores = _dot_nt(q_hat, k_hat)
                o_blk = o_blk + _dot(scores.astype(BF16), v_bf[0:lo, :])
            outs.append(o_blk)
        return jnp.concatenate(outs, axis=0)

    def one_chunk(r0, factored):
        rows = pl.ds(r0, c)
        hs = [slice(h * HEAD_DIM, (h + 1) * HEAD_DIM) for h in range(heads)]
        q = _silu(q_ref[rows, :].astype(F32))
        f = lb_all + (1.0 - lb_all) * jax.nn.sigmoid(f_ref[rows, :])
        v_bf = i_ref[rows, :]
        v = v_bf.astype(F32)
        lf = jnp.log(f)
        k = 1.0 - f
        lf_hi, lf_lo = _split_bf16(lf)
        b2 = _dot(tril, jnp.concatenate([lf_hi, lf_lo], axis=1))
        width = lf.shape[1]
        b = b2[:, :width] + b2[:, width:]
        b_last = b[c - 1:c, :]
        q_dec = (q * jnp.exp(b)).astype(BF16)
        k_dec = (k * jnp.exp(b_last - b)).astype(BF16)
        decay = jnp.exp(b_last)
        st = [st_ref[h] for h in range(heads)]
        st_bf = [s.astype(BF16) for s in st]
        if factored:
            k_grow = (k * jnp.exp(-b)).astype(BF16)
            both = [_dot_nt(q_dec[:, hs[h]], jnp.concatenate([st_bf[h], k_grow[:, hs[h]]], axis=0))
                    for h in range(heads)]
            scores = [jnp.where(row >= col, both[h][:, HEAD_DIM:], 0.0).astype(BF16)
                      for h in range(heads)]
            outs = [both[h][:, :HEAD_DIM] + _dot(scores[h], v_bf[:, hs[h]]) for h in range(heads)]
        else:
            outs = [intra_pairwise(q[:, hs[h]], k[:, hs[h]], v[:, hs[h]], v_bf[:, hs[h]],
                                   b[:, hs[h]], _dot_nt(q_dec[:, hs[h]], st_bf[h]))
                    for h in range(heads)]
        for h in range(heads):
            st_ref[h] = st[h] * decay[:, hs[h]] + _dot_tn(v_bf[:, hs[h]], k_dec[:, hs[h]])
        inv = [lax.rsqrt(jnp.mean(o * o, axis=-1, keepdims=True) + RMS_EPS) for o in outs]
        o = jnp.concatenate([outs[h] * inv[h] for h in range(heads)], axis=1)
        o_ref[rows, :] = (o * gn_all * _silu(g_ref[rows, :].astype(F32))).astype(o_ref.dtype)

    groups = ws_ref.shape[0]
    gdim = u_ref.shape[-1] // groups
    gm_row = lax.broadcasted_iota(jnp.int32, (GM_BLOCK, GM_BLOCK), 0) // CHUNK
    gm_col = lax.broadcasted_iota(jnp.int32, (GM_BLOCK, GM_BLOCK), 1) // CHUNK
    ws = [jnp.where(gm_row >= gm_col, ws_ref[g], 0.0).astype(BF16) for g in range(groups)]

    def gmlp_block(r0):
        rows = pl.ds(r0, GM_BLOCK)
        u = _gelu(u_ref[rows, :].astype(F32))
        v = _layer_norm(_gelu(v_ref[rows, :].astype(F32)), vg_ref[...], vb_ref[...])
        v_bf = v.astype(BF16)
        mixed = jnp.concatenate(
            [_dot(ws[g], v_bf[:, g * gdim:(g + 1) * gdim]) for g in range(groups)], axis=1)
        y = u * (mixed + bs_ref[...])
        inv = lax.rsqrt(jnp.mean(y * y, axis=-1, keepdims=True) + RMS_EPS)
        ogm_ref[rows, :] = (y * inv * og_ref[...]).astype(ogm_ref.dtype)

    def run_chunks(factored):
        per_trip = 4 if (factored and n_chunks % 4 == 0) else 1
        trip_rows = per_trip * c
        inline_gmlp = trip_rows % GM_BLOCK == 0

        def chunk_body(ci, carry):
            for u in range(per_trip):
                one_chunk(pl.multiple_of((ci * per_trip + u) * c, c), factored)
            if inline_gmlp:
                for n in range(trip_rows // GM_BLOCK):
                    gmlp_block(pl.multiple_of(ci * trip_rows + n * GM_BLOCK, GM_BLOCK))
            return carry

        lax.fori_loop(0, n_chunks // per_trip, chunk_body, 0)
        if not inline_gmlp:
            for n in range(n_chunks * c // GM_BLOCK):
                gmlp_block(n * GM_BLOCK)

    bounded = jnp.min(lb_all) >= FACTORED_MIN_LB
    pl.when(bounded)(lambda: run_chunks(True))
    pl.when(jnp.logical_not(bounded))(lambda: run_chunks(False))


def _mixer(proj3, f_pre3, lb_logits, norm_g, v_norm_g, v_norm_b, w_s, b_s, out_norm_g, sb):
    bsz, s, _ = proj3.shape
    hg_width = norm_g.shape[0]
    heads = hg_width // HEAD_DIM
    gw = v_norm_g.shape[0]
    assert gw == hg_width and sb % GM_BLOCK == 0
    groups = w_s.shape[0]
    bias_full = jnp.repeat(b_s.T, gw // groups, axis=1)

    def sect(k):
        return pl.BlockSpec((None, sb, hg_width), lambda b, j, k=k: (b, j, k))

    row_vec = pl.BlockSpec((1, hg_width), lambda b, j: (0, 0))
    return pl.pallas_call(
        functools.partial(_mixer_kernel, n_chunks=sb // CHUNK),
        grid=(bsz, s // sb),
        in_specs=[sect(0), sect(0), sect(2), sect(3),
                  pl.BlockSpec((lb_logits.shape[0], hg_width), lambda b, j: (0, 0)),
                  row_vec, sect(4), sect(5), row_vec, row_vec,
                  pl.BlockSpec((groups, GM_BLOCK, GM_BLOCK), lambda b, j: (0, 0, 0)),
                  pl.BlockSpec((GM_BLOCK, gw), lambda b, j: (0, 0)),
                  row_vec],
        out_specs=[pl.BlockSpec((None, sb, hg_width), lambda b, j: (b, j, 0)),
                   pl.BlockSpec((None, sb, gw), lambda b, j: (b, j, 0))],
        out_shape=[jax.ShapeDtypeStruct((bsz, s, hg_width), BF16),
                   jax.ShapeDtypeStruct((bsz, s, gw), BF16)],
        scratch_shapes=[pltpu.VMEM((heads, HEAD_DIM, HEAD_DIM), F32)],
        compiler_params=_cparams(("parallel", "arbitrary")),
        name="mixer",
    )(proj3, f_pre3, proj3, proj3, lb_logits, norm_g.reshape(1, hg_width), proj3, proj3,
      v_norm_g.reshape(1, gw), v_norm_b.reshape(1, gw), w_s, bias_full, out_norm_g.reshape(1, gw))


def _out_proj_kernel(yh_ref, yg_ref, h0_ref, wa_ref, wb_ref, g_ref, b_ref, wr_ref, rb_ref,
                     wsg_ref, wsd_ref, base_ref, hp_ref, eidx_ref, gate_ref, rank_ref, cnt_ref,
                     carry_ref, *, alpha):
    @pl.when(pl.program_id(0) == 0)
    def _():
        carry_ref[...] = jnp.zeros_like(carry_ref)

    mix = _dot(yh_ref[...], wa_ref[...]) + _dot(yg_ref[...], wb_ref[...])
    h1 = _layer_norm(alpha * h0_ref[...] + mix, g_ref[...], b_ref[...])
    hp_ref[...] = _pack_halves(h1)
    w_hi, w_lo = _split_bf16(wr_ref[...])
    h_hi, h_lo = _split_bf16(h1)
    logits = _dot_nt(w_hi, h_hi) + (_dot_nt(w_hi, h_lo) + _dot_nt(w_lo, h_hi))
    gu = _dot(h_hi, wsg_ref[...])
    half = gu.shape[1] // 2
    hid = (_silu(gu[:, :half]) * gu[:, half:]).astype(BF16)
    base_ref[...] = alpha * h1 + _dot(hid, wsd_ref[...])
    eidx, gates, rank, total = _route_tile(logits, rb_ref[...], carry_ref[...])
    eidx_ref[...] = eidx
    gate_ref[...] = gates
    rank_ref[...] = rank
    carry_ref[...] = total
    cnt_ref[...] = total.astype(jnp.int32)


def _out_proj(y_hg, y_gm, h0, w_out_bf, g, b, w_router_t, router_bias, ws_gu_bf, ws_down_bf, alpha, tm):
    t, d = h0.shape
    sg = ws_gu_bf.shape[1]
    sd = ws_down_bf.shape[0]
    hw = y_hg.shape[1]
    gw = y_gm.shape[1]
    e = w_router_t.shape[0]
    return pl.pallas_call(
        functools.partial(_out_proj_kernel, alpha=alpha),
        grid=(t // tm,),
        in_specs=[pl.BlockSpec((tm, hw), lambda i: (i, 0)),
                  pl.BlockSpec((tm, gw), lambda i: (i, 0)),
                  pl.BlockSpec((tm, d), lambda i: (i, 0)),
                  pl.BlockSpec((hw, d), lambda i: (0, 0)),
                  pl.BlockSpec((gw, d), lambda i: (0, 0)),
                  pl.BlockSpec((1, d), lambda i: (0, 0)),
                  pl.BlockSpec((1, d), lambda i: (0, 0)),
                  pl.BlockSpec((e, d), lambda i: (0, 0)),
                  pl.BlockSpec((e, 1), lambda i: (0, 0)),
                  pl.BlockSpec((d, sg), lambda i: (0, 0)),
                  pl.BlockSpec((sd, d), lambda i: (0, 0))],
        out_specs=[pl.BlockSpec((tm, d), lambda i: (i, 0)),
                   pl.BlockSpec((tm, d // 2), lambda i: (i, 0)),
                   pl.BlockSpec((TOP_K, tm), lambda i: (0, i)),
                   pl.BlockSpec((TOP_K, tm), lambda i: (0, i)),
                   pl.BlockSpec((TOP_K, tm), lambda i: (0, i)),
                   pl.BlockSpec((e, 1), lambda i: (0, 0))],
        out_shape=[jax.ShapeDtypeStruct((t, d), F32), jax.ShapeDtypeStruct((t, d // 2), jnp.uint32),
                   jax.ShapeDtypeStruct((TOP_K, t), jnp.int32),
                   jax.ShapeDtypeStruct((TOP_K, t), F32),
                   jax.ShapeDtypeStruct((TOP_K, t), jnp.int32),
                   jax.ShapeDtypeStruct((e, 1), jnp.int32)],
        scratch_shapes=[pltpu.VMEM((e, 1), F32)],
        compiler_params=_cparams(("arbitrary",)),
        name="out_proj",
    )(y_hg, y_gm, h0, w_out_bf[:hw], w_out_bf[hw:], g.reshape(1, d), b.reshape(1, d), w_router_t,
      router_bias.astype(F32).reshape(e, 1), ws_gu_bf, ws_down_bf)


def _route_tile(logits, bias, carry):
    n_exp, tt = logits.shape
    per_group = n_exp // N_GROUPS
    neg_inf = jnp.float32(-jnp.inf)
    scores = jax.nn.sigmoid(logits)
    biased = scores + bias

    gio = lax.broadcasted_iota(jnp.int32, (per_group, tt), 0)
    blocks, gs_rows = [], []
    for g in range(N_GROUPS):
        blk = biased[g * per_group:(g + 1) * per_group, :]
        m1 = jnp.max(blk, axis=0, keepdims=True)
        first = jnp.min(jnp.where(blk == m1, gio, per_group), axis=0, keepdims=True)
        m2 = jnp.max(jnp.where(gio == first, neg_inf, blk), axis=0, keepdims=True)
        blocks.append(blk)
        gs_rows.append(m1 + m2)
    gs = jnp.concatenate(gs_rows, axis=0)
    gidx = lax.broadcasted_iota(jnp.int32, (N_GROUPS, tt), 0)
    beaten = jnp.zeros((N_GROUPS, tt), jnp.int32)
    for g in range(N_GROUPS):
        r = gs_rows[g]
        beaten = beaten + jnp.where((r > gs) | ((r == gs) & (g < gidx)), 1, 0)
    keep = beaten < TOPK_GROUPS
    cand = jnp.concatenate(
        [jnp.where(keep[g:g + 1, :], blocks[g], neg_inf) for g in range(N_GROUPS)], axis=0)

    eio = lax.broadcasted_iota(jnp.int32, (n_exp, tt), 0)
    idx_rows, w_rows = [], []
    chosen = jnp.zeros((n_exp, tt), F32)
    for _ in range(TOP_K):
        m = jnp.max(cand, axis=0, keepdims=True)
        idx = jnp.min(jnp.where(cand == m, eio, n_exp), axis=0, keepdims=True)
        sel = eio == idx
        w_rows.append(jnp.sum(jnp.where(sel, scores, 0.0), axis=0, keepdims=True))
        idx_rows.append(idx)
        cand = jnp.where(sel, neg_inf, cand)
        chosen = jnp.where(sel, 1.0, chosen)
    eidx = jnp.concatenate(idx_rows, axis=0)
    w = jnp.concatenate(w_rows, axis=0)
    gates = w / jnp.sum(w, axis=0, keepdims=True) * ROUTED_SCALE

    tr = lax.broadcasted_iota(jnp.int32, (tt, tt), 0)
    tc = lax.broadcasted_iota(jnp.int32, (tt, tt), 1)
    before = (tr < tc).astype(BF16)
    cum = _dot(chosen.astype(BF16), before) + carry
    rank = jnp.concatenate(
        [jnp.sum(jnp.where(eio == idx_rows[k], cum, 0.0), axis=0, keepdims=True)
         for k in range(TOP_K)], axis=0).astype(jnp.int32)
    total = carry + jnp.sum(chosen, axis=1, keepdims=True)
    return eidx, gates, rank, total


def _chunked(table, chunk):
    k, t = table.shape
    return table.reshape(k, t // chunk, chunk).transpose(1, 0, 2)


def _sc_workers():
    info = plsc.get_sparse_core_info()
    return info.num_cores, info.num_cores * info.num_subcores


def _slot_tables(eidx, rank, pad_start, n_rows, chunk):
    k_top, t = eidx.shape
    n_exp = pad_start.shape[0]
    n_cores, n_workers = _sc_workers()
    n_chunks = t // chunk
    per_worker = n_chunks // n_workers
    lanes = plsc.get_sparse_core_info().num_lanes
    mesh = plsc.VectorSubcoreMesh(core_axis_name="c", subcore_axis_name="s")

    @functools.partial(
        pl.kernel, mesh=mesh,
        out_type=(jax.ShapeDtypeStruct((n_rows, TOKEN_ROW), jnp.int32),
                  jax.ShapeDtypeStruct((n_chunks, k_top, chunk), jnp.int32)),
        scratch_types=[pltpu.VMEM((n_exp,), jnp.int32),
                       pltpu.VMEM((k_top, chunk), jnp.int32),
                       pltpu.VMEM((k_top, chunk), jnp.int32),
                       pltpu.VMEM((k_top, chunk), jnp.int32),
                       pltpu.VMEM((chunk, TOKEN_ROW), jnp.int32),
                       pltpu.SemaphoreType.DMA],
        compiler_params=pltpu.CompilerParams(needs_layout_passes=False),
        name="slot_tables",
    )
    def build(e_hbm, r_hbm, ps_hbm, tok_hbm, d_hbm, ps_v, e_v, r_v, idx_v, rows_v, sem):
        wid = lax.axis_index("s") * n_cores + lax.axis_index("c")
        pltpu.sync_copy(ps_hbm, ps_v)

        @pl.loop(0, per_worker)
        def _(j):
            c = wid * per_worker + j
            pltpu.sync_copy(e_hbm.at[c], e_v)
            pltpu.sync_copy(r_hbm.at[c], r_v)
            for k in range(k_top):
                for s in range(chunk // lanes):
                    sl = pl.ds(s * lanes, lanes)
                    idx_v[k, sl] = plsc.load_gather(ps_v, [e_v[k, sl]]) + r_v[k, sl]
            for i in range(chunk):
                rows_v[i, pl.ds(0, lanes)] = jnp.full((lanes,), c * chunk + i, jnp.int32)
            pltpu.sync_copy(idx_v, d_hbm.at[c])
            copies = [pltpu.async_copy(rows_v, tok_hbm.at[idx_v.at[k]], sem) for k in range(k_top)]
            for cp in copies:
                cp.wait()

    return build(_chunked(eidx, chunk), _chunked(rank, chunk), pad_start)


def _gather_back(dest_chunks, ys, sub):
    n_chunks, k_top, chunk = dest_chunks.shape
    t = n_chunks * chunk
    dh = ys.shape[1]
    n_cores, n_workers = _sc_workers()
    per_worker = n_chunks // n_workers
    mesh = plsc.VectorSubcoreMesh(core_axis_name="c", subcore_axis_name="s")

    @functools.partial(
        pl.kernel, mesh=mesh,
        out_type=jax.ShapeDtypeStruct((k_top, t, dh), ys.dtype),
        scratch_types=[pltpu.VMEM((k_top, chunk), jnp.int32),
                       pltpu.VMEM((k_top, sub, dh), ys.dtype),
                       pltpu.SemaphoreType.DMA],
        name="gather_back",
    )
    def gather_rows(ys_hbm, dest_hbm, o_hbm, idx_v, rows_v, sem):
        wid = lax.axis_index("s") * n_cores + lax.axis_index("c")

        @pl.loop(0, per_worker)
        def _(j):
            c = wid * per_worker + j
            pltpu.sync_copy(dest_hbm.at[c], idx_v)
            for q in range(chunk // sub):
                copies = [pltpu.async_copy(ys_hbm.at[idx_v.at[k, pl.ds(q * sub, sub)]], rows_v.at[k], sem)
                          for k in range(k_top)]
                for cp in copies:
                    cp.wait()
                for k in range(k_top):
                    pltpu.sync_copy(rows_v.at[k], o_hbm.at[k, pl.ds(c * chunk + q * sub, sub)])

    return gather_rows(ys, dest_chunks)


def _experts_kernel(bs_ref, cnt_ref, wgu_hbm, wd_hbm, h_hbm, tok_hbm, ys_hbm, wgu_buf, wd_buf,
                    h_v, xg, ybuf, tokbuf, wsem, hsem, tsem, ysem):
    e = pl.program_id(0)
    n_exp = pl.num_programs(0)
    n_slots, mb, _ = ybuf.shape
    w_slots = wgu_buf.shape[0]
    t_slots = tokbuf.shape[0]
    n_tok = h_v.shape[0]
    n_blocks = bs_ref[n_exp]
    b0 = bs_ref[e]
    b1 = bs_ref[e + 1]
    count = cnt_ref[e]

    def w_copies(ex):
        slot = lax.rem(ex, w_slots)
        return (pltpu.make_async_copy(wgu_hbm.at[ex], wgu_buf.at[slot], wsem.at[slot]),
                pltpu.make_async_copy(wd_hbm.at[ex], wd_buf.at[slot], wsem.at[slot]))

    def tok_copy(b):
        src = jnp.minimum(b, n_blocks - 1)
        slot = lax.rem(b, t_slots)
        return pltpu.make_async_copy(tok_hbm.at[pl.ds(src, 1)], tokbuf.at[slot], tsem.at[slot])

    def y_copy(b):
        slot = lax.rem(b, n_slots)
        return pltpu.make_async_copy(ybuf.at[slot],
                                     ys_hbm.at[pl.ds(pl.multiple_of(b * mb, mb), mb)], ysem.at[slot])

    def gather_rows(b):
        ids = lax.rem(b, t_slots)
        dst = lax.rem(b, xg.shape[0])
        for i in range(mb):
            xg[dst, pl.ds(i, 1), :] = h_v[pl.ds(tokbuf[ids, 0, i], 1), :]

    @pl.when(e == 0)
    def _():
        resident = pltpu.make_async_copy(h_hbm, h_v, hsem)
        resident.start()
        for i in range(w_slots - 1):
            @pl.when(i < n_exp)
            def _():
                for cp in w_copies(i):
                    cp.start()
        for i in range(t_slots - 1):
            tok_copy(i).start()
        resident.wait()
        for i in range(GATHER_AHEAD):
            tok_copy(i).wait()
            gather_rows(i)

    @pl.when(e + w_slots - 1 < n_exp)
    def _():
        for cp in w_copies(e + w_slots - 1):
            cp.start()

    for cp in w_copies(e):
        cp.wait()
    w_slot = lax.rem(e, w_slots)

    kh = wgu_buf.shape[1] // 2
    half = wgu_buf.shape[2] // 2

    def compute(blocks):
        rows = lax.broadcasted_iota(jnp.int32, (mb, xg.shape[2]), 0)
        halves = []
        for b in blocks:
            valid = count - (b - b0) * mb
            words = jnp.where(rows < valid, xg[lax.rem(b, xg.shape[0])], jnp.uint32(0))
            xa, xb = _unpack_halves(words)
            halves.append((xa.astype(BF16), xb.astype(BF16)))
        w_a = wgu_buf[w_slot, 0:kh, :].astype(BF16)
        w_b = wgu_buf[w_slot, kh:, :].astype(BF16)
        gus = [_dot(xa, w_a) + _dot(xb, w_b) for xa, xb in halves]
        hids = [(_silu(gu[:, :half]) * gu[:, half:]).astype(BF16) for gu in gus]
        w_d = wd_buf[w_slot].astype(BF16)
        outs = [_dot(hid, w_d) for hid in hids]
        for b, y in zip(blocks, outs):
            ybuf[lax.rem(b, n_slots)] = _pack_halves(y)

    def step(b, width):
        blocks = [b + i for i in range(width)]
        for bb in blocks:
            tok_copy(bb + t_slots - 1).start()
            tok_copy(bb + GATHER_AHEAD).wait()
        for bb in blocks:
            @pl.when(bb >= n_slots)
            def _():
                y_copy(bb - n_slots).wait()
        compute(blocks)
        for bb in blocks:
            gather_rows(bb + GATHER_AHEAD)
        for bb in blocks:
            y_copy(bb).start(priority=RING_DMA_PRIORITY)

    n_pairs = (b1 - b0) // 2

    def pair(p, carry):
        step(b0 + 2 * p, 2)
        return carry

    lax.fori_loop(0, n_pairs, pair, 0)

    @pl.when(b0 + 2 * n_pairs < b1)
    def _():
        step(b1 - 1, 1)

    @pl.when(e == n_exp - 1)
    def _():
        for i in range(1, n_slots + 1):
            @pl.when(n_blocks >= i)
            def _():
                y_copy(n_blocks - i).wait()
        for i in range(GATHER_AHEAD, t_slots - 1):
            tok_copy(n_blocks + i).wait()


def _experts(block_start, counts, h1_packed, tok_ids, w_gu, w_down):
    t, dh = h1_packed.shape
    n_rows = tok_ids.shape[0] * tok_ids.shape[1]
    n_exp, d, gu_w = w_gu.shape
    ed = w_down.shape[1]
    return pl.pallas_call(
        _experts_kernel,
        grid_spec=pltpu.PrefetchScalarGridSpec(
            num_scalar_prefetch=2,
            grid=(n_exp,),
            in_specs=[pl.BlockSpec(memory_space=pl.ANY)] * 4,
            out_specs=pl.BlockSpec(memory_space=pl.ANY),
            scratch_shapes=[pltpu.VMEM((WEIGHT_SLOTS, d, gu_w), w_gu.dtype),
                            pltpu.VMEM((WEIGHT_SLOTS, ed, d), w_down.dtype),
                            pltpu.VMEM((t, dh), h1_packed.dtype),
                            pltpu.VMEM((GATHER_AHEAD + BLOCKS_PER_STEP, MOE_BLOCK, dh), jnp.uint32),
                            pltpu.VMEM((RING_SLOTS, MOE_BLOCK, dh), jnp.uint32),
                            pltpu.SMEM((TOKEN_SLOTS, 1, MOE_BLOCK), jnp.int32),
                            pltpu.SemaphoreType.DMA((WEIGHT_SLOTS,)),
                            pltpu.SemaphoreType.DMA,
                            pltpu.SemaphoreType.DMA((TOKEN_SLOTS,)),
                            pltpu.SemaphoreType.DMA((RING_SLOTS,))]),
        out_shape=jax.ShapeDtypeStruct((n_rows, dh), jnp.uint32),
        compiler_params=pltpu.CompilerParams(dimension_semantics=("arbitrary",),
                                             vmem_limit_bytes=EXPERTS_VMEM_LIMIT),
        name="experts",
    )(block_start, counts, w_gu, w_down, h1_packed, tok_ids)


def _combine_kernel(base_ref, w_ref, g_ref, b_ref, yg_ref, *rest):
    o_ref = rest[-1]
    acc = base_ref[...]
    dh = acc.shape[1] // 2
    acc_a = acc[:, :dh]
    acc_b = acc[:, dh:]
    w = w_ref[...]
    for k in range(TOP_K):
        ya, yb = _unpack_halves(yg_ref[k])
        acc_a = acc_a + ya * w[:, k:k + 1]
        acc_b = acc_b + yb * w[:, k:k + 1]
    o_ref[...] = _layer_norm(jnp.concatenate([acc_a, acc_b], axis=1), g_ref[...], b_ref[...])


def _combine(base, w_tk, g, b, yg, prev, part, tc):
    t, d = base.shape
    tiles = yg.shape[1] // tc
    off = part * tiles
    in_specs = [pl.BlockSpec((tc, d), lambda i: (i + off, 0)),
                pl.BlockSpec((tc, TOP_K), lambda i: (i + off, 0)),
                pl.BlockSpec((1, d), lambda i: (0, 0)),
                pl.BlockSpec((1, d), lambda i: (0, 0)),
                pl.BlockSpec((TOP_K, tc, yg.shape[2]), lambda i: (0, i, 0))]
    args = [base, w_tk, g.reshape(1, d), b.reshape(1, d), yg]
    aliases = {}
    if prev is not None:
        in_specs.append(pl.BlockSpec(memory_space=pl.ANY))
        args.append(prev)
        aliases = {len(args) - 1: 0}
    return pl.pallas_call(
        _combine_kernel,
        grid=(tiles,),
        in_specs=in_specs,
        out_specs=pl.BlockSpec((tc, d), lambda i: (i + off, 0)),
        out_shape=jax.ShapeDtypeStruct((t, d), F32),
        input_output_aliases=aliases,
        compiler_params=_cparams(("parallel",)),
        name="combine",
    )(*args)


def _moe(base, h1_packed, eidx, w_kt, rank, counts, w_exp_gu, w_exp_down, ln_g, ln_b):
    t, d = base.shape
    n_exp = counts.shape[0]

    counts = counts.reshape(n_exp)
    blocks_per_expert = (counts + MOE_BLOCK - 1) // MOE_BLOCK
    block_start = jnp.concatenate(
        [jnp.zeros((1,), jnp.int32), jnp.cumsum(blocks_per_expert).astype(jnp.int32)])
    pad_start = block_start[:n_exp] * MOE_BLOCK
    n_blocks = -(-(t * TOP_K + n_exp * (MOE_BLOCK - 1)) // MOE_BLOCK)

    tok_rows, dest_chunks = _slot_tables(eidx, rank, pad_start, n_blocks * MOE_BLOCK, chunk=SLOT_CHUNK)
    tok_ids = jnp.clip(tok_rows[:, 0], 0, t - 1).reshape(n_blocks, MOE_BLOCK)
    ys = _experts(block_start, counts, h1_packed, tok_ids, w_exp_gu, w_exp_down)
    per_part = dest_chunks.shape[0] // COMBINE_PARTS
    w_tk = w_kt.T
    out = None
    for part in range(COMBINE_PARTS):
        yg = _gather_back(dest_chunks[part * per_part:(part + 1) * per_part], ys, sub=GATHER_CHUNK)
        out = _combine(base, w_tk, ln_g, ln_b, yg, out, part, tc=min(256, t))
    return out


def kernel(x, ln_in_g, ln_in_b, w_in, hg_lb_logits, hg_norm_g, gm_v_norm_g, gm_v_norm_b, gm_w_s, gm_b_s, gm_out_norm_g, w_out, ln1_g, ln1_b, w_router, router_bias, w_exp_gu, w_exp_down, w_shared_gu, w_shared_down, ln2_g, ln2_b):
    bsz, s, d = x.shape
    depth = w_in.shape[0]
    assert depth == 1, "the lower-bound table row used in the hgrn kernel assumes one layer"
    alpha = (2.0 * depth) ** 0.25
    t = bsz * s
    hg_width = hg_norm_g.shape[1]
    gm_width = gm_v_norm_g.shape[1]
    tm = min(512, t)

    x2 = x.reshape(t, d)
    h0, proj, f_pre = _in_proj(x2, ln_in_g, ln_in_b, w_in[0].astype(BF16), hg_width, hg_width, tm)
    proj3 = proj.reshape(bsz, s, proj.shape[1])
    y_hg, y_gm = _mixer(proj3, f_pre.reshape(bsz, s, hg_width), hg_lb_logits.astype(F32),
                        hg_norm_g[0], gm_v_norm_g[0], gm_v_norm_b[0], gm_w_s[0], gm_b_s[0],
                        gm_out_norm_g[0], sb=min(512, s))
    base, h1_packed, eidx, gates, rank, counts = _out_proj(
        y_hg.reshape(t, hg_width), y_gm.reshape(t, gm_width), h0, w_out[0].astype(BF16),
        ln1_g[0], ln1_b[0], w_router[0].T, router_bias[0], w_shared_gu[0].astype(BF16),
        w_shared_down[0].astype(BF16), alpha, tm)
    out = _moe(base, h1_packed, eidx, gates, rank, counts, w_exp_gu[0], w_exp_down[0],
               ln2_g[0], ln2_b[0])
    return out.reshape(bsz, s, d)
```

```python
import functools

import jax
import jax.numpy as jnp
from jax import lax
from jax.experimental import pallas as pl
from jax.experimental.pallas import tpu as pltpu
from jax.experimental.pallas import tpu_sc as plsc

F32 = jnp.float32
BF16 = jnp.bfloat16

LN_EPS = 1e-5
RMS_EPS = 1e-6
CHUNK = 64
SUB = 16
MAX_FACTORED_EXPONENT = 80.0
FACTORED_MIN_LB = float(2.718281828459045 ** (-MAX_FACTORED_EXPONENT / CHUNK))
HEAD_DIM = 128
GM_BLOCK = 128
TOP_K = 8
N_GROUPS = 8
TOPK_GROUPS = 4
ROUTED_SCALE = 2.5
MOE_BLOCK = 128
WEIGHT_SLOTS = 3
RING_SLOTS = 8
RING_DMA_PRIORITY = 1
BLOCKS_PER_STEP = 2
GATHER_AHEAD = 2
TOKEN_SLOTS = 8
TOKEN_ROW = 128
SLOT_CHUNK = 64
COMBINE_PARTS = 2
GATHER_CHUNK = 16
VMEM_LIMIT = 48 * 1024 * 1024
EXPERTS_VMEM_LIMIT = 56 * 1024 * 1024


def _cparams(sem):
    return pltpu.CompilerParams(dimension_semantics=sem, vmem_limit_bytes=VMEM_LIMIT)


def _layer_norm(x, g, b):
    mu = jnp.mean(x, axis=-1, keepdims=True)
    xc = x - mu
    var = jnp.mean(xc * xc, axis=-1, keepdims=True)
    return xc * lax.rsqrt(var + LN_EPS) * g + b


def _silu(x):
    return x * jax.nn.sigmoid(x)


def _gelu(x):
    return 0.5 * x * (1.0 + lax.erf(x * (2.0 ** -0.5)))


def _dot(a, b):
    return jnp.dot(a, b, preferred_element_type=F32)


def _dot_nt(a, b, precision=None):
    return lax.dot_general(a, b, (((1,), (1,)), ((), ())), preferred_element_type=F32,
                           precision=precision)


def _dot_tn(a, b):
    return lax.dot_general(a, b, (((0,), (0,)), ((), ())), preferred_element_type=F32)


def _split_bf16(x):
    hi = x.astype(BF16)
    return hi, (x - hi.astype(F32)).astype(BF16)


def _pack_halves(x):
    n = x.shape[1] // 2
    hi = lax.bitcast_convert_type(x[:, :n].astype(BF16).astype(F32), jnp.uint32)
    lo = lax.bitcast_convert_type(x[:, n:].astype(BF16).astype(F32), jnp.uint32)
    return hi | (lo >> 16)


def _unpack_halves(w):
    a = lax.bitcast_convert_type(w & jnp.uint32(0xFFFF0000), F32)
    b = lax.bitcast_convert_type(w << 16, F32)
    return a, b


def _in_proj_kernel(x_ref, g_ref, b_ref, w_ref, h_ref, p_ref, f_ref, *, f_lo):
    h = _layer_norm(x_ref[...], g_ref[...], b_ref[...])
    h_ref[...] = h
    p = _dot(h.astype(BF16), w_ref[...])
    p_ref[...] = p.astype(BF16)
    f_ref[...] = p[:, f_lo:f_lo + f_ref.shape[1]]


def _in_proj(x2, g, b, w_bf, f_lo, f_width, tm):
    t, d = x2.shape
    n = w_bf.shape[1]
    return pl.pallas_call(
        functools.partial(_in_proj_kernel, f_lo=f_lo),
        grid=(t // tm,),
        in_specs=[pl.BlockSpec((tm, d), lambda i: (i, 0)),
                  pl.BlockSpec((1, d), lambda i: (0, 0)),
                  pl.BlockSpec((1, d), lambda i: (0, 0)),
                  pl.BlockSpec((d, n), lambda i: (0, 0))],
        out_specs=[pl.BlockSpec((tm, d), lambda i: (i, 0)),
                   pl.BlockSpec((tm, n), lambda i: (i, 0)),
                   pl.BlockSpec((tm, f_width), lambda i: (i, 0))],
        out_shape=[jax.ShapeDtypeStruct((t, d), F32), jax.ShapeDtypeStruct((t, n), BF16),
                   jax.ShapeDtypeStruct((t, f_width), F32)],
        compiler_params=_cparams(("parallel",)),
        name="in_proj",
    )(x2, g.reshape(1, d), b.reshape(1, d), w_bf)


def _mixer_kernel(q_ref, f_ref, i_ref, g_ref, lbl_ref, gn_ref, u_ref, v_ref, vg_ref, vb_ref, ws_ref,
                  bs_ref, og_ref, o_ref, ogm_ref, st_ref, *, n_chunks):
    @pl.when(pl.program_id(1) == 0)
    def _():
        st_ref[...] = jnp.zeros_like(st_ref)

    lg = lbl_ref[...]
    ex = jnp.exp(lg - jnp.max(lg, axis=0, keepdims=True))
    lb_all = ex[0:1, :] / jnp.sum(ex, axis=0, keepdims=True)
    gn_all = gn_ref[...]

    c = CHUNK
    heads = st_ref.shape[0]
    row = lax.broadcasted_iota(jnp.int32, (c, c), 0)
    col = lax.broadcasted_iota(jnp.int32, (c, c), 1)
    tril = (row >= col).astype(BF16)
    sub_row = lax.broadcasted_iota(jnp.int32, (SUB, HEAD_DIM), 0)
    ones = jnp.ones((HEAD_DIM, HEAD_DIM), BF16)
    neg_inf = jnp.float32(-jnp.inf)

    def intra_pairwise(q, k, v, v_bf, b, o_inter):
        outs = []
        for blk in range(c // SUB):
            lo = blk * SUB
            b_i = b[lo:lo + SUB, :]
            q_i = q[lo:lo + SUB, :]
            k_i = k[lo:lo + SUB, :]
            v_i = v[lo:lo + SUB, :]
            prods = []
            for j in range(SUB):
                diff = jnp.where(sub_row >= j, b_i - b_i[j:j + 1, :], neg_inf)
                prods.append(q_i * k_i[j:j + 1, :] * jnp.exp(diff))
            p_all = jnp.concatenate(prods, axis=0).astype(BF16)
            r_all = _dot(p_all, ones)
            o_blk = o_inter[lo:lo + SUB, :]
            for j in range(SUB):
                o_blk = o_blk + r_all[j * SUB:(j + 1) * SUB, :] * v_i[j:j + 1, :]
            if blk > 0:
                b_ref = b[lo - 1:lo, :]
                q_hat = (q_i * jnp.exp(b_i - b_ref)).astype(BF16)
                k_hat = (k[0:lo, :] * jnp.exp(b_ref - b[0:lo, :])).astype(BF16)
                scores = _dot_nt(q_hat, k_hat)
                o_blk = o_blk + _dot(scores.astype(BF16), v_bf[0:lo, :])
            outs.append(o_blk)
        return jnp.concatenate(outs, axis=0)

    def one_chunk(r0, factored):
        rows = pl.ds(r0, c)
        hs = [slice(h * HEAD_DIM, (h + 1) * HEAD_DIM) for h in range(heads)]
        q = _silu(q_ref[rows, :].astype(F32))
        f = lb_all + (1.0 - lb_all) * jax.nn.sigmoid(f_ref[rows, :])
        v_bf = i_ref[rows, :]
        v = v_bf.astype(F32)
        lf = jnp.log(f)
        k = 1.0 - f
        lf_hi, lf_lo = _split_bf16(lf)
        b2 = _dot(tril, jnp.concatenate([lf_hi, lf_lo], axis=1))
        width = lf.shape[1]
        b = b2[:, :width] + b2[:, width:]
        b_last = b[c - 1:c, :]
        q_dec = (q * jnp.exp(b)).astype(BF16)
        k_dec = (k * jnp.exp(b_last - b)).astype(BF16)
        decay = jnp.exp(b_last)
        st = [st_ref[h] for h in range(heads)]
        st_bf = [s.astype(BF16) for s in st]
        if factored:
            k_grow = (k * jnp.exp(-b)).astype(BF16)
            both = [_dot_nt(q_dec[:, hs[h]], jnp.concatenate([st_bf[h], k_grow[:, hs[h]]], axis=0))
                    for h in range(heads)]
            scores = [jnp.where(row >= col, both[h][:, HEAD_DIM:], 0.0).astype(BF16)
                      for h in range(heads)]
            outs = [both[h][:, :HEAD_DIM] + _dot(scores[h], v_bf[:, hs[h]]) for h in range(heads)]
        else:
            outs = [intra_pairwise(q[:, hs[h]], k[:, hs[h]], v[:, hs[h]], v_bf[:, hs[h]],
                                   b[:, hs[h]], _dot_nt(q_dec[:, hs[h]], st_bf[h]))
                    for h in range(heads)]
        for h in range(heads):
            st_ref[h] = st[h] * decay[:, hs[h]] + _dot_tn(v_bf[:, hs[h]], k_dec[:, hs[h]])
        inv = [lax.rsqrt(jnp.mean(o * o, axis=-1, keepdims=True) + RMS_EPS) for o in outs]
        o = jnp.concatenate([outs[h] * inv[h] for h in range(heads)], axis=1)
        o_ref[rows, :] = (o * gn_all * _silu(g_ref[rows, :].astype(F32))).astype(o_ref.dtype)

    groups = ws_ref.shape[0]
    gdim = u_ref.shape[-1] // groups
    gm_row = lax.broadcasted_iota(jnp.int32, (GM_BLOCK, GM_BLOCK), 0) // CHUNK
    gm_col = lax.broadcasted_iota(jnp.int32, (GM_BLOCK, GM_BLOCK), 1) // CHUNK
    ws = [jnp.where(gm_row >= gm_col, ws_ref[g], 0.0).astype(BF16) for g in range(groups)]

    def gmlp_block(r0):
        rows = pl.ds(r0, GM_BLOCK)
        u = _gelu(u_ref[rows, :].astype(F32))
        v = _layer_norm(_gelu(v_ref[rows, :].astype(F32)), vg_ref[...], vb_ref[...])
        v_bf = v.astype(BF16)
        mixed = jnp.concatenate(
            [_dot(ws[g], v_bf[:, g * gdim:(g + 1) * gdim]) for g in range(groups)], axis=1)
        y = u * (mixed + bs_ref[...])
        inv = lax.rsqrt(jnp.mean(y * y, axis=-1, keepdims=True) + RMS_EPS)
        ogm_ref[rows, :] = (y * inv * og_ref[...]).astype(ogm_ref.dtype)

    def run_chunks(factored):
        per_trip = 4 if (factored and n_chunks % 4 == 0) else 1
        trip_rows = per_trip * c
        inline_gmlp = trip_rows % GM_BLOCK == 0

        def chunk_body(ci, carry):
            for u in range(per_trip):
                one_chunk(pl.multiple_of((ci * per_trip + u) * c, c), factored)
            if inline_gmlp:
                for n in range(trip_rows // GM_BLOCK):
                    gmlp_block(pl.multiple_of(ci * trip_rows + n * GM_BLOCK, GM_BLOCK))
            return carry

        lax.fori_loop(0, n_chunks // per_trip, chunk_body, 0)
        if not inline_gmlp:
            for n in range(n_chunks * c // GM_BLOCK):
                gmlp_block(n * GM_BLOCK)

    bounded = jnp.min(lb_all) >= FACTORED_MIN_LB
    pl.when(bounded)(lambda: run_chunks(True))
    pl.when(jnp.logical_not(bounded))(lambda: run_chunks(False))


def _mixer(proj3, f_pre3, lb_logits, norm_g, v_norm_g, v_norm_b, w_s, b_s, out_norm_g, sb):
    bsz, s, _ = proj3.shape
    hg_width = norm_g.shape[0]
    heads = hg_width // HEAD_DIM
    gw = v_norm_g.shape[0]
    assert gw == hg_width and sb % GM_BLOCK == 0
    groups = w_s.shape[0]
    bias_full = jnp.repeat(b_s.T, gw // groups, axis=1)

    def sect(k):
        return pl.BlockSpec((None, sb, hg_width), lambda b, j, k=k: (b, j, k))

    row_vec = pl.BlockSpec((1, hg_width), lambda b, j: (0, 0))
    return pl.pallas_call(
        functools.partial(_mixer_kernel, n_chunks=sb // CHUNK),
        grid=(bsz, s // sb),
        in_specs=[sect(0), sect(0), sect(2), sect(3),
                  pl.BlockSpec((lb_logits.shape[0], hg_width), lambda b, j: (0, 0)),
                  row_vec, sect(4), sect(5), row_vec, row_vec,
                  pl.BlockSpec((groups, GM_BLOCK, GM_BLOCK), lambda b, j: (0, 0, 0)),
                  pl.BlockSpec((GM_BLOCK, gw), lambda b, j: (0, 0)),
                  row_vec],
        out_specs=[pl.BlockSpec((None, sb, hg_width), lambda b, j: (b, j, 0)),
                   pl.BlockSpec((None, sb, gw), lambda b, j: (b, j, 0))],
        out_shape=[jax.ShapeDtypeStruct((bsz, s, hg_width), BF16),
                   jax.ShapeDtypeStruct((bsz, s, gw), BF16)],
        scratch_shapes=[pltpu.VMEM((heads, HEAD_DIM, HEAD_DIM), F32)],
        compiler_params=_cparams(("parallel", "arbitrary")),
        name="mixer",
    )(proj3, f_pre3, proj3, proj3, lb_logits, norm_g.reshape(1, hg_width), proj3, proj3,
      v_norm_g.reshape(1, gw), v_norm_b.reshape(1, gw), w_s, bias_full, out_norm_g.reshape(1, gw))


def _out_proj_kernel(yh_ref, yg_ref, h0_ref, wa_ref, wb_ref, g_ref, b_ref, wr_ref, rb_ref,
                     wsg_ref, wsd_ref, base_ref, hp_ref, eidx_ref, gate_ref, rank_ref, cnt_ref,
                     carry_ref, *, alpha):
    @pl.when(pl.program_id(0) == 0)
    def _():
        carry_ref[...] = jnp.zeros_like(carry_ref)

    mix = _dot(yh_ref[...], wa_ref[...]) + _dot(yg_ref[...], wb_ref[...])
    h1 = _layer_norm(alpha * h0_ref[...] + mix, g_ref[...], b_ref[...])
    hp_ref[...] = _pack_halves(h1)
    w_hi, w_lo = _split_bf16(wr_ref[...])
    h_hi, h_lo = _split_bf16(h1)
    logits = _dot_nt(w_hi, h_hi) + (_dot_nt(w_hi, h_lo) + _dot_nt(w_lo, h_hi))
    gu = _dot(h_hi, wsg_ref[...])
    half = gu.shape[1] // 2
    hid = (_silu(gu[:, :half]) * gu[:, half:]).astype(BF16)
    base_ref[...] = alpha * h1 + _dot(hid, wsd_ref[...])
    eidx, gates, rank, total = _route_tile(logits, rb_ref[...], carry_ref[...])
    eidx_ref[...] = eidx
    gate_ref[...] = gates
    rank_ref[...] = rank
    carry_ref[...] = total
    cnt_ref[...] = total.astype(jnp.int32)


def _out_proj(y_hg, y_gm, h0, w_out_bf, g, b, w_router_t, router_bias, ws_gu_bf, ws_down_bf, alpha, tm):
    t, d = h0.shape
    sg = ws_gu_bf.shape[1]
    sd = ws_down_bf.shape[0]
    hw = y_hg.shape[1]
    gw = y_gm.shape[1]
    e = w_router_t.shape[0]
    return pl.pallas_call(
        functools.partial(_out_proj_kernel, alpha=alpha),
        grid=(t // tm,),
        in_specs=[pl.BlockSpec((tm, hw), lambda i: (i, 0)),
                  pl.BlockSpec((tm, gw), lambda i: (i, 0)),
                  pl.BlockSpec((tm, d), lambda i: (i, 0)),
                  pl.BlockSpec((hw, d), lambda i: (0, 0)),
                  pl.BlockSpec((gw, d), lambda i: (0, 0)),
                  pl.BlockSpec((1, d), lambda i: (0, 0)),
                  pl.BlockSpec((1, d), lambda i: (0, 0)),
                  pl.BlockSpec((e, d), lambda i: (0, 0)),
                  pl.BlockSpec((e, 1), lambda i: (0, 0)),
                  pl.BlockSpec((d, sg), lambda i: (0, 0)),
                  pl.BlockSpec((sd, d), lambda i: (0, 0))],
        out_specs=[pl.BlockSpec((tm, d), lambda i: (i, 0)),
                   pl.BlockSpec((tm, d // 2), lambda i: (i, 0)),
                   pl.BlockSpec((TOP_K, tm), lambda i: (0, i)),
                   pl.BlockSpec((TOP_K, tm), lambda i: (0, i)),
                   pl.BlockSpec((TOP_K, tm), lambda i: (0, i)),
                   pl.BlockSpec((e, 1), lambda i: (0, 0))],
        out_shape=[jax.ShapeDtypeStruct((t, d), F32), jax.ShapeDtypeStruct((t, d // 2), jnp.uint32),
                   jax.ShapeDtypeStruct((TOP_K, t), jnp.int32),
                   jax.ShapeDtypeStruct((TOP_K, t), F32),
                   jax.ShapeDtypeStruct((TOP_K, t), jnp.int32),
                   jax.ShapeDtypeStruct((e, 1), jnp.int32)],
        scratch_shapes=[pltpu.VMEM((e, 1), F32)],
        compiler_params=_cparams(("arbitrary",)),
        name="out_proj",
    )(y_hg, y_gm, h0, w_out_bf[:hw], w_out_bf[hw:], g.reshape(1, d), b.reshape(1, d), w_router_t,
      router_bias.astype(F32).reshape(e, 1), ws_gu_bf, ws_down_bf)


def _route_tile(logits, bias, carry):
    n_exp, tt = logits.shape
    per_group = n_exp // N_GROUPS
    neg_inf = jnp.float32(-jnp.inf)
    scores = jax.nn.sigmoid(logits)
    biased = scores + bias

    gio = lax.broadcasted_iota(jnp.int32, (per_group, tt), 0)
    blocks, gs_rows = [], []
    for g in range(N_GROUPS):
        blk = biased[g * per_group:(g + 1) * per_group, :]
        m1 = jnp.max(blk, axis=0, keepdims=True)
        first = jnp.min(jnp.where(blk == m1, gio, per_group), axis=0, keepdims=True)
        m2 = jnp.max(jnp.where(gio == first, neg_inf, blk), axis=0, keepdims=True)
        blocks.append(blk)
        gs_rows.append(m1 + m2)
    gs = jnp.concatenate(gs_rows, axis=0)
    gidx = lax.broadcasted_iota(jnp.int32, (N_GROUPS, tt), 0)
    beaten = jnp.zeros((N_GROUPS, tt), jnp.int32)
    for g in range(N_GROUPS):
        r = gs_rows[g]
        beaten = beaten + jnp.where((r > gs) | ((r == gs) & (g < gidx)), 1, 0)
    keep = beaten < TOPK_GROUPS
    cand = jnp.concatenate(
        [jnp.where(keep[g:g + 1, :], blocks[g], neg_inf) for g in range(N_GROUPS)], axis=0)

    eio = lax.broadcasted_iota(jnp.int32, (n_exp, tt), 0)
    idx_rows, w_rows = [], []
    chosen = jnp.zeros((n_exp, tt), F32)
    for _ in range(TOP_K):
        m = jnp.max(cand, axis=0, keepdims=True)
        idx = jnp.min(jnp.where(cand == m, eio, n_exp), axis=0, keepdims=True)
        sel = eio == idx
        w_rows.append(jnp.sum(jnp.where(sel, scores, 0.0), axis=0, keepdims=True))
        idx_rows.append(idx)
        cand = jnp.where(sel, neg_inf, cand)
        chosen = jnp.where(sel, 1.0, chosen)
    eidx = jnp.concatenate(idx_rows, axis=0)
    w = jnp.concatenate(w_rows, axis=0)
    gates = w / jnp.sum(w, axis=0, keepdims=True) * ROUTED_SCALE

    tr = lax.broadcasted_iota(jnp.int32, (tt, tt), 0)
    tc = lax.broadcasted_iota(jnp.int32, (tt, tt), 1)
    before = (tr < tc).astype(BF16)
    cum = _dot(chosen.astype(BF16), before) + carry
    rank = jnp.concatenate(
        [jnp.sum(jnp.where(eio == idx_rows[k], cum, 0.0), axis=0, keepdims=True)
         for k in range(TOP_K)], axis=0).astype(jnp.int32)
    total = carry + jnp.sum(chosen, axis=1, keepdims=True)
    return eidx, gates, rank, total


def _chunked(table, chunk):
    k, t = table.shape
    return table.reshape(k, t // chunk, chunk).transpose(1, 0, 2)


def _sc_workers():
    info = plsc.get_sparse_core_info()
    return info.num_cores, info.num_cores * info.num_subcores


def _slot_tables(eidx, rank, pad_start, n_rows, chunk):
    k_top, t = eidx.shape
    n_exp = pad_start.shape[0]
    n_cores, n_workers = _sc_workers()
    n_chunks = t // chunk
    per_worker = n_chunks // n_workers
    lanes = plsc.get_sparse_core_info().num_lanes
    mesh = plsc.VectorSubcoreMesh(core_axis_name="c", subcore_axis_name="s")

    @functools.partial(
        pl.kernel, mesh=mesh,
        out_type=(jax.ShapeDtypeStruct((n_rows, TOKEN_ROW), jnp.int32),
                  jax.ShapeDtypeStruct((n_chunks, k_top, chunk), jnp.int32)),
        scratch_types=[pltpu.VMEM((n_exp,), jnp.int32),
                       pltpu.VMEM((k_top, chunk), jnp.int32),
                       pltpu.VMEM((k_top, chunk), jnp.int32),
                       pltpu.VMEM((k_top, chunk), jnp.int32),
                       pltpu.VMEM((chunk, TOKEN_ROW), jnp.int32),
                       pltpu.SemaphoreType.DMA],
        compiler_params=pltpu.CompilerParams(needs_layout_passes=False),
        name="slot_tables",
    )
    def build(e_hbm, r_hbm, ps_hbm, tok_hbm, d_hbm, ps_v, e_v, r_v, idx_v, rows_v, sem):
        wid = lax.axis_index("s") * n_cores + lax.axis_index("c")
        pltpu.sync_copy(ps_hbm, ps_v)

        @pl.loop(0, per_worker)
        def _(j):
            c = wid * per_worker + j
            pltpu.sync_copy(e_hbm.at[c], e_v)
            pltpu.sync_copy(r_hbm.at[c], r_v)
            for k in range(k_top):
                for s in range(chunk // lanes):
                    sl = pl.ds(s * lanes, lanes)
                    idx_v[k, sl] = plsc.load_gather(ps_v, [e_v[k, sl]]) + r_v[k, sl]
            for i in range(chunk):
                rows_v[i, pl.ds(0, lanes)] = jnp.full((lanes,), c * chunk + i, jnp.int32)
            pltpu.sync_copy(idx_v, d_hbm.at[c])
            copies = [pltpu.async_copy(rows_v, tok_hbm.at[idx_v.at[k]], sem) for k in range(k_top)]
            for cp in copies:
                cp.wait()

    return build(_chunked(eidx, chunk), _chunked(rank, chunk), pad_start)


def _gather_back(dest_chunks, ys, sub):
    n_chunks, k_top, chunk = dest_chunks.shape
    t = n_chunks * chunk
    dh = ys.shape[1]
    n_cores, n_workers = _sc_workers()
    per_worker = n_chunks // n_workers
    mesh = plsc.VectorSubcoreMesh(core_axis_name="c", subcore_axis_name="s")

    @functools.partial(
        pl.kernel, mesh=mesh,
        out_type=jax.ShapeDtypeStruct((k_top, t, dh), ys.dtype),
        scratch_types=[pltpu.VMEM((k_top, chunk), jnp.int32),
                       pltpu.VMEM((k_top, sub, dh), ys.dtype),
                       pltpu.SemaphoreType.DMA],
        name="gather_back",
    )
    def gather_rows(ys_hbm, dest_hbm, o_hbm, idx_v, rows_v, sem):
        wid = lax.axis_index("s") * n_cores + lax.axis_index("c")

        @pl.loop(0, per_worker)
        def _(j):
            c = wid * per_worker + j
            pltpu.sync_copy(dest_hbm.at[c], idx_v)
            for q in range(chunk // sub):
                copies = [pltpu.async_copy(ys_hbm.at[idx_v.at[k, pl.ds(q * sub, sub)]], rows_v.at[k], sem)
                          for k in range(k_top)]
                for cp in copies:
                    cp.wait()
                for k in range(k_top):
                    pltpu.sync_copy(rows_v.at[k], o_hbm.at[k, pl.ds(c * chunk + q * sub, sub)])

    return gather_rows(ys, dest_chunks)


def _experts_kernel(bs_ref, cnt_ref, wgu_hbm, wd_hbm, h_hbm, tok_hbm, ys_hbm, wgu_buf, wd_buf,
                    h_v, xg, ybuf, tokbuf, wsem, hsem, tsem, ysem):
    e = pl.program_id(0)
    n_exp = pl.num_programs(0)
    n_slots, mb, _ = ybuf.shape
    w_slots = wgu_buf.shape[0]
    t_slots = tokbuf.shape[0]
    n_tok = h_v.shape[0]
    n_blocks = bs_ref[n_exp]
    b0 = bs_ref[e]
    b1 = bs_ref[e + 1]
    count = cnt_ref[e]

    def w_copies(ex):
        slot = lax.rem(ex, w_slots)
        return (pltpu.make_async_copy(wgu_hbm.at[ex], wgu_buf.at[slot], wsem.at[slot]),
                pltpu.make_async_copy(wd_hbm.at[ex], wd_buf.at[slot], wsem.at[slot]))

    def tok_copy(b):
        src = jnp.minimum(b, n_blocks - 1)
        slot = lax.rem(b, t_slots)
        return pltpu.make_async_copy(tok_hbm.at[pl.ds(src, 1)], tokbuf.at[slot], tsem.at[slot])

    def y_copy(b):
        slot = lax.rem(b, n_slots)
        return pltpu.make_async_copy(ybuf.at[slot],
                                     ys_hbm.at[pl.ds(pl.multiple_of(b * mb, mb), mb)], ysem.at[slot])

    def gather_rows(b):
        ids = lax.rem(b, t_slots)
        dst = lax.rem(b, xg.shape[0])
        for i in range(mb):
            xg[dst, pl.ds(i, 1), :] = h_v[pl.ds(tokbuf[ids, 0, i], 1), :]

    @pl.when(e == 0)
    def _():
        resident = pltpu.make_async_copy(h_hbm, h_v, hsem)
        resident.start()
        for i in range(w_slots - 1):
            @pl.when(i < n_exp)
            def _():
                for cp in w_copies(i):
                    cp.start()
        for i in range(t_slots - 1):
            tok_copy(i).start()
        resident.wait()
        for i in range(GATHER_AHEAD):
            tok_copy(i).wait()
            gather_rows(i)

    @pl.when(e + w_slots - 1 < n_exp)
    def _():
        for cp in w_copies(e + w_slots - 1):
            cp.start()

    for cp in w_copies(e):
        cp.wait()
    w_slot = lax.rem(e, w_slots)

    kh = wgu_buf.shape[1] // 2
    half = wgu_buf.shape[2] // 2

    def compute(blocks):
        rows = lax.broadcasted_iota(jnp.int32, (mb, xg.shape[2]), 0)
        halves = []
        for b in blocks:
            valid = count - (b - b0) * mb
            words = jnp.where(rows < valid, xg[lax.rem(b, xg.shape[0])], jnp.uint32(0))
            xa, xb = _unpack_halves(words)
            halves.append((xa.astype(BF16), xb.astype(BF16)))
        w_a = wgu_buf[w_slot, 0:kh, :].astype(BF16)
        w_b = wgu_buf[w_slot, kh:, :].astype(BF16)
        gus = [_dot(xa, w_a) + _dot(xb, w_b) for xa, xb in halves]
        hids = [(_silu(gu[:, :half]) * gu[:, half:]).astype(BF16) for gu in gus]
        w_d = wd_buf[w_slot].astype(BF16)
        outs = [_dot(hid, w_d) for hid in hids]
        for b, y in zip(blocks, outs):
            ybuf[lax.rem(b, n_slots)] = _pack_halves(y)

    def step(b, width):
        blocks = [b + i for i in range(width)]
        for bb in blocks:
            tok_copy(bb + t_slots - 1).start()
            tok_copy(bb + GATHER_AHEAD).wait()
        for bb in blocks:
            @pl.when(bb >= n_slots)
            def _():
                y_copy(bb - n_slots).wait()
        compute(blocks)
        for bb in blocks:
            gather_rows(bb + GATHER_AHEAD)
        for bb in blocks:
            y_copy(bb).start(priority=RING_DMA_PRIORITY)

    n_pairs = (b1 - b0) // 2

    def pair(p, carry):
        step(b0 + 2 * p, 2)
        return carry

    lax.fori_loop(0, n_pairs, pair, 0)

    @pl.when(b0 + 2 * n_pairs < b1)
    def _():
        step(b1 - 1, 1)

    @pl.when(e == n_exp - 1)
    def _():
        for i in range(1, n_slots + 1):
            @pl.when(n_blocks >= i)
            def _():
                y_copy(n_blocks - i).wait()
        for i in range(GATHER_AHEAD, t_slots - 1):
            tok_copy(n_blocks + i).wait()


def _experts(block_start, counts, h1_packed, tok_ids, w_gu, w_down):
    t, dh = h1_packed.shape
    n_rows = tok_ids.shape[0] * tok_ids.shape[1]
    n_exp, d, gu_w = w_gu.shape
    ed = w_down.shape[1]
    return pl.pallas_call(
        _experts_kernel,
        grid_spec=pltpu.PrefetchScalarGridSpec(
            num_scalar_prefetch=2,
            grid=(n_exp,),
            in_specs=[pl.BlockSpec(memory_space=pl.ANY)] * 4,
            out_specs=pl.BlockSpec(memory_space=pl.ANY),
            scratch_shapes=[pltpu.VMEM((WEIGHT_SLOTS, d, gu_w), w_gu.dtype),
                            pltpu.VMEM((WEIGHT_SLOTS, ed, d), w_down.dtype),
                            pltpu.VMEM((t, dh), h1_packed.dtype),
                            pltpu.VMEM((GATHER_AHEAD + BLOCKS_PER_STEP, MOE_BLOCK, dh), jnp.uint32),
                            pltpu.VMEM((RING_SLOTS, MOE_BLOCK, dh), jnp.uint32),
                            pltpu.SMEM((TOKEN_SLOTS, 1, MOE_BLOCK), jnp.int32),
                            pltpu.SemaphoreType.DMA((WEIGHT_SLOTS,)),
                            pltpu.SemaphoreType.DMA,
                            pltpu.SemaphoreType.DMA((TOKEN_SLOTS,)),
                            pltpu.SemaphoreType.DMA((RING_SLOTS,))]),
        out_shape=jax.ShapeDtypeStruct((n_rows, dh), jnp.uint32),
        compiler_params=pltpu.CompilerParams(dimension_semantics=("arbitrary",),
                                             vmem_limit_bytes=EXPERTS_VMEM_LIMIT),
        name="experts",
    )(block_start, counts, w_gu, w_down, h1_packed, tok_ids)


def _combine_kernel(base_ref, w_ref, g_ref, b_ref, yg_ref, *rest):
    o_ref = rest[-1]
    acc = base_ref[...]
    dh = acc.shape[1] // 2
    acc_a = acc[:, :dh]
    acc_b = acc[:, dh:]
    w = w_ref[...]
    for k in range(TOP_K):
        ya, yb = _unpack_halves(yg_ref[k])
        acc_a = acc_a + ya * w[:, k:k + 1]
        acc_b = acc_b + yb * w[:, k:k + 1]
    o_ref[...] = _layer_norm(jnp.concatenate([acc_a, acc_b], axis=1), g_ref[...], b_ref[...])


def _combine(base, w_tk, g, b, yg, prev, part, tc):
    t, d = base.shape
    tiles = yg.shape[1] // tc
    off = part * tiles
    in_specs = [pl.BlockSpec((tc, d), lambda i: (i + off, 0)),
                pl.BlockSpec((tc, TOP_K), lambda i: (i + off, 0)),
                pl.BlockSpec((1, d), lambda i: (0, 0)),
                pl.BlockSpec((1, d), lambda i: (0, 0)),
                pl.BlockSpec((TOP_K, tc, yg.shape[2]), lambda i: (0, i, 0),
                             pipeline_mode=pl.Buffered(3))]
    args = [base, w_tk, g.reshape(1, d), b.reshape(1, d), yg]
    out_spec = pl.BlockSpec((tc, d), lambda i: (i + off, 0))
    aliases = {}
    if prev is not None:
        args.append(prev)
        aliases = {len(args) - 1: 0}

    def outer(*refs):
        o_hbm = refs[-1]
        pltpu.emit_pipeline(_combine_kernel, grid=(tiles,), in_specs=in_specs,
                            out_specs=[out_spec])(*refs[:5], o_hbm)

    return pl.pallas_call(
        outer,
        in_specs=[pl.BlockSpec(memory_space=pl.ANY)] * len(args),
        out_specs=pl.BlockSpec(memory_space=pl.ANY),
        out_shape=jax.ShapeDtypeStruct((t, d), F32),
        input_output_aliases=aliases,
        compiler_params=pltpu.CompilerParams(vmem_limit_bytes=VMEM_LIMIT),
        name="combine",
    )(*args)


def _moe(base, h1_packed, eidx, w_kt, rank, counts, w_exp_gu, w_exp_down, ln_g, ln_b):
    t, d = base.shape
    n_exp = counts.shape[0]

    counts = counts.reshape(n_exp)
    blocks_per_expert = (counts + MOE_BLOCK - 1) // MOE_BLOCK
    block_start = jnp.concatenate(
        [jnp.zeros((1,), jnp.int32), jnp.cumsum(blocks_per_expert).astype(jnp.int32)])
    pad_start = block_start[:n_exp] * MOE_BLOCK
    n_blocks = -(-(t * TOP_K + n_exp * (MOE_BLOCK - 1)) // MOE_BLOCK)

    tok_rows, dest_chunks = _slot_tables(eidx, rank, pad_start, n_blocks * MOE_BLOCK, chunk=SLOT_CHUNK)
    tok_ids = jnp.clip(tok_rows[:, 0], 0, t - 1).reshape(n_blocks, MOE_BLOCK)
    ys = _experts(block_start, counts, h1_packed, tok_ids, w_exp_gu, w_exp_down)
    per_part = dest_chunks.shape[0] // COMBINE_PARTS
    w_tk = w_kt.T
    out = None
    for part in range(COMBINE_PARTS):
        yg = _gather_back(dest_chunks[part * per_part:(part + 1) * per_part], ys, sub=GATHER_CHUNK)
        out = _combine(base, w_tk, ln_g, ln_b, yg, out, part, tc=min(256, t))
    return out


def kernel(x, ln_in_g, ln_in_b, w_in, hg_lb_logits, hg_norm_g, gm_v_norm_g, gm_v_norm_b, gm_w_s, gm_b_s, gm_out_norm_g, w_out, ln1_g, ln1_b, w_router, router_bias, w_exp_gu, w_exp_down, w_shared_gu, w_shared_down, ln2_g, ln2_b):
    bsz, s, d = x.shape
    depth = w_in.shape[0]
    assert depth == 1, "the lower-bound table row used in the hgrn kernel assumes one layer"
    alpha = (2.0 * depth) ** 0.25
    t = bsz * s
    hg_width = hg_norm_g.shape[1]
    gm_width = gm_v_norm_g.shape[1]
    tm = min(512, t)

    x2 = x.reshape(t, d)
    h0, proj, f_pre = _in_proj(x2, ln_in_g, ln_in_b, w_in[0].astype(BF16), hg_width, hg_width, tm)
    proj3 = proj.reshape(bsz, s, proj.shape[1])
    y_hg, y_gm = _mixer(proj3, f_pre.reshape(bsz, s, hg_width), hg_lb_logits.astype(F32),
                        hg_norm_g[0], gm_v_norm_g[0], gm_v_norm_b[0], gm_w_s[0], gm_b_s[0],
                        gm_out_norm_g[0], sb=min(512, s))
    base, h1_packed, eidx, gates, rank, counts = _out_proj(
        y_hg.reshape(t, hg_width), y_gm.reshape(t, gm_width), h0, w_out[0].astype(BF16),
        ln1_g[0], ln1_b[0], w_router[0].T, router_bias[0], w_shared_gu[0].astype(BF16),
        w_shared_down[0].astype(BF16), alpha, tm)
    out = _moe(base, h1_packed, eidx, gates, rank, counts, w_exp_gu[0], w_exp_down[0],
               ln2_g[0], ln2_b[0])
    return out.reshape(bsz, s, d)
```
